```python
import math
import jax
import jax.numpy as jnp
from jax import lax
import numpy as np

D_MODEL = 1024
BATCH = 32
SEQ = 256
DEPTH = 2
DEC_BATCH = 4
DEC_SEQ = 2048
PAST_LEN = 256

GRID_W = 64
ROPE_BASE = 10000.0
EPS = 1e-6
NEG_INF = -1e30
BLK = 128

HQ_A = 8
HKV_A = 2
G_A = HQ_A // HKV_A
HD_A = 64
WINDOW = 128

D_B = 512
GS_B = 16
G_B = D_B // GS_B
P_B = 64

H_C = 8
Q_LORA = 256
KV_LORA = 128
NOPE_C = 64
ROPE_C = 32
QK_C = NOPE_C + ROPE_C
V_C = 64

N_BRANCH = 3
IN_SPLITS = (HQ_A * HD_A, HKV_A * HD_A, HKV_A * HD_A, D_B, Q_LORA, KV_LORA, ROPE_C, N_BRANCH * D_MODEL)
IN_DIM = HQ_A * HD_A + 2 * HKV_A * HD_A + D_B + Q_LORA + KV_LORA + ROPE_C + N_BRANCH * D_MODEL

N_EXP = 32
TOP_K = 4
D_FF = 1024
SWIGLU_ALPHA = 1.702
SWIGLU_LIMIT = 7.0
MOE_BLK = 128

kernel_name = 'hybrid_flow_backbone_step'


def rms_norm(x, g):
    xf = x.astype(jnp.float32)
    y = xf * lax.rsqrt(jnp.mean(xf * xf, axis=-1, keepdims=True) + EPS)
    return (y * g.astype(jnp.float32)).astype(x.dtype)


def modulate(x, shift, scale):
    return x * (1 + scale) + shift


def adaln_modulation(cvec, p):
    mod = jax.nn.silu(cvec) @ p['w_ada'] + p['b_ada']
    return jnp.split(mod, 6, axis=-1)


def axial_rope_tables(n_tokens, rot_dim):
    rows = n_tokens // GRID_W
    row = jnp.broadcast_to(jnp.arange(rows, dtype=jnp.float32)[:, None], (rows, GRID_W)).reshape(-1)
    col = jnp.broadcast_to(jnp.arange(GRID_W, dtype=jnp.float32)[None, :], (rows, GRID_W)).reshape(-1)
    nf = rot_dim // 4
    inv_freq = ROPE_BASE ** (-jnp.arange(nf, dtype=jnp.float32) / nf)
    ang = jnp.stack([row[:, None] * inv_freq, col[:, None] * inv_freq], axis=1)
    return jnp.cos(ang), jnp.sin(ang)


def apply_axial_rope(x, cos, sin):
    nf = cos.shape[-1]
    xs = x.astype(jnp.float32).reshape(x.shape[:-1] + (2, 2, nf))
    x1, x2 = xs[..., 0, :], xs[..., 1, :]
    cc, ss = cos[None, :, None], sin[None, :, None]
    out = jnp.stack([x1 * cc - x2 * ss, x1 * ss + x2 * cc], axis=-2)
    return out.reshape(x.shape).astype(x.dtype)


def attend(q, k, v, scale, sink=None):
    s = jnp.einsum('bqhgd,bkhd->bhgqk', q, k).astype(jnp.float32) * scale
    if sink is not None:
        col = jnp.broadcast_to(sink.astype(jnp.float32).reshape(1, q.shape[2], q.shape[3], 1, 1), s.shape[:-1] + (1,))
        p = jax.nn.softmax(jnp.concatenate([s, col], axis=-1), axis=-1)[..., :-1]
    else:
        p = jax.nn.softmax(s, axis=-1)
    return jnp.einsum('bhgqk,bkhd->bqhgd', p.astype(v.dtype), v)


def windowed_attn_with_context(q, k, v, k_ctx, v_ctx, sink, scale):
    bsz, n = q.shape[:2]
    nb = n // BLK
    qb = q.reshape(bsz, nb, BLK, HKV_A, G_A, HD_A)
    pad = ((0, 0), (BLK, BLK), (0, 0), (0, 0))
    kb = jnp.pad(k, pad).reshape(bsz, nb + 2, BLK, HKV_A, HD_A)
    vb = jnp.pad(v, pad).reshape(bsz, nb + 2, BLK, HKV_A, HD_A)
    k_band = jnp.concatenate([kb[:, :-2], kb[:, 1:-1], kb[:, 2:]], axis=2)
    v_band = jnp.concatenate([vb[:, :-2], vb[:, 1:-1], vb[:, 2:]], axis=2)
    blocks = jnp.arange(nb)[:, None]
    q_pos = blocks * BLK + jnp.arange(BLK)[None, :]
    k_pos = (blocks - 1) * BLK + jnp.arange(3 * BLK)[None, :]
    rel = k_pos[:, None, :] - q_pos[:, :, None]
    valid = (jnp.abs(rel) <= WINDOW) & (k_pos[:, None, :] >= 0) & (k_pos[:, None, :] < n)
    s_loc = jnp.einsum('bnqhgd,bnkhd->bnhgqk', qb, k_band).astype(jnp.float32) * scale
    s_loc = jnp.where(valid[None, :, None, None], s_loc, NEG_INF)
    s_ctx = jnp.einsum('bnqhgd,bkhd->bnhgqk', qb, k_ctx).astype(jnp.float32) * scale
    col = jnp.broadcast_to(sink.astype(jnp.float32).reshape(1, 1, HKV_A, G_A, 1, 1), s_loc.shape[:-1] + (1,))
    p = jax.nn.softmax(jnp.concatenate([s_loc, s_ctx, col], axis=-1), axis=-1).astype(v.dtype)
    o = (jnp.einsum('bnhgqk,bnkhd->bnqhgd', p[..., :3 * BLK], v_band)
         + jnp.einsum('bnhgqk,bkhd->bnqhgd', p[..., 3 * BLK:-1], v_ctx))
    return o.reshape(bsz, n, HQ_A * HD_A)


def mla_queries(cq, p, rope):
    bsz, n = cq.shape[:2]
    q = (rms_norm(cq, p['q_a_norm_c']) @ p['w_uq_c']).reshape(bsz, n, H_C, QK_C)
    q = rms_norm(q, p['q_norm_c'])
    if rope is not None:
        q = jnp.concatenate([q[..., :NOPE_C], apply_axial_rope(q[..., NOPE_C:], *rope)], axis=-1)
    return q


def mla_keys(ckv_n, krope, p, rope):
    bsz, n = ckv_n.shape[:2]
    kv = (ckv_n @ p['w_ukv_c']).reshape(bsz, n, H_C, NOPE_C + V_C)
    k = jnp.concatenate([kv[..., :NOPE_C], jnp.broadcast_to(krope[:, :, None, :], (bsz, n, H_C, ROPE_C))], axis=-1)
    k = rms_norm(k, p['k_norm_c'])
    if rope is not None:
        k = jnp.concatenate([k[..., :NOPE_C], apply_axial_rope(k[..., NOPE_C:], *rope)], axis=-1)
    return k, kv[..., NOPE_C:]


def mla_latent_attn(q, k, v, scale):
    bsz, n = q.shape[:2]
    nb = n // BLK
    qb = jnp.moveaxis(q.reshape(bsz, nb, BLK, H_C, 1, QK_C), 1, 0)
    ob = lax.map(lambda qi: attend(qi, k, v, scale), qb)
    return jnp.moveaxis(ob, 0, 1).reshape(bsz, n, H_C * V_C)


def complex_affine_combine(left, right):
    a_re1, a_im1, b_re1, b_im1 = left
    a_re2, a_im2, b_re2, b_im2 = right
    return (a_re2 * a_re1 - a_im2 * a_im1,
            a_re2 * a_im1 + a_im2 * a_re1,
            a_re2 * b_re1 - a_im2 * b_im1 + b_re2,
            a_re2 * b_im1 + a_im2 * b_re1 + b_im2)


def s5_bidirectional(u, p, h0_re, h0_im):
    bsz, n, _ = u.shape
    ug = u.astype(jnp.float32).reshape(bsz, n, G_B, GS_B)
    y = p['ssm_d'].astype(jnp.float32).reshape(G_B, GS_B) * ug
    fin_re, fin_im = [], []
    for direction in range(2):
        lr = p['ssm_lam_re'][direction].astype(jnp.float32)
        li = p['ssm_lam_im'][direction].astype(jnp.float32)
        dt = jnp.exp(p['ssm_log_dt'][direction].astype(jnp.float32))[:, None]
        decay = jnp.exp(lr * dt)
        ab_re, ab_im = decay * jnp.cos(li * dt), decay * jnp.sin(li * dt)
        den = lr * lr + li * li
        f_re = ((ab_re - 1) * lr + ab_im * li) / den
        f_im = (ab_im * lr - (ab_re - 1) * li) / den
        b_re = p['ssm_b_re'][direction].astype(jnp.float32)
        b_im = p['ssm_b_im'][direction].astype(jnp.float32)
        bb_re = f_re[..., None] * b_re - f_im[..., None] * b_im
        bb_im = f_re[..., None] * b_im + f_im[..., None] * b_re
        bu_re = jnp.einsum('gps,blgs->blgp', bb_re, ug)
        bu_im = jnp.einsum('gps,blgs->blgp', bb_im, ug)
        s_re = h0_re[:, direction].astype(jnp.float32)
        s_im = h0_im[:, direction].astype(jnp.float32)
        first = 0 if direction == 0 else n - 1
        bu_re = bu_re.at[:, first].add(ab_re * s_re - ab_im * s_im)
        bu_im = bu_im.at[:, first].add(ab_re * s_im + ab_im * s_re)
        a_re = jnp.broadcast_to(ab_re, bu_re.shape)
        a_im = jnp.broadcast_to(ab_im, bu_im.shape)
        _, _, x_re, x_im = lax.associative_scan(complex_affine_combine, (a_re, a_im, bu_re, bu_im),
                                                reverse=(direction == 1), axis=1)
        c_re = p['ssm_c_re'][direction].astype(jnp.float32)
        c_im = p['ssm_c_im'][direction].astype(jnp.float32)
        y = y + jnp.einsum('gsp,blgp->blgs', c_re, x_re) - jnp.einsum('gsp,blgp->blgs', c_im, x_im)
        last = n - 1 if direction == 0 else 0
        fin_re.append(x_re[:, last])
        fin_im.append(x_im[:, last])
    y = jax.nn.gelu(y.reshape(bsz, n, D_B))
    y = y * jax.nn.sigmoid(y @ p['w_glu'].astype(jnp.float32) + p['b_glu'].astype(jnp.float32))
    return y.astype(u.dtype), jnp.stack(fin_re, axis=1), jnp.stack(fin_im, axis=1)


def clamped_swiglu(h):
    glu = jnp.minimum(h[..., ::2], SWIGLU_LIMIT)
    lin = jnp.clip(h[..., 1::2], -SWIGLU_LIMIT, SWIGLU_LIMIT)
    return glu * jax.nn.sigmoid(SWIGLU_ALPHA * glu) * (lin + 1)


def moe(h, p):
    bsz, n_tok, d = h.shape
    xt = h.reshape(-1, d)
    n = xt.shape[0]
    logits = (xt @ p['w_router'] + p['b_router']).astype(jnp.float32)
    top_val, top_idx = lax.top_k(logits, TOP_K)
    top_w = jax.nn.softmax(top_val, axis=-1)
    flat_e = top_idx.reshape(-1)
    n_assign = n * TOP_K
    order = jnp.argsort(flat_e)
    sorted_e = flat_e[order]
    counts = jnp.bincount(flat_e, length=N_EXP)
    padded = (counts + MOE_BLK - 1) // MOE_BLK * MOE_BLK
    pad_end = jnp.cumsum(padded)
    pad_start = pad_end - padded
    start = jnp.cumsum(counts) - counts
    dest = pad_start[sorted_e] + jnp.arange(n_assign) - start[sorted_e]
    n_blocks = -(-n_assign // MOE_BLK) + N_EXP
    buf_len = n_blocks * MOE_BLK
    buf_tok = jnp.full((buf_len,), n, jnp.int32).at[dest].set((order // TOP_K).astype(jnp.int32))
    buf_w = jnp.zeros((buf_len,), jnp.float32).at[dest].set(top_w.reshape(-1)[order])
    block_e = jnp.minimum(jnp.searchsorted(pad_end, jnp.arange(n_blocks) * MOE_BLK, side='right'), N_EXP - 1)
    x_pad = jnp.concatenate([xt, jnp.zeros((1, d), xt.dtype)], axis=0)
    xb = x_pad[buf_tok].reshape(n_blocks, MOE_BLK, d)

    def expert_block(args):
        xblk, e = args
        hh = xblk @ p['w_up'][e] + p['b_up'][e]
        return clamped_swiglu(hh) @ p['w_down'][e] + p['b_down'][e]

    yb = lax.map(expert_block, (xb, block_e)).reshape(buf_len, d)
    y = jnp.zeros((n + 1, d), jnp.float32).at[buf_tok].add(yb.astype(jnp.float32) * buf_w[:, None])[:n]
    return y.reshape(bsz, n_tok, d).astype(h.dtype)


def split_inputs(h, w_in):
    idx = np.cumsum(IN_SPLITS)[:-1].tolist()
    return jnp.split(h @ w_in, idx, axis=-1)


def merge_branches(o_a, o_b, o_c, gate_logits, p):
    bsz, n = o_a.shape[:2]
    g = jax.nn.sigmoid(gate_logits.astype(jnp.float32)).astype(o_a.dtype).reshape(bsz, n, N_BRANCH, D_MODEL)
    m = (g[..., 0, :] * (o_a @ p['w_br_a']) + g[..., 1, :] * (o_b @ p['w_br_b'])
         + g[..., 2, :] * (o_c @ p['w_br_c']))
    return m @ p['w_out']


def context_layer(x, mod, p):
    shift1, scale1, gate1, shift2, scale2, gate2 = mod
    bsz, n = x.shape[:2]
    h = modulate(rms_norm(x, p['norm_mix_g']), shift1, scale1)
    qa, ka, va, ub, cq, ckv, krope, gl = split_inputs(h, p['w_in'])
    qa = rms_norm(qa.reshape(bsz, n, HQ_A, HD_A), p['q_norm_a']).reshape(bsz, n, HKV_A, G_A, HD_A)
    ka = rms_norm(ka.reshape(bsz, n, HKV_A, HD_A), p['k_norm_a'])
    va = va.reshape(bsz, n, HKV_A, HD_A)
    o_a = attend(qa, ka, va, HD_A ** -0.5, p['sink_a']).reshape(bsz, n, HQ_A * HD_A)
    zeros = jnp.zeros((bsz, 2, G_B, P_B), jnp.float32)
    o_b, fin_re, fin_im = s5_bidirectional(ub, p, zeros, zeros)
    ckv_n = rms_norm(ckv, p['kv_a_norm_c'])
    k_c, v_c = mla_keys(ckv_n, krope, p, None)
    q_c = mla_queries(cq, p, None)
    o_c = attend(q_c[:, :, :, None], k_c, v_c, QK_C ** -0.5).reshape(bsz, n, H_C * V_C)
    x = x + gate1 * merge_branches(o_a, o_b, o_c, gl, p)
    x = x + gate2 * moe(modulate(rms_norm(x, p['norm_ffn_g']), shift2, scale2), p)
    return x, (ka, va, ckv_n, krope, fin_re, fin_im)


def latent_layer(x, mod, cache, p, rope_a, rope_c):
    shift1, scale1, gate1, shift2, scale2, gate2 = mod
    k_ctx_a, v_ctx_a, ckv_ctx, krope_ctx, h0_re, h0_im = cache
    bsz, n = x.shape[:2]
    h = modulate(rms_norm(x, p['norm_mix_g']), shift1, scale1)
    qa, ka, va, ub, cq, ckv, krope, gl = split_inputs(h, p['w_in'])
    qa = apply_axial_rope(rms_norm(qa.reshape(bsz, n, HQ_A, HD_A), p['q_norm_a']), *rope_a)
    qa = qa.reshape(bsz, n, HKV_A, G_A, HD_A)
    ka = apply_axial_rope(rms_norm(ka.reshape(bsz, n, HKV_A, HD_A), p['k_norm_a']), *rope_a)
    va = va.reshape(bsz, n, HKV_A, HD_A)
    o_a = windowed_attn_with_context(qa, ka, va, k_ctx_a, v_ctx_a, p['sink_a'], HD_A ** -0.5)
    o_b, _, _ = s5_bidirectional(ub, p, h0_re, h0_im)
    k_lat, v_lat = mla_keys(rms_norm(ckv, p['kv_a_norm_c']), krope, p, rope_c)
    k_ctx, v_ctx = mla_keys(ckv_ctx, krope_ctx, p, None)
    q_c = mla_queries(cq, p, rope_c)
    o_c = mla_latent_attn(q_c, jnp.concatenate([k_lat, k_ctx], axis=1), jnp.concatenate([v_lat, v_ctx], axis=1),
                          QK_C ** -0.5)
    x = x + gate1 * merge_branches(o_a, o_b, o_c, gl, p)
    x = x + gate2 * moe(modulate(rms_norm(x, p['norm_ffn_g']), shift2, scale2), p)
    return x


def setup_inputs(seed: int = 0) -> dict:
    key = jax.random.key(seed)
    ks = iter(jax.random.split(key, 64))

    def nrm(shape, scale):
        return scale * jax.random.normal(next(ks), shape, jnp.float32)

    def gain(shape):
        return 1.0 + nrm(shape, 0.02)

    f_a = HQ_A * HD_A
    f_c = H_C * V_C
    lam_n = jnp.arange(P_B, dtype=jnp.float32)
    return {
        'x_prompt': nrm((BATCH, SEQ, D_MODEL), 1.0),
        'x_sample': nrm((DEC_BATCH, DEC_SEQ, D_MODEL), 1.0),
        'c': nrm((DEC_BATCH, D_MODEL), 1.0),
        'cache_attn_k': nrm((DEC_BATCH, DEPTH, PAST_LEN, HKV_A, HD_A), 1.0),
        'cache_attn_v': nrm((DEC_BATCH, DEPTH, PAST_LEN, HKV_A, HD_A), 1.0),
        'cache_mla_ckv': nrm((DEC_BATCH, DEPTH, PAST_LEN, KV_LORA), 1.0),
        'cache_mla_krope': nrm((DEC_BATCH, DEPTH, PAST_LEN, ROPE_C), 1.0),
        'state_ssm_re': nrm((DEC_BATCH, DEPTH, 2, G_B, P_B), 0.3),
        'state_ssm_im': nrm((DEC_BATCH, DEPTH, 2, G_B, P_B), 0.3),
        'c_ctx': nrm((D_MODEL,), 1.0),
        'w_ada': nrm((DEPTH, D_MODEL, 6 * D_MODEL), 0.5 * D_MODEL ** -0.5),
        'b_ada': nrm((DEPTH, 6 * D_MODEL), 0.02),
        'norm_mix_g': gain((DEPTH, D_MODEL)),
        'norm_ffn_g': gain((DEPTH, D_MODEL)),
        'w_in': nrm((DEPTH, D_MODEL, IN_DIM), D_MODEL ** -0.5),
        'q_norm_a': gain((DEPTH, HD_A)),
        'k_norm_a': gain((DEPTH, HD_A)),
        'sink_a': nrm((DEPTH, HQ_A), 0.5),
        'q_a_norm_c': gain((DEPTH, Q_LORA)),
        'kv_a_norm_c': gain((DEPTH, KV_LORA)),
        'w_uq_c': nrm((DEPTH, Q_LORA, H_C * QK_C), Q_LORA ** -0.5),
        'w_ukv_c': nrm((DEPTH, KV_LORA, H_C * (NOPE_C + V_C)), KV_LORA ** -0.5),
        'q_norm_c': gain((DEPTH, QK_C)),
        'k_norm_c': gain((DEPTH, QK_C)),
        'ssm_lam_re': -0.5 + nrm((DEPTH, 2, G_B, P_B), 0.02),
        'ssm_lam_im': math.pi * lam_n + nrm((DEPTH, 2, G_B, P_B), 0.02),
        'ssm_log_dt': jax.random.uniform(next(ks), (DEPTH, 2, G_B), jnp.float32, math.log(1e-3), math.log(1e-1)),
        'ssm_b_re': nrm((DEPTH, 2, G_B, P_B, GS_B), (2 * GS_B) ** -0.5),
        'ssm_b_im': nrm((DEPTH, 2, G_B, P_B, GS_B), (2 * GS_B) ** -0.5),
        'ssm_c_re': nrm((DEPTH, 2, G_B, GS_B, P_B), P_B ** -0.5),
        'ssm_c_im': nrm((DEPTH, 2, G_B, GS_B, P_B), P_B ** -0.5),
        'ssm_d': nrm((DEPTH, D_B), 1.0),
        'w_glu': nrm((DEPTH, D_B, D_B), D_B ** -0.5),
        'b_glu': nrm((DEPTH, D_B), 0.02),
        'w_br_a': nrm((DEPTH, f_a, D_MODEL), f_a ** -0.5),
        'w_br_b': nrm((DEPTH, D_B, D_MODEL), D_B ** -0.5),
        'w_br_c': nrm((DEPTH, f_c, D_MODEL), f_c ** -0.5),
        'w_out': nrm((DEPTH, D_MODEL, D_MODEL), D_MODEL ** -0.5),
        'w_router': nrm((DEPTH, D_MODEL, N_EXP), D_MODEL ** -0.5),
        'b_router': nrm((DEPTH, N_EXP), 0.01),
        'w_up': nrm((DEPTH, N_EXP, D_MODEL, 2 * D_FF), D_MODEL ** -0.5),
        'b_up': nrm((DEPTH, N_EXP, 2 * D_FF), 0.01),
        'w_down': nrm((DEPTH, N_EXP, D_FF, D_MODEL), D_FF ** -0.5),
        'b_down': nrm((DEPTH, N_EXP, D_MODEL), 0.01),
    }


def reference(x_prompt, x_sample, c, cache_attn_k, cache_attn_v, cache_mla_ckv, cache_mla_krope, state_ssm_re,
              state_ssm_im, c_ctx, w_ada, b_ada, norm_mix_g, norm_ffn_g, w_in, q_norm_a, k_norm_a, sink_a,
              q_a_norm_c, kv_a_norm_c, w_uq_c, w_ukv_c, q_norm_c, k_norm_c, ssm_lam_re, ssm_lam_im, ssm_log_dt,
              ssm_b_re, ssm_b_im, ssm_c_re, ssm_c_im, ssm_d, w_glu, b_glu, w_br_a, w_br_b, w_br_c, w_out,
              w_router, b_router, w_up, b_up, w_down, b_down):
    def layer_params(l):
        return dict(w_ada=w_ada[l], b_ada=b_ada[l], norm_mix_g=norm_mix_g[l], norm_ffn_g=norm_ffn_g[l],
                    w_in=w_in[l], q_norm_a=q_norm_a[l], k_norm_a=k_norm_a[l], sink_a=sink_a[l],
                    q_a_norm_c=q_a_norm_c[l], kv_a_norm_c=kv_a_norm_c[l], w_uq_c=w_uq_c[l], w_ukv_c=w_ukv_c[l],
                    q_norm_c=q_norm_c[l], k_norm_c=k_norm_c[l], ssm_lam_re=ssm_lam_re[l], ssm_lam_im=ssm_lam_im[l],
                    ssm_log_dt=ssm_log_dt[l], ssm_b_re=ssm_b_re[l], ssm_b_im=ssm_b_im[l], ssm_c_re=ssm_c_re[l],
                    ssm_c_im=ssm_c_im[l], ssm_d=ssm_d[l], w_glu=w_glu[l], b_glu=b_glu[l], w_br_a=w_br_a[l],
                    w_br_b=w_br_b[l], w_br_c=w_br_c[l], w_out=w_out[l], w_router=w_router[l],
                    b_router=b_router[l], w_up=w_up[l], b_up=b_up[l], w_down=w_down[l], b_down=b_down[l])

    x = x_prompt
    ctx_k, ctx_v, ctx_ckv, ctx_kr, ctx_sre, ctx_sim = [], [], [], [], [], []
    for l in range(DEPTH):
        p = layer_params(l)
        x, (ka, va, ckv_n, kr, s_re, s_im) = context_layer(x, adaln_modulation(c_ctx, p), p)
        ctx_k.append(ka)
        ctx_v.append(va)
        ctx_ckv.append(ckv_n)
        ctx_kr.append(kr)
        ctx_sre.append(s_re)
        ctx_sim.append(s_im)
    y_prompt = x

    n_lat = x_sample.shape[1]
    rope_a = axial_rope_tables(n_lat, HD_A)
    rope_c = axial_rope_tables(n_lat, ROPE_C)
    x = x_sample
    for l in range(DEPTH):
        p = layer_params(l)
        mod = [m[:, None, :] for m in adaln_modulation(c, p)]
        cache = (cache_attn_k[:, l], cache_attn_v[:, l], cache_mla_ckv[:, l], cache_mla_krope[:, l],
                 state_ssm_re[:, l], state_ssm_im[:, l])
        x = latent_layer(x, mod, cache, p, rope_a, rope_c)
    y_sample = x

    new_attn_k = jnp.stack(ctx_k, axis=1)
    new_attn_v = jnp.stack(ctx_v, axis=1)
    new_mla_ckv = jnp.stack(ctx_ckv, axis=1)
    new_mla_krope = jnp.stack(ctx_kr, axis=1)
    new_ssm_re = jnp.stack(ctx_sre, axis=1)
    new_ssm_im = jnp.stack(ctx_sim, axis=1)
    return (y_prompt, y_sample, new_attn_k, new_attn_v, new_mla_ckv, new_mla_krope, new_ssm_re, new_ssm_im)
```

```python
import functools
import math

import jax
import jax.numpy as jnp
from jax import lax
from jax.experimental import pallas as pl
from jax.experimental.pallas import tpu as pltpu

D_MODEL = 1024
BATCH = 32
SEQ = 256
DEPTH = 2
DEC_BATCH = 4
DEC_SEQ = 2048
PAST_LEN = 256
GRID_W = 64
ROPE_BASE = 10000.0
EPS = 1e-6
NEG_INF = -1e30
BLK = 128
HQ_A, HKV_A, HD_A = 8, 2, 64
G_A = HQ_A // HKV_A
WINDOW = 128
D_B, GS_B, P_B = 512, 16, 64
G_B = D_B // GS_B
H_C, Q_LORA, KV_LORA, NOPE_C, ROPE_C, V_C = 8, 256, 128, 64, 32, 64
QK_C = NOPE_C + ROPE_C
N_EXP, TOP_K, D_FF = 32, 4, 1024
SWIGLU_ALPHA, SWIGLU_LIMIT = 1.702, 7.0

N_CTX = BATCH * SEQ
N_LAT = DEC_BATCH * DEC_SEQ
T_ALL = N_CTX + N_LAT
N_GROUPS = 8

LANES = 128
SUBLANES = 8

C_GL = 0
C_QA = 3 * D_MODEL
C_UB = C_QA + HQ_A * LANES
C_CQ = C_UB + D_B
C_KA = C_CQ + Q_LORA
C_VA = C_KA + LANES
C_CKV = C_VA + LANES
C_KR = C_CKV + LANES
N_PROJ = C_KR + LANES

N_ASSIGN = T_ALL * TOP_K
N_BLOCKS = N_ASSIGN // BLK + N_EXP
BUF_LEN = N_BLOCKS * BLK

F32 = jnp.float32
BF16 = jnp.bfloat16
VMEM_LIMIT = 56 * 1024 * 1024


def _cp(sem, vmem=VMEM_LIMIT):
    return pltpu.CompilerParams(dimension_semantics=sem, vmem_limit_bytes=vmem)


def _group_of_tile(i, tm):
    n_ctx_tiles = N_CTX // tm
    per_batch = DEC_SEQ // tm
    return jnp.where(i < n_ctx_tiles, 0, 1 + (i - n_ctx_tiles) // per_batch)


def _sigmoid(x):
    return 1.0 / (1.0 + jnp.exp(-x))


def _adaln_kernel(c_ref, w_ref, b_ref, o_ref):
    c = c_ref[...]
    s = c * _sigmoid(c)
    o_ref[...] = jnp.dot(s.astype(BF16), w_ref[...].astype(BF16), preferred_element_type=F32) + b_ref[...]


def adaln(cvecs, w_ada, b_ada):
    tn = 1536
    out = pl.pallas_call(
        _adaln_kernel,
        out_shape=jax.ShapeDtypeStruct((DEPTH, N_GROUPS, 6 * D_MODEL), F32),
        grid=(DEPTH, 6 * D_MODEL // tn),
        in_specs=[
            pl.BlockSpec((N_GROUPS, D_MODEL), lambda l, j: (0, 0)),
            pl.BlockSpec((None, D_MODEL, tn), lambda l, j: (l, 0, j)),
            pl.BlockSpec((None, 1, tn), lambda l, j: (l, 0, j)),
        ],
        out_specs=pl.BlockSpec((None, N_GROUPS, tn), lambda l, j: (l, 0, j)),
        compiler_params=_cp(("arbitrary", "arbitrary")),
        name="adaln",
    )(cvecs, w_ada, b_ada.reshape(DEPTH, 1, 6 * D_MODEL))
    return out.reshape(DEPTH, N_GROUPS, 6, D_MODEL)


def _rms(x, g):
    ms = jnp.mean(x * x, axis=-1, keepdims=True)
    return x * lax.rsqrt(ms + EPS) * g


def _inproj_kernel(x_ref, g_ref, mod_ref, w_ref, o_ref):
    h = _rms(x_ref[...], g_ref[...])
    h = h * (1.0 + mod_ref[1:2, :]) + mod_ref[0:1, :]
    o_ref[...] = jnp.dot(h.astype(BF16), w_ref[...], preferred_element_type=F32)


def inproj(x, g, mod, w):
    tm, tn = 512, 1792
    return pl.pallas_call(
        _inproj_kernel,
        out_shape=jax.ShapeDtypeStruct((T_ALL, N_PROJ), F32),
        grid=(N_PROJ // tn, T_ALL // tm),
        in_specs=[
            pl.BlockSpec((tm, D_MODEL), lambda j, i: (i, 0)),
            pl.BlockSpec((1, D_MODEL), lambda j, i: (0, 0)),
            pl.BlockSpec((None, 6, D_MODEL), lambda j, i: (_group_of_tile(i, tm), 0, 0)),
            pl.BlockSpec((D_MODEL, tn), lambda j, i: (0, j)),
        ],
        out_specs=pl.BlockSpec((tm, tn), lambda j, i: (i, j)),
        compiler_params=_cp(("arbitrary", "arbitrary")),
        name="inproj",
    )(x, g.reshape(1, D_MODEL), mod, w)


def _rope(slab, cos, sin_lo, sin_hi, half):
    up = pltpu.roll(slab, LANES - half, axis=1)
    dn = pltpu.roll(slab, half, axis=1)
    return slab * cos + up * sin_lo + dn * sin_hi


def _slot_norm(slab, gain, n_real):
    ms = jnp.sum(slab * slab, axis=-1, keepdims=True) * (1.0 / n_real)
    return slab * lax.rsqrt(ms + EPS) * gain


def _mla_keys(ckvn, kr_blk, wukv_ref, gk, rope_c):
    kv = jnp.dot(ckvn.astype(BF16), wukv_ref[...], preferred_element_type=F32)
    ks = []
    for h in range(H_C):
        slab = kv[:, h * LANES:(h + 1) * LANES] + kr_blk
        slab = _slot_norm(slab, gk, QK_C)
        if rope_c is not None:
            slab = _rope(slab, *rope_c, ROPE_C // 4)
        ks.append(slab.astype(BF16))
    return jnp.concatenate(ks, axis=1), kv[:, H_C * LANES:].astype(BF16)


def _prep_kernel(qa_ref, cq_ref, ka_ref, ckv_ref, kr_ref, tab_ref,
                 gqa_ref, gka_ref, gcq_ref, gckv_ref, gqc_ref, gkc_ref, wuq_ref, wukv_ref,
                 qa_o, ka_o, qc_o, kc_o, vc_o, ckvn_o):
    tab = tab_ref[...]
    rope_a = (tab[:, 0:128], tab[:, 128:256], tab[:, 256:384])
    rope_c = (tab[:, 384:512], tab[:, 512:640], tab[:, 640:768])
    lane = lax.broadcasted_iota(jnp.int32, (1, LANES), 1)

    gqa = gqa_ref[...]
    qa = qa_ref[...]
    outs = []
    for h in range(HQ_A):
        slab = _slot_norm(qa[:, h * LANES:(h + 1) * LANES], gqa, HD_A)
        slab = _rope(slab, *rope_a, HD_A // 4) * (HD_A ** -0.5)
        outs.append(slab.astype(BF16))
    qa_o[...] = jnp.concatenate(outs, axis=1)

    ka = ka_ref[...]
    sq = ka * ka
    lo = lane < HD_A
    ms_lo = jnp.sum(jnp.where(lo, sq, 0.0), axis=-1, keepdims=True)
    ms_hi = jnp.sum(jnp.where(lo, 0.0, sq), axis=-1, keepdims=True)
    rs = jnp.where(lo, lax.rsqrt(ms_lo * (1.0 / HD_A) + EPS), lax.rsqrt(ms_hi * (1.0 / HD_A) + EPS))
    ka_o[...] = _rope(ka * rs * gka_ref[...], *rope_a, HD_A // 4)

    cqn = _rms(cq_ref[...], gcq_ref[...])
    q = jnp.dot(cqn.astype(BF16), wuq_ref[...], preferred_element_type=F32)
    gqc = gqc_ref[...]
    outs = []
    for h in range(H_C):
        slab = _slot_norm(q[:, h * LANES:(h + 1) * LANES], gqc, QK_C)
        slab = _rope(slab, *rope_c, ROPE_C // 4) * (QK_C ** -0.5)
        outs.append(slab.astype(BF16))
    qc_o[...] = jnp.concatenate(outs, axis=1)

    ckvn = _rms(ckv_ref[...], gckv_ref[...])
    ckvn_o[...] = ckvn
    kc, vc = _mla_keys(ckvn, kr_ref[...], wukv_ref, gkc_ref[...], rope_c)
    kc_o[...] = kc
    vc_o[...] = vc


def prep(proj, tab, gqa, gka, gcq, gckv, gqc, gkc, wuq, wukv):
    tm = 256
    n_ctx_tiles = N_CTX // tm
    per_batch = DEC_SEQ // tm

    def tab_map(i):
        return (jnp.where(i < n_ctx_tiles, 0, per_batch + (i - n_ctx_tiles) % per_batch), 0)

    def col(width, off):
        return pl.BlockSpec((tm, width), lambda i: (i, off // width))

    def full(shape):
        return pl.BlockSpec(shape, lambda i: (0,) * len(shape))

    def row_out(width):
        return pl.BlockSpec((tm, width), lambda i: (i, 0))

    return pl.pallas_call(
        _prep_kernel,
        out_shape=(
            jax.ShapeDtypeStruct((T_ALL, HQ_A * LANES), BF16),
            jax.ShapeDtypeStruct((T_ALL, LANES), F32),
            jax.ShapeDtypeStruct((T_ALL, H_C * LANES), BF16),
            jax.ShapeDtypeStruct((T_ALL, H_C * LANES), BF16),
            jax.ShapeDtypeStruct((T_ALL, H_C * V_C), BF16),
            jax.ShapeDtypeStruct((T_ALL, KV_LORA), F32),
        ),
        grid=(T_ALL // tm,),
        in_specs=[
            col(HQ_A * LANES, C_QA), col(Q_LORA, C_CQ), col(LANES, C_KA), col(LANES, C_CKV), col(LANES, C_KR),
            pl.BlockSpec((tm, 6 * LANES), tab_map),
            full((1, LANES)), full((1, LANES)), full((1, Q_LORA)), full((1, KV_LORA)),
            full((1, LANES)), full((1, LANES)),
            full((Q_LORA, H_C * LANES)), full((KV_LORA, H_C * LANES + H_C * V_C)),
        ],
        out_specs=(row_out(HQ_A * LANES), row_out(LANES), row_out(H_C * LANES), row_out(H_C * LANES),
                   row_out(H_C * V_C), row_out(KV_LORA)),
        compiler_params=_cp(("arbitrary",)),
        name="prep",
    )(proj, proj, proj, proj, proj, tab, gqa, gka, gcq, gckv, gqc, gkc, wuq, wukv)


def _cachekeys_kernel(ckv_ref, kr_ref, gkc_ref, wukv_ref, kc_o, vc_o):
    kc, vc = _mla_keys(ckv_ref[...], kr_ref[...], wukv_ref, gkc_ref[...], None)
    kc_o[...] = kc
    vc_o[...] = vc


def cache_keys(ckv, kr_blk, gkc, wukv):
    r = ckv.shape[0]
    tm = 256
    return pl.pallas_call(
        _cachekeys_kernel,
        out_shape=(jax.ShapeDtypeStruct((r, H_C * LANES), BF16), jax.ShapeDtypeStruct((r, H_C * V_C), BF16)),
        grid=(r // tm,),
        in_specs=[
            pl.BlockSpec((tm, LANES), lambda i: (i, 0)),
            pl.BlockSpec((tm, LANES), lambda i: (i, 0)),
            pl.BlockSpec((1, LANES), lambda i: (0, 0)),
            pl.BlockSpec((KV_LORA, H_C * LANES + H_C * V_C), lambda i: (0, 0)),
        ],
        out_specs=(pl.BlockSpec((tm, H_C * LANES), lambda i: (i, 0)), pl.BlockSpec((tm, H_C * V_C), lambda i: (i, 0))),
        compiler_params=_cp(("arbitrary",)),
        name="cache_keys",
    )(ckv, kr_blk, gkc, wukv)


def _attn_body(q_ref, segs, sink_ref, o_ref, *, n_heads, k_slot, v_slab, v_half, tq, band_qi=None):
    outs = []
    for h in range(n_heads):
        qh = q_ref[:, h * LANES:(h + 1) * LANES]
        scores = []
        for k_ref, _, off in segs:
            kh = k_ref[:, k_slot(h) * LANES:(k_slot(h) + 1) * LANES].astype(BF16)
            s = lax.dot_general(qh, kh, (((1,), (1,)), ((), ())), preferred_element_type=F32)
            if off is not None:
                tk = s.shape[1]
                blk = band_qi + off
                q_pos = band_qi * tq + lax.broadcasted_iota(jnp.int32, (tq, tk), 0)
                k_pos = blk * tk + lax.broadcasted_iota(jnp.int32, (tq, tk), 1)
                ok = (jnp.abs(k_pos - q_pos) <= WINDOW) & (blk >= 0) & (blk < DEC_SEQ // tk)
                s = jnp.where(ok, s, NEG_INF)
            scores.append(s)
        m = scores[0].max(axis=-1, keepdims=True)
        for s in scores[1:]:
            m = jnp.maximum(m, s.max(axis=-1, keepdims=True))
        if sink_ref is not None:
            sink = sink_ref[h:h + 1, 0:1]
            m = jnp.maximum(m, sink)
            denom = jnp.exp(sink - m)
        else:
            denom = jnp.zeros_like(m)
        acc = None
        for s, (_, v_ref, _) in zip(scores, segs):
            p = jnp.exp(s - m)
            denom = denom + p.sum(axis=-1, keepdims=True)
            vs = v_ref[:, v_slab(h) * LANES:(v_slab(h) + 1) * LANES].astype(BF16)
            pv = jnp.dot(p.astype(BF16), vs, preferred_element_type=F32)
            acc = pv if acc is None else acc + pv
        half = v_half(h)
        outs.append(acc[:, half * 64:(half + 1) * 64] * (1.0 / denom))
    o_ref[...] = jnp.concatenate(outs, axis=1).astype(o_ref.dtype)


_A_CFG = dict(n_heads=HQ_A, k_slot=lambda h: 0, v_slab=lambda h: 0, v_half=lambda h: h // G_A)
_C_CFG = dict(n_heads=H_C, k_slot=lambda h: h, v_slab=lambda h: h // 2, v_half=lambda h: h % 2)


def _attn_a_ctx_kernel(q_ref, k_ref, v_ref, sink_ref, o_ref):
    _attn_body(q_ref, [(k_ref, v_ref, None)], sink_ref, o_ref, tq=SEQ, **_A_CFG)


def _attn_a_lat_kernel(q_ref, k0, k1, k2, v0, v1, v2, kc_ref, vc_ref, sink_ref, o_ref):
    qi = pl.program_id(1)
    segs = [(k0, v0, -1), (k1, v1, 0), (k2, v2, 1), (kc_ref, vc_ref, None)]
    _attn_body(q_ref, segs, sink_ref, o_ref, tq=BLK, band_qi=qi, **_A_CFG)


def _attn_c_ctx_kernel(q_ref, k_ref, v_ref, o_ref):
    _attn_body(q_ref, [(k_ref, v_ref, None)], None, o_ref, tq=SEQ, **_C_CFG)


def _attn_c_lat_kernel(q_ref, k_ref, v_ref, kc_ref, vc_ref, o_ref):
    _attn_body(q_ref, [(k_ref, v_ref, None), (kc_ref, vc_ref, None)], None, o_ref, tq=256, **_C_CFG)


def attn_a_ctx(qa, ka, proj, sink):
    return pl.pallas_call(
        _attn_a_ctx_kernel,
        out_shape=jax.ShapeDtypeStruct((N_CTX, HQ_A * HD_A), BF16),
        grid=(BATCH,),
        in_specs=[
            pl.BlockSpec((SEQ, HQ_A * LANES), lambda b: (b, 0)),
            pl.BlockSpec((SEQ, LANES), lambda b: (b, 0)),
            pl.BlockSpec((SEQ, LANES), lambda b: (b, C_VA // LANES)),
            pl.BlockSpec((SUBLANES, LANES), lambda b: (0, 0)),
        ],
        out_specs=pl.BlockSpec((SEQ, HQ_A * HD_A), lambda b: (b, 0)),
        compiler_params=_cp(("arbitrary",)),
        name="attn_a_ctx",
    )(qa, ka, proj, sink)


def attn_a_lat(qa, ka, proj, k_cache, v_cache, sink):
    nb = DEC_SEQ // BLK
    base = N_CTX // BLK

    def band(off, colblk):
        return pl.BlockSpec((BLK, LANES), lambda b, i: (base + b * nb + jnp.clip(i + off, 0, nb - 1), colblk))

    return pl.pallas_call(
        _attn_a_lat_kernel,
        out_shape=jax.ShapeDtypeStruct((N_LAT, HQ_A * HD_A), BF16),
        grid=(DEC_BATCH, nb),
        in_specs=[
            pl.BlockSpec((BLK, HQ_A * LANES), lambda b, i: (base + b * nb + i, 0)),
            band(-1, 0), band(0, 0), band(1, 0),
            band(-1, C_VA // LANES), band(0, C_VA // LANES), band(1, C_VA // LANES),
            pl.BlockSpec((None, PAST_LEN, LANES), lambda b, i: (b, 0, 0)),
            pl.BlockSpec((None, PAST_LEN, LANES), lambda b, i: (b, 0, 0)),
            pl.BlockSpec((SUBLANES, LANES), lambda b, i: (0, 0)),
        ],
        out_specs=pl.BlockSpec((BLK, HQ_A * HD_A), lambda b, i: (b * nb + i, 0)),
        compiler_params=_cp(("arbitrary", "arbitrary")),
        name="attn_a_lat",
    )(qa, ka, ka, ka, proj, proj, proj, k_cache, v_cache, sink)


def attn_c_ctx(qc, kc, vc):
    return pl.pallas_call(
        _attn_c_ctx_kernel,
        out_shape=jax.ShapeDtypeStruct((N_CTX, H_C * V_C), BF16),
        grid=(BATCH,),
        in_specs=[
            pl.BlockSpec((SEQ, H_C * LANES), lambda b: (b, 0)),
            pl.BlockSpec((SEQ, H_C * LANES), lambda b: (b, 0)),
            pl.BlockSpec((SEQ, H_C * V_C), lambda b: (b, 0)),
        ],
        out_specs=pl.BlockSpec((SEQ, H_C * V_C), lambda b: (b, 0)),
        compiler_params=_cp(("arbitrary",)),
        name="attn_c_ctx",
    )(qc, kc, vc)


def attn_c_lat(qc, kc, vc, kc_cache, vc_cache):
    tq = 256
    nq = DEC_SEQ // tq
    qbase = N_CTX // tq
    kbase = N_CTX // DEC_SEQ
    return pl.pallas_call(
        _attn_c_lat_kernel,
        out_shape=jax.ShapeDtypeStruct((N_LAT, H_C * V_C), BF16),
        grid=(DEC_BATCH, nq),
        in_specs=[
            pl.BlockSpec((tq, H_C * LANES), lambda b, i: (qbase + b * nq + i, 0)),
            pl.BlockSpec((DEC_SEQ, H_C * LANES), lambda b, i: (kbase + b, 0)),
            pl.BlockSpec((DEC_SEQ, H_C * V_C), lambda b, i: (kbase + b, 0)),
            pl.BlockSpec((PAST_LEN, H_C * LANES), lambda b, i: (b, 0)),
            pl.BlockSpec((PAST_LEN, H_C * V_C), lambda b, i: (b, 0)),
        ],
        out_specs=pl.BlockSpec((tq, H_C * V_C), lambda b, i: (b * nq + i, 0)),
        compiler_params=_cp(("arbitrary", "arbitrary")),
        name="attn_c_lat",
    )(qc, kc, vc, kc_cache, vc_cache)


S5_TC = 256
S5_CH = 512
S5_UNROLL = 8


def _s5_kernel(u_ref, wb_ref, wc_ref, are_ref, aim_ref, h0re_ref, h0im_ref,
               y_ref, fre_ref, fim_ref, bre, bim, sre, sim, *, mixed):
    k = pl.program_id(2)

    @pl.when(k == 0)
    def _():
        sre[...] = h0re_ref[...]
        sim[...] = h0im_ref[...]

    rows = S5_TC * SUBLANES
    u2 = u_ref[...].reshape(rows, LANES).astype(BF16)
    if mixed:
        fwd_row = (lax.broadcasted_iota(jnp.int32, (rows, 1), 0) % SUBLANES) < (SUBLANES // 2)

    def proj_b(lo, hi):
        r = jnp.dot(u2, wb_ref[0, :, lo:hi], preferred_element_type=F32)
        if mixed:
            r = jnp.where(fwd_row, r, jnp.dot(u2, wb_ref[1, :, lo:hi], preferred_element_type=F32))
        return r.reshape(S5_TC, SUBLANES, S5_CH)

    bre[...] = proj_b(0, S5_CH)
    bim[...] = proj_b(S5_CH, 2 * S5_CH)

    a_re = are_ref[...]
    a_im = aim_ref[...]

    def outer(i, carry):
        sr, si = carry
        for j in range(S5_UNROLL):
            t = i * S5_UNROLL + j
            nr = a_re * sr - a_im * si + bre[t]
            ni = a_re * si + a_im * sr + bim[t]
            bre[t] = nr
            bim[t] = ni
            sr, si = nr, ni
        return sr, si

    sr, si = lax.fori_loop(0, S5_TC // S5_UNROLL, outer, (sre[...], sim[...]))
    sre[...] = sr
    sim[...] = si
    fre_ref[...] = sr
    fim_ref[...] = si

    xr = bre[...].reshape(rows, S5_CH).astype(BF16)
    xi = bim[...].reshape(rows, S5_CH).astype(BF16)

    def proj_c(d):
        return (jnp.dot(xr, wc_ref[d, 0:S5_CH, :], preferred_element_type=F32)
                + jnp.dot(xi, wc_ref[d, S5_CH:, :], preferred_element_type=F32))

    y = proj_c(0)
    if mixed:
        y = jnp.where(fwd_row, y, proj_c(1))
    y_ref[...] = y.reshape(S5_TC, SUBLANES, LANES)


def s5_scan(seqs, wb, wc, a_re, a_im, h0_re, h0_im, *, mixed):
    length, ns, _ = seqs.shape
    nsb = ns // SUBLANES
    nj = D_B // LANES
    nk = length // S5_TC
    nd = 2 if mixed else 1
    per_dir = nsb // 2 if not mixed else 1

    def dmap(sb):
        return 0 if mixed else sb // per_dir

    kern = functools.partial(_s5_kernel, mixed=mixed)
    return pl.pallas_call(
        kern,
        out_shape=(
            jax.ShapeDtypeStruct((length, ns, D_B), F32),
            jax.ShapeDtypeStruct((ns, G_B * P_B), F32),
            jax.ShapeDtypeStruct((ns, G_B * P_B), F32),
        ),
        grid=(nsb, nj, nk),
        in_specs=[
            pl.BlockSpec((S5_TC, SUBLANES, LANES), lambda sb, j, k: (k, sb, j)),
            pl.BlockSpec((nd, None, LANES, 2 * S5_CH), lambda sb, j, k: (dmap(sb), j, 0, 0)),
            pl.BlockSpec((nd, None, 2 * S5_CH, LANES), lambda sb, j, k: (dmap(sb), j, 0, 0)),
            pl.BlockSpec((None, SUBLANES, S5_CH), lambda sb, j, k: (dmap(sb), 0, j)),
            pl.BlockSpec((None, SUBLANES, S5_CH), lambda sb, j, k: (dmap(sb), 0, j)),
            pl.BlockSpec((SUBLANES, S5_CH), lambda sb, j, k: (sb, j)),
            pl.BlockSpec((SUBLANES, S5_CH), lambda sb, j, k: (sb, j)),
        ],
        out_specs=(
            pl.BlockSpec((S5_TC, SUBLANES, LANES), lambda sb, j, k: (k, sb, j)),
            pl.BlockSpec((SUBLANES, S5_CH), lambda sb, j, k: (sb, j)),
            pl.BlockSpec((SUBLANES, S5_CH), lambda sb, j, k: (sb, j)),
        ),
        scratch_shapes=[
            pltpu.VMEM((S5_TC, SUBLANES, S5_CH), F32), pltpu.VMEM((S5_TC, SUBLANES, S5_CH), F32),
            pltpu.VMEM((SUBLANES, S5_CH), F32), pltpu.VMEM((SUBLANES, S5_CH), F32),
        ],
        compiler_params=_cp(("arbitrary", "arbitrary", "arbitrary")),
        name="s5_scan_mixed" if mixed else "s5_scan",
    )(seqs, wb, wc, a_re, a_im, h0_re, h0_im)


def _s5fin_kernel(u_ref, y_ref, d_ref, w_ref, b_ref, o_ref):
    y = d_ref[...] * u_ref[...] + y_ref[...]
    y = 0.5 * y * (1.0 + jnp.tanh(math.sqrt(2.0 / math.pi) * (y + 0.044715 * (y * y * y))))
    z = jnp.dot(y.astype(BF16), w_ref[...], preferred_element_type=F32) + b_ref[...]
    o_ref[...] = (y * _sigmoid(z)).astype(o_ref.dtype)


def s5_finish(proj, y, d, w_glu, b_glu):
    tm = 512
    return pl.pallas_call(
        _s5fin_kernel,
        out_shape=jax.ShapeDtypeStruct((T_ALL, D_B), BF16),
        grid=(T_ALL // tm,),
        in_specs=[
            pl.BlockSpec((tm, D_B), lambda i: (i, C_UB // D_B)),
            pl.BlockSpec((tm, D_B), lambda i: (i, 0)),
            pl.BlockSpec((1, D_B), lambda i: (0, 0)),
            pl.BlockSpec((D_B, D_B), lambda i: (0, 0)),
            pl.BlockSpec((1, D_B), lambda i: (0, 0)),
        ],
        out_specs=pl.BlockSpec((tm, D_B), lambda i: (i, 0)),
        compiler_params=_cp(("arbitrary",)),
        name="s5_finish",
    )(proj, y, d, w_glu, b_glu)


def _merge_kernel(x_ref, gl_ref, oa_ref, ob_ref, oc_ref, mod_ref, g_ref, wa_ref, wb_ref, wc_ref, wo_ref,
                  x1_ref, h2_ref):
    m = None
    for br, (o_ref, w_ref) in enumerate(((oa_ref, wa_ref), (ob_ref, wb_ref), (oc_ref, wc_ref))):
        gate = _sigmoid(gl_ref[:, br * D_MODEL:(br + 1) * D_MODEL])
        t = gate * jnp.dot(o_ref[...], w_ref[...], preferred_element_type=F32)
        m = t if m is None else m + t
    x1 = x_ref[...] + mod_ref[2:3, :] * jnp.dot(m.astype(BF16), wo_ref[...], preferred_element_type=F32)
    x1_ref[...] = x1
    h2 = _rms(x1, g_ref[...])
    h2_ref[...] = h2 * (1.0 + mod_ref[4:5, :]) + mod_ref[3:4, :]


def merge(x, proj, oa, ob, oc, mod, g, wa, wb, wc, wo):
    tm = 256
    full = lambda shape: pl.BlockSpec(shape, lambda i: (0,) * len(shape))
    return pl.pallas_call(
        _merge_kernel,
        out_shape=(jax.ShapeDtypeStruct((T_ALL, D_MODEL), F32), jax.ShapeDtypeStruct((T_ALL, D_MODEL), F32)),
        grid=(T_ALL // tm,),
        in_specs=[
            pl.BlockSpec((tm, D_MODEL), lambda i: (i, 0)),
            pl.BlockSpec((tm, 3 * D_MODEL), lambda i: (i, 0)),
            pl.BlockSpec((tm, 512), lambda i: (i, 0)),
            pl.BlockSpec((tm, 512), lambda i: (i, 0)),
            pl.BlockSpec((tm, 512), lambda i: (i, 0)),
            pl.BlockSpec((None, 6, D_MODEL), lambda i: (_group_of_tile(i, tm), 0, 0)),
            full((1, D_MODEL)),
            full((512, D_MODEL)), full((512, D_MODEL)), full((512, D_MODEL)), full((D_MODEL, D_MODEL)),
        ],
        out_specs=(pl.BlockSpec((tm, D_MODEL), lambda i: (i, 0)), pl.BlockSpec((tm, D_MODEL), lambda i: (i, 0))),
        compiler_params=_cp(("arbitrary",)),
        name="merge",
    )(x, proj, oa, ob, oc, mod, g, wa, wb, wc, wo)


ROUTE_TM = 256


def _router_kernel(h_ref, w_ref, b_ref, idx_ref, rank_ref, wgt_ref, cnt_ref, cnt_acc):
    i = pl.program_id(0)

    @pl.when(i == 0)
    def _():
        cnt_acc[...] = jnp.zeros_like(cnt_acc)

    logits = jnp.dot(h_ref[...], w_ref[...], preferred_element_type=F32,
                     precision=lax.Precision.HIGHEST) + b_ref[...]
    lane_i = lax.broadcasted_iota(jnp.int32, (ROUTE_TM, LANES), 1)
    lane = lane_i.astype(F32)
    r_i = lax.broadcasted_iota(jnp.int32, (ROUTE_TM, ROUTE_TM), 0)
    c_i = lax.broadcasted_iota(jnp.int32, (ROUTE_TM, ROUTE_TM), 1)
    earlier = jnp.where(c_i < r_i, 1.0, 0.0).astype(BF16)

    cnt = cnt_acc[...]
    idx_out = jnp.zeros((ROUTE_TM, LANES), F32)
    rank_out = jnp.zeros((ROUTE_TM, LANES), F32)
    val_out = jnp.zeros((ROUTE_TM, LANES), F32)
    v0 = None
    esum = None
    for k in range(TOP_K):
        m = logits.max(axis=-1, keepdims=True)
        sel = jnp.min(jnp.where(logits == m, lane, float(LANES)), axis=-1, keepdims=True)
        hit = lane == sel
        logits = jnp.where(hit, -jnp.inf, logits)
        onehot = jnp.where(hit, 1.0, 0.0)
        within = jnp.dot(earlier, onehot.astype(BF16), preferred_element_type=F32)
        rank = jnp.sum(onehot * (within + cnt), axis=-1, keepdims=True)
        cnt = cnt + jnp.sum(onehot, axis=0, keepdims=True)
        if k == 0:
            v0 = m
        e = jnp.exp(m - v0)
        esum = e if esum is None else esum + e
        idx_out = jnp.where(lane_i == k, sel, idx_out)
        rank_out = jnp.where(lane_i == k, rank, rank_out)
        val_out = jnp.where(lane_i == k, e, val_out)
    cnt_acc[...] = cnt
    cnt_ref[...] = cnt
    idx_ref[...] = idx_out.astype(jnp.int32)
    rank_ref[...] = rank_out.astype(jnp.int32)
    wgt_ref[...] = val_out * (1.0 / esum)


def router(h2, w_router, b_router):
    tile = lambda: pl.BlockSpec((ROUTE_TM, LANES), lambda i: (i, 0))
    return pl.pallas_call(
        _router_kernel,
        out_shape=(
            jax.ShapeDtypeStruct((T_ALL, LANES), jnp.int32),
            jax.ShapeDtypeStruct((T_ALL, LANES), jnp.int32),
            jax.ShapeDtypeStruct((T_ALL, LANES), F32),
            jax.ShapeDtypeStruct((1, LANES), F32),
        ),
        grid=(T_ALL // ROUTE_TM,),
        in_specs=[
            pl.BlockSpec((ROUTE_TM, D_MODEL), lambda i: (i, 0)),
            pl.BlockSpec((D_MODEL, LANES), lambda i: (0, 0)),
            pl.BlockSpec((1, LANES), lambda i: (0, 0)),
        ],
        out_specs=(tile(), tile(), tile(), pl.BlockSpec((1, LANES), lambda i: (0, 0))),
        scratch_shapes=[pltpu.VMEM((1, LANES), F32)],
        compiler_params=_cp(("arbitrary",)),
        name="router",
    )(h2, w_router, b_router)


DISP_TM = 256


def _dispatch_kernel(dest_ref, h_hbm, xs_in, xs_hbm, sem):
    del xs_in
    i = pl.program_id(0)
    base = i * DISP_TM

    def issue(t, c):
        tok = base + t
        for k in range(TOP_K):
            d = dest_ref[tok * TOP_K + k]
            pltpu.make_async_copy(h_hbm.at[pl.ds(tok, 1), :], xs_hbm.at[pl.ds(d, 1), :], sem).start()
        return c

    lax.fori_loop(0, DISP_TM, issue, 0)
    n = DISP_TM * TOP_K
    pltpu.make_async_copy(h_hbm.at[pl.ds(0, n), :], xs_hbm.at[pl.ds(0, n), :], sem).wait()


def dispatch(dest, h2, xs_zero):
    return pl.pallas_call(
        _dispatch_kernel,
        out_shape=jax.ShapeDtypeStruct((BUF_LEN, D_MODEL), F32),
        grid_spec=pltpu.PrefetchScalarGridSpec(
            num_scalar_prefetch=1,
            grid=(T_ALL // DISP_TM,),
            in_specs=[pl.BlockSpec(memory_space=pl.ANY), pl.BlockSpec(memory_space=pl.ANY)],
            out_specs=pl.BlockSpec(memory_space=pl.ANY),
            scratch_shapes=[pltpu.SemaphoreType.DMA],
        ),
        input_output_aliases={2: 0},
        compiler_params=_cp(("arbitrary",)),
        name="dispatch",
    )(dest, h2, xs_zero)


def _expert_kernel(be_ref, x_ref, wg_ref, wl_ref, bg_ref, bl_ref, wd_ref, bd_ref, y_ref):
    del be_ref
    x = x_ref[...].astype(BF16)
    hg = jnp.dot(x, wg_ref[...], preferred_element_type=F32) + bg_ref[...]
    hl = jnp.dot(x, wl_ref[...], preferred_element_type=F32) + bl_ref[...]
    glu = jnp.minimum(hg, SWIGLU_LIMIT)
    lin = jnp.clip(hl, -SWIGLU_LIMIT, SWIGLU_LIMIT)
    act = glu * _sigmoid(SWIGLU_ALPHA * glu) * (lin + 1.0)
    y_ref[...] = jnp.dot(act.astype(BF16), wd_ref[...], preferred_element_type=F32) + bd_ref[...]


def experts(block_e, xs, wg, wl, bg, bl, wd, bd):
    wspec = lambda: pl.BlockSpec((None, D_MODEL, D_FF), lambda i, be: (be[i], 0, 0))
    bspec = lambda: pl.BlockSpec((None, 1, D_FF), lambda i, be: (be[i], 0, 0))
    return pl.pallas_call(
        _expert_kernel,
        out_shape=jax.ShapeDtypeStruct((BUF_LEN, D_MODEL), F32),
        grid_spec=pltpu.PrefetchScalarGridSpec(
            num_scalar_prefetch=1,
            grid=(N_BLOCKS,),
            in_specs=[
                pl.BlockSpec((BLK, D_MODEL), lambda i, be: (i, 0)),
                wspec(), wspec(), bspec(), bspec(), wspec(), bspec(),
            ],
            out_specs=pl.BlockSpec((BLK, D_MODEL), lambda i, be: (i, 0)),
        ),
        compiler_params=_cp(("arbitrary",)),
        name="experts",
    )(block_e, xs, wg, wl, bg, bl, wd, bd)


COMB_TM = 128


def _combine_kernel(dest_ref, ys_hbm, x_ref, w_ref, mod_ref, o_ref, buf, sem):
    i = pl.program_id(0)
    base = i * COMB_TM

    def issue(t, c):
        tok = base + t
        for k in range(TOP_K):
            d = dest_ref[tok * TOP_K + k]
            pltpu.make_async_copy(ys_hbm.at[pl.ds(d, 1), :], buf.at[pl.ds(k * COMB_TM + t, 1), :], sem).start()
        return c

    lax.fori_loop(0, COMB_TM, issue, 0)
    n = COMB_TM * TOP_K
    pltpu.make_async_copy(ys_hbm.at[pl.ds(0, n), :], buf, sem).wait()
    w = w_ref[...]
    acc = None
    for k in range(TOP_K):
        t = w[:, k:k + 1] * buf[k * COMB_TM:(k + 1) * COMB_TM, :]
        acc = t if acc is None else acc + t
    o_ref[...] = x_ref[...] + mod_ref[5:6, :] * acc


def combine(dest, ys, x1, wgt, mod):
    return pl.pallas_call(
        _combine_kernel,
        out_shape=jax.ShapeDtypeStruct((T_ALL, D_MODEL), F32),
        grid_spec=pltpu.PrefetchScalarGridSpec(
            num_scalar_prefetch=1,
            grid=(T_ALL // COMB_TM,),
            in_specs=[
                pl.BlockSpec(memory_space=pl.ANY),
                pl.BlockSpec((COMB_TM, D_MODEL), lambda i, d: (i, 0)),
                pl.BlockSpec((COMB_TM, LANES), lambda i, d: (i, 0)),
                pl.BlockSpec((None, 6, D_MODEL), lambda i, d: (_group_of_tile(i, COMB_TM), 0, 0)),
            ],
            out_specs=pl.BlockSpec((COMB_TM, D_MODEL), lambda i, d: (i, 0)),
            scratch_shapes=[pltpu.VMEM((COMB_TM * TOP_K, D_MODEL), F32), pltpu.SemaphoreType.DMA],
        ),
        compiler_params=_cp(("arbitrary",)),
        name="combine",
    )(dest, ys, x1, wgt, mod)


def _rope_table():
    pos = jnp.arange(DEC_SEQ)
    row = (pos // GRID_W).astype(F32)[:, None]
    col = (pos % GRID_W).astype(F32)[:, None]

    def parts(rot_dim):
        nf = rot_dim // 4
        inv = ROPE_BASE ** (-jnp.arange(nf, dtype=F32) / nf)
        cr, sr, cc, sc = jnp.cos(row * inv), jnp.sin(row * inv), jnp.cos(col * inv), jnp.sin(col * inv)
        z = jnp.zeros_like(sr)
        cos = jnp.concatenate([cr, cr, cc, cc], axis=1)
        lo = jnp.concatenate([-sr, z, -sc, z], axis=1)
        hi = jnp.concatenate([z, sr, z, sc], axis=1)
        return cos, lo, hi

    ca, la, ha = (jnp.tile(t, (1, 2)) for t in parts(HD_A))
    cc, lc, hc = parts(ROPE_C)
    pad = lambda t, fill: jnp.pad(t, ((0, 0), (NOPE_C, LANES - QK_C)), constant_values=fill)
    lat = jnp.concatenate([ca, la, ha, pad(cc, 1.0), pad(lc, 0.0), pad(hc, 0.0)], axis=1)
    ones, zeros = jnp.ones((DEC_SEQ, LANES), F32), jnp.zeros((DEC_SEQ, LANES), F32)
    ident = jnp.concatenate([ones, zeros, zeros, ones, zeros, zeros], axis=1)
    return jnp.concatenate([ident, lat], axis=0)


def _arrange_w_in(w_in):
    o = 0
    parts = {}
    for name, n in (("qa", HQ_A * HD_A), ("ka", HKV_A * HD_A), ("va", HKV_A * HD_A), ("ub", D_B), ("cq", Q_LORA),
                    ("ckv", KV_LORA), ("kr", ROPE_C), ("gl", 3 * D_MODEL)):
        parts[name] = w_in[..., o:o + n]
        o += n
    qa = parts["qa"].reshape(DEPTH, D_MODEL, HKV_A, G_A, 1, HD_A)
    eye = jnp.eye(HKV_A, dtype=F32).reshape(1, 1, HKV_A, 1, HKV_A, 1)
    qa_slots = (qa * eye).reshape(DEPTH, D_MODEL, HQ_A * LANES)
    kr = jnp.pad(parts["kr"], ((0, 0), (0, 0), (NOPE_C, LANES - QK_C)))
    w = jnp.concatenate([parts["gl"], qa_slots, parts["ub"], parts["cq"], parts["ka"], parts["va"], parts["ckv"], kr],
                        axis=-1)
    return w.astype(BF16)


def _s5_params(lam_re, lam_im, log_dt, b_re, b_im, c_re, c_im):
    dt = jnp.exp(log_dt)[..., None]
    decay = jnp.exp(lam_re * dt)
    ab_re, ab_im = decay * jnp.cos(lam_im * dt), decay * jnp.sin(lam_im * dt)
    den = lam_re * lam_re + lam_im * lam_im
    f_re = ((ab_re - 1) * lam_re + ab_im * lam_im) / den
    f_im = (ab_im * lam_re - (ab_re - 1) * lam_im) / den
    bb_re = f_re[..., None] * b_re - f_im[..., None] * b_im
    bb_im = f_re[..., None] * b_im + f_im[..., None] * b_re
    nj, gpb = D_B // LANES, LANES // GS_B
    eye = jnp.eye(gpb, dtype=F32)

    def blockdiag_b(bb):
        t = bb.transpose(0, 1, 3, 2).reshape(2, nj, gpb, GS_B, P_B)
        return (t[:, :, :, :, None, :] * eye[None, None, :, None, :, None]).reshape(2, nj, LANES, gpb * P_B)

    def blockdiag_c(cc):
        t = cc.transpose(0, 1, 3, 2).reshape(2, nj, gpb, P_B, GS_B)
        return (t[:, :, :, :, None, :] * eye[None, None, :, None, :, None]).reshape(2, nj, gpb * P_B, LANES)

    wb = jnp.concatenate([blockdiag_b(bb_re), blockdiag_b(bb_im)], axis=-1).astype(BF16)
    wc = jnp.concatenate([blockdiag_c(c_re), -blockdiag_c(c_im)], axis=-2).astype(BF16)
    return wb, wc, ab_re.reshape(2, G_B * P_B), ab_im.reshape(2, G_B * P_B)


def _to_time_major(u):
    f = u.transpose(1, 0, 2)
    return jnp.concatenate([f, f[::-1]], axis=1)


def _from_time_major(y, b):
    return (y[:, :b] + y[::-1, b:]).transpose(1, 0, 2)


def kernel(x_prompt, x_sample, c, cache_attn_k, cache_attn_v, cache_mla_ckv, cache_mla_krope, state_ssm_re, state_ssm_im, c_ctx, w_ada, b_ada, norm_mix_g, norm_ffn_g, w_in, q_norm_a, k_norm_a, sink_a, q_a_norm_c, kv_a_norm_c, w_uq_c, w_ukv_c, q_norm_c, k_norm_c, ssm_lam_re, ssm_lam_im, ssm_log_dt, ssm_b_re, ssm_b_im, ssm_c_re, ssm_c_im, ssm_d, w_glu, b_glu, w_br_a, w_br_b, w_br_c, w_out, w_router, b_router, w_up, b_up, w_down, b_down):
    x = jnp.concatenate([x_prompt.reshape(N_CTX, D_MODEL), x_sample.reshape(N_LAT, D_MODEL)], axis=0)
    cvecs = jnp.concatenate([c_ctx[None], c, jnp.zeros((N_GROUPS - 1 - DEC_BATCH, D_MODEL), F32)], axis=0)
    mods = adaln(cvecs, w_ada, b_ada)

    tab = _rope_table()
    w_in_r = _arrange_w_in(w_in)
    pad_slot = lambda g: jnp.pad(g, ((0, 0), (0, LANES - QK_C))).reshape(DEPTH, 1, LANES)
    gqa = jnp.tile(q_norm_a, (1, 2)).reshape(DEPTH, 1, LANES)
    gka = jnp.tile(k_norm_a, (1, 2)).reshape(DEPTH, 1, LANES)
    gqc, gkc = pad_slot(q_norm_c), pad_slot(k_norm_c)
    wuq = jnp.pad(w_uq_c.reshape(DEPTH, Q_LORA, H_C, QK_C), ((0, 0), (0, 0), (0, 0), (0, LANES - QK_C)))
    wuq = wuq.reshape(DEPTH, Q_LORA, H_C * LANES).astype(BF16)
    wukv4 = w_ukv_c.reshape(DEPTH, KV_LORA, H_C, NOPE_C + V_C)
    wuk = jnp.pad(wukv4[..., :NOPE_C], ((0, 0), (0, 0), (0, 0), (0, LANES - NOPE_C))).reshape(DEPTH, KV_LORA, H_C * LANES)
    wuv = wukv4[..., NOPE_C:].reshape(DEPTH, KV_LORA, H_C * V_C)
    wukv = jnp.concatenate([wuk, wuv], axis=-1).astype(BF16)
    sink = jnp.broadcast_to(sink_a[:, :, None], (DEPTH, HQ_A, LANES))
    w_router_p = jnp.pad(w_router, ((0, 0), (0, 0), (0, LANES - N_EXP)))
    b_router_p = jnp.pad(b_router, ((0, 0), (0, LANES - N_EXP)), constant_values=-jnp.inf).reshape(DEPTH, 1, LANES)
    w_up_g = w_up[..., 0::2].astype(BF16)
    w_up_l = w_up[..., 1::2].astype(BF16)
    b_up_g = b_up[..., 0::2].reshape(DEPTH, N_EXP, 1, D_FF)
    b_up_l = b_up[..., 1::2].reshape(DEPTH, N_EXP, 1, D_FF)
    w_down_b = w_down.astype(BF16)
    b_down_r = b_down.reshape(DEPTH, N_EXP, 1, D_MODEL)
    kr_cache = jnp.pad(cache_mla_krope, ((0, 0), (0, 0), (0, 0), (NOPE_C, LANES - QK_C)))
    zeros_state = jnp.zeros((2 * BATCH, G_B * P_B), F32)

    new_k, new_v, new_ckv, new_kr, new_sre, new_sim = [], [], [], [], [], []
    for l in range(DEPTH):
        mod = mods[l]
        proj = inproj(x, norm_mix_g[l], mod, w_in_r[l])
        qa, ka, qc, kc, vc, ckvn = prep(proj, tab, gqa[l], gka[l], q_a_norm_c[l].reshape(1, Q_LORA),
                                         kv_a_norm_c[l].reshape(1, KV_LORA), gqc[l], gkc[l], wuq[l], wukv[l])
        kc_cache, vc_cache = cache_keys(cache_mla_ckv[:, l].reshape(DEC_BATCH * PAST_LEN, KV_LORA),
                                        kr_cache[:, l].reshape(DEC_BATCH * PAST_LEN, LANES), gkc[l], wukv[l])
        oa_ctx = attn_a_ctx(qa, ka, proj, sink[l])
        oa_lat = attn_a_lat(qa, ka, proj, cache_attn_k[:, l].reshape(DEC_BATCH, PAST_LEN, LANES),
                            cache_attn_v[:, l].reshape(DEC_BATCH, PAST_LEN, LANES), sink[l])
        oa = jnp.concatenate([oa_ctx, oa_lat], axis=0)
        oc = jnp.concatenate([attn_c_ctx(qc, kc, vc), attn_c_lat(qc, kc, vc, kc_cache, vc_cache)], axis=0)
        wb, wc, a_re, a_im = _s5_params(ssm_lam_re[l], ssm_lam_im[l], ssm_log_dt[l], ssm_b_re[l], ssm_b_im[l],
                                        ssm_c_re[l], ssm_c_im[l])
        ub = proj[:, C_UB:C_UB + D_B]
        a8 = lambda a: jnp.broadcast_to(a[:, None, :], (2, SUBLANES, G_B * P_B))
        a_mixed = lambda a: jnp.repeat(a, SUBLANES // 2, axis=0)[None]
        y_ctx, f_re, f_im = s5_scan(_to_time_major(ub[:N_CTX].reshape(BATCH, SEQ, D_B)), wb, wc, a8(a_re), a8(a_im),
                                    zeros_state, zeros_state, mixed=False)
        h0 = lambda s: s[:, l].transpose(1, 0, 2, 3).reshape(2 * DEC_BATCH, G_B * P_B)
        y_lat, _, _ = s5_scan(_to_time_major(ub[N_CTX:].reshape(DEC_BATCH, DEC_SEQ, D_B)), wb, wc,
                              a_mixed(a_re), a_mixed(a_im), h0(state_ssm_re), h0(state_ssm_im), mixed=True)
        y = jnp.concatenate([_from_time_major(y_ctx, BATCH).reshape(N_CTX, D_B),
                             _from_time_major(y_lat, DEC_BATCH).reshape(N_LAT, D_B)], axis=0)
        ob = s5_finish(proj, y, ssm_d[l].reshape(1, D_B), w_glu[l].astype(BF16), b_glu[l].reshape(1, D_B))
        x1, h2 = merge(x, proj, oa, ob, oc, mod, norm_ffn_g[l].reshape(1, D_MODEL), w_br_a[l].astype(BF16),
                       w_br_b[l].astype(BF16), w_br_c[l].astype(BF16), w_out[l].astype(BF16))
        idx, rank, wgt, cnt = router(h2, w_router_p[l], b_router_p[l])
        counts = cnt[0, :N_EXP].astype(jnp.int32)
        padded = (counts + BLK - 1) // BLK * BLK
        pad_end = jnp.cumsum(padded)
        pad_start = pad_end - padded
        dest = (pad_start[idx[:, :TOP_K]] + rank[:, :TOP_K]).reshape(-1).astype(jnp.int32)
        block_e = jnp.minimum(jnp.searchsorted(pad_end, jnp.arange(N_BLOCKS) * BLK, side="right"), N_EXP - 1)
        xs = dispatch(dest, h2, jnp.zeros((BUF_LEN, D_MODEL), F32))
        ys = experts(block_e.astype(jnp.int32), xs, w_up_g[l], w_up_l[l], b_up_g[l], b_up_l[l], w_down_b[l],
                     b_down_r[l])
        x = combine(dest, ys, x1, wgt, mod)

        new_k.append(ka[:N_CTX].reshape(BATCH, SEQ, HKV_A, HD_A))
        new_v.append(proj[:N_CTX, C_VA:C_VA + LANES].reshape(BATCH, SEQ, HKV_A, HD_A))
        new_ckv.append(ckvn[:N_CTX].reshape(BATCH, SEQ, KV_LORA))
        new_kr.append(proj[:N_CTX, C_KR + NOPE_C:C_KR + QK_C].reshape(BATCH, SEQ, ROPE_C))
        new_sre.append(f_re.reshape(2, BATCH, G_B, P_B).transpose(1, 0, 2, 3))
        new_sim.append(f_im.reshape(2, BATCH, G_B, P_B).transpose(1, 0, 2, 3))

    y_prompt = x[:N_CTX].reshape(BATCH, SEQ, D_MODEL)
    y_sample = x[N_CTX:].reshape(DEC_BATCH, DEC_SEQ, D_MODEL)
    return (y_prompt, y_sample, jnp.stack(new_k, axis=1), jnp.stack(new_v, axis=1), jnp.stack(new_ckv, axis=1),
            jnp.stack(new_kr, axis=1), jnp.stack(new_sre, axis=1), jnp.stack(new_sim, axis=1))
```

```python
import functools
import math

import jax
import jax.numpy as jnp
from jax import lax
from jax.experimental import pallas as pl
from jax.experimental.pallas import tpu as pltpu

D_MODEL = 1024
BATCH = 32
SEQ = 256
DEPTH = 2
DEC_BATCH = 4
DEC_SEQ = 2048
PAST_LEN = 256
GRID_W = 64
ROPE_BASE = 10000.0
EPS = 1e-6
NEG_INF = -1e30
BLK = 128
HQ_A, HKV_A, HD_A = 8, 2, 64
G_A = HQ_A // HKV_A
WINDOW = 128
D_B, GS_B, P_B = 512, 16, 64
G_B = D_B // GS_B
H_C, Q_LORA, KV_LORA, NOPE_C, ROPE_C, V_C = 8, 256, 128, 64, 32, 64
QK_C = NOPE_C + ROPE_C
N_EXP, TOP_K, D_FF = 32, 4, 1024
SWIGLU_ALPHA, SWIGLU_LIMIT = 1.702, 7.0

N_CTX = BATCH * SEQ
N_LAT = DEC_BATCH * DEC_SEQ
T_ALL = N_CTX + N_LAT
N_GROUPS = 8

LANES = 128
SUBLANES = 8

C_GL = 0
C_QA = 3 * D_MODEL
C_UB = C_QA + HQ_A * LANES
C_CQ = C_UB + D_B
C_KA = C_CQ + Q_LORA
C_VA = C_KA + LANES
C_CKV = C_VA + LANES
C_KR = C_CKV + LANES
N_PROJ = C_KR + LANES

N_ASSIGN = T_ALL * TOP_K
N_BLOCKS = N_ASSIGN // BLK + N_EXP
BUF_LEN = N_BLOCKS * BLK

F32 = jnp.float32
BF16 = jnp.bfloat16
VMEM_LIMIT = 56 * 1024 * 1024


def _cp(sem, vmem=VMEM_LIMIT):
    return pltpu.CompilerParams(dimension_semantics=sem, vmem_limit_bytes=vmem)


def _group_of_tile(i, tm):
    n_ctx_tiles = N_CTX // tm
    per_batch = DEC_SEQ // tm
    return jnp.where(i < n_ctx_tiles, 0, 1 + (i - n_ctx_tiles) // per_batch)


def _sigmoid(x):
    return 1.0 / (1.0 + jnp.exp(-x))


def _adaln_kernel(c_ref, w_ref, b_ref, o_ref):
    c = c_ref[...]
    s = c * _sigmoid(c)
    o_ref[...] = jnp.dot(s.astype(BF16), w_ref[...].astype(BF16), preferred_element_type=F32) + b_ref[...]


def adaln(cvecs, w_ada, b_ada):
    tn = 1536
    out = pl.pallas_call(
        _adaln_kernel,
        out_shape=jax.ShapeDtypeStruct((DEPTH, N_GROUPS, 6 * D_MODEL), F32),
        grid=(DEPTH, 6 * D_MODEL // tn),
        in_specs=[
            pl.BlockSpec((N_GROUPS, D_MODEL), lambda l, j: (0, 0)),
            pl.BlockSpec((None, D_MODEL, tn), lambda l, j: (l, 0, j)),
            pl.BlockSpec((None, 1, tn), lambda l, j: (l, 0, j)),
        ],
        out_specs=pl.BlockSpec((None, N_GROUPS, tn), lambda l, j: (l, 0, j)),
        compiler_params=_cp(("arbitrary", "arbitrary")),
        name="adaln",
    )(cvecs, w_ada, b_ada.reshape(DEPTH, 1, 6 * D_MODEL))
    return out.reshape(DEPTH, N_GROUPS, 6, D_MODEL)


def _rms(x, g):
    ms = jnp.mean(x * x, axis=-1, keepdims=True)
    return x * lax.rsqrt(ms + EPS) * g


def _inproj_kernel(x_ref, g_ref, mod_ref, w_ref, o_ref):
    h = _rms(x_ref[...], g_ref[...])
    h = h * (1.0 + mod_ref[1:2, :]) + mod_ref[0:1, :]
    o_ref[...] = jnp.dot(h.astype(BF16), w_ref[...], preferred_element_type=F32)


def inproj(x, g, mod, w):
    tm, tn = 512, 1792
    return pl.pallas_call(
        _inproj_kernel,
        out_shape=jax.ShapeDtypeStruct((T_ALL, N_PROJ), F32),
        grid=(N_PROJ // tn, T_ALL // tm),
        in_specs=[
            pl.BlockSpec((tm, D_MODEL), lambda j, i: (i, 0)),
            pl.BlockSpec((1, D_MODEL), lambda j, i: (0, 0)),
            pl.BlockSpec((None, 6, D_MODEL), lambda j, i: (_group_of_tile(i, tm), 0, 0)),
            pl.BlockSpec((D_MODEL, tn), lambda j, i: (0, j)),
        ],
        out_specs=pl.BlockSpec((tm, tn), lambda j, i: (i, j)),
        compiler_params=_cp(("arbitrary", "arbitrary")),
        name="inproj",
    )(x, g.reshape(1, D_MODEL), mod, w)


def _rope(slab, cos, sin_lo, sin_hi, half):
    up = pltpu.roll(slab, LANES - half, axis=1)
    dn = pltpu.roll(slab, half, axis=1)
    return slab * cos + up * sin_lo + dn * sin_hi


def _slot_norm(slab, gain, n_real):
    ms = jnp.sum(slab * slab, axis=-1, keepdims=True) * (1.0 / n_real)
    return slab * lax.rsqrt(ms + EPS) * gain


def _mla_keys(ckvn, kr_blk, wukv_ref, gk, rope_c):
    kv = jnp.dot(ckvn.astype(BF16), wukv_ref[...], preferred_element_type=F32)
    ks = []
    for h in range(H_C):
        slab = kv[:, h * LANES:(h + 1) * LANES] + kr_blk
        slab = _slot_norm(slab, gk, QK_C)
        if rope_c is not None:
            slab = _rope(slab, *rope_c, ROPE_C // 4)
        ks.append(slab.astype(BF16))
    return jnp.concatenate(ks, axis=1), kv[:, H_C * LANES:].astype(BF16)


def _prep_kernel(qa_ref, cq_ref, ka_ref, ckv_ref, kr_ref, tab_ref,
                 gqa_ref, gka_ref, gcq_ref, gckv_ref, gqc_ref, gkc_ref, wuq_ref, wukv_ref,
                 qa_o, ka_o, qc_o, kc_o, vc_o, ckvn_o):
    tab = tab_ref[...]
    rope_a = (tab[:, 0:128], tab[:, 128:256], tab[:, 256:384])
    rope_c = (tab[:, 384:512], tab[:, 512:640], tab[:, 640:768])
    lane = lax.broadcasted_iota(jnp.int32, (1, LANES), 1)

    gqa = gqa_ref[...]
    qa = qa_ref[...]
    outs = []
    for h in range(HQ_A):
        slab = _slot_norm(qa[:, h * LANES:(h + 1) * LANES], gqa, HD_A)
        slab = _rope(slab, *rope_a, HD_A // 4) * (HD_A ** -0.5)
        outs.append(slab.astype(BF16))
    qa_o[...] = jnp.concatenate(outs, axis=1)

    ka = ka_ref[...]
    sq = ka * ka
    lo = lane < HD_A
    ms_lo = jnp.sum(jnp.where(lo, sq, 0.0), axis=-1, keepdims=True)
    ms_hi = jnp.sum(jnp.where(lo, 0.0, sq), axis=-1, keepdims=True)
    rs = jnp.where(lo, lax.rsqrt(ms_lo * (1.0 / HD_A) + EPS), lax.rsqrt(ms_hi * (1.0 / HD_A) + EPS))
    ka_o[...] = _rope(ka * rs * gka_ref[...], *rope_a, HD_A // 4)

    cqn = _rms(cq_ref[...], gcq_ref[...])
    q = jnp.dot(cqn.astype(BF16), wuq_ref[...], preferred_element_type=F32)
    gqc = gqc_ref[...]
    outs = []
    for h in range(H_C):
        slab = _slot_norm(q[:, h * LANES:(h + 1) * LANES], gqc, QK_C)
        slab = _rope(slab, *rope_c, ROPE_C // 4) * (QK_C ** -0.5)
        outs.append(slab.astype(BF16))
    qc_o[...] = jnp.concatenate(outs, axis=1)

    ckvn = _rms(ckv_ref[...], gckv_ref[...])
    ckvn_o[...] = ckvn
    kc, vc = _mla_keys(ckvn, kr_ref[...], wukv_ref, gkc_ref[...], rope_c)
    kc_o[...] = kc
    vc_o[...] = vc


def prep(proj, tab, gqa, gka, gcq, gckv, gqc, gkc, wuq, wukv):
    tm = 256
    n_ctx_tiles = N_CTX // tm
    per_batch = DEC_SEQ // tm

    def tab_map(i):
        return (jnp.where(i < n_ctx_tiles, 0, per_batch + (i - n_ctx_tiles) % per_batch), 0)

    def col(width, off):
        return pl.BlockSpec((tm, width), lambda i: (i, off // width))

    def full(shape):
        return pl.BlockSpec(shape, lambda i: (0,) * len(shape))

    def row_out(width):
        return pl.BlockSpec((tm, width), lambda i: (i, 0))

    return pl.pallas_call(
        _prep_kernel,
        out_shape=(
            jax.ShapeDtypeStruct((T_ALL, HQ_A * LANES), BF16),
            jax.ShapeDtypeStruct((T_ALL, LANES), F32),
            jax.ShapeDtypeStruct((T_ALL, H_C * LANES), BF16),
            jax.ShapeDtypeStruct((T_ALL, H_C * LANES), BF16),
            jax.ShapeDtypeStruct((T_ALL, H_C * V_C), BF16),
            jax.ShapeDtypeStruct((T_ALL, KV_LORA), F32),
        ),
        grid=(T_ALL // tm,),
        in_specs=[
            col(HQ_A * LANES, C_QA), col(Q_LORA, C_CQ), col(LANES, C_KA), col(LANES, C_CKV), col(LANES, C_KR),
            pl.BlockSpec((tm, 6 * LANES), tab_map),
            full((1, LANES)), full((1, LANES)), full((1, Q_LORA)), full((1, KV_LORA)),
            full((1, LANES)), full((1, LANES)),
            full((Q_LORA, H_C * LANES)), full((KV_LORA, H_C * LANES + H_C * V_C)),
        ],
        out_specs=(row_out(HQ_A * LANES), row_out(LANES), row_out(H_C * LANES), row_out(H_C * LANES),
                   row_out(H_C * V_C), row_out(KV_LORA)),
        compiler_params=_cp(("arbitrary",)),
        name="prep",
    )(proj, proj, proj, proj, proj, tab, gqa, gka, gcq, gckv, gqc, gkc, wuq, wukv)


def _cachekeys_kernel(ckv_ref, kr_ref, gkc_ref, wukv_ref, kc_o, vc_o):
    kc, vc = _mla_keys(ckv_ref[...], kr_ref[...], wukv_ref, gkc_ref[...], None)
    kc_o[...] = kc
    vc_o[...] = vc


def cache_keys(ckv, kr_blk, gkc, wukv):
    r = ckv.shape[0]
    tm = 256
    return pl.pallas_call(
        _cachekeys_kernel,
        out_shape=(jax.ShapeDtypeStruct((r, H_C * LANES), BF16), jax.ShapeDtypeStruct((r, H_C * V_C), BF16)),
        grid=(r // tm,),
        in_specs=[
            pl.BlockSpec((tm, LANES), lambda i: (i, 0)),
            pl.BlockSpec((tm, LANES), lambda i: (i, 0)),
            pl.BlockSpec((1, LANES), lambda i: (0, 0)),
            pl.BlockSpec((KV_LORA, H_C * LANES + H_C * V_C), lambda i: (0, 0)),
        ],
        out_specs=(pl.BlockSpec((tm, H_C * LANES), lambda i: (i, 0)), pl.BlockSpec((tm, H_C * V_C), lambda i: (i, 0))),
        compiler_params=_cp(("arbitrary",)),
        name="cache_keys",
    )(ckv, kr_blk, gkc, wukv)


def _attn_body(q_ref, segs, sink_ref, o_ref, *, n_heads, k_slot, v_slab, v_half, tq, band_qi=None):
    outs = []
    for h in range(n_heads):
        qh = q_ref[:, h * LANES:(h + 1) * LANES]
        scores = []
        for k_ref, _, off in segs:
            kh = k_ref[:, k_slot(h) * LANES:(k_slot(h) + 1) * LANES].astype(BF16)
            s = lax.dot_general(qh, kh, (((1,), (1,)), ((), ())), preferred_element_type=F32)
            if off is not None:
                tk = s.shape[1]
                blk = band_qi + off
                q_pos = band_qi * tq + lax.broadcasted_iota(jnp.int32, (tq, tk), 0)
                k_pos = blk * tk + lax.broadcasted_iota(jnp.int32, (tq, tk), 1)
                ok = (jnp.abs(k_pos - q_pos) <= WINDOW) & (blk >= 0) & (blk < DEC_SEQ // tk)
                s = jnp.where(ok, s, NEG_INF)
            scores.append(s)
        m = scores[0].max(axis=-1, keepdims=True)
        for s in scores[1:]:
            m = jnp.maximum(m, s.max(axis=-1, keepdims=True))
        if sink_ref is not None:
            sink = sink_ref[h:h + 1, 0:1]
            m = jnp.maximum(m, sink)
            denom = jnp.exp(sink - m)
        else:
            denom = jnp.zeros_like(m)
        acc = None
        for s, (_, v_ref, _) in zip(scores, segs):
            p = jnp.exp(s - m)
            denom = denom + p.sum(axis=-1, keepdims=True)
            vs = v_ref[:, v_slab(h) * LANES:(v_slab(h) + 1) * LANES].astype(BF16)
            pv = jnp.dot(p.astype(BF16), vs, preferred_element_type=F32)
            acc = pv if acc is None else acc + pv
        half = v_half(h)
        outs.append(acc[:, half * 64:(half + 1) * 64] * (1.0 / denom))
    o_ref[...] = jnp.concatenate(outs, axis=1).astype(o_ref.dtype)


_A_CFG = dict(n_heads=HQ_A, k_slot=lambda h: 0, v_slab=lambda h: 0, v_half=lambda h: h // G_A)
_C_CFG = dict(n_heads=H_C, k_slot=lambda h: h, v_slab=lambda h: h // 2, v_half=lambda h: h % 2)


def _attn_a_ctx_kernel(q_ref, k_ref, v_ref, sink_ref, o_ref):
    _attn_body(q_ref, [(k_ref, v_ref, None)], sink_ref, o_ref, tq=SEQ, **_A_CFG)


def _attn_a_lat_kernel(q_ref, k0, k1, k2, v0, v1, v2, kc_ref, vc_ref, sink_ref, o_ref):
    qi = pl.program_id(1)
    segs = [(k0, v0, -1), (k1, v1, 0), (k2, v2, 1), (kc_ref, vc_ref, None)]
    _attn_body(q_ref, segs, sink_ref, o_ref, tq=BLK, band_qi=qi, **_A_CFG)


def _attn_c_ctx_kernel(q_ref, k_ref, v_ref, o_ref):
    _attn_body(q_ref, [(k_ref, v_ref, None)], None, o_ref, tq=SEQ, **_C_CFG)


def _attn_c_lat_kernel(q_ref, k_ref, v_ref, kc_ref, vc_ref, o_ref):
    _attn_body(q_ref, [(k_ref, v_ref, None), (kc_ref, vc_ref, None)], None, o_ref, tq=256, **_C_CFG)


def attn_a_ctx(qa, ka, proj, sink):
    return pl.pallas_call(
        _attn_a_ctx_kernel,
        out_shape=jax.ShapeDtypeStruct((N_CTX, HQ_A * HD_A), BF16),
        grid=(BATCH,),
        in_specs=[
            pl.BlockSpec((SEQ, HQ_A * LANES), lambda b: (b, 0)),
            pl.BlockSpec((SEQ, LANES), lambda b: (b, 0)),
            pl.BlockSpec((SEQ, LANES), lambda b: (b, C_VA // LANES)),
            pl.BlockSpec((SUBLANES, LANES), lambda b: (0, 0)),
        ],
        out_specs=pl.BlockSpec((SEQ, HQ_A * HD_A), lambda b: (b, 0)),
        compiler_params=_cp(("arbitrary",)),
        name="attn_a_ctx",
    )(qa, ka, proj, sink)


def attn_a_lat(qa, ka, proj, k_cache, v_cache, sink):
    nb = DEC_SEQ // BLK
    base = N_CTX // BLK

    def band(off, colblk):
        return pl.BlockSpec((BLK, LANES), lambda b, i: (base + b * nb + jnp.clip(i + off, 0, nb - 1), colblk))

    return pl.pallas_call(
        _attn_a_lat_kernel,
        out_shape=jax.ShapeDtypeStruct((N_LAT, HQ_A * HD_A), BF16),
        grid=(DEC_BATCH, nb),
        in_specs=[
            pl.BlockSpec((BLK, HQ_A * LANES), lambda b, i: (base + b * nb + i, 0)),
            band(-1, 0), band(0, 0), band(1, 0),
            band(-1, C_VA // LANES), band(0, C_VA // LANES), band(1, C_VA // LANES),
            pl.BlockSpec((None, PAST_LEN, LANES), lambda b, i: (b, 0, 0)),
            pl.BlockSpec((None, PAST_LEN, LANES), lambda b, i: (b, 0, 0)),
            pl.BlockSpec((SUBLANES, LANES), lambda b, i: (0, 0)),
        ],
        out_specs=pl.BlockSpec((BLK, HQ_A * HD_A), lambda b, i: (b * nb + i, 0)),
        compiler_params=_cp(("arbitrary", "arbitrary")),
        name="attn_a_lat",
    )(qa, ka, ka, ka, proj, proj, proj, k_cache, v_cache, sink)


def attn_c_ctx(qc, kc, vc):
    return pl.pallas_call(
        _attn_c_ctx_kernel,
        out_shape=jax.ShapeDtypeStruct((N_CTX, H_C * V_C), BF16),
        grid=(BATCH,),
        in_specs=[
            pl.BlockSpec((SEQ, H_C * LANES), lambda b: (b, 0)),
            pl.BlockSpec((SEQ, H_C * LANES), lambda b: (b, 0)),
            pl.BlockSpec((SEQ, H_C * V_C), lambda b: (b, 0)),
        ],
        out_specs=pl.BlockSpec((SEQ, H_C * V_C), lambda b: (b, 0)),
        compiler_params=_cp(("arbitrary",)),
        name="attn_c_ctx",
    )(qc, kc, vc)


def attn_c_lat(qc, kc, vc, kc_cache, vc_cache):
    tq = 256
    nq = DEC_SEQ // tq
    qbase = N_CTX // tq
    kbase = N_CTX // DEC_SEQ
    return pl.pallas_call(
        _attn_c_lat_kernel,
        out_shape=jax.ShapeDtypeStruct((N_LAT, H_C * V_C), BF16),
        grid=(DEC_BATCH, nq),
        in_specs=[
            pl.BlockSpec((tq, H_C * LANES), lambda b, i: (qbase + b * nq + i, 0)),
            pl.BlockSpec((DEC_SEQ, H_C * LANES), lambda b, i: (kbase + b, 0)),
            pl.BlockSpec((DEC_SEQ, H_C * V_C), lambda b, i: (kbase + b, 0)),
            pl.BlockSpec((PAST_LEN, H_C * LANES), lambda b, i: (b, 0)),
            pl.BlockSpec((PAST_LEN, H_C * V_C), lambda b, i: (b, 0)),
        ],
        out_specs=pl.BlockSpec((tq, H_C * V_C), lambda b, i: (b * nq + i, 0)),
        compiler_params=_cp(("arbitrary", "arbitrary")),
        name="attn_c_lat",
    )(qc, kc, vc, kc_cache, vc_cache)


S5_TC = 256
S5_CH = 512
S5_UNROLL = 8


def _s5_kernel(u_ref, wb_ref, wc_ref, are_ref, aim_ref, h0re_ref, h0im_ref,
               y_ref, fre_ref, fim_ref, bre, bim, sre, sim, *, mixed):
    k = pl.program_id(2)

    @pl.when(k == 0)
    def _():
        sre[...] = h0re_ref[...]
        sim[...] = h0im_ref[...]

    rows = S5_TC * SUBLANES
    u2 = u_ref[...].reshape(rows, LANES).astype(BF16)
    if mixed:
        fwd_row = (lax.broadcasted_iota(jnp.int32, (rows, 1), 0) % SUBLANES) < (SUBLANES // 2)

    def proj_b(lo, hi):
        r = jnp.dot(u2, wb_ref[0, :, lo:hi], preferred_element_type=F32)
        if mixed:
            r = jnp.where(fwd_row, r, jnp.dot(u2, wb_ref[1, :, lo:hi], preferred_element_type=F32))
        return r.reshape(S5_TC, SUBLANES, S5_CH)

    bre[...] = proj_b(0, S5_CH)
    bim[...] = proj_b(S5_CH, 2 * S5_CH)

    a_re = are_ref[...]
    a_im = aim_ref[...]

    def outer(i, carry):
        sr, si = carry
        for j in range(S5_UNROLL):
            t = i * S5_UNROLL + j
            nr = a_re * sr - a_im * si + bre[t]
            ni = a_re * si + a_im * sr + bim[t]
            bre[t] = nr
            bim[t] = ni
            sr, si = nr, ni
        return sr, si

    sr, si = lax.fori_loop(0, S5_TC // S5_UNROLL, outer, (sre[...], sim[...]))
    sre[...] = sr
    sim[...] = si
    fre_ref[...] = sr
    fim_ref[...] = si

    xr = bre[...].reshape(rows, S5_CH).astype(BF16)
    xi = bim[...].reshape(rows, S5_CH).astype(BF16)

    def proj_c(d):
        return (jnp.dot(xr, wc_ref[d, 0:S5_CH, :], preferred_element_type=F32)
                + jnp.dot(xi, wc_ref[d, S5_CH:, :], preferred_element_type=F32))

    y = proj_c(0)
    if mixed:
        y = jnp.where(fwd_row, y, proj_c(1))
    y_ref[...] = y.reshape(S5_TC, SUBLANES, LANES)


def s5_scan(seqs, wb, wc, a_re, a_im, h0_re, h0_im, *, mixed):
    length, ns, _ = seqs.shape
    nsb = ns // SUBLANES
    nj = D_B // LANES
    nk = length // S5_TC
    nd = 2 if mixed else 1
    per_dir = nsb // 2 if not mixed else 1

    def dmap(sb):
        return 0 if mixed else sb // per_dir

    kern = functools.partial(_s5_kernel, mixed=mixed)
    return pl.pallas_call(
        kern,
        out_shape=(
            jax.ShapeDtypeStruct((length, ns, D_B), F32),
            jax.ShapeDtypeStruct((ns, G_B * P_B), F32),
            jax.ShapeDtypeStruct((ns, G_B * P_B), F32),
        ),
        grid=(nsb, nj, nk),
        in_specs=[
            pl.BlockSpec((S5_TC, SUBLANES, LANES), lambda sb, j, k: (k, sb, j)),
            pl.BlockSpec((nd, None, LANES, 2 * S5_CH), lambda sb, j, k: (dmap(sb), j, 0, 0)),
            pl.BlockSpec((nd, None, 2 * S5_CH, LANES), lambda sb, j, k: (dmap(sb), j, 0, 0)),
            pl.BlockSpec((None, SUBLANES, S5_CH), lambda sb, j, k: (dmap(sb), 0, j)),
            pl.BlockSpec((None, SUBLANES, S5_CH), lambda sb, j, k: (dmap(sb), 0, j)),
            pl.BlockSpec((SUBLANES, S5_CH), lambda sb, j, k: (sb, j)),
            pl.BlockSpec((SUBLANES, S5_CH), lambda sb, j, k: (sb, j)),
        ],
        out_specs=(
            pl.BlockSpec((S5_TC, SUBLANES, LANES), lambda sb, j, k: (k, sb, j)),
            pl.BlockSpec((SUBLANES, S5_CH), lambda sb, j, k: (sb, j)),
            pl.BlockSpec((SUBLANES, S5_CH), lambda sb, j, k: (sb, j)),
        ),
        scratch_shapes=[
            pltpu.VMEM((S5_TC, SUBLANES, S5_CH), F32), pltpu.VMEM((S5_TC, SUBLANES, S5_CH), F32),
            pltpu.VMEM((SUBLANES, S5_CH), F32), pltpu.VMEM((SUBLANES, S5_CH), F32),
        ],
        compiler_params=_cp(("arbitrary", "arbitrary", "arbitrary")),
        name="s5_scan_mixed" if mixed else "s5_scan",
    )(seqs, wb, wc, a_re, a_im, h0_re, h0_im)


def _s5fin_kernel(u_ref, y_ref, d_ref, w_ref, b_ref, o_ref):
    y = d_ref[...] * u_ref[...] + y_ref[...]
    y = 0.5 * y * (1.0 + jnp.tanh(math.sqrt(2.0 / math.pi) * (y + 0.044715 * (y * y * y))))
    z = jnp.dot(y.astype(BF16), w_ref[...], preferred_element_type=F32) + b_ref[...]
    o_ref[...] = (y * _sigmoid(z)).astype(o_ref.dtype)


def s5_finish(proj, y, d, w_glu, b_glu):
    tm = 512
    return pl.pallas_call(
        _s5fin_kernel,
        out_shape=jax.ShapeDtypeStruct((T_ALL, D_B), BF16),
        grid=(T_ALL // tm,),
        in_specs=[
            pl.BlockSpec((tm, D_B), lambda i: (i, C_UB // D_B)),
            pl.BlockSpec((tm, D_B), lambda i: (i, 0)),
            pl.BlockSpec((1, D_B), lambda i: (0, 0)),
            pl.BlockSpec((D_B, D_B), lambda i: (0, 0)),
            pl.BlockSpec((1, D_B), lambda i: (0, 0)),
        ],
        out_specs=pl.BlockSpec((tm, D_B), lambda i: (i, 0)),
        compiler_params=_cp(("arbitrary",)),
        name="s5_finish",
    )(proj, y, d, w_glu, b_glu)


def _merge_kernel(x_ref, gl_ref, oa_ref, ob_ref, oc_ref, mod_ref, g_ref, wa_ref, wb_ref, wc_ref, wo_ref,
                  x1_ref, h2_ref):
    m = None
    for br, (o_ref, w_ref) in enumerate(((oa_ref, wa_ref), (ob_ref, wb_ref), (oc_ref, wc_ref))):
        gate = _sigmoid(gl_ref[:, br * D_MODEL:(br + 1) * D_MODEL])
        t = gate * jnp.dot(o_ref[...], w_ref[...], preferred_element_type=F32)
        m = t if m is None else m + t
    x1 = x_ref[...] + mod_ref[2:3, :] * jnp.dot(m.astype(BF16), wo_ref[...], preferred_element_type=F32)
    x1_ref[...] = x1
    h2 = _rms(x1, g_ref[...])
    h2_ref[...] = h2 * (1.0 + mod_ref[4:5, :]) + mod_ref[3:4, :]


def merge(x, proj, oa, ob, oc, mod, g, wa, wb, wc, wo):
    tm = 256
    full = lambda shape: pl.BlockSpec(shape, lambda i: (0,) * len(shape))
    return pl.pallas_call(
        _merge_kernel,
        out_shape=(jax.ShapeDtypeStruct((T_ALL, D_MODEL), F32), jax.ShapeDtypeStruct((T_ALL, D_MODEL), F32)),
        grid=(T_ALL // tm,),
        in_specs=[
            pl.BlockSpec((tm, D_MODEL), lambda i: (i, 0)),
            pl.BlockSpec((tm, 3 * D_MODEL), lambda i: (i, 0)),
            pl.BlockSpec((tm, 512), lambda i: (i, 0)),
            pl.BlockSpec((tm, 512), lambda i: (i, 0)),
            pl.BlockSpec((tm, 512), lambda i: (i, 0)),
            pl.BlockSpec((None, 6, D_MODEL), lambda i: (_group_of_tile(i, tm), 0, 0)),
            full((1, D_MODEL)),
            full((512, D_MODEL)), full((512, D_MODEL)), full((512, D_MODEL)), full((D_MODEL, D_MODEL)),
        ],
        out_specs=(pl.BlockSpec((tm, D_MODEL), lambda i: (i, 0)), pl.BlockSpec((tm, D_MODEL), lambda i: (i, 0))),
        compiler_params=_cp(("arbitrary",)),
        name="merge",
    )(x, proj, oa, ob, oc, mod, g, wa, wb, wc, wo)


ROUTE_TM = 256


def _router_kernel(h_ref, w_ref, b_ref, idx_ref, rank_ref, wgt_ref, cnt_ref, cnt_acc):
    i = pl.program_id(0)

    @pl.when(i == 0)
    def _():
        cnt_acc[...] = jnp.zeros_like(cnt_acc)

    logits = jnp.dot(h_ref[...], w_ref[...], preferred_element_type=F32,
                     precision=lax.Precision.HIGHEST) + b_ref[...]
    lane_i = lax.broadcasted_iota(jnp.int32, (ROUTE_TM, LANES), 1)
    lane = lane_i.astype(F32)
    r_i = lax.broadcasted_iota(jnp.int32, (ROUTE_TM, ROUTE_TM), 0)
    c_i = lax.broadcasted_iota(jnp.int32, (ROUTE_TM, ROUTE_TM), 1)
    earlier = jnp.where(c_i < r_i, 1.0, 0.0).astype(BF16)

    cnt = cnt_acc[...]
    idx_out = jnp.zeros((ROUTE_TM, LANES), F32)
    rank_out = jnp.zeros((ROUTE_TM, LANES), F32)
    val_out = jnp.zeros((ROUTE_TM, LANES), F32)
    v0 = None
    esum = None
    for k in range(TOP_K):
        m = logits.max(axis=-1, keepdims=True)
        sel = jnp.min(jnp.where(logits == m, lane, float(LANES)), axis=-1, keepdims=True)
        hit = lane == sel
        logits = jnp.where(hit, -jnp.inf, logits)
        onehot = jnp.where(hit, 1.0, 0.0)
        within = jnp.dot(earlier, onehot.astype(BF16), preferred_element_type=F32)
        rank = jnp.sum(onehot * (within + cnt), axis=-1, keepdims=True)
        cnt = cnt + jnp.sum(onehot, axis=0, keepdims=True)
        if k == 0:
            v0 = m
        e = jnp.exp(m - v0)
        esum = e if esum is None else esum + e
        idx_out = jnp.where(lane_i == k, sel, idx_out)
        rank_out = jnp.where(lane_i == k, rank, rank_out)
        val_out = jnp.where(lane_i == k, e, val_out)
    cnt_acc[...] = cnt
    cnt_ref[...] = cnt
    idx_ref[...] = idx_out.astype(jnp.int32)
    rank_ref[...] = rank_out.astype(jnp.int32)
    wgt_ref[...] = val_out * (1.0 / esum)


def router(h2, w_router, b_router):
    tile = lambda: pl.BlockSpec((ROUTE_TM, LANES), lambda i: (i, 0))
    return pl.pallas_call(
        _router_kernel,
        out_shape=(
            jax.ShapeDtypeStruct((T_ALL, LANES), jnp.int32),
            jax.ShapeDtypeStruct((T_ALL, LANES), jnp.int32),
            jax.ShapeDtypeStruct((T_ALL, LANES), F32),
            jax.ShapeDtypeStruct((1, LANES), F32),
        ),
        grid=(T_ALL // ROUTE_TM,),
        in_specs=[
            pl.BlockSpec((ROUTE_TM, D_MODEL), lambda i: (i, 0)),
            pl.BlockSpec((D_MODEL, LANES), lambda i: (0, 0)),
            pl.BlockSpec((1, LANES), lambda i: (0, 0)),
        ],
        out_specs=(tile(), tile(), tile(), pl.BlockSpec((1, LANES), lambda i: (0, 0))),
        scratch_shapes=[pltpu.VMEM((1, LANES), F32)],
        compiler_params=_cp(("arbitrary",)),
        name="router",
    )(h2, w_router, b_router)


DISP_TM = 256


def _dispatch_kernel(dest_ref, h_ref, xs_in, xs_hbm, sem):
    del xs_in
    i = pl.program_id(0)
    base = i * DISP_TM

    def issue(t, c):
        tok = base + t
        for k in range(TOP_K):
            d = dest_ref[tok * TOP_K + k]
            pltpu.make_async_copy(h_ref.at[pl.ds(t, 1), :], xs_hbm.at[pl.ds(d, 1), :], sem).start()
        return c

    lax.fori_loop(0, DISP_TM, issue, 0)
    n = DISP_TM * TOP_K
    pltpu.make_async_copy(xs_hbm.at[pl.ds(0, n), :], xs_hbm.at[pl.ds(0, n), :], sem).wait()


def dispatch(dest, h2, xs_zero):
    return pl.pallas_call(
        _dispatch_kernel,
        out_shape=jax.ShapeDtypeStruct((BUF_LEN, D_MODEL), F32),
        grid_spec=pltpu.PrefetchScalarGridSpec(
            num_scalar_prefetch=1,
            grid=(T_ALL // DISP_TM,),
            in_specs=[pl.BlockSpec((DISP_TM, D_MODEL), lambda i, d: (i, 0)), pl.BlockSpec(memory_space=pl.ANY)],
            out_specs=pl.BlockSpec(memory_space=pl.ANY),
            scratch_shapes=[pltpu.SemaphoreType.DMA],
        ),
        input_output_aliases={2: 0},
        compiler_params=_cp(("arbitrary",)),
        name="dispatch",
    )(dest, h2, xs_zero)


PAIR_TILE = 2 * LANES


def _wperm_kernel(w_ref, s_ref, o_ref):
    w = w_ref[...].astype(BF16)
    for j in range(w.shape[1] // PAIR_TILE):
        cols = slice(j * PAIR_TILE, (j + 1) * PAIR_TILE)
        o_ref[:, cols] = jnp.dot(w[:, cols], s_ref[...], preferred_element_type=F32).astype(BF16)


def permute_w_up(w_up):
    tn = 512
    r = lax.broadcasted_iota(jnp.int32, (PAIR_TILE, PAIR_TILE), 0)
    c = lax.broadcasted_iota(jnp.int32, (PAIR_TILE, PAIR_TILE), 1)
    sel = (r == jnp.where(c < LANES, 2 * c, 2 * (c - LANES) + 1)).astype(BF16)
    w = w_up.reshape(DEPTH * N_EXP, D_MODEL, 2 * D_FF)
    out = pl.pallas_call(
        _wperm_kernel,
        out_shape=jax.ShapeDtypeStruct(w.shape, BF16),
        grid=(DEPTH * N_EXP, 2 * D_FF // tn),
        in_specs=[
            pl.BlockSpec((None, D_MODEL, tn), lambda e, j: (e, 0, j)),
            pl.BlockSpec((PAIR_TILE, PAIR_TILE), lambda e, j: (0, 0)),
        ],
        out_specs=pl.BlockSpec((None, D_MODEL, tn), lambda e, j: (e, 0, j)),
        compiler_params=_cp(("arbitrary", "arbitrary")),
        name="permute_w_up",
    )(w, sel)
    return out.reshape(DEPTH, N_EXP, D_MODEL, 2 * D_FF)


def _expert_kernel(be_ref, x_ref, wu_ref, bu_ref, wd_ref, bd_ref, y_ref):
    del be_ref
    x = x_ref[...].astype(BF16)
    h = jnp.dot(x, wu_ref[...], preferred_element_type=F32) + bu_ref[...]
    acts = []
    for j in range(2 * D_FF // PAIR_TILE):
        glu = jnp.minimum(h[:, j * PAIR_TILE:j * PAIR_TILE + LANES], SWIGLU_LIMIT)
        lin = jnp.clip(h[:, j * PAIR_TILE + LANES:(j + 1) * PAIR_TILE], -SWIGLU_LIMIT, SWIGLU_LIMIT)
        acts.append((glu * _sigmoid(SWIGLU_ALPHA * glu) * (lin + 1.0)).astype(BF16))
    act = jnp.concatenate(acts, axis=1)
    y_ref[...] = jnp.dot(act, wd_ref[...], preferred_element_type=F32) + bd_ref[...]


def experts(block_e, xs, wu, bu, wd, bd):
    return pl.pallas_call(
        _expert_kernel,
        out_shape=jax.ShapeDtypeStruct((BUF_LEN, D_MODEL), F32),
        grid_spec=pltpu.PrefetchScalarGridSpec(
            num_scalar_prefetch=1,
            grid=(N_BLOCKS,),
            in_specs=[
                pl.BlockSpec((BLK, D_MODEL), lambda i, be: (i, 0)),
                pl.BlockSpec((None, D_MODEL, 2 * D_FF), lambda i, be: (be[i], 0, 0)),
                pl.BlockSpec((None, 1, 2 * D_FF), lambda i, be: (be[i], 0, 0)),
                pl.BlockSpec((None, D_FF, D_MODEL), lambda i, be: (be[i], 0, 0)),
                pl.BlockSpec((None, 1, D_MODEL), lambda i, be: (be[i], 0, 0)),
            ],
            out_specs=pl.BlockSpec((BLK, D_MODEL), lambda i, be: (i, 0)),
        ),
        compiler_params=_cp(("arbitrary",)),
        name="experts",
    )(block_e, xs, wu, bu, wd, bd)


COMB_TM = 128


def _combine_kernel(dest_ref, ys_hbm, x_ref, w_ref, mod_ref, o_ref, buf, sem):
    i = pl.program_id(0)
    base = i * COMB_TM

    def issue(t, c):
        tok = base + t
        for k in range(TOP_K):
            d = dest_ref[tok * TOP_K + k]
            pltpu.make_async_copy(ys_hbm.at[pl.ds(d, 1), :], buf.at[pl.ds(k * COMB_TM + t, 1), :], sem).start()
        return c

    lax.fori_loop(0, COMB_TM, issue, 0)
    n = COMB_TM * TOP_K
    pltpu.make_async_copy(ys_hbm.at[pl.ds(0, n), :], buf, sem).wait()
    w = w_ref[...]
    acc = None
    for k in range(TOP_K):
        t = w[:, k:k + 1] * buf[k * COMB_TM:(k + 1) * COMB_TM, :]
        acc = t if acc is None else acc + t
    o_ref[...] = x_ref[...] + mod_ref[5:6, :] * acc


def combine(dest, ys, x1, wgt, mod):
    return pl.pallas_call(
        _combine_kernel,
        out_shape=jax.ShapeDtypeStruct((T_ALL, D_MODEL), F32),
        grid_spec=pltpu.PrefetchScalarGridSpec(
            num_scalar_prefetch=1,
            grid=(T_ALL // COMB_TM,),
            in_specs=[
                pl.BlockSpec(memory_space=pl.ANY),
                pl.BlockSpec((COMB_TM, D_MODEL), lambda i, d: (i, 0)),
                pl.BlockSpec((COMB_TM, LANES), lambda i, d: (i, 0)),
                pl.BlockSpec((None, 6, D_MODEL), lambda i, d: (_group_of_tile(i, COMB_TM), 0, 0)),
            ],
            out_specs=pl.BlockSpec((COMB_TM, D_MODEL), lambda i, d: (i, 0)),
            scratch_shapes=[pltpu.VMEM((COMB_TM * TOP_K, D_MODEL), F32), pltpu.SemaphoreType.DMA],
        ),
        compiler_params=_cp(("arbitrary",)),
        name="combine",
    )(dest, ys, x1, wgt, mod)


def _rope_table():
    pos = jnp.arange(DEC_SEQ)
    row = (pos // GRID_W).astype(F32)[:, None]
    col = (pos % GRID_W).astype(F32)[:, None]

    def parts(rot_dim):
        nf = rot_dim // 4
        inv = ROPE_BASE ** (-jnp.arange(nf, dtype=F32) / nf)
        cr, sr, cc, sc = jnp.cos(row * inv), jnp.sin(row * inv), jnp.cos(col * inv), jnp.sin(col * inv)
        z = jnp.zeros_like(sr)
        cos = jnp.concatenate([cr, cr, cc, cc], axis=1)
        lo = jnp.concatenate([-sr, z, -sc, z], axis=1)
        hi = jnp.concatenate([z, sr, z, sc], axis=1)
        return cos, lo, hi

    ca, la, ha = (jnp.tile(t, (1, 2)) for t in parts(HD_A))
    cc, lc, hc = parts(ROPE_C)
    pad = lambda t, fill: jnp.pad(t, ((0, 0), (NOPE_C, LANES - QK_C)), constant_values=fill)
    lat = jnp.concatenate([ca, la, ha, pad(cc, 1.0), pad(lc, 0.0), pad(hc, 0.0)], axis=1)
    ones, zeros = jnp.ones((DEC_SEQ, LANES), F32), jnp.zeros((DEC_SEQ, LANES), F32)
    ident = jnp.concatenate([ones, zeros, zeros, ones, zeros, zeros], axis=1)
    return jnp.concatenate([ident, lat], axis=0)


def _arrange_w_in(w_in):
    o = 0
    parts = {}
    for name, n in (("qa", HQ_A * HD_A), ("ka", HKV_A * HD_A), ("va", HKV_A * HD_A), ("ub", D_B), ("cq", Q_LORA),
                    ("ckv", KV_LORA), ("kr", ROPE_C), ("gl", 3 * D_MODEL)):
        parts[name] = w_in[..., o:o + n]
        o += n
    qa = parts["qa"].reshape(DEPTH, D_MODEL, HKV_A, G_A, 1, HD_A)
    eye = jnp.eye(HKV_A, dtype=F32).reshape(1, 1, HKV_A, 1, HKV_A, 1)
    qa_slots = (qa * eye).reshape(DEPTH, D_MODEL, HQ_A * LANES)
    kr = jnp.pad(parts["kr"], ((0, 0), (0, 0), (NOPE_C, LANES - QK_C)))
    w = jnp.concatenate([parts["gl"], qa_slots, parts["ub"], parts["cq"], parts["ka"], parts["va"], parts["ckv"], kr],
                        axis=-1)
    return w.astype(BF16)


def _s5_params(lam_re, lam_im, log_dt, b_re, b_im, c_re, c_im):
    dt = jnp.exp(log_dt)[..., None]
    decay = jnp.exp(lam_re * dt)
    ab_re, ab_im = decay * jnp.cos(lam_im * dt), decay * jnp.sin(lam_im * dt)
    den = lam_re * lam_re + lam_im * lam_im
    f_re = ((ab_re - 1) * lam_re + ab_im * lam_im) / den
    f_im = (ab_im * lam_re - (ab_re - 1) * lam_im) / den
    bb_re = f_re[..., None] * b_re - f_im[..., None] * b_im
    bb_im = f_re[..., None] * b_im + f_im[..., None] * b_re
    nj, gpb = D_B // LANES, LANES // GS_B
    eye = jnp.eye(gpb, dtype=F32)

    def blockdiag_b(bb):
        t = bb.transpose(0, 1, 3, 2).reshape(2, nj, gpb, GS_B, P_B)
        return (t[:, :, :, :, None, :] * eye[None, None, :, None, :, None]).reshape(2, nj, LANES, gpb * P_B)

    def blockdiag_c(cc):
        t = cc.transpose(0, 1, 3, 2).reshape(2, nj, gpb, P_B, GS_B)
        return (t[:, :, :, :, None, :] * eye[None, None, :, None, :, None]).reshape(2, nj, gpb * P_B, LANES)

    wb = jnp.concatenate([blockdiag_b(bb_re), blockdiag_b(bb_im)], axis=-1).astype(BF16)
    wc = jnp.concatenate([blockdiag_c(c_re), -blockdiag_c(c_im)], axis=-2).astype(BF16)
    return wb, wc, ab_re.reshape(2, G_B * P_B), ab_im.reshape(2, G_B * P_B)


def _to_time_major(u):
    f = u.transpose(1, 0, 2)
    return jnp.concatenate([f, f[::-1]], axis=1)


def _from_time_major(y, b):
    return (y[:, :b] + y[::-1, b:]).transpose(1, 0, 2)


def kernel(x_prompt, x_sample, c, cache_attn_k, cache_attn_v, cache_mla_ckv, cache_mla_krope, state_ssm_re, state_ssm_im, c_ctx, w_ada, b_ada, norm_mix_g, norm_ffn_g, w_in, q_norm_a, k_norm_a, sink_a, q_a_norm_c, kv_a_norm_c, w_uq_c, w_ukv_c, q_norm_c, k_norm_c, ssm_lam_re, ssm_lam_im, ssm_log_dt, ssm_b_re, ssm_b_im, ssm_c_re, ssm_c_im, ssm_d, w_glu, b_glu, w_br_a, w_br_b, w_br_c, w_out, w_router, b_router, w_up, b_up, w_down, b_down):
    x = jnp.concatenate([x_prompt.reshape(N_CTX, D_MODEL), x_sample.reshape(N_LAT, D_MODEL)], axis=0)
    cvecs = jnp.concatenate([c_ctx[None], c, jnp.zeros((N_GROUPS - 1 - DEC_BATCH, D_MODEL), F32)], axis=0)
    mods = adaln(cvecs, w_ada, b_ada)

    tab = _rope_table()
    w_in_r = _arrange_w_in(w_in)
    pad_slot = lambda g: jnp.pad(g, ((0, 0), (0, LANES - QK_C))).reshape(DEPTH, 1, LANES)
    gqa = jnp.tile(q_norm_a, (1, 2)).reshape(DEPTH, 1, LANES)
    gka = jnp.tile(k_norm_a, (1, 2)).reshape(DEPTH, 1, LANES)
    gqc, gkc = pad_slot(q_norm_c), pad_slot(k_norm_c)
    wuq = jnp.pad(w_uq_c.reshape(DEPTH, Q_LORA, H_C, QK_C), ((0, 0), (0, 0), (0, 0), (0, LANES - QK_C)))
    wuq = wuq.reshape(DEPTH, Q_LORA, H_C * LANES).astype(BF16)
    wukv4 = w_ukv_c.reshape(DEPTH, KV_LORA, H_C, NOPE_C + V_C)
    wuk = jnp.pad(wukv4[..., :NOPE_C], ((0, 0), (0, 0), (0, 0), (0, LANES - NOPE_C))).reshape(DEPTH, KV_LORA, H_C * LANES)
    wuv = wukv4[..., NOPE_C:].reshape(DEPTH, KV_LORA, H_C * V_C)
    wukv = jnp.concatenate([wuk, wuv], axis=-1).astype(BF16)
    sink = jnp.broadcast_to(sink_a[:, :, None], (DEPTH, HQ_A, LANES))
    w_router_p = jnp.pad(w_router, ((0, 0), (0, 0), (0, LANES - N_EXP)))
    b_router_p = jnp.pad(b_router, ((0, 0), (0, LANES - N_EXP)), constant_values=-jnp.inf).reshape(DEPTH, 1, LANES)
    w_up_p = permute_w_up(w_up)
    b_up_p = b_up.reshape(DEPTH, N_EXP, 2 * D_FF // PAIR_TILE, LANES, 2).transpose(0, 1, 2, 4, 3)
    b_up_p = b_up_p.reshape(DEPTH, N_EXP, 1, 2 * D_FF)
    w_down_b = w_down.astype(BF16)
    b_down_r = b_down.reshape(DEPTH, N_EXP, 1, D_MODEL)
    kr_cache = jnp.pad(cache_mla_krope, ((0, 0), (0, 0), (0, 0), (NOPE_C, LANES - QK_C)))
    zeros_state = jnp.zeros((2 * BATCH, G_B * P_B), F32)

    new_k, new_v, new_ckv, new_kr, new_sre, new_sim = [], [], [], [], [], []
    for l in range(DEPTH):
        mod = mods[l]
        proj = inproj(x, norm_mix_g[l], mod, w_in_r[l])
        qa, ka, qc, kc, vc, ckvn = prep(proj, tab, gqa[l], gka[l], q_a_norm_c[l].reshape(1, Q_LORA),
                                         kv_a_norm_c[l].reshape(1, KV_LORA), gqc[l], gkc[l], wuq[l], wukv[l])
        kc_cache, vc_cache = cache_keys(cache_mla_ckv[:, l].reshape(DEC_BATCH * PAST_LEN, KV_LORA),
                                        kr_cache[:, l].reshape(DEC_BATCH * PAST_LEN, LANES), gkc[l], wukv[l])
        oa_ctx = attn_a_ctx(qa, ka, proj, sink[l])
        oa_lat = attn_a_lat(qa, ka, proj, cache_attn_k[:, l].reshape(DEC_BATCH, PAST_LEN, LANES),
                            cache_attn_v[:, l].reshape(DEC_BATCH, PAST_LEN, LANES), sink[l])
        oa = jnp.concatenate([oa_ctx, oa_lat], axis=0)
        oc = jnp.concatenate([attn_c_ctx(qc, kc, vc), attn_c_lat(qc, kc, vc, kc_cache, vc_cache)], axis=0)
        wb, wc, a_re, a_im = _s5_params(ssm_lam_re[l], ssm_lam_im[l], ssm_log_dt[l], ssm_b_re[l], ssm_b_im[l],
                                        ssm_c_re[l], ssm_c_im[l])
        ub = proj[:, C_UB:C_UB + D_B]
        a8 = lambda a: jnp.broadcast_to(a[:, None, :], (2, SUBLANES, G_B * P_B))
        a_mixed = lambda a: jnp.repeat(a, SUBLANES // 2, axis=0)[None]
        y_ctx, f_re, f_im = s5_scan(_to_time_major(ub[:N_CTX].reshape(BATCH, SEQ, D_B)), wb, wc, a8(a_re), a8(a_im),
                                    zeros_state, zeros_state, mixed=False)
        h0 = lambda s: s[:, l].transpose(1, 0, 2, 3).reshape(2 * DEC_BATCH, G_B * P_B)
        y_lat, _, _ = s5_scan(_to_time_major(ub[N_CTX:].reshape(DEC_BATCH, DEC_SEQ, D_B)), wb, wc,
                              a_mixed(a_re), a_mixed(a_im), h0(state_ssm_re), h0(state_ssm_im), mixed=True)
        y = jnp.concatenate([_from_time_major(y_ctx, BATCH).reshape(N_CTX, D_B),
                             _from_time_major(y_lat, DEC_BATCH).reshape(N_LAT, D_B)], axis=0)
        ob = s5_finish(proj, y, ssm_d[l].reshape(1, D_B), w_glu[l].astype(BF16), b_glu[l].reshape(1, D_B))
        x1, h2 = merge(x, proj, oa, ob, oc, mod, norm_ffn_g[l].reshape(1, D_MODEL), w_br_a[l].astype(BF16),
                       w_br_b[l].astype(BF16), w_br_c[l].astype(BF16), w_out[l].astype(BF16))
        idx, rank, wgt, cnt = router(h2, w_router_p[l], b_router_p[l])
        counts = cnt[0, :N_EXP].astype(jnp.int32)
        padded = (counts + BLK - 1) // BLK * BLK
        pad_end = jnp.cumsum(padded)
        pad_start = pad_end - padded
        dest = (pad_start[idx[:, :TOP_K]] + rank[:, :TOP_K]).reshape(-1).astype(jnp.int32)
        block_start = jnp.arange(N_BLOCKS, dtype=jnp.int32) * BLK
        block_e = jnp.minimum(jnp.sum(pad_end[None, :] <= block_start[:, None], axis=1), N_EXP - 1)
        xs = dispatch(dest, h2, jnp.zeros((BUF_LEN, D_MODEL), F32))
        ys = experts(block_e.astype(jnp.int32), xs, w_up_p[l], b_up_p[l], w_down_b[l], b_down_r[l])
        x = combine(dest, ys, x1, wgt, mod)

        new_k.append(ka[:N_CTX].reshape(BATCH, SEQ, HKV_A, HD_A))
        new_v.append(proj[:N_CTX, C_VA:C_VA + LANES].reshape(BATCH, SEQ, HKV_A, HD_A))
        new_ckv.append(ckvn[:N_CTX].reshape(BATCH, SEQ, KV_LORA))
        new_kr.append(proj[:N_CTX, C_KR + NOPE_C:C_KR + QK_C].reshape(BATCH, SEQ, ROPE_C))
        new_sre.append(f_re.reshape(2, BATCH, G_B, P_B).transpose(1, 0, 2, 3))
        new_sim.append(f_im.reshape(2, BATCH, G_B, P_B).transpose(1, 0, 2, 3))

    y_prompt = x[:N_CTX].reshape(BATCH, SEQ, D_MODEL)
    y_sample = x[N_CTX:].reshape(DEC_BATCH, DEC_SEQ, D_MODEL)
    return (y_prompt, y_sample, jnp.stack(new_k, axis=1), jnp.stack(new_v, axis=1), jnp.stack(new_ckv, axis=1),
            jnp.stack(new_kr, axis=1), jnp.stack(new_sre, axis=1), jnp.stack(new_sim, axis=1))
```

```python
import functools
import math

import jax
import jax.numpy as jnp
from jax import lax
from jax.experimental import pallas as pl
from jax.experimental.pallas import tpu as pltpu

D_MODEL = 1024
BATCH = 32
SEQ = 256
DEPTH = 2
DEC_BATCH = 4
DEC_SEQ = 2048
PAST_LEN = 256
GRID_W = 64
ROPE_BASE = 10000.0
EPS = 1e-6
NEG_INF = -1e30
BLK = 128
HQ_A, HKV_A, HD_A = 8, 2, 64
G_A = HQ_A // HKV_A
WINDOW = 128
D_B, GS_B, P_B = 512, 16, 64
G_B = D_B // GS_B
H_C, Q_LORA, KV_LORA, NOPE_C, ROPE_C, V_C = 8, 256, 128, 64, 32, 64
QK_C = NOPE_C + ROPE_C
N_EXP, TOP_K, D_FF = 32, 4, 1024
SWIGLU_ALPHA, SWIGLU_LIMIT = 1.702, 7.0

N_CTX = BATCH * SEQ
N_LAT = DEC_BATCH * DEC_SEQ
T_ALL = N_CTX + N_LAT
N_GROUPS = 8

LANES = 128
SUBLANES = 8

C_GL = 0
C_QA = 3 * D_MODEL
C_UB = C_QA + HQ_A * LANES
C_CQ = C_UB + D_B
C_KA = C_CQ + Q_LORA
C_VA = C_KA + LANES
C_CKV = C_VA + LANES
C_KR = C_CKV + LANES
N_PROJ = C_KR + LANES

N_ASSIGN = T_ALL * TOP_K
N_BLOCKS = N_ASSIGN // BLK + N_EXP
BUF_LEN = N_BLOCKS * BLK

F32 = jnp.float32
BF16 = jnp.bfloat16
VMEM_LIMIT = 56 * 1024 * 1024


def _cp(sem, vmem=VMEM_LIMIT):
    return pltpu.CompilerParams(dimension_semantics=sem, vmem_limit_bytes=vmem)


def _group_of_tile(i, tm):
    n_ctx_tiles = N_CTX // tm
    per_batch = DEC_SEQ // tm
    return jnp.where(i < n_ctx_tiles, 0, 1 + (i - n_ctx_tiles) // per_batch)


def _sigmoid(x):
    return 1.0 / (1.0 + jnp.exp(-x))


def _adaln_kernel(c_ref, w_ref, b_ref, o_ref):
    c = c_ref[...]
    s = c * _sigmoid(c)
    o_ref[...] = jnp.dot(s.astype(BF16), w_ref[...].astype(BF16), preferred_element_type=F32) + b_ref[...]


def adaln(cvecs, w_ada, b_ada):
    tn = 1536
    out = pl.pallas_call(
        _adaln_kernel,
        out_shape=jax.ShapeDtypeStruct((DEPTH, N_GROUPS, 6 * D_MODEL), F32),
        grid=(DEPTH, 6 * D_MODEL // tn),
        in_specs=[
            pl.BlockSpec((N_GROUPS, D_MODEL), lambda l, j: (0, 0)),
            pl.BlockSpec((None, D_MODEL, tn), lambda l, j: (l, 0, j)),
            pl.BlockSpec((None, 1, tn), lambda l, j: (l, 0, j)),
        ],
        out_specs=pl.BlockSpec((None, N_GROUPS, tn), lambda l, j: (l, 0, j)),
        compiler_params=_cp(("arbitrary", "arbitrary")),
        name="adaln",
    )(cvecs, w_ada, b_ada.reshape(DEPTH, 1, 6 * D_MODEL))
    return out.reshape(DEPTH, N_GROUPS, 6, D_MODEL)


def _rms(x, g):
    ms = jnp.mean(x * x, axis=-1, keepdims=True)
    return x * lax.rsqrt(ms + EPS) * g


def _inproj_kernel(x_ref, g_ref, mod_ref, w_ref, o_ref):
    h = _rms(x_ref[...], g_ref[...])
    h = h * (1.0 + mod_ref[1:2, :]) + mod_ref[0:1, :]
    o_ref[...] = jnp.dot(h.astype(BF16), w_ref[...], preferred_element_type=F32)


def inproj(x, g, mod, w):
    tm, tn = 512, 1792
    return pl.pallas_call(
        _inproj_kernel,
        out_shape=jax.ShapeDtypeStruct((T_ALL, N_PROJ), F32),
        grid=(N_PROJ // tn, T_ALL // tm),
        in_specs=[
            pl.BlockSpec((tm, D_MODEL), lambda j, i: (i, 0)),
            pl.BlockSpec((1, D_MODEL), lambda j, i: (0, 0)),
            pl.BlockSpec((None, 6, D_MODEL), lambda j, i: (_group_of_tile(i, tm), 0, 0)),
            pl.BlockSpec((D_MODEL, tn), lambda j, i: (0, j)),
        ],
        out_specs=pl.BlockSpec((tm, tn), lambda j, i: (i, j)),
        compiler_params=_cp(("arbitrary", "arbitrary")),
        name="inproj",
    )(x, g.reshape(1, D_MODEL), mod, w)


def _rope(slab, cos, sin_lo, sin_hi, half):
    up = pltpu.roll(slab, LANES - half, axis=1)
    dn = pltpu.roll(slab, half, axis=1)
    return slab * cos + up * sin_lo + dn * sin_hi


def _slot_norm(slab, gain, n_real):
    ms = jnp.sum(slab * slab, axis=-1, keepdims=True) * (1.0 / n_real)
    return slab * lax.rsqrt(ms + EPS) * gain


def _mla_keys(ckvn, kr_blk, wukv_ref, gk, rope_c):
    kv = jnp.dot(ckvn.astype(BF16), wukv_ref[...], preferred_element_type=F32)
    ks = []
    for h in range(H_C):
        slab = kv[:, h * LANES:(h + 1) * LANES] + kr_blk
        slab = _slot_norm(slab, gk, QK_C)
        if rope_c is not None:
            slab = _rope(slab, *rope_c, ROPE_C // 4)
        ks.append(slab.astype(BF16))
    return jnp.concatenate(ks, axis=1), kv[:, H_C * LANES:].astype(BF16)


def _prep_kernel(qa_ref, cq_ref, ka_ref, ckv_ref, kr_ref, tab_ref,
                 gqa_ref, gka_ref, gcq_ref, gckv_ref, gqc_ref, gkc_ref, wuq_ref, wukv_ref,
                 qa_o, ka_o, qc_o, kc_o, vc_o, ckvn_o):
    tab = tab_ref[...]
    rope_a = (tab[:, 0:128], tab[:, 128:256], tab[:, 256:384])
    rope_c = (tab[:, 384:512], tab[:, 512:640], tab[:, 640:768])
    lane = lax.broadcasted_iota(jnp.int32, (1, LANES), 1)

    gqa = gqa_ref[...]
    qa = qa_ref[...]
    outs = []
    for h in range(HQ_A):
        slab = _slot_norm(qa[:, h * LANES:(h + 1) * LANES], gqa, HD_A)
        slab = _rope(slab, *rope_a, HD_A // 4) * (HD_A ** -0.5)
        outs.append(slab.astype(BF16))
    qa_o[...] = jnp.concatenate(outs, axis=1)

    ka = ka_ref[...]
    sq = ka * ka
    lo = lane < HD_A
    ms_lo = jnp.sum(jnp.where(lo, sq, 0.0), axis=-1, keepdims=True)
    ms_hi = jnp.sum(jnp.where(lo, 0.0, sq), axis=-1, keepdims=True)
    rs = jnp.where(lo, lax.rsqrt(ms_lo * (1.0 / HD_A) + EPS), lax.rsqrt(ms_hi * (1.0 / HD_A) + EPS))
    ka_o[...] = _rope(ka * rs * gka_ref[...], *rope_a, HD_A // 4)

    cqn = _rms(cq_ref[...], gcq_ref[...])
    q = jnp.dot(cqn.astype(BF16), wuq_ref[...], preferred_element_type=F32)
    gqc = gqc_ref[...]
    outs = []
    for h in range(H_C):
        slab = _slot_norm(q[:, h * LANES:(h + 1) * LANES], gqc, QK_C)
        slab = _rope(slab, *rope_c, ROPE_C // 4) * (QK_C ** -0.5)
        outs.append(slab.astype(BF16))
    qc_o[...] = jnp.concatenate(outs, axis=1)

    ckvn = _rms(ckv_ref[...], gckv_ref[...])
    ckvn_o[...] = ckvn
    kc, vc = _mla_keys(ckvn, kr_ref[...], wukv_ref, gkc_ref[...], rope_c)
    kc_o[...] = kc
    vc_o[...] = vc


def prep(proj, tab, gqa, gka, gcq, gckv, gqc, gkc, wuq, wukv):
    tm = 256
    n_ctx_tiles = N_CTX // tm
    per_batch = DEC_SEQ // tm

    def tab_map(i):
        return (jnp.where(i < n_ctx_tiles, 0, per_batch + (i - n_ctx_tiles) % per_batch), 0)

    def col(width, off):
        return pl.BlockSpec((tm, width), lambda i: (i, off // width))

    def full(shape):
        return pl.BlockSpec(shape, lambda i: (0,) * len(shape))

    def row_out(width):
        return pl.BlockSpec((tm, width), lambda i: (i, 0))

    return pl.pallas_call(
        _prep_kernel,
        out_shape=(
            jax.ShapeDtypeStruct((T_ALL, HQ_A * LANES), BF16),
            jax.ShapeDtypeStruct((T_ALL, LANES), F32),
            jax.ShapeDtypeStruct((T_ALL, H_C * LANES), BF16),
            jax.ShapeDtypeStruct((T_ALL, H_C * LANES), BF16),
            jax.ShapeDtypeStruct((T_ALL, H_C * V_C), BF16),
            jax.ShapeDtypeStruct((T_ALL, KV_LORA), F32),
        ),
        grid=(T_ALL // tm,),
        in_specs=[
            col(HQ_A * LANES, C_QA), col(Q_LORA, C_CQ), col(LANES, C_KA), col(LANES, C_CKV), col(LANES, C_KR),
            pl.BlockSpec((tm, 6 * LANES), tab_map),
            full((1, LANES)), full((1, LANES)), full((1, Q_LORA)), full((1, KV_LORA)),
            full((1, LANES)), full((1, LANES)),
            full((Q_LORA, H_C * LANES)), full((KV_LORA, H_C * LANES + H_C * V_C)),
        ],
        out_specs=(row_out(HQ_A * LANES), row_out(LANES), row_out(H_C * LANES), row_out(H_C * LANES),
                   row_out(H_C * V_C), row_out(KV_LORA)),
        compiler_params=_cp(("arbitrary",)),
        name="prep",
    )(proj, proj, proj, proj, proj, tab, gqa, gka, gcq, gckv, gqc, gkc, wuq, wukv)


def _cachekeys_kernel(ckv_ref, kr_ref, gkc_ref, wukv_ref, kc_o, vc_o):
    kc, vc = _mla_keys(ckv_ref[...], kr_ref[...], wukv_ref, gkc_ref[...], None)
    kc_o[...] = kc
    vc_o[...] = vc


def cache_keys(ckv, kr_blk, gkc, wukv):
    r = ckv.shape[0]
    tm = 256
    return pl.pallas_call(
        _cachekeys_kernel,
        out_shape=(jax.ShapeDtypeStruct((r, H_C * LANES), BF16), jax.ShapeDtypeStruct((r, H_C * V_C), BF16)),
        grid=(r // tm,),
        in_specs=[
            pl.BlockSpec((tm, LANES), lambda i: (i, 0)),
            pl.BlockSpec((tm, LANES), lambda i: (i, 0)),
            pl.BlockSpec((1, LANES), lambda i: (0, 0)),
            pl.BlockSpec((KV_LORA, H_C * LANES + H_C * V_C), lambda i: (0, 0)),
        ],
        out_specs=(pl.BlockSpec((tm, H_C * LANES), lambda i: (i, 0)), pl.BlockSpec((tm, H_C * V_C), lambda i: (i, 0))),
        compiler_params=_cp(("arbitrary",)),
        name="cache_keys",
    )(ckv, kr_blk, gkc, wukv)


def _attn_body(q_ref, segs, sink_ref, o_ref, *, n_heads, k_slot, v_slab, v_half, tq, band_qi=None):
    outs = []
    for h in range(n_heads):
        qh = q_ref[:, h * LANES:(h + 1) * LANES]
        scores = []
        for k_ref, _, off in segs:
            kh = k_ref[:, k_slot(h) * LANES:(k_slot(h) + 1) * LANES].astype(BF16)
            s = lax.dot_general(qh, kh, (((1,), (1,)), ((), ())), preferred_element_type=F32)
            if off is not None:
                tk = s.shape[1]
                blk = band_qi + off
                q_pos = band_qi * tq + lax.broadcasted_iota(jnp.int32, (tq, tk), 0)
                k_pos = blk * tk + lax.broadcasted_iota(jnp.int32, (tq, tk), 1)
                ok = (jnp.abs(k_pos - q_pos) <= WINDOW) & (blk >= 0) & (blk < DEC_SEQ // tk)
                s = jnp.where(ok, s, NEG_INF)
            scores.append(s)
        m = scores[0].max(axis=-1, keepdims=True)
        for s in scores[1:]:
            m = jnp.maximum(m, s.max(axis=-1, keepdims=True))
        if sink_ref is not None:
            sink = sink_ref[h:h + 1, 0:1]
            m = jnp.maximum(m, sink)
            denom = jnp.exp(sink - m)
        else:
            denom = jnp.zeros_like(m)
        acc = None
        for s, (_, v_ref, _) in zip(scores, segs):
            p = jnp.exp(s - m)
            denom = denom + p.sum(axis=-1, keepdims=True)
            vs = v_ref[:, v_slab(h) * LANES:(v_slab(h) + 1) * LANES].astype(BF16)
            pv = jnp.dot(p.astype(BF16), vs, preferred_element_type=F32)
            acc = pv if acc is None else acc + pv
        half = v_half(h)
        outs.append(acc[:, half * 64:(half + 1) * 64] * (1.0 / denom))
    o_ref[...] = jnp.concatenate(outs, axis=1).astype(o_ref.dtype)


_A_CFG = dict(n_heads=HQ_A, k_slot=lambda h: 0, v_slab=lambda h: 0, v_half=lambda h: h // G_A)
_C_CFG = dict(n_heads=H_C, k_slot=lambda h: h, v_slab=lambda h: h // 2, v_half=lambda h: h % 2)


def _attn_a_ctx_kernel(q_ref, k_ref, v_ref, sink_ref, o_ref):
    _attn_body(q_ref, [(k_ref, v_ref, None)], sink_ref, o_ref, tq=SEQ, **_A_CFG)


def _attn_a_lat_kernel(q_ref, k0, k1, k2, v0, v1, v2, kc_ref, vc_ref, sink_ref, o_ref):
    qi = pl.program_id(1)
    segs = [(k0, v0, -1), (k1, v1, 0), (k2, v2, 1), (kc_ref, vc_ref, None)]
    _attn_body(q_ref, segs, sink_ref, o_ref, tq=BLK, band_qi=qi, **_A_CFG)


def _attn_c_ctx_kernel(q_ref, k_ref, v_ref, o_ref):
    _attn_body(q_ref, [(k_ref, v_ref, None)], None, o_ref, tq=SEQ, **_C_CFG)


def _attn_c_lat_kernel(q_ref, k_ref, v_ref, kc_ref, vc_ref, o_ref):
    _attn_body(q_ref, [(k_ref, v_ref, None), (kc_ref, vc_ref, None)], None, o_ref, tq=256, **_C_CFG)


def attn_a_ctx(qa, ka, proj, sink):
    return pl.pallas_call(
        _attn_a_ctx_kernel,
        out_shape=jax.ShapeDtypeStruct((N_CTX, HQ_A * HD_A), BF16),
        grid=(BATCH,),
        in_specs=[
            pl.BlockSpec((SEQ, HQ_A * LANES), lambda b: (b, 0)),
            pl.BlockSpec((SEQ, LANES), lambda b: (b, 0)),
            pl.BlockSpec((SEQ, LANES), lambda b: (b, C_VA // LANES)),
            pl.BlockSpec((SUBLANES, LANES), lambda b: (0, 0)),
        ],
        out_specs=pl.BlockSpec((SEQ, HQ_A * HD_A), lambda b: (b, 0)),
        compiler_params=_cp(("arbitrary",)),
        name="attn_a_ctx",
    )(qa, ka, proj, sink)


def attn_a_lat(qa, ka, proj, k_cache, v_cache, sink):
    nb = DEC_SEQ // BLK
    base = N_CTX // BLK

    def band(off, colblk):
        return pl.BlockSpec((BLK, LANES), lambda b, i: (base + b * nb + jnp.clip(i + off, 0, nb - 1), colblk))

    return pl.pallas_call(
        _attn_a_lat_kernel,
        out_shape=jax.ShapeDtypeStruct((N_LAT, HQ_A * HD_A), BF16),
        grid=(DEC_BATCH, nb),
        in_specs=[
            pl.BlockSpec((BLK, HQ_A * LANES), lambda b, i: (base + b * nb + i, 0)),
            band(-1, 0), band(0, 0), band(1, 0),
            band(-1, C_VA // LANES), band(0, C_VA // LANES), band(1, C_VA // LANES),
            pl.BlockSpec((None, PAST_LEN, LANES), lambda b, i: (b, 0, 0)),
            pl.BlockSpec((None, PAST_LEN, LANES), lambda b, i: (b, 0, 0)),
            pl.BlockSpec((SUBLANES, LANES), lambda b, i: (0, 0)),
        ],
        out_specs=pl.BlockSpec((BLK, HQ_A * HD_A), lambda b, i: (b * nb + i, 0)),
        compiler_params=_cp(("arbitrary", "arbitrary")),
        name="attn_a_lat",
    )(qa, ka, ka, ka, proj, proj, proj, k_cache, v_cache, sink)


def attn_c_ctx(qc, kc, vc):
    return pl.pallas_call(
        _attn_c_ctx_kernel,
        out_shape=jax.ShapeDtypeStruct((N_CTX, H_C * V_C), BF16),
        grid=(BATCH,),
        in_specs=[
            pl.BlockSpec((SEQ, H_C * LANES), lambda b: (b, 0)),
            pl.BlockSpec((SEQ, H_C * LANES), lambda b: (b, 0)),
            pl.BlockSpec((SEQ, H_C * V_C), lambda b: (b, 0)),
        ],
        out_specs=pl.BlockSpec((SEQ, H_C * V_C), lambda b: (b, 0)),
        compiler_params=_cp(("arbitrary",)),
        name="attn_c_ctx",
    )(qc, kc, vc)


def attn_c_lat(qc, kc, vc, kc_cache, vc_cache):
    tq = 256
    nq = DEC_SEQ // tq
    qbase = N_CTX // tq
    kbase = N_CTX // DEC_SEQ
    return pl.pallas_call(
        _attn_c_lat_kernel,
        out_shape=jax.ShapeDtypeStruct((N_LAT, H_C * V_C), BF16),
        grid=(DEC_BATCH, nq),
        in_specs=[
            pl.BlockSpec((tq, H_C * LANES), lambda b, i: (qbase + b * nq + i, 0)),
            pl.BlockSpec((DEC_SEQ, H_C * LANES), lambda b, i: (kbase + b, 0)),
            pl.BlockSpec((DEC_SEQ, H_C * V_C), lambda b, i: (kbase + b, 0)),
            pl.BlockSpec((PAST_LEN, H_C * LANES), lambda b, i: (b, 0)),
            pl.BlockSpec((PAST_LEN, H_C * V_C), lambda b, i: (b, 0)),
        ],
        out_specs=pl.BlockSpec((tq, H_C * V_C), lambda b, i: (b * nq + i, 0)),
        compiler_params=_cp(("arbitrary", "arbitrary")),
        name="attn_c_lat",
    )(qc, kc, vc, kc_cache, vc_cache)


S5_TC = 256
S5_CH = 512
S5_UNROLL = 8


def _s5_kernel(u_ref, wb_ref, wc_ref, are_ref, aim_ref, h0re_ref, h0im_ref,
               y_ref, fre_ref, fim_ref, bre, bim, sre, sim, *, mixed):
    k = pl.program_id(2)

    @pl.when(k == 0)
    def _():
        sre[...] = h0re_ref[...]
        sim[...] = h0im_ref[...]

    rows = S5_TC * SUBLANES
    u2 = u_ref[...].reshape(rows, LANES).astype(BF16)
    if mixed:
        fwd_row = (lax.broadcasted_iota(jnp.int32, (rows, 1), 0) % SUBLANES) < (SUBLANES // 2)

    def proj_b(lo, hi):
        r = jnp.dot(u2, wb_ref[0, :, lo:hi], preferred_element_type=F32)
        if mixed:
            r = jnp.where(fwd_row, r, jnp.dot(u2, wb_ref[1, :, lo:hi], preferred_element_type=F32))
        return r.reshape(S5_TC, SUBLANES, S5_CH)

    bre[...] = proj_b(0, S5_CH)
    bim[...] = proj_b(S5_CH, 2 * S5_CH)

    a_re = are_ref[...]
    a_im = aim_ref[...]

    def outer(i, carry):
        sr, si = carry
        for j in range(S5_UNROLL):
            t = i * S5_UNROLL + j
            nr = a_re * sr - a_im * si + bre[t]
            ni = a_re * si + a_im * sr + bim[t]
            bre[t] = nr
            bim[t] = ni
            sr, si = nr, ni
        return sr, si

    sr, si = lax.fori_loop(0, S5_TC // S5_UNROLL, outer, (sre[...], sim[...]))
    sre[...] = sr
    sim[...] = si
    fre_ref[...] = sr
    fim_ref[...] = si

    xr = bre[...].reshape(rows, S5_CH).astype(BF16)
    xi = bim[...].reshape(rows, S5_CH).astype(BF16)

    def proj_c(d):
        return (jnp.dot(xr, wc_ref[d, 0:S5_CH, :], preferred_element_type=F32)
                + jnp.dot(xi, wc_ref[d, S5_CH:, :], preferred_element_type=F32))

    y = proj_c(0)
    if mixed:
        y = jnp.where(fwd_row, y, proj_c(1))
    y_ref[...] = y.reshape(S5_TC, SUBLANES, LANES)


def s5_scan(seqs, wb, wc, a_re, a_im, h0_re, h0_im, *, mixed):
    length, ns, _ = seqs.shape
    nsb = ns // SUBLANES
    nj = D_B // LANES
    nk = length // S5_TC
    nd = 2 if mixed else 1
    per_dir = nsb // 2 if not mixed else 1

    def dmap(sb):
        return 0 if mixed else sb // per_dir

    kern = functools.partial(_s5_kernel, mixed=mixed)
    return pl.pallas_call(
        kern,
        out_shape=(
            jax.ShapeDtypeStruct((length, ns, D_B), F32),
            jax.ShapeDtypeStruct((ns, G_B * P_B), F32),
            jax.ShapeDtypeStruct((ns, G_B * P_B), F32),
        ),
        grid=(nsb, nj, nk),
        in_specs=[
            pl.BlockSpec((S5_TC, SUBLANES, LANES), lambda sb, j, k: (k, sb, j)),
            pl.BlockSpec((nd, None, LANES, 2 * S5_CH), lambda sb, j, k: (dmap(sb), j, 0, 0)),
            pl.BlockSpec((nd, None, 2 * S5_CH, LANES), lambda sb, j, k: (dmap(sb), j, 0, 0)),
            pl.BlockSpec((None, SUBLANES, S5_CH), lambda sb, j, k: (dmap(sb), 0, j)),
            pl.BlockSpec((None, SUBLANES, S5_CH), lambda sb, j, k: (dmap(sb), 0, j)),
            pl.BlockSpec((SUBLANES, S5_CH), lambda sb, j, k: (sb, j)),
            pl.BlockSpec((SUBLANES, S5_CH), lambda sb, j, k: (sb, j)),
        ],
        out_specs=(
            pl.BlockSpec((S5_TC, SUBLANES, LANES), lambda sb, j, k: (k, sb, j)),
            pl.BlockSpec((SUBLANES, S5_CH), lambda sb, j, k: (sb, j)),
            pl.BlockSpec((SUBLANES, S5_CH), lambda sb, j, k: (sb, j)),
        ),
        scratch_shapes=[
            pltpu.VMEM((S5_TC, SUBLANES, S5_CH), F32), pltpu.VMEM((S5_TC, SUBLANES, S5_CH), F32),
            pltpu.VMEM((SUBLANES, S5_CH), F32), pltpu.VMEM((SUBLANES, S5_CH), F32),
        ],
        compiler_params=_cp(("arbitrary", "arbitrary", "arbitrary")),
        name="s5_scan_mixed" if mixed else "s5_scan",
    )(seqs, wb, wc, a_re, a_im, h0_re, h0_im)


def _s5fin_kernel(u_ref, y_ref, d_ref, w_ref, b_ref, o_ref):
    y = d_ref[...] * u_ref[...] + y_ref[...]
    y = 0.5 * y * (1.0 + jnp.tanh(math.sqrt(2.0 / math.pi) * (y + 0.044715 * (y * y * y))))
    z = jnp.dot(y.astype(BF16), w_ref[...], preferred_element_type=F32) + b_ref[...]
    o_ref[...] = (y * _sigmoid(z)).astype(o_ref.dtype)


def s5_finish(proj, y, d, w_glu, b_glu):
    tm = 512
    return pl.pallas_call(
        _s5fin_kernel,
        out_shape=jax.ShapeDtypeStruct((T_ALL, D_B), BF16),
        grid=(T_ALL // tm,),
        in_specs=[
            pl.BlockSpec((tm, D_B), lambda i: (i, C_UB // D_B)),
            pl.BlockSpec((tm, D_B), lambda i: (i, 0)),
            pl.BlockSpec((1, D_B), lambda i: (0, 0)),
            pl.BlockSpec((D_B, D_B), lambda i: (0, 0)),
            pl.BlockSpec((1, D_B), lambda i: (0, 0)),
        ],
        out_specs=pl.BlockSpec((tm, D_B), lambda i: (i, 0)),
        compiler_params=_cp(("arbitrary",)),
        name="s5_finish",
    )(proj, y, d, w_glu, b_glu)


def _merge_kernel(x_ref, gl_ref, oa_ref, ob_ref, oc_ref, mod_ref, g_ref, wa_ref, wb_ref, wc_ref, wo_ref,
                  x1_ref, h2_ref):
    m = None
    for br, (o_ref, w_ref) in enumerate(((oa_ref, wa_ref), (ob_ref, wb_ref), (oc_ref, wc_ref))):
        gate = _sigmoid(gl_ref[:, br * D_MODEL:(br + 1) * D_MODEL])
        t = gate * jnp.dot(o_ref[...], w_ref[...], preferred_element_type=F32)
        m = t if m is None else m + t
    x1 = x_ref[...] + mod_ref[2:3, :] * jnp.dot(m.astype(BF16), wo_ref[...], preferred_element_type=F32)
    x1_ref[...] = x1
    h2 = _rms(x1, g_ref[...])
    h2_ref[...] = h2 * (1.0 + mod_ref[4:5, :]) + mod_ref[3:4, :]


def merge(x, proj, oa, ob, oc, mod, g, wa, wb, wc, wo):
    tm = 256
    full = lambda shape: pl.BlockSpec(shape, lambda i: (0,) * len(shape))
    return pl.pallas_call(
        _merge_kernel,
        out_shape=(jax.ShapeDtypeStruct((T_ALL, D_MODEL), F32), jax.ShapeDtypeStruct((T_ALL, D_MODEL), F32)),
        grid=(T_ALL // tm,),
        in_specs=[
            pl.BlockSpec((tm, D_MODEL), lambda i: (i, 0)),
            pl.BlockSpec((tm, 3 * D_MODEL), lambda i: (i, 0)),
            pl.BlockSpec((tm, 512), lambda i: (i, 0)),
            pl.BlockSpec((tm, 512), lambda i: (i, 0)),
            pl.BlockSpec((tm, 512), lambda i: (i, 0)),
            pl.BlockSpec((None, 6, D_MODEL), lambda i: (_group_of_tile(i, tm), 0, 0)),
            full((1, D_MODEL)),
            full((512, D_MODEL)), full((512, D_MODEL)), full((512, D_MODEL)), full((D_MODEL, D_MODEL)),
        ],
        out_specs=(pl.BlockSpec((tm, D_MODEL), lambda i: (i, 0)), pl.BlockSpec((tm, D_MODEL), lambda i: (i, 0))),
        compiler_params=_cp(("arbitrary",)),
        name="merge",
    )(x, proj, oa, ob, oc, mod, g, wa, wb, wc, wo)


ROUTE_TM = 256


def _router_kernel(h_ref, w_ref, b_ref, idx_ref, rank_ref, wgt_ref, cnt_ref, cnt_acc):
    i = pl.program_id(0)

    @pl.when(i == 0)
    def _():
        cnt_acc[...] = jnp.zeros_like(cnt_acc)

    logits = jnp.dot(h_ref[...], w_ref[...], preferred_element_type=F32,
                     precision=lax.Precision.HIGHEST) + b_ref[...]
    lane_i = lax.broadcasted_iota(jnp.int32, (ROUTE_TM, LANES), 1)
    lane = lane_i.astype(F32)
    r_i = lax.broadcasted_iota(jnp.int32, (ROUTE_TM, ROUTE_TM), 0)
    c_i = lax.broadcasted_iota(jnp.int32, (ROUTE_TM, ROUTE_TM), 1)
    earlier = jnp.where(c_i < r_i, 1.0, 0.0).astype(BF16)

    cnt = cnt_acc[...]
    idx_out = jnp.zeros((ROUTE_TM, LANES), F32)
    rank_out = jnp.zeros((ROUTE_TM, LANES), F32)
    val_out = jnp.zeros((ROUTE_TM, LANES), F32)
    v0 = None
    esum = None
    for k in range(TOP_K):
        m = logits.max(axis=-1, keepdims=True)
        sel = jnp.min(jnp.where(logits == m, lane, float(LANES)), axis=-1, keepdims=True)
        hit = lane == sel
        logits = jnp.where(hit, -jnp.inf, logits)
        onehot = jnp.where(hit, 1.0, 0.0)
        within = jnp.dot(earlier, onehot.astype(BF16), preferred_element_type=F32)
        rank = jnp.sum(onehot * (within + cnt), axis=-1, keepdims=True)
        cnt = cnt + jnp.sum(onehot, axis=0, keepdims=True)
        if k == 0:
            v0 = m
        e = jnp.exp(m - v0)
        esum = e if esum is None else esum + e
        idx_out = jnp.where(lane_i == k, sel, idx_out)
        rank_out = jnp.where(lane_i == k, rank, rank_out)
        val_out = jnp.where(lane_i == k, e, val_out)
    cnt_acc[...] = cnt
    cnt_ref[...] = cnt
    idx_ref[...] = idx_out.astype(jnp.int32)
    rank_ref[...] = rank_out.astype(jnp.int32)
    wgt_ref[...] = val_out * (1.0 / esum)


def router(h2, w_router, b_router):
    tile = lambda: pl.BlockSpec((ROUTE_TM, LANES), lambda i: (i, 0))
    return pl.pallas_call(
        _router_kernel,
        out_shape=(
            jax.ShapeDtypeStruct((T_ALL, LANES), jnp.int32),
            jax.ShapeDtypeStruct((T_ALL, LANES), jnp.int32),
            jax.ShapeDtypeStruct((T_ALL, LANES), F32),
            jax.ShapeDtypeStruct((1, LANES), F32),
        ),
        grid=(T_ALL // ROUTE_TM,),
        in_specs=[
            pl.BlockSpec((ROUTE_TM, D_MODEL), lambda i: (i, 0)),
            pl.BlockSpec((D_MODEL, LANES), lambda i: (0, 0)),
            pl.BlockSpec((1, LANES), lambda i: (0, 0)),
        ],
        out_specs=(tile(), tile(), tile(), pl.BlockSpec((1, LANES), lambda i: (0, 0))),
        scratch_shapes=[pltpu.VMEM((1, LANES), F32)],
        compiler_params=_cp(("arbitrary",)),
        name="router",
    )(h2, w_router, b_router)


PLAN_UNROLL = 8
N_DUMP = BLK
assert TOP_K == 4 and T_ALL & (T_ALL - 1) == 0


def _invert_kernel(dest_ref, fill_hbm, inv_ref):
    pltpu.sync_copy(fill_hbm, inv_ref)
    tok_step = PLAN_UNROLL // TOP_K

    def put(i, c):
        for j in range(PLAN_UNROLL):
            inv_ref[dest_ref[i * PLAN_UNROLL + j]] = i * tok_step + ((j % TOP_K) * T_ALL + j // TOP_K)
        return c

    lax.fori_loop(0, N_ASSIGN // PLAN_UNROLL, put, 0)


def invert_plan(dest):
    fill = N_ASSIGN + (jnp.arange(BUF_LEN, dtype=jnp.int32) & (N_DUMP - 1))
    return pl.pallas_call(
        _invert_kernel,
        out_shape=jax.ShapeDtypeStruct((BUF_LEN,), jnp.int32),
        in_specs=[pl.BlockSpec(memory_space=pltpu.SMEM), pl.BlockSpec(memory_space=pl.ANY)],
        out_specs=pl.BlockSpec(memory_space=pltpu.SMEM),
        name="invert_plan",
    )(dest, fill)


PAIR_TILE = 2 * LANES


def _wperm_kernel(w_ref, s_ref, o_ref):
    w = w_ref[...].astype(BF16)
    for j in range(w.shape[1] // PAIR_TILE):
        cols = slice(j * PAIR_TILE, (j + 1) * PAIR_TILE)
        o_ref[:, cols] = jnp.dot(w[:, cols], s_ref[...], preferred_element_type=F32).astype(BF16)


def permute_w_up(w_up):
    tn = 512
    r = lax.broadcasted_iota(jnp.int32, (PAIR_TILE, PAIR_TILE), 0)
    c = lax.broadcasted_iota(jnp.int32, (PAIR_TILE, PAIR_TILE), 1)
    sel = (r == jnp.where(c < LANES, 2 * c, 2 * (c - LANES) + 1)).astype(BF16)
    w = w_up.reshape(DEPTH * N_EXP, D_MODEL, 2 * D_FF)
    out = pl.pallas_call(
        _wperm_kernel,
        out_shape=jax.ShapeDtypeStruct(w.shape, BF16),
        grid=(DEPTH * N_EXP, 2 * D_FF // tn),
        in_specs=[
            pl.BlockSpec((None, D_MODEL, tn), lambda e, j: (e, 0, j)),
            pl.BlockSpec((PAIR_TILE, PAIR_TILE), lambda e, j: (0, 0)),
        ],
        out_specs=pl.BlockSpec((None, D_MODEL, tn), lambda e, j: (e, 0, j)),
        compiler_params=_cp(("arbitrary", "arbitrary")),
        name="permute_w_up",
    )(w, sel)
    return out.reshape(DEPTH, N_EXP, D_MODEL, 2 * D_FF)


def _expert_kernel(be_ref, inv_ref, h_hbm, wu_ref, bu_ref, wd_ref, bd_ref, comb_hbm, xbuf, ybuf, gsem, ssem):
    del be_ref
    i = pl.program_id(0)
    n = pl.num_programs(0)
    slot = i % 2
    other = 1 - slot

    def gather(block, s):
        base = block * BLK
        for r in range(BLK):
            tok = inv_ref[base + r] & (T_ALL - 1)
            pltpu.make_async_copy(h_hbm.at[pl.ds(tok, 1), :], xbuf.at[s, pl.ds(r, 1), :], gsem.at[s]).start()

    def scatter(block, s):
        base = block * BLK
        for r in range(BLK):
            a = inv_ref[base + r]
            pltpu.make_async_copy(ybuf.at[s, pl.ds(r, 1), :], comb_hbm.at[pl.ds(a, 1), :], ssem.at[s]).start()

    def wait_block(sem):
        pltpu.make_async_copy(h_hbm.at[pl.ds(0, BLK), :], xbuf.at[0], sem).wait()

    @pl.when(i == 0)
    def _():
        ybuf[...] = jnp.zeros_like(ybuf)
        gather(0, 0)
        scatter(0, 0)

    wait_block(gsem.at[slot])
    wait_block(ssem.at[slot])
    gather(jnp.minimum(i + 1, n - 1), other)
    scatter(jnp.maximum(i - 1, 0), other)

    x = xbuf[slot].astype(BF16)
    h = jnp.dot(x, wu_ref[...], preferred_element_type=F32) + bu_ref[...]
    acts = []
    for j in range(2 * D_FF // PAIR_TILE):
        glu = jnp.minimum(h[:, j * PAIR_TILE:j * PAIR_TILE + LANES], SWIGLU_LIMIT)
        lin = jnp.clip(h[:, j * PAIR_TILE + LANES:(j + 1) * PAIR_TILE], -SWIGLU_LIMIT, SWIGLU_LIMIT)
        acts.append((glu * _sigmoid(SWIGLU_ALPHA * glu) * (lin + 1.0)).astype(BF16))
    act = jnp.concatenate(acts, axis=1)
    ybuf[slot] = jnp.dot(act, wd_ref[...], preferred_element_type=F32) + bd_ref[...]

    @pl.when(i == n - 1)
    def _():
        wait_block(ssem.at[other])
        wait_block(gsem.at[other])
        scatter(i, slot)
        wait_block(ssem.at[slot])


def experts(block_e, inv, h2, wu, bu, wd, bd):
    return pl.pallas_call(
        _expert_kernel,
        out_shape=jax.ShapeDtypeStruct((N_ASSIGN + N_DUMP, D_MODEL), F32),
        grid_spec=pltpu.PrefetchScalarGridSpec(
            num_scalar_prefetch=2,
            grid=(N_BLOCKS,),
            in_specs=[
                pl.BlockSpec(memory_space=pl.ANY),
                pl.BlockSpec((None, D_MODEL, 2 * D_FF), lambda i, be, inv: (be[i], 0, 0)),
                pl.BlockSpec((None, 1, 2 * D_FF), lambda i, be, inv: (be[i], 0, 0)),
                pl.BlockSpec((None, D_FF, D_MODEL), lambda i, be, inv: (be[i], 0, 0)),
                pl.BlockSpec((None, 1, D_MODEL), lambda i, be, inv: (be[i], 0, 0)),
            ],
            out_specs=pl.BlockSpec(memory_space=pl.ANY),
            scratch_shapes=[
                pltpu.VMEM((2, BLK, D_MODEL), F32), pltpu.VMEM((2, BLK, D_MODEL), F32),
                pltpu.SemaphoreType.DMA((2,)), pltpu.SemaphoreType.DMA((2,)),
            ],
        ),
        compiler_params=_cp(("arbitrary",)),
        name="experts",
    )(block_e, inv, h2, wu, bu, wd, bd)


COMB_TM = 256


def _combine_kernel(c0_ref, c1_ref, c2_ref, c3_ref, x_ref, w_ref, mod_ref, o_ref):
    w = w_ref[...]
    acc = None
    for k, c_ref in enumerate((c0_ref, c1_ref, c2_ref, c3_ref)):
        t = w[:, k:k + 1] * c_ref[...]
        acc = t if acc is None else acc + t
    o_ref[...] = x_ref[...] + mod_ref[5:6, :] * acc


def combine(comb, x1, wgt, mod):
    tiles = T_ALL // COMB_TM
    kth = lambda k: pl.BlockSpec((COMB_TM, D_MODEL), lambda i: (k * tiles + i, 0))
    return pl.pallas_call(
        _combine_kernel,
        out_shape=jax.ShapeDtypeStruct((T_ALL, D_MODEL), F32),
        grid=(tiles,),
        in_specs=[
            kth(0), kth(1), kth(2), kth(3),
            pl.BlockSpec((COMB_TM, D_MODEL), lambda i: (i, 0)),
            pl.BlockSpec((COMB_TM, LANES), lambda i: (i, 0)),
            pl.BlockSpec((None, 6, D_MODEL), lambda i: (_group_of_tile(i, COMB_TM), 0, 0)),
        ],
        out_specs=pl.BlockSpec((COMB_TM, D_MODEL), lambda i: (i, 0)),
        compiler_params=_cp(("arbitrary",)),
        name="combine",
    )(comb, comb, comb, comb, x1, wgt, mod)


def _rope_table():
    pos = jnp.arange(DEC_SEQ)
    row = (pos // GRID_W).astype(F32)[:, None]
    col = (pos % GRID_W).astype(F32)[:, None]

    def parts(rot_dim):
        nf = rot_dim // 4
        inv = ROPE_BASE ** (-jnp.arange(nf, dtype=F32) / nf)
        cr, sr, cc, sc = jnp.cos(row * inv), jnp.sin(row * inv), jnp.cos(col * inv), jnp.sin(col * inv)
        z = jnp.zeros_like(sr)
        cos = jnp.concatenate([cr, cr, cc, cc], axis=1)
        lo = jnp.concatenate([-sr, z, -sc, z], axis=1)
        hi = jnp.concatenate([z, sr, z, sc], axis=1)
        return cos, lo, hi

    ca, la, ha = (jnp.tile(t, (1, 2)) for t in parts(HD_A))
    cc, lc, hc = parts(ROPE_C)
    pad = lambda t, fill: jnp.pad(t, ((0, 0), (NOPE_C, LANES - QK_C)), constant_values=fill)
    lat = jnp.concatenate([ca, la, ha, pad(cc, 1.0), pad(lc, 0.0), pad(hc, 0.0)], axis=1)
    ones, zeros = jnp.ones((DEC_SEQ, LANES), F32), jnp.zeros((DEC_SEQ, LANES), F32)
    ident = jnp.concatenate([ones, zeros, zeros, ones, zeros, zeros], axis=1)
    return jnp.concatenate([ident, lat], axis=0)


def _arrange_w_in(w_in):
    o = 0
    parts = {}
    for name, n in (("qa", HQ_A * HD_A), ("ka", HKV_A * HD_A), ("va", HKV_A * HD_A), ("ub", D_B), ("cq", Q_LORA),
                    ("ckv", KV_LORA), ("kr", ROPE_C), ("gl", 3 * D_MODEL)):
        parts[name] = w_in[..., o:o + n]
        o += n
    qa = parts["qa"].reshape(DEPTH, D_MODEL, HKV_A, G_A, 1, HD_A)
    eye = jnp.eye(HKV_A, dtype=F32).reshape(1, 1, HKV_A, 1, HKV_A, 1)
    qa_slots = (qa * eye).reshape(DEPTH, D_MODEL, HQ_A * LANES)
    kr = jnp.pad(parts["kr"], ((0, 0), (0, 0), (NOPE_C, LANES - QK_C)))
    w = jnp.concatenate([parts["gl"], qa_slots, parts["ub"], parts["cq"], parts["ka"], parts["va"], parts["ckv"], kr],
                        axis=-1)
    return w.astype(BF16)


def _s5_params(lam_re, lam_im, log_dt, b_re, b_im, c_re, c_im):
    dt = jnp.exp(log_dt)[..., None]
    decay = jnp.exp(lam_re * dt)
    ab_re, ab_im = decay * jnp.cos(lam_im * dt), decay * jnp.sin(lam_im * dt)
    den = lam_re * lam_re + lam_im * lam_im
    f_re = ((ab_re - 1) * lam_re + ab_im * lam_im) / den
    f_im = (ab_im * lam_re - (ab_re - 1) * lam_im) / den
    bb_re = f_re[..., None] * b_re - f_im[..., None] * b_im
    bb_im = f_re[..., None] * b_im + f_im[..., None] * b_re
    nj, gpb = D_B // LANES, LANES // GS_B
    eye = jnp.eye(gpb, dtype=F32)

    def blockdiag_b(bb):
        t = bb.transpose(0, 1, 3, 2).reshape(2, nj, gpb, GS_B, P_B)
        return (t[:, :, :, :, None, :] * eye[None, None, :, None, :, None]).reshape(2, nj, LANES, gpb * P_B)

    def blockdiag_c(cc):
        t = cc.transpose(0, 1, 3, 2).reshape(2, nj, gpb, P_B, GS_B)
        return (t[:, :, :, :, None, :] * eye[None, None, :, None, :, None]).reshape(2, nj, gpb * P_B, LANES)

    wb = jnp.concatenate([blockdiag_b(bb_re), blockdiag_b(bb_im)], axis=-1).astype(BF16)
    wc = jnp.concatenate([blockdiag_c(c_re), -blockdiag_c(c_im)], axis=-2).astype(BF16)
    return wb, wc, ab_re.reshape(2, G_B * P_B), ab_im.reshape(2, G_B * P_B)


def _to_time_major(u):
    f = u.transpose(1, 0, 2)
    return jnp.concatenate([f, f[::-1]], axis=1)


def _from_time_major(y, b):
    return (y[:, :b] + y[::-1, b:]).transpose(1, 0, 2)


def kernel(x_prompt, x_sample, c, cache_attn_k, cache_attn_v, cache_mla_ckv, cache_mla_krope, state_ssm_re, state_ssm_im, c_ctx, w_ada, b_ada, norm_mix_g, norm_ffn_g, w_in, q_norm_a, k_norm_a, sink_a, q_a_norm_c, kv_a_norm_c, w_uq_c, w_ukv_c, q_norm_c, k_norm_c, ssm_lam_re, ssm_lam_im, ssm_log_dt, ssm_b_re, ssm_b_im, ssm_c_re, ssm_c_im, ssm_d, w_glu, b_glu, w_br_a, w_br_b, w_br_c, w_out, w_router, b_router, w_up, b_up, w_down, b_down):
    x = jnp.concatenate([x_prompt.reshape(N_CTX, D_MODEL), x_sample.reshape(N_LAT, D_MODEL)], axis=0)
    cvecs = jnp.concatenate([c_ctx[None], c, jnp.zeros((N_GROUPS - 1 - DEC_BATCH, D_MODEL), F32)], axis=0)
    mods = adaln(cvecs, w_ada, b_ada)

    tab = _rope_table()
    w_in_r = _arrange_w_in(w_in)
    pad_slot = lambda g: jnp.pad(g, ((0, 0), (0, LANES - QK_C))).reshape(DEPTH, 1, LANES)
    gqa = jnp.tile(q_norm_a, (1, 2)).reshape(DEPTH, 1, LANES)
    gka = jnp.tile(k_norm_a, (1, 2)).reshape(DEPTH, 1, LANES)
    gqc, gkc = pad_slot(q_norm_c), pad_slot(k_norm_c)
    wuq = jnp.pad(w_uq_c.reshape(DEPTH, Q_LORA, H_C, QK_C), ((0, 0), (0, 0), (0, 0), (0, LANES - QK_C)))
    wuq = wuq.reshape(DEPTH, Q_LORA, H_C * LANES).astype(BF16)
    wukv4 = w_ukv_c.reshape(DEPTH, KV_LORA, H_C, NOPE_C + V_C)
    wuk = jnp.pad(wukv4[..., :NOPE_C], ((0, 0), (0, 0), (0, 0), (0, LANES - NOPE_C))).reshape(DEPTH, KV_LORA, H_C * LANES)
    wuv = wukv4[..., NOPE_C:].reshape(DEPTH, KV_LORA, H_C * V_C)
    wukv = jnp.concatenate([wuk, wuv], axis=-1).astype(BF16)
    sink = jnp.broadcast_to(sink_a[:, :, None], (DEPTH, HQ_A, LANES))
    w_router_p = jnp.pad(w_router, ((0, 0), (0, 0), (0, LANES - N_EXP)))
    b_router_p = jnp.pad(b_router, ((0, 0), (0, LANES - N_EXP)), constant_values=-jnp.inf).reshape(DEPTH, 1, LANES)
    w_up_p = permute_w_up(w_up)
    b_up_p = b_up.reshape(DEPTH, N_EXP, 2 * D_FF // PAIR_TILE, LANES, 2).transpose(0, 1, 2, 4, 3)
    b_up_p = b_up_p.reshape(DEPTH, N_EXP, 1, 2 * D_FF)
    w_down_b = w_down.astype(BF16)
    b_down_r = b_down.reshape(DEPTH, N_EXP, 1, D_MODEL)
    kr_cache = jnp.pad(cache_mla_krope, ((0, 0), (0, 0), (0, 0), (NOPE_C, LANES - QK_C)))
    zeros_state = jnp.zeros((2 * BATCH, G_B * P_B), F32)

    new_k, new_v, new_ckv, new_kr, new_sre, new_sim = [], [], [], [], [], []
    for l in range(DEPTH):
        mod = mods[l]
        proj = inproj(x, norm_mix_g[l], mod, w_in_r[l])
        qa, ka, qc, kc, vc, ckvn = prep(proj, tab, gqa[l], gka[l], q_a_norm_c[l].reshape(1, Q_LORA),
                                         kv_a_norm_c[l].reshape(1, KV_LORA), gqc[l], gkc[l], wuq[l], wukv[l])
        kc_cache, vc_cache = cache_keys(cache_mla_ckv[:, l].reshape(DEC_BATCH * PAST_LEN, KV_LORA),
                                        kr_cache[:, l].reshape(DEC_BATCH * PAST_LEN, LANES), gkc[l], wukv[l])
        oa_ctx = attn_a_ctx(qa, ka, proj, sink[l])
        oa_lat = attn_a_lat(qa, ka, proj, cache_attn_k[:, l].reshape(DEC_BATCH, PAST_LEN, LANES),
                            cache_attn_v[:, l].reshape(DEC_BATCH, PAST_LEN, LANES), sink[l])
        oa = jnp.concatenate([oa_ctx, oa_lat], axis=0)
        oc = jnp.concatenate([attn_c_ctx(qc, kc, vc), attn_c_lat(qc, kc, vc, kc_cache, vc_cache)], axis=0)
        wb, wc, a_re, a_im = _s5_params(ssm_lam_re[l], ssm_lam_im[l], ssm_log_dt[l], ssm_b_re[l], ssm_b_im[l],
                                        ssm_c_re[l], ssm_c_im[l])
        ub = proj[:, C_UB:C_UB + D_B]
        a8 = lambda a: jnp.broadcast_to(a[:, None, :], (2, SUBLANES, G_B * P_B))
        a_mixed = lambda a: jnp.repeat(a, SUBLANES // 2, axis=0)[None]
        y_ctx, f_re, f_im = s5_scan(_to_time_major(ub[:N_CTX].reshape(BATCH, SEQ, D_B)), wb, wc, a8(a_re), a8(a_im),
                                    zeros_state, zeros_state, mixed=False)
        h0 = lambda s: s[:, l].transpose(1, 0, 2, 3).reshape(2 * DEC_BATCH, G_B * P_B)
        y_lat, _, _ = s5_scan(_to_time_major(ub[N_CTX:].reshape(DEC_BATCH, DEC_SEQ, D_B)), wb, wc,
                              a_mixed(a_re), a_mixed(a_im), h0(state_ssm_re), h0(state_ssm_im), mixed=True)
        y = jnp.concatenate([_from_time_major(y_ctx, BATCH).reshape(N_CTX, D_B),
                             _from_time_major(y_lat, DEC_BATCH).reshape(N_LAT, D_B)], axis=0)
        ob = s5_finish(proj, y, ssm_d[l].reshape(1, D_B), w_glu[l].astype(BF16), b_glu[l].reshape(1, D_B))
        x1, h2 = merge(x, proj, oa, ob, oc, mod, norm_ffn_g[l].reshape(1, D_MODEL), w_br_a[l].astype(BF16),
                       w_br_b[l].astype(BF16), w_br_c[l].astype(BF16), w_out[l].astype(BF16))
        idx, rank, wgt, cnt = router(h2, w_router_p[l], b_router_p[l])
        counts = cnt[0, :N_EXP].astype(jnp.int32)
        padded = (counts + BLK - 1) // BLK * BLK
        pad_end = jnp.cumsum(padded)
        pad_start = pad_end - padded
        dest = (pad_start[idx[:, :TOP_K]] + rank[:, :TOP_K]).reshape(-1).astype(jnp.int32)
        block_start = jnp.arange(N_BLOCKS, dtype=jnp.int32) * BLK
        block_e = jnp.minimum(jnp.sum(pad_end[None, :] <= block_start[:, None], axis=1), N_EXP - 1)
        comb = experts(block_e.astype(jnp.int32), invert_plan(dest), h2, w_up_p[l], b_up_p[l], w_down_b[l],
                       b_down_r[l])
        x = combine(comb, x1, wgt, mod)

        new_k.append(ka[:N_CTX].reshape(BATCH, SEQ, HKV_A, HD_A))
        new_v.append(proj[:N_CTX, C_VA:C_VA + LANES].reshape(BATCH, SEQ, HKV_A, HD_A))
        new_ckv.append(ckvn[:N_CTX].reshape(BATCH, SEQ, KV_LORA))
        new_kr.append(proj[:N_CTX, C_KR + NOPE_C:C_KR + QK_C].reshape(BATCH, SEQ, ROPE_C))
        new_sre.append(f_re.reshape(2, BATCH, G_B, P_B).transpose(1, 0, 2, 3))
        new_sim.append(f_im.reshape(2, BATCH, G_B, P_B).transpose(1, 0, 2, 3))

    y_prompt = x[:N_CTX].reshape(BATCH, SEQ, D_MODEL)
    y_sample = x[N_CTX:].reshape(DEC_BATCH, DEC_SEQ, D_MODEL)
    return (y_prompt, y_sample, jnp.stack(new_k, axis=1), jnp.stack(new_v, axis=1), jnp.stack(new_ckv, axis=1),
            jnp.stack(new_kr, axis=1), jnp.stack(new_sre, axis=1), jnp.stack(new_sim, axis=1))
```

```python
import functools
import math

import jax
import jax.numpy as jnp
from jax import lax
from jax.experimental import pallas as pl
from jax.experimental.pallas import tpu as pltpu

D_MODEL = 1024
BATCH = 32
SEQ = 256
DEPTH = 2
DEC_BATCH = 4
DEC_SEQ = 2048
PAST_LEN = 256
GRID_W = 64
ROPE_BASE = 10000.0
EPS = 1e-6
NEG_INF = -1e30
BLK = 128
HQ_A, HKV_A, HD_A = 8, 2, 64
G_A = HQ_A // HKV_A
WINDOW = 128
D_B, GS_B, P_B = 512, 16, 64
G_B = D_B // GS_B
H_C, Q_LORA, KV_LORA, NOPE_C, ROPE_C, V_C = 8, 256, 128, 64, 32, 64
QK_C = NOPE_C + ROPE_C
N_EXP, TOP_K, D_FF = 32, 4, 1024
SWIGLU_ALPHA, SWIGLU_LIMIT = 1.702, 7.0

N_CTX = BATCH * SEQ
N_LAT = DEC_BATCH * DEC_SEQ
T_ALL = N_CTX + N_LAT
N_GROUPS = 8

LANES = 128
SUBLANES = 8

C_GL = 0
C_QA = 3 * D_MODEL
C_UB = C_QA + HQ_A * LANES
C_CQ = C_UB + D_B
C_KA = C_CQ + Q_LORA
C_VA = C_KA + LANES
C_CKV = C_VA + LANES
C_KR = C_CKV + LANES
N_PROJ = C_KR + LANES

N_ASSIGN = T_ALL * TOP_K
N_BLOCKS = N_ASSIGN // BLK + N_EXP
BUF_LEN = N_BLOCKS * BLK

F32 = jnp.float32
BF16 = jnp.bfloat16
VMEM_LIMIT = 56 * 1024 * 1024


def _cp(sem, vmem=VMEM_LIMIT):
    return pltpu.CompilerParams(dimension_semantics=sem, vmem_limit_bytes=vmem)


def _group_of_tile(i, tm):
    n_ctx_tiles = N_CTX // tm
    per_batch = DEC_SEQ // tm
    return jnp.where(i < n_ctx_tiles, 0, 1 + (i - n_ctx_tiles) // per_batch)


def _sigmoid(x):
    return 1.0 / (1.0 + jnp.exp(-x))


def _adaln_kernel(c_ref, w_ref, b_ref, o_ref):
    c = c_ref[...]
    s = c * _sigmoid(c)
    o_ref[...] = jnp.dot(s.astype(BF16), w_ref[...].astype(BF16), preferred_element_type=F32) + b_ref[...]


def adaln(cvecs, w_ada, b_ada):
    tn = 1536
    out = pl.pallas_call(
        _adaln_kernel,
        out_shape=jax.ShapeDtypeStruct((DEPTH, N_GROUPS, 6 * D_MODEL), F32),
        grid=(DEPTH, 6 * D_MODEL // tn),
        in_specs=[
            pl.BlockSpec((N_GROUPS, D_MODEL), lambda l, j: (0, 0)),
            pl.BlockSpec((None, D_MODEL, tn), lambda l, j: (l, 0, j)),
            pl.BlockSpec((None, 1, tn), lambda l, j: (l, 0, j)),
        ],
        out_specs=pl.BlockSpec((None, N_GROUPS, tn), lambda l, j: (l, 0, j)),
        compiler_params=_cp(("arbitrary", "arbitrary")),
        name="adaln",
    )(cvecs, w_ada, b_ada.reshape(DEPTH, 1, 6 * D_MODEL))
    return out.reshape(DEPTH, N_GROUPS, 6, D_MODEL)


def _rms(x, g):
    ms = jnp.mean(x * x, axis=-1, keepdims=True)
    return x * lax.rsqrt(ms + EPS) * g


def _inproj_kernel(x_ref, g_ref, mod_ref, w_ref, o_ref):
    h = _rms(x_ref[...], g_ref[...])
    h = h * (1.0 + mod_ref[1:2, :]) + mod_ref[0:1, :]
    o_ref[...] = jnp.dot(h.astype(BF16), w_ref[...], preferred_element_type=F32)


def inproj(x, g, mod, w):
    tm, tn = 512, 1792
    return pl.pallas_call(
        _inproj_kernel,
        out_shape=jax.ShapeDtypeStruct((T_ALL, N_PROJ), F32),
        grid=(N_PROJ // tn, T_ALL // tm),
        in_specs=[
            pl.BlockSpec((tm, D_MODEL), lambda j, i: (i, 0)),
            pl.BlockSpec((1, D_MODEL), lambda j, i: (0, 0)),
            pl.BlockSpec((None, 6, D_MODEL), lambda j, i: (_group_of_tile(i, tm), 0, 0)),
            pl.BlockSpec((D_MODEL, tn), lambda j, i: (0, j)),
        ],
        out_specs=pl.BlockSpec((tm, tn), lambda j, i: (i, j)),
        compiler_params=_cp(("arbitrary", "arbitrary")),
        name="inproj",
    )(x, g.reshape(1, D_MODEL), mod, w)


def _rope(slab, cos, sin_lo, sin_hi, half):
    up = pltpu.roll(slab, LANES - half, axis=1)
    dn = pltpu.roll(slab, half, axis=1)
    return slab * cos + up * sin_lo + dn * sin_hi


def _slot_norm(slab, gain, n_real):
    ms = jnp.sum(slab * slab, axis=-1, keepdims=True) * (1.0 / n_real)
    return slab * lax.rsqrt(ms + EPS) * gain


def _mla_keys(ckvn, kr_blk, wukv_ref, gk, rope_c):
    kv = jnp.dot(ckvn.astype(BF16), wukv_ref[...], preferred_element_type=F32)
    ks = []
    for h in range(H_C):
        slab = kv[:, h * LANES:(h + 1) * LANES] + kr_blk
        slab = _slot_norm(slab, gk, QK_C)
        if rope_c is not None:
            slab = _rope(slab, *rope_c, ROPE_C // 4)
        ks.append(slab.astype(BF16))
    return jnp.concatenate(ks, axis=1), kv[:, H_C * LANES:].astype(BF16)


PREP_TM = 256


def _prep_kernel(qa_ref, cq_ref, ka_ref, ckv_ref, kr_ref, tab_ref,
                 gqa_ref, gka_ref, gcq_ref, gckv_ref, gqc_ref, gkc_ref, wuq_ref, wukv_ref,
                 qa_o, ka_o, qc_o, kc_o, vc_o, ckvn_o):
    def body(rotate):
        if rotate:
            tab = tab_ref[...]
            rope_a = (tab[:, 0:128], tab[:, 128:256], tab[:, 256:384])
            rope_c = (tab[:, 384:512], tab[:, 512:640], tab[:, 640:768])
            rot_a = lambda x: _rope(x, *rope_a, HD_A // 4)
            rot_c = lambda x: _rope(x, *rope_c, ROPE_C // 4)
        else:
            rope_c = None
            rot_a = rot_c = lambda x: x
        lane = lax.broadcasted_iota(jnp.int32, (1, LANES), 1)

        gqa = gqa_ref[...]
        qa = qa_ref[...]
        outs = []
        for h in range(HQ_A):
            slab = _slot_norm(qa[:, h * LANES:(h + 1) * LANES], gqa, HD_A)
            outs.append((rot_a(slab) * (HD_A ** -0.5)).astype(BF16))
        qa_o[...] = jnp.concatenate(outs, axis=1)

        ka = ka_ref[...]
        sq = ka * ka
        lo = lane < HD_A
        ms_lo = jnp.sum(jnp.where(lo, sq, 0.0), axis=-1, keepdims=True)
        ms_hi = jnp.sum(jnp.where(lo, 0.0, sq), axis=-1, keepdims=True)
        rs = jnp.where(lo, lax.rsqrt(ms_lo * (1.0 / HD_A) + EPS), lax.rsqrt(ms_hi * (1.0 / HD_A) + EPS))
        ka_o[...] = rot_a(ka * rs * gka_ref[...])

        cqn = _rms(cq_ref[...], gcq_ref[...])
        q = jnp.dot(cqn.astype(BF16), wuq_ref[...], preferred_element_type=F32)
        gqc = gqc_ref[...]
        outs = []
        for h in range(H_C):
            slab = _slot_norm(q[:, h * LANES:(h + 1) * LANES], gqc, QK_C)
            outs.append((rot_c(slab) * (QK_C ** -0.5)).astype(BF16))
        qc_o[...] = jnp.concatenate(outs, axis=1)

        ckvn = _rms(ckv_ref[...], gckv_ref[...])
        ckvn_o[...] = ckvn
        kc, vc = _mla_keys(ckvn, kr_ref[...], wukv_ref, gkc_ref[...], rope_c)
        kc_o[...] = kc
        vc_o[...] = vc

    is_latent = pl.program_id(0) >= N_CTX // PREP_TM
    pl.when(is_latent)(lambda: body(True))
    pl.when(jnp.logical_not(is_latent))(lambda: body(False))


def prep(proj, tab, gqa, gka, gcq, gckv, gqc, gkc, wuq, wukv):
    tm = PREP_TM
    n_ctx_tiles = N_CTX // tm
    per_batch = DEC_SEQ // tm

    def tab_map(i):
        return (jnp.where(i < n_ctx_tiles, 0, per_batch + (i - n_ctx_tiles) % per_batch), 0)

    def col(width, off):
        return pl.BlockSpec((tm, width), lambda i: (i, off // width))

    def full(shape):
        return pl.BlockSpec(shape, lambda i: (0,) * len(shape))

    def row_out(width):
        return pl.BlockSpec((tm, width), lambda i: (i, 0))

    return pl.pallas_call(
        _prep_kernel,
        out_shape=(
            jax.ShapeDtypeStruct((T_ALL, HQ_A * LANES), BF16),
            jax.ShapeDtypeStruct((T_ALL, LANES), F32),
            jax.ShapeDtypeStruct((T_ALL, H_C * LANES), BF16),
            jax.ShapeDtypeStruct((T_ALL, H_C * LANES), BF16),
            jax.ShapeDtypeStruct((T_ALL, H_C * V_C), BF16),
            jax.ShapeDtypeStruct((T_ALL, KV_LORA), F32),
        ),
        grid=(T_ALL // tm,),
        in_specs=[
            col(HQ_A * LANES, C_QA), col(Q_LORA, C_CQ), col(LANES, C_KA), col(LANES, C_CKV), col(LANES, C_KR),
            pl.BlockSpec((tm, 6 * LANES), tab_map),
            full((1, LANES)), full((1, LANES)), full((1, Q_LORA)), full((1, KV_LORA)),
            full((1, LANES)), full((1, LANES)),
            full((Q_LORA, H_C * LANES)), full((KV_LORA, H_C * LANES + H_C * V_C)),
        ],
        out_specs=(row_out(HQ_A * LANES), row_out(LANES), row_out(H_C * LANES), row_out(H_C * LANES),
                   row_out(H_C * V_C), row_out(KV_LORA)),
        compiler_params=_cp(("arbitrary",)),
        name="prep",
    )(proj, proj, proj, proj, proj, tab, gqa, gka, gcq, gckv, gqc, gkc, wuq, wukv)


def _cachekeys_kernel(ckv_ref, kr_ref, gkc_ref, wukv_ref, kc_o, vc_o):
    kc, vc = _mla_keys(ckv_ref[...], kr_ref[...], wukv_ref, gkc_ref[...], None)
    kc_o[...] = kc
    vc_o[...] = vc


def cache_keys(ckv, kr_blk, gkc, wukv):
    r = ckv.shape[0]
    tm = 256
    return pl.pallas_call(
        _cachekeys_kernel,
        out_shape=(jax.ShapeDtypeStruct((r, H_C * LANES), BF16), jax.ShapeDtypeStruct((r, H_C * V_C), BF16)),
        grid=(r // tm,),
        in_specs=[
            pl.BlockSpec((tm, LANES), lambda i: (i, 0)),
            pl.BlockSpec((tm, LANES), lambda i: (i, 0)),
            pl.BlockSpec((1, LANES), lambda i: (0, 0)),
            pl.BlockSpec((KV_LORA, H_C * LANES + H_C * V_C), lambda i: (0, 0)),
        ],
        out_specs=(pl.BlockSpec((tm, H_C * LANES), lambda i: (i, 0)), pl.BlockSpec((tm, H_C * V_C), lambda i: (i, 0))),
        compiler_params=_cp(("arbitrary",)),
        name="cache_keys",
    )(ckv, kr_blk, gkc, wukv)


def _attn_body(q_ref, segs, sink_ref, o_ref, *, n_heads, k_slot, v_slab, v_half, tq, band_qi=None):
    outs = []
    for h in range(n_heads):
        qh = q_ref[:, h * LANES:(h + 1) * LANES]
        scores = []
        for k_ref, _, off in segs:
            kh = k_ref[:, k_slot(h) * LANES:(k_slot(h) + 1) * LANES].astype(BF16)
            s = lax.dot_general(qh, kh, (((1,), (1,)), ((), ())), preferred_element_type=F32)
            if off is not None:
                tk = s.shape[1]
                blk = band_qi + off
                q_pos = band_qi * tq + lax.broadcasted_iota(jnp.int32, (tq, tk), 0)
                k_pos = blk * tk + lax.broadcasted_iota(jnp.int32, (tq, tk), 1)
                ok = (jnp.abs(k_pos - q_pos) <= WINDOW) & (blk >= 0) & (blk < DEC_SEQ // tk)
                s = jnp.where(ok, s, NEG_INF)
            scores.append(s)
        m = scores[0].max(axis=-1, keepdims=True)
        for s in scores[1:]:
            m = jnp.maximum(m, s.max(axis=-1, keepdims=True))
        if sink_ref is not None:
            sink = sink_ref[h:h + 1, 0:1]
            m = jnp.maximum(m, sink)
            denom = jnp.exp(sink - m)
        else:
            denom = jnp.zeros_like(m)
        acc = None
        for s, (_, v_ref, _) in zip(scores, segs):
            p = jnp.exp(s - m)
            denom = denom + p.sum(axis=-1, keepdims=True)
            vs = v_ref[:, v_slab(h) * LANES:(v_slab(h) + 1) * LANES].astype(BF16)
            pv = jnp.dot(p.astype(BF16), vs, preferred_element_type=F32)
            acc = pv if acc is None else acc + pv
        half = v_half(h)
        outs.append(acc[:, half * 64:(half + 1) * 64] * (1.0 / denom))
    o_ref[...] = jnp.concatenate(outs, axis=1).astype(o_ref.dtype)


_C_CFG = dict(n_heads=H_C, k_slot=lambda h: h, v_slab=lambda h: h // 2, v_half=lambda h: h % 2)


def _attn_a_body(q_ref, segs, sink_ref, o_ref, *, tq, band_qi=None):
    rows = HQ_A * tq
    q = jnp.concatenate([q_ref[:, h * LANES:(h + 1) * LANES] for h in range(HQ_A)], axis=0)
    sink = jnp.concatenate([jnp.broadcast_to(sink_ref[h:h + 1, 0:1], (tq, 1)) for h in range(HQ_A)], axis=0)
    scores = []
    for k_ref, _, off in segs:
        s = lax.dot_general(q, k_ref[...].astype(BF16), (((1,), (1,)), ((), ())), preferred_element_type=F32)
        if off is not None:
            tk = s.shape[1]
            blk = band_qi + off
            q_pos = band_qi * tq + (lax.broadcasted_iota(jnp.int32, (rows, tk), 0) & (tq - 1))
            k_pos = blk * tk + lax.broadcasted_iota(jnp.int32, (rows, tk), 1)
            ok = (jnp.abs(k_pos - q_pos) <= WINDOW) & (blk >= 0) & (blk < DEC_SEQ // tk)
            s = jnp.where(ok, s, NEG_INF)
        scores.append(s)
    m = sink
    for s in scores:
        m = jnp.maximum(m, s.max(axis=-1, keepdims=True))
    denom = jnp.exp(sink - m)
    acc = None
    for s, (_, v_ref, _) in zip(scores, segs):
        p = jnp.exp(s - m)
        denom = denom + p.sum(axis=-1, keepdims=True)
        pv = jnp.dot(p.astype(BF16), v_ref[...].astype(BF16), preferred_element_type=F32)
        acc = pv if acc is None else acc + pv
    acc = acc * (1.0 / denom)
    outs = []
    for h in range(HQ_A):
        kv = h // G_A
        outs.append(acc[h * tq:(h + 1) * tq, kv * HD_A:(kv + 1) * HD_A])
    o_ref[...] = jnp.concatenate(outs, axis=1).astype(o_ref.dtype)


def _attn_a_ctx_kernel(q_ref, k_ref, v_ref, sink_ref, o_init, o_ref):
    del o_init
    _attn_a_body(q_ref, [(k_ref, v_ref, None)], sink_ref, o_ref, tq=SEQ)


def _attn_a_lat_kernel(q_ref, k0, k1, k2, v0, v1, v2, kc_ref, vc_ref, sink_ref, o_ctx, o_ref):
    del o_ctx
    qi = pl.program_id(1)
    segs = [(k0, v0, -1), (k1, v1, 0), (k2, v2, 1), (kc_ref, vc_ref, None)]
    _attn_a_body(q_ref, segs, sink_ref, o_ref, tq=BLK, band_qi=qi)


def _attn_c_ctx_kernel(q_ref, k_ref, v_ref, o_init, o_ref):
    del o_init
    _attn_body(q_ref, [(k_ref, v_ref, None)], None, o_ref, tq=SEQ, **_C_CFG)


def _attn_c_lat_kernel(q_ref, k_ref, v_ref, kc_ref, vc_ref, o_ctx, o_ref):
    del o_ctx
    _attn_body(q_ref, [(k_ref, v_ref, None), (kc_ref, vc_ref, None)], None, o_ref, tq=256, **_C_CFG)


def attn_a_ctx(qa, ka, proj, sink):
    return pl.pallas_call(
        _attn_a_ctx_kernel,
        out_shape=jax.ShapeDtypeStruct((T_ALL, HQ_A * HD_A), BF16),
        grid=(BATCH,),
        in_specs=[
            pl.BlockSpec((SEQ, HQ_A * LANES), lambda b: (b, 0)),
            pl.BlockSpec((SEQ, LANES), lambda b: (b, 0)),
            pl.BlockSpec((SEQ, LANES), lambda b: (b, C_VA // LANES)),
            pl.BlockSpec((SUBLANES, LANES), lambda b: (0, 0)),
            pl.BlockSpec(memory_space=pl.ANY),
        ],
        out_specs=pl.BlockSpec((SEQ, HQ_A * HD_A), lambda b: (b, 0)),
        input_output_aliases={4: 0},
        compiler_params=_cp(("arbitrary",)),
        name="attn_a_ctx",
    )(qa, ka, proj, sink, jnp.zeros((T_ALL, HQ_A * HD_A), BF16))


def attn_a_lat(qa, ka, proj, k_cache, v_cache, sink, o_all):
    nb = DEC_SEQ // BLK
    base = N_CTX // BLK

    def band(off, colblk):
        return pl.BlockSpec((BLK, LANES), lambda b, i: (base + b * nb + jnp.clip(i + off, 0, nb - 1), colblk))

    return pl.pallas_call(
        _attn_a_lat_kernel,
        out_shape=jax.ShapeDtypeStruct((T_ALL, HQ_A * HD_A), BF16),
        grid=(DEC_BATCH, nb),
        in_specs=[
            pl.BlockSpec((BLK, HQ_A * LANES), lambda b, i: (base + b * nb + i, 0)),
            band(-1, 0), band(0, 0), band(1, 0),
            band(-1, C_VA // LANES), band(0, C_VA // LANES), band(1, C_VA // LANES),
            pl.BlockSpec((None, PAST_LEN, LANES), lambda b, i: (b, 0, 0)),
            pl.BlockSpec((None, PAST_LEN, LANES), lambda b, i: (b, 0, 0)),
            pl.BlockSpec((SUBLANES, LANES), lambda b, i: (0, 0)),
            pl.BlockSpec(memory_space=pl.ANY),
        ],
        out_specs=pl.BlockSpec((BLK, HQ_A * HD_A), lambda b, i: (base + b * nb + i, 0)),
        input_output_aliases={10: 0},
        compiler_params=_cp(("arbitrary", "arbitrary")),
        name="attn_a_lat",
    )(qa, ka, ka, ka, proj, proj, proj, k_cache, v_cache, sink, o_all)


def attn_c_ctx(qc, kc, vc):
    return pl.pallas_call(
        _attn_c_ctx_kernel,
        out_shape=jax.ShapeDtypeStruct((T_ALL, H_C * V_C), BF16),
        grid=(BATCH,),
        in_specs=[
            pl.BlockSpec((SEQ, H_C * LANES), lambda b: (b, 0)),
            pl.BlockSpec((SEQ, H_C * LANES), lambda b: (b, 0)),
            pl.BlockSpec((SEQ, H_C * V_C), lambda b: (b, 0)),
            pl.BlockSpec(memory_space=pl.ANY),
        ],
        out_specs=pl.BlockSpec((SEQ, H_C * V_C), lambda b: (b, 0)),
        input_output_aliases={3: 0},
        compiler_params=_cp(("arbitrary",)),
        name="attn_c_ctx",
    )(qc, kc, vc, jnp.zeros((T_ALL, H_C * V_C), BF16))


def attn_c_lat(qc, kc, vc, kc_cache, vc_cache, o_all):
    tq = 256
    nq = DEC_SEQ // tq
    qbase = N_CTX // tq
    kbase = N_CTX // DEC_SEQ
    return pl.pallas_call(
        _attn_c_lat_kernel,
        out_shape=jax.ShapeDtypeStruct((T_ALL, H_C * V_C), BF16),
        grid=(DEC_BATCH, nq),
        in_specs=[
            pl.BlockSpec((tq, H_C * LANES), lambda b, i: (qbase + b * nq + i, 0)),
            pl.BlockSpec((DEC_SEQ, H_C * LANES), lambda b, i: (kbase + b, 0)),
            pl.BlockSpec((DEC_SEQ, H_C * V_C), lambda b, i: (kbase + b, 0)),
            pl.BlockSpec((PAST_LEN, H_C * LANES), lambda b, i: (b, 0)),
            pl.BlockSpec((PAST_LEN, H_C * V_C), lambda b, i: (b, 0)),
            pl.BlockSpec(memory_space=pl.ANY),
        ],
        out_specs=pl.BlockSpec((tq, H_C * V_C), lambda b, i: (qbase + b * nq + i, 0)),
        input_output_aliases={5: 0},
        compiler_params=_cp(("arbitrary", "arbitrary")),
        name="attn_c_lat",
    )(qc, kc, vc, kc_cache, vc_cache, o_all)


S5_TC = 256
S5_CH = 512
S5_UNROLL = 8


def _s5_kernel(u_ref, wb_ref, wc_ref, are_ref, aim_ref, h0re_ref, h0im_ref,
               y_ref, fre_ref, fim_ref, bre, bim, sre, sim, *, mixed):
    k = pl.program_id(2)

    @pl.when(k == 0)
    def _():
        sre[...] = h0re_ref[...]
        sim[...] = h0im_ref[...]

    rows = S5_TC * SUBLANES
    u2 = u_ref[...].reshape(rows, LANES).astype(BF16)
    if mixed:
        fwd_row = (lax.broadcasted_iota(jnp.int32, (rows, 1), 0) % SUBLANES) < (SUBLANES // 2)

    def proj_b(lo, hi):
        r = jnp.dot(u2, wb_ref[0, :, lo:hi], preferred_element_type=F32)
        if mixed:
            r = jnp.where(fwd_row, r, jnp.dot(u2, wb_ref[1, :, lo:hi], preferred_element_type=F32))
        return r.reshape(S5_TC, SUBLANES, S5_CH)

    bre[...] = proj_b(0, S5_CH)
    bim[...] = proj_b(S5_CH, 2 * S5_CH)

    a_re = are_ref[...]
    a_im = aim_ref[...]

    def outer(i, carry):
        sr, si = carry
        for j in range(S5_UNROLL):
            t = i * S5_UNROLL + j
            nr = a_re * sr - a_im * si + bre[t]
            ni = a_re * si + a_im * sr + bim[t]
            bre[t] = nr
            bim[t] = ni
            sr, si = nr, ni
        return sr, si

    sr, si = lax.fori_loop(0, S5_TC // S5_UNROLL, outer, (sre[...], sim[...]))
    sre[...] = sr
    sim[...] = si
    fre_ref[...] = sr
    fim_ref[...] = si

    xr = bre[...].reshape(rows, S5_CH).astype(BF16)
    xi = bim[...].reshape(rows, S5_CH).astype(BF16)

    def proj_c(d):
        return (jnp.dot(xr, wc_ref[d, 0:S5_CH, :], preferred_element_type=F32)
                + jnp.dot(xi, wc_ref[d, S5_CH:, :], preferred_element_type=F32))

    y = proj_c(0)
    if mixed:
        y = jnp.where(fwd_row, y, proj_c(1))
    y_ref[...] = y.reshape(S5_TC, SUBLANES, LANES)


def s5_scan(seqs, wb, wc, a_re, a_im, h0_re, h0_im, *, mixed):
    length, ns, _ = seqs.shape
    nsb = ns // SUBLANES
    nj = D_B // LANES
    nk = length // S5_TC
    nd = 2 if mixed else 1
    per_dir = nsb // 2 if not mixed else 1

    def dmap(sb):
        return 0 if mixed else sb // per_dir

    kern = functools.partial(_s5_kernel, mixed=mixed)
    return pl.pallas_call(
        kern,
        out_shape=(
            jax.ShapeDtypeStruct((length, ns, D_B), F32),
            jax.ShapeDtypeStruct((ns, G_B * P_B), F32),
            jax.ShapeDtypeStruct((ns, G_B * P_B), F32),
        ),
        grid=(nsb, nj, nk),
        in_specs=[
            pl.BlockSpec((S5_TC, SUBLANES, LANES), lambda sb, j, k: (k, sb, j)),
            pl.BlockSpec((nd, None, LANES, 2 * S5_CH), lambda sb, j, k: (dmap(sb), j, 0, 0)),
            pl.BlockSpec((nd, None, 2 * S5_CH, LANES), lambda sb, j, k: (dmap(sb), j, 0, 0)),
            pl.BlockSpec((None, SUBLANES, S5_CH), lambda sb, j, k: (dmap(sb), 0, j)),
            pl.BlockSpec((None, SUBLANES, S5_CH), lambda sb, j, k: (dmap(sb), 0, j)),
            pl.BlockSpec((SUBLANES, S5_CH), lambda sb, j, k: (sb, j)),
            pl.BlockSpec((SUBLANES, S5_CH), lambda sb, j, k: (sb, j)),
        ],
        out_specs=(
            pl.BlockSpec((S5_TC, SUBLANES, LANES), lambda sb, j, k: (k, sb, j)),
            pl.BlockSpec((SUBLANES, S5_CH), lambda sb, j, k: (sb, j)),
            pl.BlockSpec((SUBLANES, S5_CH), lambda sb, j, k: (sb, j)),
        ),
        scratch_shapes=[
            pltpu.VMEM((S5_TC, SUBLANES, S5_CH), F32), pltpu.VMEM((S5_TC, SUBLANES, S5_CH), F32),
            pltpu.VMEM((SUBLANES, S5_CH), F32), pltpu.VMEM((SUBLANES, S5_CH), F32),
        ],
        compiler_params=_cp(("arbitrary", "arbitrary", "arbitrary")),
        name="s5_scan_mixed" if mixed else "s5_scan",
    )(seqs, wb, wc, a_re, a_im, h0_re, h0_im)


def _s5fin_kernel(u_ref, y_ref, d_ref, w_ref, b_ref, o_ref):
    y = d_ref[...] * u_ref[...] + y_ref[...]
    y = 0.5 * y * (1.0 + jnp.tanh(math.sqrt(2.0 / math.pi) * (y + 0.044715 * (y * y * y))))
    z = jnp.dot(y.astype(BF16), w_ref[...], preferred_element_type=F32) + b_ref[...]
    o_ref[...] = (y * _sigmoid(z)).astype(o_ref.dtype)


def s5_finish(proj, y, d, w_glu, b_glu):
    tm = 512
    return pl.pallas_call(
        _s5fin_kernel,
        out_shape=jax.ShapeDtypeStruct((T_ALL, D_B), BF16),
        grid=(T_ALL // tm,),
        in_specs=[
            pl.BlockSpec((tm, D_B), lambda i: (i, C_UB // D_B)),
            pl.BlockSpec((tm, D_B), lambda i: (i, 0)),
            pl.BlockSpec((1, D_B), lambda i: (0, 0)),
            pl.BlockSpec((D_B, D_B), lambda i: (0, 0)),
            pl.BlockSpec((1, D_B), lambda i: (0, 0)),
        ],
        out_specs=pl.BlockSpec((tm, D_B), lambda i: (i, 0)),
        compiler_params=_cp(("arbitrary",)),
        name="s5_finish",
    )(proj, y, d, w_glu, b_glu)


def _merge_kernel(x_ref, gl_ref, oa_ref, ob_ref, oc_ref, mod_ref, g_ref, wa_ref, wb_ref, wc_ref, wo_ref,
                  x1_ref, h2_ref):
    m = None
    for br, (o_ref, w_ref) in enumerate(((oa_ref, wa_ref), (ob_ref, wb_ref), (oc_ref, wc_ref))):
        gate = _sigmoid(gl_ref[:, br * D_MODEL:(br + 1) * D_MODEL])
        t = gate * jnp.dot(o_ref[...], w_ref[...], preferred_element_type=F32)
        m = t if m is None else m + t
    x1 = x_ref[...] + mod_ref[2:3, :] * jnp.dot(m.astype(BF16), wo_ref[...], preferred_element_type=F32)
    x1_ref[...] = x1
    h2 = _rms(x1, g_ref[...])
    h2_ref[...] = h2 * (1.0 + mod_ref[4:5, :]) + mod_ref[3:4, :]


def merge(x, proj, oa, ob, oc, mod, g, wa, wb, wc, wo):
    tm = 256
    full = lambda shape: pl.BlockSpec(shape, lambda i: (0,) * len(shape))
    return pl.pallas_call(
        _merge_kernel,
        out_shape=(jax.ShapeDtypeStruct((T_ALL, D_MODEL), F32), jax.ShapeDtypeStruct((T_ALL, D_MODEL), F32)),
        grid=(T_ALL // tm,),
        in_specs=[
            pl.BlockSpec((tm, D_MODEL), lambda i: (i, 0)),
            pl.BlockSpec((tm, 3 * D_MODEL), lambda i: (i, 0)),
            pl.BlockSpec((tm, 512), lambda i: (i, 0)),
            pl.BlockSpec((tm, 512), lambda i: (i, 0)),
            pl.BlockSpec((tm, 512), lambda i: (i, 0)),
            pl.BlockSpec((None, 6, D_MODEL), lambda i: (_group_of_tile(i, tm), 0, 0)),
            full((1, D_MODEL)),
            full((512, D_MODEL)), full((512, D_MODEL)), full((512, D_MODEL)), full((D_MODEL, D_MODEL)),
        ],
        out_specs=(pl.BlockSpec((tm, D_MODEL), lambda i: (i, 0)), pl.BlockSpec((tm, D_MODEL), lambda i: (i, 0))),
        compiler_params=_cp(("arbitrary",)),
        name="merge",
    )(x, proj, oa, ob, oc, mod, g, wa, wb, wc, wo)


ROUTE_TM = 256


def _router_kernel(h_ref, w_ref, b_ref, idx_ref, rank_ref, wgt_ref, cnt_ref, cnt_acc):
    i = pl.program_id(0)

    @pl.when(i == 0)
    def _():
        cnt_acc[...] = jnp.zeros_like(cnt_acc)

    logits = jnp.dot(h_ref[...], w_ref[...], preferred_element_type=F32,
                     precision=lax.Precision.HIGHEST) + b_ref[...]
    lane_i = lax.broadcasted_iota(jnp.int32, (ROUTE_TM, LANES), 1)
    lane = lane_i.astype(F32)
    r_i = lax.broadcasted_iota(jnp.int32, (ROUTE_TM, ROUTE_TM), 0)
    c_i = lax.broadcasted_iota(jnp.int32, (ROUTE_TM, ROUTE_TM), 1)
    earlier = jnp.where(c_i < r_i, 1.0, 0.0).astype(BF16)

    cnt = cnt_acc[...]
    idx_out = jnp.zeros((ROUTE_TM, LANES), F32)
    rank_out = jnp.zeros((ROUTE_TM, LANES), F32)
    val_out = jnp.zeros((ROUTE_TM, LANES), F32)
    v0 = None
    esum = None
    for k in range(TOP_K):
        m = logits.max(axis=-1, keepdims=True)
        sel = jnp.min(jnp.where(logits == m, lane, float(LANES)), axis=-1, keepdims=True)
        hit = lane == sel
        logits = jnp.where(hit, -jnp.inf, logits)
        onehot = jnp.where(hit, 1.0, 0.0)
        within = jnp.dot(earlier, onehot.astype(BF16), preferred_element_type=F32)
        rank = jnp.sum(onehot * (within + cnt), axis=-1, keepdims=True)
        cnt = cnt + jnp.sum(onehot, axis=0, keepdims=True)
        if k == 0:
            v0 = m
        e = jnp.exp(m - v0)
        esum = e if esum is None else esum + e
        idx_out = jnp.where(lane_i == k, sel, idx_out)
        rank_out = jnp.where(lane_i == k, rank, rank_out)
        val_out = jnp.where(lane_i == k, e, val_out)
    cnt_acc[...] = cnt
    cnt_ref[...] = cnt
    idx_ref[...] = idx_out.astype(jnp.int32)
    rank_ref[...] = rank_out.astype(jnp.int32)
    wgt_ref[...] = val_out * (1.0 / esum)


def router(h2, w_router, b_router):
    tile = lambda: pl.BlockSpec((ROUTE_TM, LANES), lambda i: (i, 0))
    return pl.pallas_call(
        _router_kernel,
        out_shape=(
            jax.ShapeDtypeStruct((T_ALL, LANES), jnp.int32),
            jax.ShapeDtypeStruct((T_ALL, LANES), jnp.int32),
            jax.ShapeDtypeStruct((T_ALL, LANES), F32),
            jax.ShapeDtypeStruct((1, LANES), F32),
        ),
        grid=(T_ALL // ROUTE_TM,),
        in_specs=[
            pl.BlockSpec((ROUTE_TM, D_MODEL), lambda i: (i, 0)),
            pl.BlockSpec((D_MODEL, LANES), lambda i: (0, 0)),
            pl.BlockSpec((1, LANES), lambda i: (0, 0)),
        ],
        out_specs=(tile(), tile(), tile(), pl.BlockSpec((1, LANES), lambda i: (0, 0))),
        scratch_shapes=[pltpu.VMEM((1, LANES), F32)],
        compiler_params=_cp(("arbitrary",)),
        name="router",
    )(h2, w_router, b_router)


PLAN_UNROLL = 8
N_DUMP = 5 * BLK
PLAN_LEN = BUF_LEN + BLK
assert TOP_K == 4 and T_ALL & (T_ALL - 1) == 0


def _invert_kernel(dest_ref, fill_hbm, inv_ref):
    pltpu.sync_copy(fill_hbm, inv_ref)
    tok_step = PLAN_UNROLL // TOP_K

    def put(i, c):
        for j in range(PLAN_UNROLL):
            inv_ref[dest_ref[i * PLAN_UNROLL + j]] = i * tok_step + ((j % TOP_K) * T_ALL + j // TOP_K)
        return c

    lax.fori_loop(0, N_ASSIGN // PLAN_UNROLL, put, 0)


def invert_plan(dest):
    r = jnp.arange(PLAN_LEN, dtype=jnp.int32)
    fill = N_ASSIGN + jnp.where(r < BLK, 2 * BLK + r, (r - BLK) & (2 * BLK - 1))
    return pl.pallas_call(
        _invert_kernel,
        out_shape=jax.ShapeDtypeStruct((PLAN_LEN,), jnp.int32),
        in_specs=[pl.BlockSpec(memory_space=pltpu.SMEM), pl.BlockSpec(memory_space=pl.ANY)],
        out_specs=pl.BlockSpec(memory_space=pltpu.SMEM),
        name="invert_plan",
    )(dest, fill)


PAIR_TILE = 2 * LANES


def _wperm_kernel(w_ref, s_ref, o_ref):
    w = w_ref[...].astype(BF16)
    for j in range(w.shape[1] // PAIR_TILE):
        cols = slice(j * PAIR_TILE, (j + 1) * PAIR_TILE)
        o_ref[:, cols] = jnp.dot(w[:, cols], s_ref[...], preferred_element_type=F32).astype(BF16)


def permute_w_up(w_up):
    tn = 512
    r = lax.broadcasted_iota(jnp.int32, (PAIR_TILE, PAIR_TILE), 0)
    c = lax.broadcasted_iota(jnp.int32, (PAIR_TILE, PAIR_TILE), 1)
    sel = (r == jnp.where(c < LANES, 2 * c, 2 * (c - LANES) + 1)).astype(BF16)
    w = w_up.reshape(DEPTH * N_EXP, D_MODEL, 2 * D_FF)
    out = pl.pallas_call(
        _wperm_kernel,
        out_shape=jax.ShapeDtypeStruct(w.shape, BF16),
        grid=(DEPTH * N_EXP, 2 * D_FF // tn),
        in_specs=[
            pl.BlockSpec((None, D_MODEL, tn), lambda e, j: (e, 0, j)),
            pl.BlockSpec((PAIR_TILE, PAIR_TILE), lambda e, j: (0, 0)),
        ],
        out_specs=pl.BlockSpec((None, D_MODEL, tn), lambda e, j: (e, 0, j)),
        compiler_params=_cp(("arbitrary", "arbitrary")),
        name="permute_w_up",
    )(w, sel)
    return out.reshape(DEPTH, N_EXP, D_MODEL, 2 * D_FF)


EXP_NBUF = 3


def _expert_kernel(be_ref, inv_ref, h_hbm, wu_ref, bu_ref, wd_ref, bd_ref, comb_hbm, xbuf, ybuf, gsem, ssem):
    del be_ref
    i = pl.program_id(0)
    n = pl.num_programs(0)
    cur = i % EXP_NBUF
    nxt = (i + 2) % EXP_NBUF
    prv = nxt

    def gather(block, s):
        base = (block + 1) * BLK
        for r in range(BLK):
            tok = inv_ref[base + r] & (T_ALL - 1)
            pltpu.make_async_copy(h_hbm.at[pl.ds(tok, 1), :], xbuf.at[s, pl.ds(r, 1), :], gsem.at[s]).start()

    def scatter(block, s):
        base = (block + 1) * BLK
        for r in range(BLK):
            a = inv_ref[base + r]
            pltpu.make_async_copy(ybuf.at[s, pl.ds(r, 1), :], comb_hbm.at[pl.ds(a, 1), :], ssem.at[s]).start()

    def wait_block(sem):
        pltpu.make_async_copy(h_hbm.at[pl.ds(0, BLK), :], xbuf.at[0], sem).wait()

    @pl.when(i == 0)
    def _():
        ybuf[...] = jnp.zeros_like(ybuf)
        for s in range(EXP_NBUF - 1):
            for r in range(BLK):
                dump = N_ASSIGN + (3 + s) * BLK + r
                pltpu.make_async_copy(ybuf.at[s, pl.ds(r, 1), :], comb_hbm.at[pl.ds(dump, 1), :], ssem.at[s]).start()
        gather(0, 0)
        gather(1, 1)

    wait_block(gsem.at[cur])
    wait_block(ssem.at[cur])

    gather(jnp.minimum(i + 2, n - 1), nxt)
    scatter(i - 1, prv)

    x = xbuf[cur].astype(BF16)
    h = jnp.dot(x, wu_ref[...], preferred_element_type=F32) + bu_ref[...]
    acts = []
    for j in range(2 * D_FF // PAIR_TILE):
        glu = jnp.minimum(h[:, j * PAIR_TILE:j * PAIR_TILE + LANES], SWIGLU_LIMIT)
        lin = jnp.clip(h[:, j * PAIR_TILE + LANES:(j + 1) * PAIR_TILE], -SWIGLU_LIMIT, SWIGLU_LIMIT)
        acts.append((glu * _sigmoid(SWIGLU_ALPHA * glu) * (lin + 1.0)).astype(BF16))
    act = jnp.concatenate(acts, axis=1)
    ybuf[cur] = jnp.dot(act, wd_ref[...], preferred_element_type=F32) + bd_ref[...]

    @pl.when(i == n - 1)
    def _():
        for s in range(EXP_NBUF):
            @pl.when(s != cur)
            def _():
                wait_block(gsem.at[s])
                wait_block(ssem.at[s])
        scatter(i, cur)
        wait_block(ssem.at[cur])


def experts(block_e, inv, h2, wu, bu, wd, bd):
    return pl.pallas_call(
        _expert_kernel,
        out_shape=jax.ShapeDtypeStruct((N_ASSIGN + N_DUMP, D_MODEL), F32),
        grid_spec=pltpu.PrefetchScalarGridSpec(
            num_scalar_prefetch=2,
            grid=(N_BLOCKS,),
            in_specs=[
                pl.BlockSpec(memory_space=pl.ANY),
                pl.BlockSpec((None, D_MODEL, 2 * D_FF), lambda i, be, inv: (be[i], 0, 0)),
                pl.BlockSpec((None, 1, 2 * D_FF), lambda i, be, inv: (be[i], 0, 0)),
                pl.BlockSpec((None, D_FF, D_MODEL), lambda i, be, inv: (be[i], 0, 0)),
                pl.BlockSpec((None, 1, D_MODEL), lambda i, be, inv: (be[i], 0, 0)),
            ],
            out_specs=pl.BlockSpec(memory_space=pl.ANY),
            scratch_shapes=[
                pltpu.VMEM((EXP_NBUF, BLK, D_MODEL), F32), pltpu.VMEM((EXP_NBUF, BLK, D_MODEL), F32),
                pltpu.SemaphoreType.DMA((EXP_NBUF,)), pltpu.SemaphoreType.DMA((EXP_NBUF,)),
            ],
        ),
        compiler_params=_cp(("arbitrary",)),
        name="experts",
    )(block_e, inv, h2, wu, bu, wd, bd)


COMB_TM = 256


def _combine_kernel(c0_ref, c1_ref, c2_ref, c3_ref, x_ref, w_ref, mod_ref, o_ref):
    w = w_ref[...]
    acc = None
    for k, c_ref in enumerate((c0_ref, c1_ref, c2_ref, c3_ref)):
        t = w[:, k:k + 1] * c_ref[...]
        acc = t if acc is None else acc + t
    o_ref[...] = x_ref[...] + mod_ref[5:6, :] * acc


def combine(comb, x1, wgt, mod):
    tiles = T_ALL // COMB_TM
    kth = lambda k: pl.BlockSpec((COMB_TM, D_MODEL), lambda i: (k * tiles + i, 0))
    return pl.pallas_call(
        _combine_kernel,
        out_shape=jax.ShapeDtypeStruct((T_ALL, D_MODEL), F32),
        grid=(tiles,),
        in_specs=[
            kth(0), kth(1), kth(2), kth(3),
            pl.BlockSpec((COMB_TM, D_MODEL), lambda i: (i, 0)),
            pl.BlockSpec((COMB_TM, LANES), lambda i: (i, 0)),
            pl.BlockSpec((None, 6, D_MODEL), lambda i: (_group_of_tile(i, COMB_TM), 0, 0)),
        ],
        out_specs=pl.BlockSpec((COMB_TM, D_MODEL), lambda i: (i, 0)),
        compiler_params=_cp(("arbitrary",)),
        name="combine",
    )(comb, comb, comb, comb, x1, wgt, mod)


def _rope_table():
    pos = jnp.arange(DEC_SEQ)
    row = (pos // GRID_W).astype(F32)[:, None]
    col = (pos % GRID_W).astype(F32)[:, None]

    def parts(rot_dim):
        nf = rot_dim // 4
        inv = ROPE_BASE ** (-jnp.arange(nf, dtype=F32) / nf)
        cr, sr, cc, sc = jnp.cos(row * inv), jnp.sin(row * inv), jnp.cos(col * inv), jnp.sin(col * inv)
        z = jnp.zeros_like(sr)
        cos = jnp.concatenate([cr, cr, cc, cc], axis=1)
        lo = jnp.concatenate([-sr, z, -sc, z], axis=1)
        hi = jnp.concatenate([z, sr, z, sc], axis=1)
        return cos, lo, hi

    ca, la, ha = (jnp.tile(t, (1, 2)) for t in parts(HD_A))
    cc, lc, hc = parts(ROPE_C)
    pad = lambda t, fill: jnp.pad(t, ((0, 0), (NOPE_C, LANES - QK_C)), constant_values=fill)
    lat = jnp.concatenate([ca, la, ha, pad(cc, 1.0), pad(lc, 0.0), pad(hc, 0.0)], axis=1)
    ones, zeros = jnp.ones((DEC_SEQ, LANES), F32), jnp.zeros((DEC_SEQ, LANES), F32)
    ident = jnp.concatenate([ones, zeros, zeros, ones, zeros, zeros], axis=1)
    return jnp.concatenate([ident, lat], axis=0)


def _arrange_w_in(w_in):
    o = 0
    parts = {}
    for name, n in (("qa", HQ_A * HD_A), ("ka", HKV_A * HD_A), ("va", HKV_A * HD_A), ("ub", D_B), ("cq", Q_LORA),
                    ("ckv", KV_LORA), ("kr", ROPE_C), ("gl", 3 * D_MODEL)):
        parts[name] = w_in[..., o:o + n]
        o += n
    qa = parts["qa"].reshape(DEPTH, D_MODEL, HKV_A, G_A, 1, HD_A)
    eye = jnp.eye(HKV_A, dtype=F32).reshape(1, 1, HKV_A, 1, HKV_A, 1)
    qa_slots = (qa * eye).reshape(DEPTH, D_MODEL, HQ_A * LANES)
    kr = jnp.pad(parts["kr"], ((0, 0), (0, 0), (NOPE_C, LANES - QK_C)))
    w = jnp.concatenate([parts["gl"], qa_slots, parts["ub"], parts["cq"], parts["ka"], parts["va"], parts["ckv"], kr],
                        axis=-1)
    return w.astype(BF16)


def _s5_params(lam_re, lam_im, log_dt, b_re, b_im, c_re, c_im):
    dt = jnp.exp(log_dt)[..., None]
    decay = jnp.exp(lam_re * dt)
    ab_re, ab_im = decay * jnp.cos(lam_im * dt), decay * jnp.sin(lam_im * dt)
    den = lam_re * lam_re + lam_im * lam_im
    f_re = ((ab_re - 1) * lam_re + ab_im * lam_im) / den
    f_im = (ab_im * lam_re - (ab_re - 1) * lam_im) / den
    bb_re = f_re[..., None] * b_re - f_im[..., None] * b_im
    bb_im = f_re[..., None] * b_im + f_im[..., None] * b_re
    nj, gpb = D_B // LANES, LANES // GS_B
    eye = jnp.eye(gpb, dtype=F32)

    def blockdiag_b(bb):
        t = bb.transpose(0, 1, 3, 2).reshape(2, nj, gpb, GS_B, P_B)
        return (t[:, :, :, :, None, :] * eye[None, None, :, None, :, None]).reshape(2, nj, LANES, gpb * P_B)

    def blockdiag_c(cc):
        t = cc.transpose(0, 1, 3, 2).reshape(2, nj, gpb, P_B, GS_B)
        return (t[:, :, :, :, None, :] * eye[None, None, :, None, :, None]).reshape(2, nj, gpb * P_B, LANES)

    wb = jnp.concatenate([blockdiag_b(bb_re), blockdiag_b(bb_im)], axis=-1).astype(BF16)
    wc = jnp.concatenate([blockdiag_c(c_re), -blockdiag_c(c_im)], axis=-2).astype(BF16)
    return wb, wc, ab_re.reshape(2, G_B * P_B), ab_im.reshape(2, G_B * P_B)


def _to_time_major(u):
    f = u.transpose(1, 0, 2)
    return jnp.concatenate([f, f[::-1]], axis=1)


def _from_time_major(y, b):
    return (y[:, :b] + y[::-1, b:]).transpose(1, 0, 2)


def kernel(x_prompt, x_sample, c, cache_attn_k, cache_attn_v, cache_mla_ckv, cache_mla_krope, state_ssm_re, state_ssm_im, c_ctx, w_ada, b_ada, norm_mix_g, norm_ffn_g, w_in, q_norm_a, k_norm_a, sink_a, q_a_norm_c, kv_a_norm_c, w_uq_c, w_ukv_c, q_norm_c, k_norm_c, ssm_lam_re, ssm_lam_im, ssm_log_dt, ssm_b_re, ssm_b_im, ssm_c_re, ssm_c_im, ssm_d, w_glu, b_glu, w_br_a, w_br_b, w_br_c, w_out, w_router, b_router, w_up, b_up, w_down, b_down):
    x = jnp.concatenate([x_prompt.reshape(N_CTX, D_MODEL), x_sample.reshape(N_LAT, D_MODEL)], axis=0)
    cvecs = jnp.concatenate([c_ctx[None], c, jnp.zeros((N_GROUPS - 1 - DEC_BATCH, D_MODEL), F32)], axis=0)
    mods = adaln(cvecs, w_ada, b_ada)

    tab = _rope_table()
    w_in_r = _arrange_w_in(w_in)
    pad_slot = lambda g: jnp.pad(g, ((0, 0), (0, LANES - QK_C))).reshape(DEPTH, 1, LANES)
    gqa = jnp.tile(q_norm_a, (1, 2)).reshape(DEPTH, 1, LANES)
    gka = jnp.tile(k_norm_a, (1, 2)).reshape(DEPTH, 1, LANES)
    gqc, gkc = pad_slot(q_norm_c), pad_slot(k_norm_c)
    wuq = jnp.pad(w_uq_c.reshape(DEPTH, Q_LORA, H_C, QK_C), ((0, 0), (0, 0), (0, 0), (0, LANES - QK_C)))
    wuq = wuq.reshape(DEPTH, Q_LORA, H_C * LANES).astype(BF16)
    wukv4 = w_ukv_c.reshape(DEPTH, KV_LORA, H_C, NOPE_C + V_C)
    wuk = jnp.pad(wukv4[..., :NOPE_C], ((0, 0), (0, 0), (0, 0), (0, LANES - NOPE_C))).reshape(DEPTH, KV_LORA, H_C * LANES)
    wuv = wukv4[..., NOPE_C:].reshape(DEPTH, KV_LORA, H_C * V_C)
    wukv = jnp.concatenate([wuk, wuv], axis=-1).astype(BF16)
    sink = jnp.broadcast_to(sink_a[:, :, None], (DEPTH, HQ_A, LANES))
    w_router_p = jnp.pad(w_router, ((0, 0), (0, 0), (0, LANES - N_EXP)))
    b_router_p = jnp.pad(b_router, ((0, 0), (0, LANES - N_EXP)), constant_values=-jnp.inf).reshape(DEPTH, 1, LANES)
    w_up_p = permute_w_up(w_up)
    b_up_p = b_up.reshape(DEPTH, N_EXP, 2 * D_FF // PAIR_TILE, LANES, 2).transpose(0, 1, 2, 4, 3)
    b_up_p = b_up_p.reshape(DEPTH, N_EXP, 1, 2 * D_FF)
    w_down_b = w_down.astype(BF16)
    b_down_r = b_down.reshape(DEPTH, N_EXP, 1, D_MODEL)
    kr_cache = jnp.pad(cache_mla_krope, ((0, 0), (0, 0), (0, 0), (NOPE_C, LANES - QK_C)))
    zeros_state = jnp.zeros((2 * BATCH, G_B * P_B), F32)

    new_k, new_v, new_ckv, new_kr, new_sre, new_sim = [], [], [], [], [], []
    for l in range(DEPTH):
        mod = mods[l]
        proj = inproj(x, norm_mix_g[l], mod, w_in_r[l])
        qa, ka, qc, kc, vc, ckvn = prep(proj, tab, gqa[l], gka[l], q_a_norm_c[l].reshape(1, Q_LORA),
                                         kv_a_norm_c[l].reshape(1, KV_LORA), gqc[l], gkc[l], wuq[l], wukv[l])
        kc_cache, vc_cache = cache_keys(cache_mla_ckv[:, l].reshape(DEC_BATCH * PAST_LEN, KV_LORA),
                                        kr_cache[:, l].reshape(DEC_BATCH * PAST_LEN, LANES), gkc[l], wukv[l])
        oa = attn_a_lat(qa, ka, proj, cache_attn_k[:, l].reshape(DEC_BATCH, PAST_LEN, LANES),
                        cache_attn_v[:, l].reshape(DEC_BATCH, PAST_LEN, LANES), sink[l],
                        attn_a_ctx(qa, ka, proj, sink[l]))
        oc = attn_c_lat(qc, kc, vc, kc_cache, vc_cache, attn_c_ctx(qc, kc, vc))
        wb, wc, a_re, a_im = _s5_params(ssm_lam_re[l], ssm_lam_im[l], ssm_log_dt[l], ssm_b_re[l], ssm_b_im[l],
                                        ssm_c_re[l], ssm_c_im[l])
        ub = proj[:, C_UB:C_UB + D_B]
        a8 = lambda a: jnp.broadcast_to(a[:, None, :], (2, SUBLANES, G_B * P_B))
        a_mixed = lambda a: jnp.repeat(a, SUBLANES // 2, axis=0)[None]
        y_ctx, f_re, f_im = s5_scan(_to_time_major(ub[:N_CTX].reshape(BATCH, SEQ, D_B)), wb, wc, a8(a_re), a8(a_im),
                                    zeros_state, zeros_state, mixed=False)
        h0 = lambda s: s[:, l].transpose(1, 0, 2, 3).reshape(2 * DEC_BATCH, G_B * P_B)
        y_lat, _, _ = s5_scan(_to_time_major(ub[N_CTX:].reshape(DEC_BATCH, DEC_SEQ, D_B)), wb, wc,
                              a_mixed(a_re), a_mixed(a_im), h0(state_ssm_re), h0(state_ssm_im), mixed=True)
        y = jnp.concatenate([_from_time_major(y_ctx, BATCH).reshape(N_CTX, D_B),
                             _from_time_major(y_lat, DEC_BATCH).reshape(N_LAT, D_B)], axis=0)
        ob = s5_finish(proj, y, ssm_d[l].reshape(1, D_B), w_glu[l].astype(BF16), b_glu[l].reshape(1, D_B))
        x1, h2 = merge(x, proj, oa, ob, oc, mod, norm_ffn_g[l].reshape(1, D_MODEL), w_br_a[l].astype(BF16),
                       w_br_b[l].astype(BF16), w_br_c[l].astype(BF16), w_out[l].astype(BF16))
        idx, rank, wgt, cnt = router(h2, w_router_p[l], b_router_p[l])
        counts = cnt[0, :N_EXP].astype(jnp.int32)
        padded = (counts + BLK - 1) // BLK * BLK
        pad_end = jnp.cumsum(padded)
        pad_start = pad_end - padded
        dest = (BLK + pad_start[idx[:, :TOP_K]] + rank[:, :TOP_K]).reshape(-1).astype(jnp.int32)
        block_start = jnp.arange(N_BLOCKS, dtype=jnp.int32) * BLK
        block_e = jnp.minimum(jnp.sum(pad_end[None, :] <= block_start[:, None], axis=1), N_EXP - 1)
        comb = experts(block_e.astype(jnp.int32), invert_plan(dest), h2, w_up_p[l], b_up_p[l], w_down_b[l],
                       b_down_r[l])
        x = combine(comb, x1, wgt, mod)

        new_k.append(ka[:N_CTX].reshape(BATCH, SEQ, HKV_A, HD_A))
        new_v.append(proj[:N_CTX, C_VA:C_VA + LANES].reshape(BATCH, SEQ, HKV_A, HD_A))
        new_ckv.append(ckvn[:N_CTX].reshape(BATCH, SEQ, KV_LORA))
        new_kr.append(proj[:N_CTX, C_KR + NOPE_C:C_KR + QK_C].reshape(BATCH, SEQ, ROPE_C))
        new_sre.append(f_re.reshape(2, BATCH, G_B, P_B).transpose(1, 0, 2, 3))
        new_sim.append(f_im.reshape(2, BATCH, G_B, P_B).transpose(1, 0, 2, 3))

    y_prompt = x[:N_CTX].reshape(BATCH, SEQ, D_MODEL)
    y_sample = x[N_CTX:].reshape(DEC_BATCH, DEC_SEQ, D_MODEL)
    return (y_prompt, y_sample, jnp.stack(new_k, axis=1), jnp.stack(new_v, axis=1), jnp.stack(new_ckv, axis=1),
            jnp.stack(new_kr, axis=1), jnp.stack(new_sre, axis=1), jnp.stack(new_sim, axis=1))
```

```python
import functools
import math

import jax
import jax.numpy as jnp
from jax import lax
from jax.experimental import pallas as pl
from jax.experimental.pallas import tpu as pltpu

D_MODEL = 1024
BATCH = 32
SEQ = 256
DEPTH = 2
DEC_BATCH = 4
DEC_SEQ = 2048
PAST_LEN = 256
GRID_W = 64
ROPE_BASE = 10000.0
EPS = 1e-6
NEG_INF = -1e30
BLK = 128
HQ_A, HKV_A, HD_A = 8, 2, 64
G_A = HQ_A // HKV_A
WINDOW = 128
D_B, GS_B, P_B = 512, 16, 64
G_B = D_B // GS_B
H_C, Q_LORA, KV_LORA, NOPE_C, ROPE_C, V_C = 8, 256, 128, 64, 32, 64
QK_C = NOPE_C + ROPE_C
N_EXP, TOP_K, D_FF = 32, 4, 1024
SWIGLU_ALPHA, SWIGLU_LIMIT = 1.702, 7.0

N_CTX = BATCH * SEQ
N_LAT = DEC_BATCH * DEC_SEQ
T_ALL = N_CTX + N_LAT
N_GROUPS = 8

LANES = 128
SUBLANES = 8

C_GL = 0
C_QA = 3 * D_MODEL
C_UB = C_QA + HQ_A * LANES
C_CQ = C_UB + D_B
C_KA = C_CQ + Q_LORA
C_VA = C_KA + LANES
C_CKV = C_VA + LANES
C_KR = C_CKV + LANES
N_PROJ = C_KR + LANES

N_ASSIGN = T_ALL * TOP_K
N_BLOCKS = N_ASSIGN // BLK + N_EXP
BUF_LEN = N_BLOCKS * BLK

F32 = jnp.float32
BF16 = jnp.bfloat16
ROW3 = (D_MODEL // LANES, LANES)
VMEM_LIMIT = 56 * 1024 * 1024


def _cp(sem, vmem=VMEM_LIMIT):
    return pltpu.CompilerParams(dimension_semantics=sem, vmem_limit_bytes=vmem)


def _group_of_tile(i, tm):
    n_ctx_tiles = N_CTX // tm
    per_batch = DEC_SEQ // tm
    return jnp.where(i < n_ctx_tiles, 0, 1 + (i - n_ctx_tiles) // per_batch)


def _sigmoid(x):
    return 1.0 / (1.0 + jnp.exp(-x))


def _adaln_kernel(c_ref, w_ref, b_ref, o_ref):
    c = c_ref[...]
    s = c * _sigmoid(c)
    o_ref[...] = jnp.dot(s.astype(BF16), w_ref[...].astype(BF16), preferred_element_type=F32) + b_ref[...]


def adaln(cvecs, w_ada, b_ada):
    tn = 1536
    out = pl.pallas_call(
        _adaln_kernel,
        out_shape=jax.ShapeDtypeStruct((DEPTH, N_GROUPS, 6 * D_MODEL), F32),
        grid=(DEPTH, 6 * D_MODEL // tn),
        in_specs=[
            pl.BlockSpec((N_GROUPS, D_MODEL), lambda l, j: (0, 0)),
            pl.BlockSpec((None, D_MODEL, tn), lambda l, j: (l, 0, j)),
            pl.BlockSpec((None, 1, tn), lambda l, j: (l, 0, j)),
        ],
        out_specs=pl.BlockSpec((None, N_GROUPS, tn), lambda l, j: (l, 0, j)),
        compiler_params=_cp(("arbitrary", "arbitrary")),
        name="adaln",
    )(cvecs, w_ada, b_ada.reshape(DEPTH, 1, 6 * D_MODEL))
    return out.reshape(DEPTH, N_GROUPS, 6, D_MODEL)


def _rms(x, g):
    ms = jnp.mean(x * x, axis=-1, keepdims=True)
    return x * lax.rsqrt(ms + EPS) * g


def _inproj_kernel(x_ref, g_ref, mod_ref, w_ref, o_ref):
    h = _rms(x_ref[...], g_ref[...])
    h = h * (1.0 + mod_ref[1:2, :]) + mod_ref[0:1, :]
    o_ref[...] = jnp.dot(h.astype(BF16), w_ref[...], preferred_element_type=F32)


def inproj(x, g, mod, w):
    tm, tn = 512, 1792
    return pl.pallas_call(
        _inproj_kernel,
        out_shape=jax.ShapeDtypeStruct((T_ALL, N_PROJ), F32),
        grid=(N_PROJ // tn, T_ALL // tm),
        in_specs=[
            pl.BlockSpec((tm, D_MODEL), lambda j, i: (i, 0)),
            pl.BlockSpec((1, D_MODEL), lambda j, i: (0, 0)),
            pl.BlockSpec((None, 6, D_MODEL), lambda j, i: (_group_of_tile(i, tm), 0, 0)),
            pl.BlockSpec((D_MODEL, tn), lambda j, i: (0, j)),
        ],
        out_specs=pl.BlockSpec((tm, tn), lambda j, i: (i, j)),
        compiler_params=_cp(("arbitrary", "arbitrary")),
        name="inproj",
    )(x, g.reshape(1, D_MODEL), mod, w)


def _rope(slab, cos, sin_lo, sin_hi, half):
    up = pltpu.roll(slab, LANES - half, axis=1)
    dn = pltpu.roll(slab, half, axis=1)
    return slab * cos + up * sin_lo + dn * sin_hi


def _slot_norm(slab, gain, n_real):
    ms = jnp.sum(slab * slab, axis=-1, keepdims=True) * (1.0 / n_real)
    return slab * lax.rsqrt(ms + EPS) * gain


def _mla_keys(ckvn, kr_blk, wukv_ref, gk, rope_c):
    kv = jnp.dot(ckvn.astype(BF16), wukv_ref[...], preferred_element_type=F32)
    ks = []
    for h in range(H_C):
        slab = kv[:, h * LANES:(h + 1) * LANES] + kr_blk
        slab = _slot_norm(slab, gk, QK_C)
        if rope_c is not None:
            slab = _rope(slab, *rope_c, ROPE_C // 4)
        ks.append(slab.astype(BF16))
    return jnp.concatenate(ks, axis=1), kv[:, H_C * LANES:].astype(BF16)


PREP_TM = 256


def _prep_kernel(qa_ref, cq_ref, ka_ref, ckv_ref, kr_ref, tab_ref,
                 gqa_ref, gka_ref, gcq_ref, gckv_ref, gqc_ref, gkc_ref, wuq_ref, wukv_ref,
                 qa_o, ka_o, qc_o, kc_o, vc_o, ckvn_o):
    def body(rotate):
        if rotate:
            tab = tab_ref[...]
            rope_a = (tab[:, 0:128], tab[:, 128:256], tab[:, 256:384])
            rope_c = (tab[:, 384:512], tab[:, 512:640], tab[:, 640:768])
            rot_a = lambda x: _rope(x, *rope_a, HD_A // 4)
            rot_c = lambda x: _rope(x, *rope_c, ROPE_C // 4)
        else:
            rope_c = None
            rot_a = rot_c = lambda x: x
        lane = lax.broadcasted_iota(jnp.int32, (1, LANES), 1)

        gqa = gqa_ref[...]
        qa = qa_ref[...]
        outs = []
        for h in range(HQ_A):
            slab = _slot_norm(qa[:, h * LANES:(h + 1) * LANES], gqa, HD_A)
            outs.append((rot_a(slab) * (HD_A ** -0.5)).astype(BF16))
        qa_o[...] = jnp.concatenate(outs, axis=1)

        ka = ka_ref[...]
        sq = ka * ka
        lo = lane < HD_A
        ms_lo = jnp.sum(jnp.where(lo, sq, 0.0), axis=-1, keepdims=True)
        ms_hi = jnp.sum(jnp.where(lo, 0.0, sq), axis=-1, keepdims=True)
        rs = jnp.where(lo, lax.rsqrt(ms_lo * (1.0 / HD_A) + EPS), lax.rsqrt(ms_hi * (1.0 / HD_A) + EPS))
        ka_o[...] = rot_a(ka * rs * gka_ref[...])

        cqn = _rms(cq_ref[...], gcq_ref[...])
        q = jnp.dot(cqn.astype(BF16), wuq_ref[...], preferred_element_type=F32)
        gqc = gqc_ref[...]
        outs = []
        for h in range(H_C):
            slab = _slot_norm(q[:, h * LANES:(h + 1) * LANES], gqc, QK_C)
            outs.append((rot_c(slab) * (QK_C ** -0.5)).astype(BF16))
        qc_o[...] = jnp.concatenate(outs, axis=1)

        ckvn = _rms(ckv_ref[...], gckv_ref[...])
        ckvn_o[...] = ckvn
        kc, vc = _mla_keys(ckvn, kr_ref[...], wukv_ref, gkc_ref[...], rope_c)
        kc_o[...] = kc
        vc_o[...] = vc

    is_latent = pl.program_id(0) >= N_CTX // PREP_TM
    pl.when(is_latent)(lambda: body(True))
    pl.when(jnp.logical_not(is_latent))(lambda: body(False))


def prep(proj, tab, gqa, gka, gcq, gckv, gqc, gkc, wuq, wukv):
    tm = PREP_TM
    n_ctx_tiles = N_CTX // tm
    per_batch = DEC_SEQ // tm

    def tab_map(i):
        return (jnp.where(i < n_ctx_tiles, 0, per_batch + (i - n_ctx_tiles) % per_batch), 0)

    def col(width, off):
        return pl.BlockSpec((tm, width), lambda i: (i, off // width))

    def full(shape):
        return pl.BlockSpec(shape, lambda i: (0,) * len(shape))

    def row_out(width):
        return pl.BlockSpec((tm, width), lambda i: (i, 0))

    return pl.pallas_call(
        _prep_kernel,
        out_shape=(
            jax.ShapeDtypeStruct((T_ALL, HQ_A * LANES), BF16),
            jax.ShapeDtypeStruct((T_ALL, LANES), F32),
            jax.ShapeDtypeStruct((T_ALL, H_C * LANES), BF16),
            jax.ShapeDtypeStruct((T_ALL, H_C * LANES), BF16),
            jax.ShapeDtypeStruct((T_ALL, H_C * V_C), BF16),
            jax.ShapeDtypeStruct((T_ALL, KV_LORA), F32),
        ),
        grid=(T_ALL // tm,),
        in_specs=[
            col(HQ_A * LANES, C_QA), col(Q_LORA, C_CQ), col(LANES, C_KA), col(LANES, C_CKV), col(LANES, C_KR),
            pl.BlockSpec((tm, 6 * LANES), tab_map),
            full((1, LANES)), full((1, LANES)), full((1, Q_LORA)), full((1, KV_LORA)),
            full((1, LANES)), full((1, LANES)),
            full((Q_LORA, H_C * LANES)), full((KV_LORA, H_C * LANES + H_C * V_C)),
        ],
        out_specs=(row_out(HQ_A * LANES), row_out(LANES), row_out(H_C * LANES), row_out(H_C * LANES),
                   row_out(H_C * V_C), row_out(KV_LORA)),
        compiler_params=_cp(("arbitrary",)),
        name="prep",
    )(proj, proj, proj, proj, proj, tab, gqa, gka, gcq, gckv, gqc, gkc, wuq, wukv)


def _cachekeys_kernel(ckv_ref, kr_ref, gkc_ref, wukv_ref, kc_o, vc_o):
    kc, vc = _mla_keys(ckv_ref[...], kr_ref[...], wukv_ref, gkc_ref[...], None)
    kc_o[...] = kc
    vc_o[...] = vc


def cache_keys(ckv, kr_blk, gkc, wukv):
    r = ckv.shape[0]
    tm = 256
    return pl.pallas_call(
        _cachekeys_kernel,
        out_shape=(jax.ShapeDtypeStruct((r, H_C * LANES), BF16), jax.ShapeDtypeStruct((r, H_C * V_C), BF16)),
        grid=(r // tm,),
        in_specs=[
            pl.BlockSpec((tm, LANES), lambda i: (i, 0)),
            pl.BlockSpec((tm, LANES), lambda i: (i, 0)),
            pl.BlockSpec((1, LANES), lambda i: (0, 0)),
            pl.BlockSpec((KV_LORA, H_C * LANES + H_C * V_C), lambda i: (0, 0)),
        ],
        out_specs=(pl.BlockSpec((tm, H_C * LANES), lambda i: (i, 0)), pl.BlockSpec((tm, H_C * V_C), lambda i: (i, 0))),
        compiler_params=_cp(("arbitrary",)),
        name="cache_keys",
    )(ckv, kr_blk, gkc, wukv)


def _attn_body(q_ref, segs, sink_ref, o_ref, *, n_heads, k_slot, v_slab, v_half, tq, band_qi=None):
    outs = []
    for h in range(n_heads):
        qh = q_ref[:, h * LANES:(h + 1) * LANES]
        scores = []
        for k_ref, _, off in segs:
            kh = k_ref[:, k_slot(h) * LANES:(k_slot(h) + 1) * LANES].astype(BF16)
            s = lax.dot_general(qh, kh, (((1,), (1,)), ((), ())), preferred_element_type=F32)
            if off is not None:
                tk = s.shape[1]
                blk = band_qi + off
                q_pos = band_qi * tq + lax.broadcasted_iota(jnp.int32, (tq, tk), 0)
                k_pos = blk * tk + lax.broadcasted_iota(jnp.int32, (tq, tk), 1)
                ok = (jnp.abs(k_pos - q_pos) <= WINDOW) & (blk >= 0) & (blk < DEC_SEQ // tk)
                s = jnp.where(ok, s, NEG_INF)
            scores.append(s)
        m = scores[0].max(axis=-1, keepdims=True)
        for s in scores[1:]:
            m = jnp.maximum(m, s.max(axis=-1, keepdims=True))
        if sink_ref is not None:
            sink = sink_ref[h:h + 1, 0:1]
            m = jnp.maximum(m, sink)
            denom = jnp.exp(sink - m)
        else:
            denom = jnp.zeros_like(m)
        acc = None
        for s, (_, v_ref, _) in zip(scores, segs):
            p = jnp.exp(s - m)
            denom = denom + p.sum(axis=-1, keepdims=True)
            vs = v_ref[:, v_slab(h) * LANES:(v_slab(h) + 1) * LANES].astype(BF16)
            pv = jnp.dot(p.astype(BF16), vs, preferred_element_type=F32)
            acc = pv if acc is None else acc + pv
        half = v_half(h)
        outs.append(acc[:, half * 64:(half + 1) * 64] * (1.0 / denom))
    o_ref[...] = jnp.concatenate(outs, axis=1).astype(o_ref.dtype)


_C_CFG = dict(n_heads=H_C, k_slot=lambda h: h, v_slab=lambda h: h // 2, v_half=lambda h: h % 2)


def _attn_a_body(q_ref, segs, sink_ref, o_ref, *, tq, band_qi=None):
    rows = HQ_A * tq
    q = jnp.concatenate([q_ref[:, h * LANES:(h + 1) * LANES] for h in range(HQ_A)], axis=0)
    sink = jnp.concatenate([jnp.broadcast_to(sink_ref[h:h + 1, 0:1], (tq, 1)) for h in range(HQ_A)], axis=0)
    scores = []
    for k_ref, _, off in segs:
        s = lax.dot_general(q, k_ref[...].astype(BF16), (((1,), (1,)), ((), ())), preferred_element_type=F32)
        if off is not None:
            tk = s.shape[1]
            blk = band_qi + off
            q_pos = band_qi * tq + (lax.broadcasted_iota(jnp.int32, (rows, tk), 0) & (tq - 1))
            k_pos = blk * tk + lax.broadcasted_iota(jnp.int32, (rows, tk), 1)
            ok = (jnp.abs(k_pos - q_pos) <= WINDOW) & (blk >= 0) & (blk < DEC_SEQ // tk)
            s = jnp.where(ok, s, NEG_INF)
        scores.append(s)
    m = sink
    for s in scores:
        m = jnp.maximum(m, s.max(axis=-1, keepdims=True))
    denom = jnp.exp(sink - m)
    acc = None
    for s, (_, v_ref, _) in zip(scores, segs):
        p = jnp.exp(s - m)
        denom = denom + p.sum(axis=-1, keepdims=True)
        pv = jnp.dot(p.astype(BF16), v_ref[...].astype(BF16), preferred_element_type=F32)
        acc = pv if acc is None else acc + pv
    acc = acc * (1.0 / denom)
    outs = []
    for h in range(HQ_A):
        kv = h // G_A
        outs.append(acc[h * tq:(h + 1) * tq, kv * HD_A:(kv + 1) * HD_A])
    o_ref[...] = jnp.concatenate(outs, axis=1).astype(o_ref.dtype)


def _attn_a_ctx_kernel(q_ref, k_ref, v_ref, sink_ref, o_init, o_ref):
    del o_init
    _attn_a_body(q_ref, [(k_ref, v_ref, None)], sink_ref, o_ref, tq=SEQ)


def _attn_a_lat_kernel(q_ref, k0, k1, k2, v0, v1, v2, kc_ref, vc_ref, sink_ref, o_ctx, o_ref):
    del o_ctx
    qi = pl.program_id(1)
    segs = [(k0, v0, -1), (k1, v1, 0), (k2, v2, 1), (kc_ref, vc_ref, None)]
    _attn_a_body(q_ref, segs, sink_ref, o_ref, tq=BLK, band_qi=qi)


def _attn_c_ctx_kernel(q_ref, k_ref, v_ref, o_init, o_ref):
    del o_init
    _attn_body(q_ref, [(k_ref, v_ref, None)], None, o_ref, tq=SEQ, **_C_CFG)


def _attn_c_lat_kernel(q_ref, k_ref, v_ref, kc_ref, vc_ref, o_ctx, o_ref):
    del o_ctx
    _attn_body(q_ref, [(k_ref, v_ref, None), (kc_ref, vc_ref, None)], None, o_ref, tq=256, **_C_CFG)


def attn_a_ctx(qa, ka, proj, sink):
    return pl.pallas_call(
        _attn_a_ctx_kernel,
        out_shape=jax.ShapeDtypeStruct((T_ALL, HQ_A * HD_A), BF16),
        grid=(BATCH,),
        in_specs=[
            pl.BlockSpec((SEQ, HQ_A * LANES), lambda b: (b, 0)),
            pl.BlockSpec((SEQ, LANES), lambda b: (b, 0)),
            pl.BlockSpec((SEQ, LANES), lambda b: (b, C_VA // LANES)),
            pl.BlockSpec((SUBLANES, LANES), lambda b: (0, 0)),
            pl.BlockSpec(memory_space=pl.ANY),
        ],
        out_specs=pl.BlockSpec((SEQ, HQ_A * HD_A), lambda b: (b, 0)),
        input_output_aliases={4: 0},
        compiler_params=_cp(("arbitrary",)),
        name="attn_a_ctx",
    )(qa, ka, proj, sink, jnp.zeros((T_ALL, HQ_A * HD_A), BF16))


def attn_a_lat(qa, ka, proj, k_cache, v_cache, sink, o_all):
    nb = DEC_SEQ // BLK
    base = N_CTX // BLK

    def band(off, colblk):
        return pl.BlockSpec((BLK, LANES), lambda b, i: (base + b * nb + jnp.clip(i + off, 0, nb - 1), colblk))

    return pl.pallas_call(
        _attn_a_lat_kernel,
        out_shape=jax.ShapeDtypeStruct((T_ALL, HQ_A * HD_A), BF16),
        grid=(DEC_BATCH, nb),
        in_specs=[
            pl.BlockSpec((BLK, HQ_A * LANES), lambda b, i: (base + b * nb + i, 0)),
            band(-1, 0), band(0, 0), band(1, 0),
            band(-1, C_VA // LANES), band(0, C_VA // LANES), band(1, C_VA // LANES),
            pl.BlockSpec((None, PAST_LEN, LANES), lambda b, i: (b, 0, 0)),
            pl.BlockSpec((None, PAST_LEN, LANES), lambda b, i: (b, 0, 0)),
            pl.BlockSpec((SUBLANES, LANES), lambda b, i: (0, 0)),
            pl.BlockSpec(memory_space=pl.ANY),
        ],
        out_specs=pl.BlockSpec((BLK, HQ_A * HD_A), lambda b, i: (base + b * nb + i, 0)),
        input_output_aliases={10: 0},
        compiler_params=_cp(("arbitrary", "arbitrary")),
        name="attn_a_lat",
    )(qa, ka, ka, ka, proj, proj, proj, k_cache, v_cache, sink, o_all)


def attn_c_ctx(qc, kc, vc):
    return pl.pallas_call(
        _attn_c_ctx_kernel,
        out_shape=jax.ShapeDtypeStruct((T_ALL, H_C * V_C), BF16),
        grid=(BATCH,),
        in_specs=[
            pl.BlockSpec((SEQ, H_C * LANES), lambda b: (b, 0)),
            pl.BlockSpec((SEQ, H_C * LANES), lambda b: (b, 0)),
            pl.BlockSpec((SEQ, H_C * V_C), lambda b: (b, 0)),
            pl.BlockSpec(memory_space=pl.ANY),
        ],
        out_specs=pl.BlockSpec((SEQ, H_C * V_C), lambda b: (b, 0)),
        input_output_aliases={3: 0},
        compiler_params=_cp(("arbitrary",)),
        name="attn_c_ctx",
    )(qc, kc, vc, jnp.zeros((T_ALL, H_C * V_C), BF16))


def attn_c_lat(qc, kc, vc, kc_cache, vc_cache, o_all):
    tq = 256
    nq = DEC_SEQ // tq
    qbase = N_CTX // tq
    kbase = N_CTX // DEC_SEQ
    return pl.pallas_call(
        _attn_c_lat_kernel,
        out_shape=jax.ShapeDtypeStruct((T_ALL, H_C * V_C), BF16),
        grid=(DEC_BATCH, nq),
        in_specs=[
            pl.BlockSpec((tq, H_C * LANES), lambda b, i: (qbase + b * nq + i, 0)),
            pl.BlockSpec((DEC_SEQ, H_C * LANES), lambda b, i: (kbase + b, 0)),
            pl.BlockSpec((DEC_SEQ, H_C * V_C), lambda b, i: (kbase + b, 0)),
            pl.BlockSpec((PAST_LEN, H_C * LANES), lambda b, i: (b, 0)),
            pl.BlockSpec((PAST_LEN, H_C * V_C), lambda b, i: (b, 0)),
            pl.BlockSpec(memory_space=pl.ANY),
        ],
        out_specs=pl.BlockSpec((tq, H_C * V_C), lambda b, i: (qbase + b * nq + i, 0)),
        input_output_aliases={5: 0},
        compiler_params=_cp(("arbitrary", "arbitrary")),
        name="attn_c_lat",
    )(qc, kc, vc, kc_cache, vc_cache, o_all)


S5_TC = 256
S5_CH = 512
S5_UNROLL = 8


def _s5_kernel(u_ref, wb_ref, wc_ref, are_ref, aim_ref, h0re_ref, h0im_ref,
               y_ref, fre_ref, fim_ref, bre, bim, sre, sim, *, mixed):
    k = pl.program_id(2)

    @pl.when(k == 0)
    def _():
        sre[...] = h0re_ref[...]
        sim[...] = h0im_ref[...]

    rows = S5_TC * SUBLANES
    u2 = u_ref[...].reshape(rows, LANES).astype(BF16)
    if mixed:
        fwd_row = (lax.broadcasted_iota(jnp.int32, (rows, 1), 0) % SUBLANES) < (SUBLANES // 2)

    def proj_b(lo, hi):
        r = jnp.dot(u2, wb_ref[0, :, lo:hi], preferred_element_type=F32)
        if mixed:
            r = jnp.where(fwd_row, r, jnp.dot(u2, wb_ref[1, :, lo:hi], preferred_element_type=F32))
        return r.reshape(S5_TC, SUBLANES, S5_CH)

    bre[...] = proj_b(0, S5_CH)
    bim[...] = proj_b(S5_CH, 2 * S5_CH)

    a_re = are_ref[...]
    a_im = aim_ref[...]

    def outer(i, carry):
        sr, si = carry
        for j in range(S5_UNROLL):
            t = i * S5_UNROLL + j
            nr = a_re * sr - a_im * si + bre[t]
            ni = a_re * si + a_im * sr + bim[t]
            bre[t] = nr
            bim[t] = ni
            sr, si = nr, ni
        return sr, si

    sr, si = lax.fori_loop(0, S5_TC // S5_UNROLL, outer, (sre[...], sim[...]))
    sre[...] = sr
    sim[...] = si
    fre_ref[...] = sr
    fim_ref[...] = si

    xr = bre[...].reshape(rows, S5_CH).astype(BF16)
    xi = bim[...].reshape(rows, S5_CH).astype(BF16)

    def proj_c(d):
        return (jnp.dot(xr, wc_ref[d, 0:S5_CH, :], preferred_element_type=F32)
                + jnp.dot(xi, wc_ref[d, S5_CH:, :], preferred_element_type=F32))

    y = proj_c(0)
    if mixed:
        y = jnp.where(fwd_row, y, proj_c(1))
    y_ref[...] = y.reshape(S5_TC, SUBLANES, LANES)


def s5_scan(seqs, wb, wc, a_re, a_im, h0_re, h0_im, *, mixed):
    length, ns, _ = seqs.shape
    nsb = ns // SUBLANES
    nj = D_B // LANES
    nk = length // S5_TC
    nd = 2 if mixed else 1
    per_dir = nsb // 2 if not mixed else 1

    def dmap(sb):
        return 0 if mixed else sb // per_dir

    kern = functools.partial(_s5_kernel, mixed=mixed)
    return pl.pallas_call(
        kern,
        out_shape=(
            jax.ShapeDtypeStruct((length, ns, D_B), F32),
            jax.ShapeDtypeStruct((ns, G_B * P_B), F32),
            jax.ShapeDtypeStruct((ns, G_B * P_B), F32),
        ),
        grid=(nsb, nj, nk),
        in_specs=[
            pl.BlockSpec((S5_TC, SUBLANES, LANES), lambda sb, j, k: (k, sb, j)),
            pl.BlockSpec((nd, None, LANES, 2 * S5_CH), lambda sb, j, k: (dmap(sb), j, 0, 0)),
            pl.BlockSpec((nd, None, 2 * S5_CH, LANES), lambda sb, j, k: (dmap(sb), j, 0, 0)),
            pl.BlockSpec((None, SUBLANES, S5_CH), lambda sb, j, k: (dmap(sb), 0, j)),
            pl.BlockSpec((None, SUBLANES, S5_CH), lambda sb, j, k: (dmap(sb), 0, j)),
            pl.BlockSpec((SUBLANES, S5_CH), lambda sb, j, k: (sb, j)),
            pl.BlockSpec((SUBLANES, S5_CH), lambda sb, j, k: (sb, j)),
        ],
        out_specs=(
            pl.BlockSpec((S5_TC, SUBLANES, LANES), lambda sb, j, k: (k, sb, j)),
            pl.BlockSpec((SUBLANES, S5_CH), lambda sb, j, k: (sb, j)),
            pl.BlockSpec((SUBLANES, S5_CH), lambda sb, j, k: (sb, j)),
        ),
        scratch_shapes=[
            pltpu.VMEM((S5_TC, SUBLANES, S5_CH), F32), pltpu.VMEM((S5_TC, SUBLANES, S5_CH), F32),
            pltpu.VMEM((SUBLANES, S5_CH), F32), pltpu.VMEM((SUBLANES, S5_CH), F32),
        ],
        compiler_params=_cp(("arbitrary", "arbitrary", "arbitrary")),
        name="s5_scan_mixed" if mixed else "s5_scan",
    )(seqs, wb, wc, a_re, a_im, h0_re, h0_im)


def _s5fin_kernel(u_ref, y_ref, d_ref, w_ref, b_ref, o_ref):
    y = d_ref[...] * u_ref[...] + y_ref[...]
    y = 0.5 * y * (1.0 + jnp.tanh(math.sqrt(2.0 / math.pi) * (y + 0.044715 * (y * y * y))))
    z = jnp.dot(y.astype(BF16), w_ref[...], preferred_element_type=F32) + b_ref[...]
    o_ref[...] = (y * _sigmoid(z)).astype(o_ref.dtype)


def s5_finish(proj, y, d, w_glu, b_glu):
    tm = 512
    return pl.pallas_call(
        _s5fin_kernel,
        out_shape=jax.ShapeDtypeStruct((T_ALL, D_B), BF16),
        grid=(T_ALL // tm,),
        in_specs=[
            pl.BlockSpec((tm, D_B), lambda i: (i, C_UB // D_B)),
            pl.BlockSpec((tm, D_B), lambda i: (i, 0)),
            pl.BlockSpec((1, D_B), lambda i: (0, 0)),
            pl.BlockSpec((D_B, D_B), lambda i: (0, 0)),
            pl.BlockSpec((1, D_B), lambda i: (0, 0)),
        ],
        out_specs=pl.BlockSpec((tm, D_B), lambda i: (i, 0)),
        compiler_params=_cp(("arbitrary",)),
        name="s5_finish",
    )(proj, y, d, w_glu, b_glu)


def _merge_kernel(x_ref, gl_ref, oa_ref, ob_ref, oc_ref, mod_ref, g_ref, wa_ref, wb_ref, wc_ref, wo_ref,
                  x1_ref, h2_ref, h2row_ref):
    m = None
    for br, (o_ref, w_ref) in enumerate(((oa_ref, wa_ref), (ob_ref, wb_ref), (oc_ref, wc_ref))):
        gate = _sigmoid(gl_ref[:, br * D_MODEL:(br + 1) * D_MODEL])
        t = gate * jnp.dot(o_ref[...], w_ref[...], preferred_element_type=F32)
        m = t if m is None else m + t
    x1 = x_ref[...] + mod_ref[2:3, :] * jnp.dot(m.astype(BF16), wo_ref[...], preferred_element_type=F32)
    x1_ref[...] = x1
    h2 = _rms(x1, g_ref[...])
    h2 = h2 * (1.0 + mod_ref[4:5, :]) + mod_ref[3:4, :]
    h2_ref[...] = h2
    h2row_ref[...] = h2.reshape((h2.shape[0],) + ROW3)


def merge(x, proj, oa, ob, oc, mod, g, wa, wb, wc, wo):
    tm = 256
    full = lambda shape: pl.BlockSpec(shape, lambda i: (0,) * len(shape))
    return pl.pallas_call(
        _merge_kernel,
        out_shape=(jax.ShapeDtypeStruct((T_ALL, D_MODEL), F32), jax.ShapeDtypeStruct((T_ALL, D_MODEL), F32),
                   jax.ShapeDtypeStruct((T_ALL,) + ROW3, F32)),
        grid=(T_ALL // tm,),
        in_specs=[
            pl.BlockSpec((tm, D_MODEL), lambda i: (i, 0)),
            pl.BlockSpec((tm, 3 * D_MODEL), lambda i: (i, 0)),
            pl.BlockSpec((tm, 512), lambda i: (i, 0)),
            pl.BlockSpec((tm, 512), lambda i: (i, 0)),
            pl.BlockSpec((tm, 512), lambda i: (i, 0)),
            pl.BlockSpec((None, 6, D_MODEL), lambda i: (_group_of_tile(i, tm), 0, 0)),
            full((1, D_MODEL)),
            full((512, D_MODEL)), full((512, D_MODEL)), full((512, D_MODEL)), full((D_MODEL, D_MODEL)),
        ],
        out_specs=(pl.BlockSpec((tm, D_MODEL), lambda i: (i, 0)), pl.BlockSpec((tm, D_MODEL), lambda i: (i, 0)),
                   pl.BlockSpec((tm,) + ROW3, lambda i: (i, 0, 0))),
        compiler_params=_cp(("arbitrary",)),
        name="merge",
    )(x, proj, oa, ob, oc, mod, g, wa, wb, wc, wo)


ROUTE_TM = 256


def _router_kernel(h_ref, w_ref, b_ref, idx_ref, rank_ref, wgt_ref, cnt_ref, cnt_acc):
    i = pl.program_id(0)

    @pl.when(i == 0)
    def _():
        cnt_acc[...] = jnp.zeros_like(cnt_acc)

    logits = jnp.dot(h_ref[...], w_ref[...], preferred_element_type=F32,
                     precision=lax.Precision.HIGHEST) + b_ref[...]
    lane_i = lax.broadcasted_iota(jnp.int32, (ROUTE_TM, LANES), 1)
    lane = lane_i.astype(F32)
    r_i = lax.broadcasted_iota(jnp.int32, (ROUTE_TM, ROUTE_TM), 0)
    c_i = lax.broadcasted_iota(jnp.int32, (ROUTE_TM, ROUTE_TM), 1)
    earlier = jnp.where(c_i < r_i, 1.0, 0.0).astype(BF16)

    cnt = cnt_acc[...]
    idx_out = jnp.zeros((ROUTE_TM, LANES), F32)
    rank_out = jnp.zeros((ROUTE_TM, LANES), F32)
    val_out = jnp.zeros((ROUTE_TM, LANES), F32)
    v0 = None
    esum = None
    for k in range(TOP_K):
        m = logits.max(axis=-1, keepdims=True)
        sel = jnp.min(jnp.where(logits == m, lane, float(LANES)), axis=-1, keepdims=True)
        hit = lane == sel
        logits = jnp.where(hit, -jnp.inf, logits)
        onehot = jnp.where(hit, 1.0, 0.0)
        within = jnp.dot(earlier, onehot.astype(BF16), preferred_element_type=F32)
        rank = jnp.sum(onehot * (within + cnt), axis=-1, keepdims=True)
        cnt = cnt + jnp.sum(onehot, axis=0, keepdims=True)
        if k == 0:
            v0 = m
        e = jnp.exp(m - v0)
        esum = e if esum is None else esum + e
        idx_out = jnp.where(lane_i == k, sel, idx_out)
        rank_out = jnp.where(lane_i == k, rank, rank_out)
        val_out = jnp.where(lane_i == k, e, val_out)
    cnt_acc[...] = cnt
    cnt_ref[...] = cnt
    idx_ref[...] = idx_out.astype(jnp.int32)
    rank_ref[...] = rank_out.astype(jnp.int32)
    wgt_ref[...] = val_out * (1.0 / esum)


def router(h2, w_router, b_router):
    tile = lambda: pl.BlockSpec((ROUTE_TM, LANES), lambda i: (i, 0))
    return pl.pallas_call(
        _router_kernel,
        out_shape=(
            jax.ShapeDtypeStruct((T_ALL, LANES), jnp.int32),
            jax.ShapeDtypeStruct((T_ALL, LANES), jnp.int32),
            jax.ShapeDtypeStruct((T_ALL, LANES), F32),
            jax.ShapeDtypeStruct((1, LANES), F32),
        ),
        grid=(T_ALL // ROUTE_TM,),
        in_specs=[
            pl.BlockSpec((ROUTE_TM, D_MODEL), lambda i: (i, 0)),
            pl.BlockSpec((D_MODEL, LANES), lambda i: (0, 0)),
            pl.BlockSpec((1, LANES), lambda i: (0, 0)),
        ],
        out_specs=(tile(), tile(), tile(), pl.BlockSpec((1, LANES), lambda i: (0, 0))),
        scratch_shapes=[pltpu.VMEM((1, LANES), F32)],
        compiler_params=_cp(("arbitrary",)),
        name="router",
    )(h2, w_router, b_router)


PLAN_UNROLL = 8
N_DUMP = 5 * BLK
PLAN_LEN = BUF_LEN + BLK
assert TOP_K == 4 and T_ALL & (T_ALL - 1) == 0


def _invert_kernel(dest_ref, fill_hbm, inv_ref):
    pltpu.sync_copy(fill_hbm, inv_ref)
    tok_step = PLAN_UNROLL // TOP_K

    def put(i, c):
        for j in range(PLAN_UNROLL):
            inv_ref[dest_ref[i * PLAN_UNROLL + j]] = i * tok_step + ((j % TOP_K) * T_ALL + j // TOP_K)
        return c

    lax.fori_loop(0, N_ASSIGN // PLAN_UNROLL, put, 0)


def invert_plan(dest):
    r = jnp.arange(PLAN_LEN, dtype=jnp.int32)
    fill = N_ASSIGN + jnp.where(r < BLK, 2 * BLK + r, (r - BLK) & (2 * BLK - 1))
    return pl.pallas_call(
        _invert_kernel,
        out_shape=jax.ShapeDtypeStruct((PLAN_LEN,), jnp.int32),
        in_specs=[pl.BlockSpec(memory_space=pltpu.SMEM), pl.BlockSpec(memory_space=pl.ANY)],
        out_specs=pl.BlockSpec(memory_space=pltpu.SMEM),
        name="invert_plan",
    )(dest, fill)


PAIR_TILE = 2 * LANES


def _wperm_kernel(w_ref, s_ref, o_ref):
    w = w_ref[...].astype(BF16)
    for j in range(w.shape[1] // PAIR_TILE):
        cols = slice(j * PAIR_TILE, (j + 1) * PAIR_TILE)
        o_ref[:, cols] = jnp.dot(w[:, cols], s_ref[...], preferred_element_type=F32).astype(BF16)


def permute_w_up(w_up):
    tn = 512
    r = lax.broadcasted_iota(jnp.int32, (PAIR_TILE, PAIR_TILE), 0)
    c = lax.broadcasted_iota(jnp.int32, (PAIR_TILE, PAIR_TILE), 1)
    sel = (r == jnp.where(c < LANES, 2 * c, 2 * (c - LANES) + 1)).astype(BF16)
    w = w_up.reshape(DEPTH * N_EXP, D_MODEL, 2 * D_FF)
    out = pl.pallas_call(
        _wperm_kernel,
        out_shape=jax.ShapeDtypeStruct(w.shape, BF16),
        grid=(DEPTH * N_EXP, 2 * D_FF // tn),
        in_specs=[
            pl.BlockSpec((None, D_MODEL, tn), lambda e, j: (e, 0, j)),
            pl.BlockSpec((PAIR_TILE, PAIR_TILE), lambda e, j: (0, 0)),
        ],
        out_specs=pl.BlockSpec((None, D_MODEL, tn), lambda e, j: (e, 0, j)),
        compiler_params=_cp(("arbitrary", "arbitrary")),
        name="permute_w_up",
    )(w, sel)
    return out.reshape(DEPTH, N_EXP, D_MODEL, 2 * D_FF)


EXP_NBUF = 3


def _expert_kernel(be_ref, inv_ref, h_hbm, wu_ref, bu_ref, wd_ref, bd_ref, comb_hbm, xbuf, ybuf, gsem, ssem):
    del be_ref
    i = pl.program_id(0)
    n = pl.num_programs(0)
    cur = i % EXP_NBUF
    nxt = (i + 2) % EXP_NBUF
    prv = nxt

    def gather(block, s):
        base = (block + 1) * BLK
        for r in range(BLK):
            tok = inv_ref[base + r] & (T_ALL - 1)
            pltpu.make_async_copy(h_hbm.at[tok], xbuf.at[s, r], gsem.at[s]).start()

    def scatter(block, s):
        base = (block + 1) * BLK
        for r in range(BLK):
            a = inv_ref[base + r]
            pltpu.make_async_copy(ybuf.at[s, r], comb_hbm.at[a], ssem.at[s]).start()

    def wait_block(sem):
        pltpu.make_async_copy(h_hbm.at[pl.ds(0, BLK)], xbuf.at[0], sem).wait()

    @pl.when(i == 0)
    def _():
        ybuf[...] = jnp.zeros_like(ybuf)
        for s in range(EXP_NBUF - 1):
            for r in range(BLK):
                dump = N_ASSIGN + (3 + s) * BLK + r
                pltpu.make_async_copy(ybuf.at[s, r], comb_hbm.at[dump], ssem.at[s]).start()
        gather(0, 0)
        gather(1, 1)

    wait_block(gsem.at[cur])
    wait_block(ssem.at[cur])

    gather(jnp.minimum(i + 2, n - 1), nxt)
    scatter(i - 1, prv)

    x = xbuf[cur].reshape(BLK, D_MODEL).astype(BF16)
    h = jnp.dot(x, wu_ref[...], preferred_element_type=F32) + bu_ref[...]
    acts = []
    for j in range(2 * D_FF // PAIR_TILE):
        glu = jnp.minimum(h[:, j * PAIR_TILE:j * PAIR_TILE + LANES], SWIGLU_LIMIT)
        lin = jnp.clip(h[:, j * PAIR_TILE + LANES:(j + 1) * PAIR_TILE], -SWIGLU_LIMIT, SWIGLU_LIMIT)
        acts.append((glu * _sigmoid(SWIGLU_ALPHA * glu) * (lin + 1.0)).astype(BF16))
    act = jnp.concatenate(acts, axis=1)
    y = jnp.dot(act, wd_ref[...], preferred_element_type=F32) + bd_ref[...]
    ybuf[cur] = y.reshape((BLK,) + ROW3)

    @pl.when(i == n - 1)
    def _():
        for s in range(EXP_NBUF):
            @pl.when(s != cur)
            def _():
                wait_block(gsem.at[s])
                wait_block(ssem.at[s])
        scatter(i, cur)
        wait_block(ssem.at[cur])


def experts(layer, block_e, inv, h2row, wu, bu, wd, bd):
    per_expert = lambda *blk: pl.BlockSpec((None, None) + blk, lambda i, be, inv: (layer, be[i], 0, 0))
    return pl.pallas_call(
        _expert_kernel,
        out_shape=jax.ShapeDtypeStruct((N_ASSIGN + N_DUMP,) + ROW3, F32),
        grid_spec=pltpu.PrefetchScalarGridSpec(
            num_scalar_prefetch=2,
            grid=(N_BLOCKS,),
            in_specs=[
                pl.BlockSpec(memory_space=pl.ANY),
                per_expert(D_MODEL, 2 * D_FF), per_expert(1, 2 * D_FF), per_expert(D_FF, D_MODEL),
                per_expert(1, D_MODEL),
            ],
            out_specs=pl.BlockSpec(memory_space=pl.ANY),
            scratch_shapes=[
                pltpu.VMEM((EXP_NBUF, BLK) + ROW3, F32), pltpu.VMEM((EXP_NBUF, BLK) + ROW3, F32),
                pltpu.SemaphoreType.DMA((EXP_NBUF,)), pltpu.SemaphoreType.DMA((EXP_NBUF,)),
            ],
        ),
        compiler_params=_cp(("arbitrary",)),
        name="experts",
    )(block_e, inv, h2row, wu, bu, wd, bd)


COMB_TM = 256


def _combine_kernel(c0_ref, c1_ref, c2_ref, c3_ref, x_ref, w_ref, mod_ref, o_ref):
    w = w_ref[...]
    acc = None
    for k, c_ref in enumerate((c0_ref, c1_ref, c2_ref, c3_ref)):
        t = w[:, k:k + 1] * c_ref[...].reshape(COMB_TM, D_MODEL)
        acc = t if acc is None else acc + t
    o_ref[...] = x_ref[...] + mod_ref[5:6, :] * acc


def combine(comb, x1, wgt, mod):
    tiles = T_ALL // COMB_TM
    kth = lambda k: pl.BlockSpec((COMB_TM,) + ROW3, lambda i: (k * tiles + i, 0, 0))
    return pl.pallas_call(
        _combine_kernel,
        out_shape=jax.ShapeDtypeStruct((T_ALL, D_MODEL), F32),
        grid=(tiles,),
        in_specs=[
            kth(0), kth(1), kth(2), kth(3),
            pl.BlockSpec((COMB_TM, D_MODEL), lambda i: (i, 0)),
            pl.BlockSpec((COMB_TM, LANES), lambda i: (i, 0)),
            pl.BlockSpec((None, 6, D_MODEL), lambda i: (_group_of_tile(i, COMB_TM), 0, 0)),
        ],
        out_specs=pl.BlockSpec((COMB_TM, D_MODEL), lambda i: (i, 0)),
        compiler_params=_cp(("arbitrary",)),
        name="combine",
    )(comb, comb, comb, comb, x1, wgt, mod)


def _rope_table():
    pos = jnp.arange(DEC_SEQ)
    row = (pos // GRID_W).astype(F32)[:, None]
    col = (pos % GRID_W).astype(F32)[:, None]

    def parts(rot_dim):
        nf = rot_dim // 4
        inv = ROPE_BASE ** (-jnp.arange(nf, dtype=F32) / nf)
        cr, sr, cc, sc = jnp.cos(row * inv), jnp.sin(row * inv), jnp.cos(col * inv), jnp.sin(col * inv)
        z = jnp.zeros_like(sr)
        cos = jnp.concatenate([cr, cr, cc, cc], axis=1)
        lo = jnp.concatenate([-sr, z, -sc, z], axis=1)
        hi = jnp.concatenate([z, sr, z, sc], axis=1)
        return cos, lo, hi

    ca, la, ha = (jnp.tile(t, (1, 2)) for t in parts(HD_A))
    cc, lc, hc = parts(ROPE_C)
    pad = lambda t, fill: jnp.pad(t, ((0, 0), (NOPE_C, LANES - QK_C)), constant_values=fill)
    lat = jnp.concatenate([ca, la, ha, pad(cc, 1.0), pad(lc, 0.0), pad(hc, 0.0)], axis=1)
    ones, zeros = jnp.ones((DEC_SEQ, LANES), F32), jnp.zeros((DEC_SEQ, LANES), F32)
    ident = jnp.concatenate([ones, zeros, zeros, ones, zeros, zeros], axis=1)
    return jnp.concatenate([ident, lat], axis=0)


def _arrange_w_in(w_in):
    o = 0
    parts = {}
    for name, n in (("qa", HQ_A * HD_A), ("ka", HKV_A * HD_A), ("va", HKV_A * HD_A), ("ub", D_B), ("cq", Q_LORA),
                    ("ckv", KV_LORA), ("kr", ROPE_C), ("gl", 3 * D_MODEL)):
        parts[name] = w_in[..., o:o + n]
        o += n
    qa = parts["qa"].reshape(DEPTH, D_MODEL, HKV_A, G_A, 1, HD_A)
    eye = jnp.eye(HKV_A, dtype=F32).reshape(1, 1, HKV_A, 1, HKV_A, 1)
    qa_slots = (qa * eye).reshape(DEPTH, D_MODEL, HQ_A * LANES)
    kr = jnp.pad(parts["kr"], ((0, 0), (0, 0), (NOPE_C, LANES - QK_C)))
    w = jnp.concatenate([parts["gl"], qa_slots, parts["ub"], parts["cq"], parts["ka"], parts["va"], parts["ckv"], kr],
                        axis=-1)
    return w.astype(BF16)


def _s5_params(lam_re, lam_im, log_dt, b_re, b_im, c_re, c_im):
    dt = jnp.exp(log_dt)[..., None]
    decay = jnp.exp(lam_re * dt)
    ab_re, ab_im = decay * jnp.cos(lam_im * dt), decay * jnp.sin(lam_im * dt)
    den = lam_re * lam_re + lam_im * lam_im
    f_re = ((ab_re - 1) * lam_re + ab_im * lam_im) / den
    f_im = (ab_im * lam_re - (ab_re - 1) * lam_im) / den
    bb_re = f_re[..., None] * b_re - f_im[..., None] * b_im
    bb_im = f_re[..., None] * b_im + f_im[..., None] * b_re
    nj, gpb = D_B // LANES, LANES // GS_B
    eye = jnp.eye(gpb, dtype=F32)

    def blockdiag_b(bb):
        t = bb.transpose(0, 1, 3, 2).reshape(2, nj, gpb, GS_B, P_B)
        return (t[:, :, :, :, None, :] * eye[None, None, :, None, :, None]).reshape(2, nj, LANES, gpb * P_B)

    def blockdiag_c(cc):
        t = cc.transpose(0, 1, 3, 2).reshape(2, nj, gpb, P_B, GS_B)
        return (t[:, :, :, :, None, :] * eye[None, None, :, None, :, None]).reshape(2, nj, gpb * P_B, LANES)

    wb = jnp.concatenate([blockdiag_b(bb_re), blockdiag_b(bb_im)], axis=-1).astype(BF16)
    wc = jnp.concatenate([blockdiag_c(c_re), -blockdiag_c(c_im)], axis=-2).astype(BF16)
    return wb, wc, ab_re.reshape(2, G_B * P_B), ab_im.reshape(2, G_B * P_B)


def _to_time_major(u):
    f = u.transpose(1, 0, 2)
    return jnp.concatenate([f, f[::-1]], axis=1)


def _from_time_major(y, b):
    return (y[:, :b] + y[::-1, b:]).transpose(1, 0, 2)


def kernel(x_prompt, x_sample, c, cache_attn_k, cache_attn_v, cache_mla_ckv, cache_mla_krope, state_ssm_re, state_ssm_im, c_ctx, w_ada, b_ada, norm_mix_g, norm_ffn_g, w_in, q_norm_a, k_norm_a, sink_a, q_a_norm_c, kv_a_norm_c, w_uq_c, w_ukv_c, q_norm_c, k_norm_c, ssm_lam_re, ssm_lam_im, ssm_log_dt, ssm_b_re, ssm_b_im, ssm_c_re, ssm_c_im, ssm_d, w_glu, b_glu, w_br_a, w_br_b, w_br_c, w_out, w_router, b_router, w_up, b_up, w_down, b_down):
    x = jnp.concatenate([x_prompt.reshape(N_CTX, D_MODEL), x_sample.reshape(N_LAT, D_MODEL)], axis=0)
    cvecs = jnp.concatenate([c_ctx[None], c, jnp.zeros((N_GROUPS - 1 - DEC_BATCH, D_MODEL), F32)], axis=0)
    mods = adaln(cvecs, w_ada, b_ada)

    tab = _rope_table()
    w_in_r = _arrange_w_in(w_in)
    pad_slot = lambda g: jnp.pad(g, ((0, 0), (0, LANES - QK_C))).reshape(DEPTH, 1, LANES)
    gqa = jnp.tile(q_norm_a, (1, 2)).reshape(DEPTH, 1, LANES)
    gka = jnp.tile(k_norm_a, (1, 2)).reshape(DEPTH, 1, LANES)
    gqc, gkc = pad_slot(q_norm_c), pad_slot(k_norm_c)
    wuq = jnp.pad(w_uq_c.reshape(DEPTH, Q_LORA, H_C, QK_C), ((0, 0), (0, 0), (0, 0), (0, LANES - QK_C)))
    wuq = wuq.reshape(DEPTH, Q_LORA, H_C * LANES).astype(BF16)
    wukv4 = w_ukv_c.reshape(DEPTH, KV_LORA, H_C, NOPE_C + V_C)
    wuk = jnp.pad(wukv4[..., :NOPE_C], ((0, 0), (0, 0), (0, 0), (0, LANES - NOPE_C))).reshape(DEPTH, KV_LORA, H_C * LANES)
    wuv = wukv4[..., NOPE_C:].reshape(DEPTH, KV_LORA, H_C * V_C)
    wukv = jnp.concatenate([wuk, wuv], axis=-1).astype(BF16)
    sink = jnp.broadcast_to(sink_a[:, :, None], (DEPTH, HQ_A, LANES))
    w_router_p = jnp.pad(w_router, ((0, 0), (0, 0), (0, LANES - N_EXP)))
    b_router_p = jnp.pad(b_router, ((0, 0), (0, LANES - N_EXP)), constant_values=-jnp.inf).reshape(DEPTH, 1, LANES)
    w_up_p = permute_w_up(w_up)
    b_up_p = b_up.reshape(DEPTH, N_EXP, 2 * D_FF // PAIR_TILE, LANES, 2).transpose(0, 1, 2, 4, 3)
    b_up_p = b_up_p.reshape(DEPTH, N_EXP, 1, 2 * D_FF)
    w_down_b = w_down.astype(BF16)
    b_down_r = b_down.reshape(DEPTH, N_EXP, 1, D_MODEL)
    kr_cache = jnp.pad(cache_mla_krope, ((0, 0), (0, 0), (0, 0), (NOPE_C, LANES - QK_C)))
    zeros_state = jnp.zeros((2 * BATCH, G_B * P_B), F32)

    new_k, new_v, new_ckv, new_kr, new_sre, new_sim = [], [], [], [], [], []
    for l in range(DEPTH):
        mod = mods[l]
        proj = inproj(x, norm_mix_g[l], mod, w_in_r[l])
        qa, ka, qc, kc, vc, ckvn = prep(proj, tab, gqa[l], gka[l], q_a_norm_c[l].reshape(1, Q_LORA),
                                         kv_a_norm_c[l].reshape(1, KV_LORA), gqc[l], gkc[l], wuq[l], wukv[l])
        kc_cache, vc_cache = cache_keys(cache_mla_ckv[:, l].reshape(DEC_BATCH * PAST_LEN, KV_LORA),
                                        kr_cache[:, l].reshape(DEC_BATCH * PAST_LEN, LANES), gkc[l], wukv[l])
        oa = attn_a_lat(qa, ka, proj, cache_attn_k[:, l].reshape(DEC_BATCH, PAST_LEN, LANES),
                        cache_attn_v[:, l].reshape(DEC_BATCH, PAST_LEN, LANES), sink[l],
                        attn_a_ctx(qa, ka, proj, sink[l]))
        oc = attn_c_lat(qc, kc, vc, kc_cache, vc_cache, attn_c_ctx(qc, kc, vc))
        wb, wc, a_re, a_im = _s5_params(ssm_lam_re[l], ssm_lam_im[l], ssm_log_dt[l], ssm_b_re[l], ssm_b_im[l],
                                        ssm_c_re[l], ssm_c_im[l])
        ub = proj[:, C_UB:C_UB + D_B]
        a8 = lambda a: jnp.broadcast_to(a[:, None, :], (2, SUBLANES, G_B * P_B))
        a_mixed = lambda a: jnp.repeat(a, SUBLANES // 2, axis=0)[None]
        y_ctx, f_re, f_im = s5_scan(_to_time_major(ub[:N_CTX].reshape(BATCH, SEQ, D_B)), wb, wc, a8(a_re), a8(a_im),
                                    zeros_state, zeros_state, mixed=False)
        h0 = lambda s: s[:, l].transpose(1, 0, 2, 3).reshape(2 * DEC_BATCH, G_B * P_B)
        y_lat, _, _ = s5_scan(_to_time_major(ub[N_CTX:].reshape(DEC_BATCH, DEC_SEQ, D_B)), wb, wc,
                              a_mixed(a_re), a_mixed(a_im), h0(state_ssm_re), h0(state_ssm_im), mixed=True)
        y = jnp.concatenate([_from_time_major(y_ctx, BATCH).reshape(N_CTX, D_B),
                             _from_time_major(y_lat, DEC_BATCH).reshape(N_LAT, D_B)], axis=0)
        ob = s5_finish(proj, y, ssm_d[l].reshape(1, D_B), w_glu[l].astype(BF16), b_glu[l].reshape(1, D_B))
        x1, h2, h2row = merge(x, proj, oa, ob, oc, mod, norm_ffn_g[l].reshape(1, D_MODEL), w_br_a[l].astype(BF16),
                       w_br_b[l].astype(BF16), w_br_c[l].astype(BF16), w_out[l].astype(BF16))
        idx, rank, wgt, cnt = router(h2, w_router_p[l], b_router_p[l])
        counts = cnt[0, :N_EXP].astype(jnp.int32)
        padded = (counts + BLK - 1) // BLK * BLK
        pad_end = jnp.cumsum(padded)
        pad_start = pad_end - padded
        dest = (BLK + pad_start[idx[:, :TOP_K]] + rank[:, :TOP_K]).reshape(-1).astype(jnp.int32)
        block_start = jnp.arange(N_BLOCKS, dtype=jnp.int32) * BLK
        block_e = jnp.minimum(jnp.sum(pad_end[None, :] <= block_start[:, None], axis=1), N_EXP - 1)
        comb = experts(l, block_e.astype(jnp.int32), invert_plan(dest), h2row, w_up_p, b_up_p, w_down_b, b_down_r)
        x = combine(comb, x1, wgt, mod)

        new_k.append(ka[:N_CTX].reshape(BATCH, SEQ, HKV_A, HD_A))
        new_v.append(proj[:N_CTX, C_VA:C_VA + LANES].reshape(BATCH, SEQ, HKV_A, HD_A))
        new_ckv.append(ckvn[:N_CTX].reshape(BATCH, SEQ, KV_LORA))
        new_kr.append(proj[:N_CTX, C_KR + NOPE_C:C_KR + QK_C].reshape(BATCH, SEQ, ROPE_C))
        new_sre.append(f_re.reshape(2, BATCH, G_B, P_B).transpose(1, 0, 2, 3))
        new_sim.append(f_im.reshape(2, BATCH, G_B, P_B).transpose(1, 0, 2, 3))

    y_prompt = x[:N_CTX].reshape(BATCH, SEQ, D_MODEL)
    y_sample = x[N_CTX:].reshape(DEC_BATCH, DEC_SEQ, D_MODEL)
    return (y_prompt, y_sample, jnp.stack(new_k, axis=1), jnp.stack(new_v, axis=1), jnp.stack(new_ckv, axis=1),
            jnp.stack(new_kr, axis=1), jnp.stack(new_sre, axis=1), jnp.stack(new_sim, axis=1))
```

```python
import functools
import math

import jax
import jax.numpy as jnp
from jax import lax
from jax.experimental import pallas as pl
from jax.experimental.pallas import tpu as pltpu

D_MODEL = 1024
BATCH = 32
SEQ = 256
DEPTH = 2
DEC_BATCH = 4
DEC_SEQ = 2048
PAST_LEN = 256
GRID_W = 64
ROPE_BASE = 10000.0
EPS = 1e-6
NEG_INF = -1e30
BLK = 128
HQ_A, HKV_A, HD_A = 8, 2, 64
G_A = HQ_A // HKV_A
WINDOW = 128
D_B, GS_B, P_B = 512, 16, 64
G_B = D_B // GS_B
H_C, Q_LORA, KV_LORA, NOPE_C, ROPE_C, V_C = 8, 256, 128, 64, 32, 64
QK_C = NOPE_C + ROPE_C
N_EXP, TOP_K, D_FF = 32, 4, 1024
SWIGLU_ALPHA, SWIGLU_LIMIT = 1.702, 7.0

N_CTX = BATCH * SEQ
N_LAT = DEC_BATCH * DEC_SEQ
T_ALL = N_CTX + N_LAT
N_GROUPS = 8

LANES = 128
SUBLANES = 8

C_GL = 0
C_QA = 3 * D_MODEL
C_UB = C_QA + HQ_A * LANES
C_CQ = C_UB + D_B
C_KA = C_CQ + Q_LORA
C_VA = C_KA + LANES
C_CKV = C_VA + LANES
C_KR = C_CKV + LANES
N_PROJ = C_KR + LANES

N_ASSIGN = T_ALL * TOP_K
N_BLOCKS = N_ASSIGN // BLK + N_EXP
BUF_LEN = N_BLOCKS * BLK

F32 = jnp.float32
BF16 = jnp.bfloat16
ROW3 = (D_MODEL // LANES, LANES)
VMEM_LIMIT = 56 * 1024 * 1024


def _cp(sem, vmem=VMEM_LIMIT):
    return pltpu.CompilerParams(dimension_semantics=sem, vmem_limit_bytes=vmem)


def _group_of_tile(i, tm):
    n_ctx_tiles = N_CTX // tm
    per_batch = DEC_SEQ // tm
    return jnp.where(i < n_ctx_tiles, 0, 1 + (i - n_ctx_tiles) // per_batch)


def _sigmoid(x):
    return 1.0 / (1.0 + jnp.exp(-x))


def _adaln_kernel(c_ref, w_ref, b_ref, o_ref):
    c = c_ref[...]
    s = c * _sigmoid(c)
    o_ref[...] = jnp.dot(s.astype(BF16), w_ref[...].astype(BF16), preferred_element_type=F32) + b_ref[...]


def adaln(cvecs, w_ada, b_ada):
    tn = 1536
    out = pl.pallas_call(
        _adaln_kernel,
        out_shape=jax.ShapeDtypeStruct((DEPTH, N_GROUPS, 6 * D_MODEL), F32),
        grid=(DEPTH, 6 * D_MODEL // tn),
        in_specs=[
            pl.BlockSpec((N_GROUPS, D_MODEL), lambda l, j: (0, 0)),
            pl.BlockSpec((None, D_MODEL, tn), lambda l, j: (l, 0, j)),
            pl.BlockSpec((None, 1, tn), lambda l, j: (l, 0, j)),
        ],
        out_specs=pl.BlockSpec((None, N_GROUPS, tn), lambda l, j: (l, 0, j)),
        compiler_params=_cp(("arbitrary", "arbitrary")),
        name="adaln",
    )(cvecs, w_ada, b_ada.reshape(DEPTH, 1, 6 * D_MODEL))
    return out.reshape(DEPTH, N_GROUPS, 6, D_MODEL)


def _rms(x, g):
    ms = jnp.mean(x * x, axis=-1, keepdims=True)
    return x * lax.rsqrt(ms + EPS) * g


def _inproj_kernel(x_ref, g_ref, mod_ref, w_ref, o_ref):
    h = _rms(x_ref[...], g_ref[...])
    h = h * (1.0 + mod_ref[1:2, :]) + mod_ref[0:1, :]
    o_ref[...] = jnp.dot(h.astype(BF16), w_ref[...], preferred_element_type=F32)


def inproj(x, g, mod, w):
    tm, tn = 512, 1792
    return pl.pallas_call(
        _inproj_kernel,
        out_shape=jax.ShapeDtypeStruct((T_ALL, N_PROJ), F32),
        grid=(N_PROJ // tn, T_ALL // tm),
        in_specs=[
            pl.BlockSpec((tm, D_MODEL), lambda j, i: (i, 0)),
            pl.BlockSpec((1, D_MODEL), lambda j, i: (0, 0)),
            pl.BlockSpec((None, 6, D_MODEL), lambda j, i: (_group_of_tile(i, tm), 0, 0)),
            pl.BlockSpec((D_MODEL, tn), lambda j, i: (0, j)),
        ],
        out_specs=pl.BlockSpec((tm, tn), lambda j, i: (i, j)),
        compiler_params=_cp(("arbitrary", "arbitrary")),
        name="inproj",
    )(x, g.reshape(1, D_MODEL), mod, w)


def _rope(slab, cos, sin_lo, sin_hi, half):
    up = pltpu.roll(slab, LANES - half, axis=1)
    dn = pltpu.roll(slab, half, axis=1)
    return slab * cos + up * sin_lo + dn * sin_hi


def _slot_norm(slab, gain, n_real):
    ms = jnp.sum(slab * slab, axis=-1, keepdims=True) * (1.0 / n_real)
    return slab * lax.rsqrt(ms + EPS) * gain


def _mla_keys(ckvn, kr_blk, wukv_ref, gk, rope_c):
    kv = jnp.dot(ckvn.astype(BF16), wukv_ref[...], preferred_element_type=F32)
    ks = []
    for h in range(H_C):
        slab = kv[:, h * LANES:(h + 1) * LANES] + kr_blk
        slab = _slot_norm(slab, gk, QK_C)
        if rope_c is not None:
            slab = _rope(slab, *rope_c, ROPE_C // 4)
        ks.append(slab.astype(BF16))
    return jnp.concatenate(ks, axis=1), kv[:, H_C * LANES:].astype(BF16)


PREP_TM = 256


def _prep_kernel(qa_ref, cq_ref, ka_ref, ckv_ref, kr_ref, tab_ref,
                 gqa_ref, gka_ref, gcq_ref, gckv_ref, gqc_ref, gkc_ref, wuq_ref, wukv_ref,
                 qa_o, ka_o, qc_o, kc_o, vc_o, ckvn_o):
    def body(rotate):
        if rotate:
            tab = tab_ref[...]
            rope_a = (tab[:, 0:128], tab[:, 128:256], tab[:, 256:384])
            rope_c = (tab[:, 384:512], tab[:, 512:640], tab[:, 640:768])
            rot_a = lambda x: _rope(x, *rope_a, HD_A // 4)
            rot_c = lambda x: _rope(x, *rope_c, ROPE_C // 4)
        else:
            rope_c = None
            rot_a = rot_c = lambda x: x
        lane = lax.broadcasted_iota(jnp.int32, (1, LANES), 1)

        gqa = gqa_ref[...]
        qa = qa_ref[...]
        outs = []
        for h in range(HQ_A):
            slab = _slot_norm(qa[:, h * LANES:(h + 1) * LANES], gqa, HD_A)
            outs.append((rot_a(slab) * (HD_A ** -0.5)).astype(BF16))
        qa_o[...] = jnp.concatenate(outs, axis=1)

        ka = ka_ref[...]
        sq = ka * ka
        lo = lane < HD_A
        ms_lo = jnp.sum(jnp.where(lo, sq, 0.0), axis=-1, keepdims=True)
        ms_hi = jnp.sum(jnp.where(lo, 0.0, sq), axis=-1, keepdims=True)
        rs = jnp.where(lo, lax.rsqrt(ms_lo * (1.0 / HD_A) + EPS), lax.rsqrt(ms_hi * (1.0 / HD_A) + EPS))
        ka_o[...] = rot_a(ka * rs * gka_ref[...])

        cqn = _rms(cq_ref[...], gcq_ref[...])
        q = jnp.dot(cqn.astype(BF16), wuq_ref[...], preferred_element_type=F32)
        gqc = gqc_ref[...]
        outs = []
        for h in range(H_C):
            slab = _slot_norm(q[:, h * LANES:(h + 1) * LANES], gqc, QK_C)
            outs.append((rot_c(slab) * (QK_C ** -0.5)).astype(BF16))
        qc_o[...] = jnp.concatenate(outs, axis=1)

        ckvn = _rms(ckv_ref[...], gckv_ref[...])
        ckvn_o[...] = ckvn
        kc, vc = _mla_keys(ckvn, kr_ref[...], wukv_ref, gkc_ref[...], rope_c)
        kc_o[...] = kc
        vc_o[...] = vc

    is_latent = pl.program_id(0) >= N_CTX // PREP_TM
    pl.when(is_latent)(lambda: body(True))
    pl.when(jnp.logical_not(is_latent))(lambda: body(False))


def prep(proj, tab, gqa, gka, gcq, gckv, gqc, gkc, wuq, wukv):
    tm = PREP_TM
    n_ctx_tiles = N_CTX // tm
    per_batch = DEC_SEQ // tm

    def tab_map(i):
        return (jnp.where(i < n_ctx_tiles, 0, per_batch + (i - n_ctx_tiles) % per_batch), 0)

    def col(width, off):
        return pl.BlockSpec((tm, width), lambda i: (i, off // width))

    def full(shape):
        return pl.BlockSpec(shape, lambda i: (0,) * len(shape))

    def row_out(width):
        return pl.BlockSpec((tm, width), lambda i: (i, 0))

    return pl.pallas_call(
        _prep_kernel,
        out_shape=(
            jax.ShapeDtypeStruct((T_ALL, HQ_A * LANES), BF16),
            jax.ShapeDtypeStruct((T_ALL, LANES), F32),
            jax.ShapeDtypeStruct((T_ALL, H_C * LANES), BF16),
            jax.ShapeDtypeStruct((T_ALL, H_C * LANES), BF16),
            jax.ShapeDtypeStruct((T_ALL, H_C * V_C), BF16),
            jax.ShapeDtypeStruct((T_ALL, KV_LORA), F32),
        ),
        grid=(T_ALL // tm,),
        in_specs=[
            col(HQ_A * LANES, C_QA), col(Q_LORA, C_CQ), col(LANES, C_KA), col(LANES, C_CKV), col(LANES, C_KR),
            pl.BlockSpec((tm, 6 * LANES), tab_map),
            full((1, LANES)), full((1, LANES)), full((1, Q_LORA)), full((1, KV_LORA)),
            full((1, LANES)), full((1, LANES)),
            full((Q_LORA, H_C * LANES)), full((KV_LORA, H_C * LANES + H_C * V_C)),
        ],
        out_specs=(row_out(HQ_A * LANES), row_out(LANES), row_out(H_C * LANES), row_out(H_C * LANES),
                   row_out(H_C * V_C), row_out(KV_LORA)),
        compiler_params=_cp(("arbitrary",)),
        name="prep",
    )(proj, proj, proj, proj, proj, tab, gqa, gka, gcq, gckv, gqc, gkc, wuq, wukv)


def _cachekeys_kernel(ckv_ref, kr_ref, gkc_ref, wukv_ref, kc_o, vc_o):
    kc, vc = _mla_keys(ckv_ref[...], kr_ref[...], wukv_ref, gkc_ref[...], None)
    kc_o[...] = kc
    vc_o[...] = vc


def cache_keys(ckv, kr_blk, gkc, wukv):
    r = ckv.shape[0]
    tm = 256
    return pl.pallas_call(
        _cachekeys_kernel,
        out_shape=(jax.ShapeDtypeStruct((r, H_C * LANES), BF16), jax.ShapeDtypeStruct((r, H_C * V_C), BF16)),
        grid=(r // tm,),
        in_specs=[
            pl.BlockSpec((tm, LANES), lambda i: (i, 0)),
            pl.BlockSpec((tm, LANES), lambda i: (i, 0)),
            pl.BlockSpec((1, LANES), lambda i: (0, 0)),
            pl.BlockSpec((KV_LORA, H_C * LANES + H_C * V_C), lambda i: (0, 0)),
        ],
        out_specs=(pl.BlockSpec((tm, H_C * LANES), lambda i: (i, 0)), pl.BlockSpec((tm, H_C * V_C), lambda i: (i, 0))),
        compiler_params=_cp(("arbitrary",)),
        name="cache_keys",
    )(ckv, kr_blk, gkc, wukv)


def _attn_body(q_ref, segs, sink_ref, o_ref, *, n_heads, k_slot, v_slab, v_half, tq, band_qi=None):
    outs = []
    for h in range(n_heads):
        qh = q_ref[:, h * LANES:(h + 1) * LANES]
        scores = []
        for k_ref, _, off in segs:
            kh = k_ref[:, k_slot(h) * LANES:(k_slot(h) + 1) * LANES].astype(BF16)
            s = lax.dot_general(qh, kh, (((1,), (1,)), ((), ())), preferred_element_type=F32)
            if off is not None:
                tk = s.shape[1]
                blk = band_qi + off
                q_pos = band_qi * tq + lax.broadcasted_iota(jnp.int32, (tq, tk), 0)
                k_pos = blk * tk + lax.broadcasted_iota(jnp.int32, (tq, tk), 1)
                ok = (jnp.abs(k_pos - q_pos) <= WINDOW) & (blk >= 0) & (blk < DEC_SEQ // tk)
                s = jnp.where(ok, s, NEG_INF)
            scores.append(s)
        m = scores[0].max(axis=-1, keepdims=True)
        for s in scores[1:]:
            m = jnp.maximum(m, s.max(axis=-1, keepdims=True))
        if sink_ref is not None:
            sink = sink_ref[h:h + 1, 0:1]
            m = jnp.maximum(m, sink)
            denom = jnp.exp(sink - m)
        else:
            denom = jnp.zeros_like(m)
        acc = None
        for s, (_, v_ref, _) in zip(scores, segs):
            p = jnp.exp(s - m)
            denom = denom + p.sum(axis=-1, keepdims=True)
            vs = v_ref[:, v_slab(h) * LANES:(v_slab(h) + 1) * LANES].astype(BF16)
            pv = jnp.dot(p.astype(BF16), vs, preferred_element_type=F32)
            acc = pv if acc is None else acc + pv
        half = v_half(h)
        outs.append(acc[:, half * 64:(half + 1) * 64] * (1.0 / denom))
    o_ref[...] = jnp.concatenate(outs, axis=1).astype(o_ref.dtype)


_C_CFG = dict(n_heads=H_C, k_slot=lambda h: h, v_slab=lambda h: h // 2, v_half=lambda h: h % 2)


def _attn_a_body(q_ref, segs, sink_ref, o_ref, *, tq, band_qi=None):
    rows = HQ_A * tq
    q = jnp.concatenate([q_ref[:, h * LANES:(h + 1) * LANES] for h in range(HQ_A)], axis=0)
    sink = jnp.concatenate([jnp.broadcast_to(sink_ref[h:h + 1, 0:1], (tq, 1)) for h in range(HQ_A)], axis=0)
    scores = []
    for k_ref, _, off in segs:
        s = lax.dot_general(q, k_ref[...].astype(BF16), (((1,), (1,)), ((), ())), preferred_element_type=F32)
        if off is not None:
            tk = s.shape[1]
            blk = band_qi + off
            q_pos = band_qi * tq + (lax.broadcasted_iota(jnp.int32, (rows, tk), 0) & (tq - 1))
            k_pos = blk * tk + lax.broadcasted_iota(jnp.int32, (rows, tk), 1)
            ok = (jnp.abs(k_pos - q_pos) <= WINDOW) & (blk >= 0) & (blk < DEC_SEQ // tk)
            s = jnp.where(ok, s, NEG_INF)
        scores.append(s)
    m = sink
    for s in scores:
        m = jnp.maximum(m, s.max(axis=-1, keepdims=True))
    denom = jnp.exp(sink - m)
    acc = None
    for s, (_, v_ref, _) in zip(scores, segs):
        p = jnp.exp(s - m)
        denom = denom + p.sum(axis=-1, keepdims=True)
        pv = jnp.dot(p.astype(BF16), v_ref[...].astype(BF16), preferred_element_type=F32)
        acc = pv if acc is None else acc + pv
    acc = acc * (1.0 / denom)
    outs = []
    for h in range(HQ_A):
        kv = h // G_A
        outs.append(acc[h * tq:(h + 1) * tq, kv * HD_A:(kv + 1) * HD_A])
    o_ref[...] = jnp.concatenate(outs, axis=1).astype(o_ref.dtype)


def _attn_a_ctx_kernel(q_ref, k_ref, v_ref, sink_ref, o_init, o_ref):
    del o_init
    _attn_a_body(q_ref, [(k_ref, v_ref, None)], sink_ref, o_ref, tq=SEQ)


def _attn_a_lat_kernel(q_ref, k0, k1, k2, v0, v1, v2, kc_ref, vc_ref, sink_ref, o_ctx, o_ref):
    del o_ctx
    qi = pl.program_id(1)
    segs = [(k0, v0, -1), (k1, v1, 0), (k2, v2, 1), (kc_ref, vc_ref, None)]
    _attn_a_body(q_ref, segs, sink_ref, o_ref, tq=BLK, band_qi=qi)


def _attn_c_ctx_kernel(q_ref, k_ref, v_ref, o_init, o_ref):
    del o_init
    _attn_body(q_ref, [(k_ref, v_ref, None)], None, o_ref, tq=SEQ, **_C_CFG)


def _attn_c_lat_kernel(q_ref, k_ref, v_ref, kc_ref, vc_ref, o_ctx, o_ref):
    del o_ctx
    _attn_body(q_ref, [(k_ref, v_ref, None), (kc_ref, vc_ref, None)], None, o_ref, tq=256, **_C_CFG)


def attn_a_ctx(qa, ka, proj, sink):
    return pl.pallas_call(
        _attn_a_ctx_kernel,
        out_shape=jax.ShapeDtypeStruct((T_ALL, HQ_A * HD_A), BF16),
        grid=(BATCH,),
        in_specs=[
            pl.BlockSpec((SEQ, HQ_A * LANES), lambda b: (b, 0)),
            pl.BlockSpec((SEQ, LANES), lambda b: (b, 0)),
            pl.BlockSpec((SEQ, LANES), lambda b: (b, C_VA // LANES)),
            pl.BlockSpec((SUBLANES, LANES), lambda b: (0, 0)),
            pl.BlockSpec(memory_space=pl.ANY),
        ],
        out_specs=pl.BlockSpec((SEQ, HQ_A * HD_A), lambda b: (b, 0)),
        input_output_aliases={4: 0},
        compiler_params=_cp(("arbitrary",)),
        name="attn_a_ctx",
    )(qa, ka, proj, sink, jnp.zeros((T_ALL, HQ_A * HD_A), BF16))


def attn_a_lat(qa, ka, proj, k_cache, v_cache, sink, o_all):
    nb = DEC_SEQ // BLK
    base = N_CTX // BLK

    def band(off, colblk):
        return pl.BlockSpec((BLK, LANES), lambda b, i: (base + b * nb + jnp.clip(i + off, 0, nb - 1), colblk))

    return pl.pallas_call(
        _attn_a_lat_kernel,
        out_shape=jax.ShapeDtypeStruct((T_ALL, HQ_A * HD_A), BF16),
        grid=(DEC_BATCH, nb),
        in_specs=[
            pl.BlockSpec((BLK, HQ_A * LANES), lambda b, i: (base + b * nb + i, 0)),
            band(-1, 0), band(0, 0), band(1, 0),
            band(-1, C_VA // LANES), band(0, C_VA // LANES), band(1, C_VA // LANES),
            pl.BlockSpec((None, PAST_LEN, LANES), lambda b, i: (b, 0, 0)),
            pl.BlockSpec((None, PAST_LEN, LANES), lambda b, i: (b, 0, 0)),
            pl.BlockSpec((SUBLANES, LANES), lambda b, i: (0, 0)),
            pl.BlockSpec(memory_space=pl.ANY),
        ],
        out_specs=pl.BlockSpec((BLK, HQ_A * HD_A), lambda b, i: (base + b * nb + i, 0)),
        input_output_aliases={10: 0},
        compiler_params=_cp(("arbitrary", "arbitrary")),
        name="attn_a_lat",
    )(qa, ka, ka, ka, proj, proj, proj, k_cache, v_cache, sink, o_all)


def attn_c_ctx(qc, kc, vc):
    return pl.pallas_call(
        _attn_c_ctx_kernel,
        out_shape=jax.ShapeDtypeStruct((T_ALL, H_C * V_C), BF16),
        grid=(BATCH,),
        in_specs=[
            pl.BlockSpec((SEQ, H_C * LANES), lambda b: (b, 0)),
            pl.BlockSpec((SEQ, H_C * LANES), lambda b: (b, 0)),
            pl.BlockSpec((SEQ, H_C * V_C), lambda b: (b, 0)),
            pl.BlockSpec(memory_space=pl.ANY),
        ],
        out_specs=pl.BlockSpec((SEQ, H_C * V_C), lambda b: (b, 0)),
        input_output_aliases={3: 0},
        compiler_params=_cp(("arbitrary",)),
        name="attn_c_ctx",
    )(qc, kc, vc, jnp.zeros((T_ALL, H_C * V_C), BF16))


def attn_c_lat(qc, kc, vc, kc_cache, vc_cache, o_all):
    tq = 256
    nq = DEC_SEQ // tq
    qbase = N_CTX // tq
    kbase = N_CTX // DEC_SEQ
    return pl.pallas_call(
        _attn_c_lat_kernel,
        out_shape=jax.ShapeDtypeStruct((T_ALL, H_C * V_C), BF16),
        grid=(DEC_BATCH, nq),
        in_specs=[
            pl.BlockSpec((tq, H_C * LANES), lambda b, i: (qbase + b * nq + i, 0)),
            pl.BlockSpec((DEC_SEQ, H_C * LANES), lambda b, i: (kbase + b, 0)),
            pl.BlockSpec((DEC_SEQ, H_C * V_C), lambda b, i: (kbase + b, 0)),
            pl.BlockSpec((PAST_LEN, H_C * LANES), lambda b, i: (b, 0)),
            pl.BlockSpec((PAST_LEN, H_C * V_C), lambda b, i: (b, 0)),
            pl.BlockSpec(memory_space=pl.ANY),
        ],
        out_specs=pl.BlockSpec((tq, H_C * V_C), lambda b, i: (qbase + b * nq + i, 0)),
        input_output_aliases={5: 0},
        compiler_params=_cp(("arbitrary", "arbitrary")),
        name="attn_c_lat",
    )(qc, kc, vc, kc_cache, vc_cache, o_all)


S5_TC = 256
S5_CH = 512
S5_UNROLL = 8


def _s5_kernel(u_ref, wb_ref, wc_ref, are_ref, aim_ref, h0re_ref, h0im_ref,
               y_ref, fre_ref, fim_ref, bre, bim, sre, sim, *, mixed):
    k = pl.program_id(2)

    @pl.when(k == 0)
    def _():
        sre[...] = h0re_ref[...]
        sim[...] = h0im_ref[...]

    rows = S5_TC * SUBLANES
    u2 = u_ref[...].reshape(rows, LANES).astype(BF16)
    if mixed:
        fwd_row = (lax.broadcasted_iota(jnp.int32, (rows, 1), 0) % SUBLANES) < (SUBLANES // 2)

    def proj_b(lo, hi):
        r = jnp.dot(u2, wb_ref[0, :, lo:hi], preferred_element_type=F32)
        if mixed:
            r = jnp.where(fwd_row, r, jnp.dot(u2, wb_ref[1, :, lo:hi], preferred_element_type=F32))
        return r.reshape(S5_TC, SUBLANES, S5_CH)

    bre[...] = proj_b(0, S5_CH)
    bim[...] = proj_b(S5_CH, 2 * S5_CH)

    a_re = are_ref[...]
    a_im = aim_ref[...]

    def outer(i, carry):
        sr, si = carry
        for j in range(S5_UNROLL):
            t = i * S5_UNROLL + j
            nr = a_re * sr - a_im * si + bre[t]
            ni = a_re * si + a_im * sr + bim[t]
            bre[t] = nr
            bim[t] = ni
            sr, si = nr, ni
        return sr, si

    sr, si = lax.fori_loop(0, S5_TC // S5_UNROLL, outer, (sre[...], sim[...]))
    sre[...] = sr
    sim[...] = si
    fre_ref[...] = sr
    fim_ref[...] = si

    xr = bre[...].reshape(rows, S5_CH).astype(BF16)
    xi = bim[...].reshape(rows, S5_CH).astype(BF16)

    def proj_c(d):
        return (jnp.dot(xr, wc_ref[d, 0:S5_CH, :], preferred_element_type=F32)
                + jnp.dot(xi, wc_ref[d, S5_CH:, :], preferred_element_type=F32))

    y = proj_c(0)
    if mixed:
        y = jnp.where(fwd_row, y, proj_c(1))
    y_ref[...] = y.reshape(S5_TC, SUBLANES, LANES)


def s5_scan(seqs, wb, wc, a_re, a_im, h0_re, h0_im, *, mixed):
    length, ns, _ = seqs.shape
    nsb = ns // SUBLANES
    nj = D_B // LANES
    nk = length // S5_TC
    nd = 2 if mixed else 1
    per_dir = nsb // 2 if not mixed else 1

    def dmap(sb):
        return 0 if mixed else sb // per_dir

    kern = functools.partial(_s5_kernel, mixed=mixed)
    return pl.pallas_call(
        kern,
        out_shape=(
            jax.ShapeDtypeStruct((length, ns, D_B), F32),
            jax.ShapeDtypeStruct((ns, G_B * P_B), F32),
            jax.ShapeDtypeStruct((ns, G_B * P_B), F32),
        ),
        grid=(nsb, nj, nk),
        in_specs=[
            pl.BlockSpec((S5_TC, SUBLANES, LANES), lambda sb, j, k: (k, sb, j)),
            pl.BlockSpec((nd, None, LANES, 2 * S5_CH), lambda sb, j, k: (dmap(sb), j, 0, 0)),
            pl.BlockSpec((nd, None, 2 * S5_CH, LANES), lambda sb, j, k: (dmap(sb), j, 0, 0)),
            pl.BlockSpec((None, SUBLANES, S5_CH), lambda sb, j, k: (dmap(sb), 0, j)),
            pl.BlockSpec((None, SUBLANES, S5_CH), lambda sb, j, k: (dmap(sb), 0, j)),
            pl.BlockSpec((SUBLANES, S5_CH), lambda sb, j, k: (sb, j)),
            pl.BlockSpec((SUBLANES, S5_CH), lambda sb, j, k: (sb, j)),
        ],
        out_specs=(
            pl.BlockSpec((S5_TC, SUBLANES, LANES), lambda sb, j, k: (k, sb, j)),
            pl.BlockSpec((SUBLANES, S5_CH), lambda sb, j, k: (sb, j)),
            pl.BlockSpec((SUBLANES, S5_CH), lambda sb, j, k: (sb, j)),
        ),
        scratch_shapes=[
            pltpu.VMEM((S5_TC, SUBLANES, S5_CH), F32), pltpu.VMEM((S5_TC, SUBLANES, S5_CH), F32),
            pltpu.VMEM((SUBLANES, S5_CH), F32), pltpu.VMEM((SUBLANES, S5_CH), F32),
        ],
        compiler_params=_cp(("arbitrary", "arbitrary", "arbitrary")),
        name="s5_scan_mixed" if mixed else "s5_scan",
    )(seqs, wb, wc, a_re, a_im, h0_re, h0_im)


def _s5fin_kernel(u_ref, y_ref, d_ref, w_ref, b_ref, o_ref):
    y = d_ref[...] * u_ref[...] + y_ref[...]
    y = 0.5 * y * (1.0 + jnp.tanh(math.sqrt(2.0 / math.pi) * (y + 0.044715 * (y * y * y))))
    z = jnp.dot(y.astype(BF16), w_ref[...], preferred_element_type=F32) + b_ref[...]
    o_ref[...] = (y * _sigmoid(z)).astype(o_ref.dtype)


def s5_finish(proj, y, d, w_glu, b_glu):
    tm = 512
    return pl.pallas_call(
        _s5fin_kernel,
        out_shape=jax.ShapeDtypeStruct((T_ALL, D_B), BF16),
        grid=(T_ALL // tm,),
        in_specs=[
            pl.BlockSpec((tm, D_B), lambda i: (i, C_UB // D_B)),
            pl.BlockSpec((tm, D_B), lambda i: (i, 0)),
            pl.BlockSpec((1, D_B), lambda i: (0, 0)),
            pl.BlockSpec((D_B, D_B), lambda i: (0, 0)),
            pl.BlockSpec((1, D_B), lambda i: (0, 0)),
        ],
        out_specs=pl.BlockSpec((tm, D_B), lambda i: (i, 0)),
        compiler_params=_cp(("arbitrary",)),
        name="s5_finish",
    )(proj, y, d, w_glu, b_glu)


def _merge_kernel(x_ref, gl_ref, oa_ref, ob_ref, oc_ref, mod_ref, g_ref, wa_ref, wb_ref, wc_ref, wo_ref,
                  x1_ref, h2_ref, h2row_ref):
    m = None
    for br, (o_ref, w_ref) in enumerate(((oa_ref, wa_ref), (ob_ref, wb_ref), (oc_ref, wc_ref))):
        gate = _sigmoid(gl_ref[:, br * D_MODEL:(br + 1) * D_MODEL])
        t = gate * jnp.dot(o_ref[...], w_ref[...], preferred_element_type=F32)
        m = t if m is None else m + t
    x1 = x_ref[...] + mod_ref[2:3, :] * jnp.dot(m.astype(BF16), wo_ref[...], preferred_element_type=F32)
    x1_ref[...] = x1
    h2 = _rms(x1, g_ref[...])
    h2 = h2 * (1.0 + mod_ref[4:5, :]) + mod_ref[3:4, :]
    h2_ref[...] = h2
    h2row_ref[...] = h2.reshape((h2.shape[0],) + ROW3)


def merge(x, proj, oa, ob, oc, mod, g, wa, wb, wc, wo):
    tm = 256
    full = lambda shape: pl.BlockSpec(shape, lambda i: (0,) * len(shape))
    return pl.pallas_call(
        _merge_kernel,
        out_shape=(jax.ShapeDtypeStruct((T_ALL, D_MODEL), F32), jax.ShapeDtypeStruct((T_ALL, D_MODEL), F32),
                   jax.ShapeDtypeStruct((T_ALL,) + ROW3, F32)),
        grid=(T_ALL // tm,),
        in_specs=[
            pl.BlockSpec((tm, D_MODEL), lambda i: (i, 0)),
            pl.BlockSpec((tm, 3 * D_MODEL), lambda i: (i, 0)),
            pl.BlockSpec((tm, 512), lambda i: (i, 0)),
            pl.BlockSpec((tm, 512), lambda i: (i, 0)),
            pl.BlockSpec((tm, 512), lambda i: (i, 0)),
            pl.BlockSpec((None, 6, D_MODEL), lambda i: (_group_of_tile(i, tm), 0, 0)),
            full((1, D_MODEL)),
            full((512, D_MODEL)), full((512, D_MODEL)), full((512, D_MODEL)), full((D_MODEL, D_MODEL)),
        ],
        out_specs=(pl.BlockSpec((tm, D_MODEL), lambda i: (i, 0)), pl.BlockSpec((tm, D_MODEL), lambda i: (i, 0)),
                   pl.BlockSpec((tm,) + ROW3, lambda i: (i, 0, 0))),
        compiler_params=_cp(("arbitrary",)),
        name="merge",
    )(x, proj, oa, ob, oc, mod, g, wa, wb, wc, wo)


ROUTE_TM = 256


def _router_kernel(h_ref, w_ref, b_ref, idx_ref, rank_ref, wgt_ref, cnt_ref, cnt_acc):
    i = pl.program_id(0)

    @pl.when(i == 0)
    def _():
        cnt_acc[...] = jnp.zeros_like(cnt_acc)

    h = h_ref[...]
    w = w_ref[...]
    h_hi = h.astype(BF16)
    h_lo = (h - h_hi.astype(F32)).astype(BF16)
    w_hi = w.astype(BF16)
    w_lo = (w - w_hi.astype(F32)).astype(BF16)
    dot = functools.partial(jnp.dot, preferred_element_type=F32)
    logits = dot(h_hi, w_hi) + (dot(h_hi, w_lo) + dot(h_lo, w_hi)) + b_ref[...]
    lane_i = lax.broadcasted_iota(jnp.int32, (ROUTE_TM, LANES), 1)
    lane = lane_i.astype(F32)
    r_i = lax.broadcasted_iota(jnp.int32, (ROUTE_TM, ROUTE_TM), 0)
    c_i = lax.broadcasted_iota(jnp.int32, (ROUTE_TM, ROUTE_TM), 1)
    earlier = jnp.where(c_i < r_i, 1.0, 0.0).astype(BF16)

    cnt = cnt_acc[...]
    idx_out = jnp.zeros((ROUTE_TM, LANES), F32)
    rank_out = jnp.zeros((ROUTE_TM, LANES), F32)
    val_out = jnp.zeros((ROUTE_TM, LANES), F32)
    v0 = None
    esum = None
    for k in range(TOP_K):
        m = logits.max(axis=-1, keepdims=True)
        sel = jnp.min(jnp.where(logits == m, lane, float(LANES)), axis=-1, keepdims=True)
        hit = lane == sel
        logits = jnp.where(hit, -jnp.inf, logits)
        onehot = jnp.where(hit, 1.0, 0.0)
        within = jnp.dot(earlier, onehot.astype(BF16), preferred_element_type=F32)
        rank = jnp.sum(onehot * (within + cnt), axis=-1, keepdims=True)
        cnt = cnt + jnp.sum(onehot, axis=0, keepdims=True)
        if k == 0:
            v0 = m
        e = jnp.exp(m - v0)
        esum = e if esum is None else esum + e
        idx_out = jnp.where(lane_i == k, sel, idx_out)
        rank_out = jnp.where(lane_i == k, rank, rank_out)
        val_out = jnp.where(lane_i == k, e, val_out)
    cnt_acc[...] = cnt
    cnt_ref[...] = cnt
    idx_ref[...] = idx_out.astype(jnp.int32)
    rank_ref[...] = rank_out.astype(jnp.int32)
    wgt_ref[...] = val_out * (1.0 / esum)


def router(h2, w_router, b_router):
    tile = lambda: pl.BlockSpec((ROUTE_TM, LANES), lambda i: (i, 0))
    return pl.pallas_call(
        _router_kernel,
        out_shape=(
            jax.ShapeDtypeStruct((T_ALL, LANES), jnp.int32),
            jax.ShapeDtypeStruct((T_ALL, LANES), jnp.int32),
            jax.ShapeDtypeStruct((T_ALL, LANES), F32),
            jax.ShapeDtypeStruct((1, LANES), F32),
        ),
        grid=(T_ALL // ROUTE_TM,),
        in_specs=[
            pl.BlockSpec((ROUTE_TM, D_MODEL), lambda i: (i, 0)),
            pl.BlockSpec((D_MODEL, LANES), lambda i: (0, 0)),
            pl.BlockSpec((1, LANES), lambda i: (0, 0)),
        ],
        out_specs=(tile(), tile(), tile(), pl.BlockSpec((1, LANES), lambda i: (0, 0))),
        scratch_shapes=[pltpu.VMEM((1, LANES), F32)],
        compiler_params=_cp(("arbitrary",)),
        name="router",
    )(h2, w_router, b_router)


PLAN_UNROLL = 8
N_DUMP = 5 * BLK
PLAN_LEN = BUF_LEN + BLK
assert TOP_K == 4 and T_ALL & (T_ALL - 1) == 0


def _invert_kernel(dest_ref, fill_hbm, inv_ref):
    pltpu.sync_copy(fill_hbm, inv_ref)
    tok_step = PLAN_UNROLL // TOP_K

    def put(i, c):
        for j in range(PLAN_UNROLL):
            inv_ref[dest_ref[i * PLAN_UNROLL + j]] = i * tok_step + ((j % TOP_K) * T_ALL + j // TOP_K)
        return c

    lax.fori_loop(0, N_ASSIGN // PLAN_UNROLL, put, 0)


def invert_plan(dest):
    r = jnp.arange(PLAN_LEN, dtype=jnp.int32)
    fill = N_ASSIGN + jnp.where(r < BLK, 2 * BLK + r, (r - BLK) & (2 * BLK - 1))
    return pl.pallas_call(
        _invert_kernel,
        out_shape=jax.ShapeDtypeStruct((PLAN_LEN,), jnp.int32),
        in_specs=[pl.BlockSpec(memory_space=pltpu.SMEM), pl.BlockSpec(memory_space=pl.ANY)],
        out_specs=pl.BlockSpec(memory_space=pltpu.SMEM),
        name="invert_plan",
    )(dest, fill)


PAIR_TILE = 2 * LANES


def _pair_selection():
    r = lax.broadcasted_iota(jnp.int32, (PAIR_TILE, PAIR_TILE), 0)
    c = lax.broadcasted_iota(jnp.int32, (PAIR_TILE, PAIR_TILE), 1)
    return (r == jnp.where(c < LANES, 2 * c, 2 * (c - LANES) + 1)).astype(BF16)


EXP_NBUF = 3


def _expert_kernel(be_ref, first_ref, inv_ref, h_hbm, wu32_ref, bu_ref, wd32_ref, bd_ref, sel_ref, comb_hbm,
                   xbuf, ybuf, wu_ref, wd_ref, gsem, ssem):
    del be_ref
    i = pl.program_id(0)
    n = pl.num_programs(0)
    cur = i % EXP_NBUF
    nxt = (i + 2) % EXP_NBUF
    prv = nxt

    def gather(block, s):
        base = (block + 1) * BLK
        for r in range(BLK):
            tok = inv_ref[base + r] & (T_ALL - 1)
            pltpu.make_async_copy(h_hbm.at[tok], xbuf.at[s, r], gsem.at[s]).start()

    def scatter(block, s):
        base = (block + 1) * BLK
        for r in range(BLK):
            a = inv_ref[base + r]
            pltpu.make_async_copy(ybuf.at[s, r], comb_hbm.at[a], ssem.at[s]).start()

    def wait_block(sem):
        pltpu.make_async_copy(h_hbm.at[pl.ds(0, BLK)], xbuf.at[0], sem).wait()

    @pl.when(i == 0)
    def _():
        ybuf[...] = jnp.zeros_like(ybuf)
        for s in range(EXP_NBUF - 1):
            for r in range(BLK):
                dump = N_ASSIGN + (3 + s) * BLK + r
                pltpu.make_async_copy(ybuf.at[s, r], comb_hbm.at[dump], ssem.at[s]).start()
        gather(0, 0)
        gather(1, 1)

    wait_block(gsem.at[cur])
    wait_block(ssem.at[cur])

    @pl.when(first_ref[i] == 1)
    def _():
        sel = sel_ref[...]
        for j in range(2 * D_FF // PAIR_TILE):
            cols = slice(j * PAIR_TILE, (j + 1) * PAIR_TILE)
            wu_ref[:, cols] = jnp.dot(wu32_ref[:, cols].astype(BF16), sel, preferred_element_type=F32).astype(BF16)
        wd_ref[...] = wd32_ref[...].astype(BF16)

    gather(jnp.minimum(i + 2, n - 1), nxt)
    scatter(i - 1, prv)

    x = xbuf[cur].reshape(BLK, D_MODEL).astype(BF16)
    h = jnp.dot(x, wu_ref[...], preferred_element_type=F32) + bu_ref[...]
    acts = []
    for j in range(2 * D_FF // PAIR_TILE):
        glu = jnp.minimum(h[:, j * PAIR_TILE:j * PAIR_TILE + LANES], SWIGLU_LIMIT)
        lin = jnp.clip(h[:, j * PAIR_TILE + LANES:(j + 1) * PAIR_TILE], -SWIGLU_LIMIT, SWIGLU_LIMIT)
        acts.append((glu * _sigmoid(SWIGLU_ALPHA * glu) * (lin + 1.0)).astype(BF16))
    act = jnp.concatenate(acts, axis=1)
    y = jnp.dot(act, wd_ref[...], preferred_element_type=F32) + bd_ref[...]
    ybuf[cur] = y.reshape((BLK,) + ROW3)

    @pl.when(i == n - 1)
    def _():
        for s in range(EXP_NBUF):
            @pl.when(s != cur)
            def _():
                wait_block(gsem.at[s])
                wait_block(ssem.at[s])
        scatter(i, cur)
        wait_block(ssem.at[cur])


def experts(layer, block_e, first, inv, h2row, wu, bu, wd, bd):
    per_expert = lambda *blk: pl.BlockSpec((None, None) + blk, lambda i, be, first, inv: (layer, be[i], 0, 0))
    return pl.pallas_call(
        _expert_kernel,
        out_shape=jax.ShapeDtypeStruct((N_ASSIGN + N_DUMP,) + ROW3, F32),
        grid_spec=pltpu.PrefetchScalarGridSpec(
            num_scalar_prefetch=3,
            grid=(N_BLOCKS,),
            in_specs=[
                pl.BlockSpec(memory_space=pl.ANY),
                per_expert(D_MODEL, 2 * D_FF), per_expert(1, 2 * D_FF), per_expert(D_FF, D_MODEL),
                per_expert(1, D_MODEL),
                pl.BlockSpec((PAIR_TILE, PAIR_TILE), lambda i, be, first, inv: (0, 0)),
            ],
            out_specs=pl.BlockSpec(memory_space=pl.ANY),
            scratch_shapes=[
                pltpu.VMEM((EXP_NBUF, BLK) + ROW3, F32), pltpu.VMEM((EXP_NBUF, BLK) + ROW3, F32),
                pltpu.VMEM((D_MODEL, 2 * D_FF), BF16), pltpu.VMEM((D_FF, D_MODEL), BF16),
                pltpu.SemaphoreType.DMA((EXP_NBUF,)), pltpu.SemaphoreType.DMA((EXP_NBUF,)),
            ],
        ),
        compiler_params=_cp(("arbitrary",)),
        name="experts",
    )(block_e, first, inv, h2row, wu, bu, wd, bd, _pair_selection())


COMB_TM = 256


def _combine_kernel(c0_ref, c1_ref, c2_ref, c3_ref, x_ref, w_ref, mod_ref, o_ref):
    w = w_ref[...]
    acc = None
    for k, c_ref in enumerate((c0_ref, c1_ref, c2_ref, c3_ref)):
        t = w[:, k:k + 1] * c_ref[...].reshape(COMB_TM, D_MODEL)
        acc = t if acc is None else acc + t
    o_ref[...] = x_ref[...] + mod_ref[5:6, :] * acc


def combine(comb, x1, wgt, mod):
    tiles = T_ALL // COMB_TM
    kth = lambda k: pl.BlockSpec((COMB_TM,) + ROW3, lambda i: (k * tiles + i, 0, 0))
    return pl.pallas_call(
        _combine_kernel,
        out_shape=jax.ShapeDtypeStruct((T_ALL, D_MODEL), F32),
        grid=(tiles,),
        in_specs=[
            kth(0), kth(1), kth(2), kth(3),
            pl.BlockSpec((COMB_TM, D_MODEL), lambda i: (i, 0)),
            pl.BlockSpec((COMB_TM, LANES), lambda i: (i, 0)),
            pl.BlockSpec((None, 6, D_MODEL), lambda i: (_group_of_tile(i, COMB_TM), 0, 0)),
        ],
        out_specs=pl.BlockSpec((COMB_TM, D_MODEL), lambda i: (i, 0)),
        compiler_params=_cp(("arbitrary",)),
        name="combine",
    )(comb, comb, comb, comb, x1, wgt, mod)


def _rope_table():
    pos = jnp.arange(DEC_SEQ)
    row = (pos // GRID_W).astype(F32)[:, None]
    col = (pos % GRID_W).astype(F32)[:, None]

    def parts(rot_dim):
        nf = rot_dim // 4
        inv = ROPE_BASE ** (-jnp.arange(nf, dtype=F32) / nf)
        cr, sr, cc, sc = jnp.cos(row * inv), jnp.sin(row * inv), jnp.cos(col * inv), jnp.sin(col * inv)
        z = jnp.zeros_like(sr)
        cos = jnp.concatenate([cr, cr, cc, cc], axis=1)
        lo = jnp.concatenate([-sr, z, -sc, z], axis=1)
        hi = jnp.concatenate([z, sr, z, sc], axis=1)
        return cos, lo, hi

    ca, la, ha = (jnp.tile(t, (1, 2)) for t in parts(HD_A))
    cc, lc, hc = parts(ROPE_C)
    pad = lambda t, fill: jnp.pad(t, ((0, 0), (NOPE_C, LANES - QK_C)), constant_values=fill)
    lat = jnp.concatenate([ca, la, ha, pad(cc, 1.0), pad(lc, 0.0), pad(hc, 0.0)], axis=1)
    ones, zeros = jnp.ones((DEC_SEQ, LANES), F32), jnp.zeros((DEC_SEQ, LANES), F32)
    ident = jnp.concatenate([ones, zeros, zeros, ones, zeros, zeros], axis=1)
    return jnp.concatenate([ident, lat], axis=0)


def _arrange_w_in(w_in):
    o = 0
    parts = {}
    for name, n in (("qa", HQ_A * HD_A), ("ka", HKV_A * HD_A), ("va", HKV_A * HD_A), ("ub", D_B), ("cq", Q_LORA),
                    ("ckv", KV_LORA), ("kr", ROPE_C), ("gl", 3 * D_MODEL)):
        parts[name] = w_in[..., o:o + n]
        o += n
    qa = parts["qa"].reshape(DEPTH, D_MODEL, HKV_A, G_A, 1, HD_A)
    eye = jnp.eye(HKV_A, dtype=F32).reshape(1, 1, HKV_A, 1, HKV_A, 1)
    qa_slots = (qa * eye).reshape(DEPTH, D_MODEL, HQ_A * LANES)
    kr = jnp.pad(parts["kr"], ((0, 0), (0, 0), (NOPE_C, LANES - QK_C)))
    w = jnp.concatenate([parts["gl"], qa_slots, parts["ub"], parts["cq"], parts["ka"], parts["va"], parts["ckv"], kr],
                        axis=-1)
    return w.astype(BF16)


def _s5_params(lam_re, lam_im, log_dt, b_re, b_im, c_re, c_im):
    dt = jnp.exp(log_dt)[..., None]
    decay = jnp.exp(lam_re * dt)
    ab_re, ab_im = decay * jnp.cos(lam_im * dt), decay * jnp.sin(lam_im * dt)
    den = lam_re * lam_re + lam_im * lam_im
    f_re = ((ab_re - 1) * lam_re + ab_im * lam_im) / den
    f_im = (ab_im * lam_re - (ab_re - 1) * lam_im) / den
    bb_re = f_re[..., None] * b_re - f_im[..., None] * b_im
    bb_im = f_re[..., None] * b_im + f_im[..., None] * b_re
    nj, gpb = D_B // LANES, LANES // GS_B
    eye = jnp.eye(gpb, dtype=F32)

    def blockdiag_b(bb):
        t = bb.transpose(0, 1, 3, 2).reshape(2, nj, gpb, GS_B, P_B)
        return (t[:, :, :, :, None, :] * eye[None, None, :, None, :, None]).reshape(2, nj, LANES, gpb * P_B)

    def blockdiag_c(cc):
        t = cc.transpose(0, 1, 3, 2).reshape(2, nj, gpb, P_B, GS_B)
        return (t[:, :, :, :, None, :] * eye[None, None, :, None, :, None]).reshape(2, nj, gpb * P_B, LANES)

    wb = jnp.concatenate([blockdiag_b(bb_re), blockdiag_b(bb_im)], axis=-1).astype(BF16)
    wc = jnp.concatenate([blockdiag_c(c_re), -blockdiag_c(c_im)], axis=-2).astype(BF16)
    return wb, wc, ab_re.reshape(2, G_B * P_B), ab_im.reshape(2, G_B * P_B)


def _to_time_major(u):
    f = u.transpose(1, 0, 2)
    return jnp.concatenate([f, f[::-1]], axis=1)


def _from_time_major(y, b):
    return (y[:, :b] + y[::-1, b:]).transpose(1, 0, 2)


def kernel(x_prompt, x_sample, c, cache_attn_k, cache_attn_v, cache_mla_ckv, cache_mla_krope, state_ssm_re, state_ssm_im, c_ctx, w_ada, b_ada, norm_mix_g, norm_ffn_g, w_in, q_norm_a, k_norm_a, sink_a, q_a_norm_c, kv_a_norm_c, w_uq_c, w_ukv_c, q_norm_c, k_norm_c, ssm_lam_re, ssm_lam_im, ssm_log_dt, ssm_b_re, ssm_b_im, ssm_c_re, ssm_c_im, ssm_d, w_glu, b_glu, w_br_a, w_br_b, w_br_c, w_out, w_router, b_router, w_up, b_up, w_down, b_down):
    x = jnp.concatenate([x_prompt.reshape(N_CTX, D_MODEL), x_sample.reshape(N_LAT, D_MODEL)], axis=0)
    cvecs = jnp.concatenate([c_ctx[None], c, jnp.zeros((N_GROUPS - 1 - DEC_BATCH, D_MODEL), F32)], axis=0)
    mods = adaln(cvecs, w_ada, b_ada)

    tab = _rope_table()
    w_in_r = _arrange_w_in(w_in)
    pad_slot = lambda g: jnp.pad(g, ((0, 0), (0, LANES - QK_C))).reshape(DEPTH, 1, LANES)
    gqa = jnp.tile(q_norm_a, (1, 2)).reshape(DEPTH, 1, LANES)
    gka = jnp.tile(k_norm_a, (1, 2)).reshape(DEPTH, 1, LANES)
    gqc, gkc = pad_slot(q_norm_c), pad_slot(k_norm_c)
    wuq = jnp.pad(w_uq_c.reshape(DEPTH, Q_LORA, H_C, QK_C), ((0, 0), (0, 0), (0, 0), (0, LANES - QK_C)))
    wuq = wuq.reshape(DEPTH, Q_LORA, H_C * LANES).astype(BF16)
    wukv4 = w_ukv_c.reshape(DEPTH, KV_LORA, H_C, NOPE_C + V_C)
    wuk = jnp.pad(wukv4[..., :NOPE_C], ((0, 0), (0, 0), (0, 0), (0, LANES - NOPE_C))).reshape(DEPTH, KV_LORA, H_C * LANES)
    wuv = wukv4[..., NOPE_C:].reshape(DEPTH, KV_LORA, H_C * V_C)
    wukv = jnp.concatenate([wuk, wuv], axis=-1).astype(BF16)
    sink = jnp.broadcast_to(sink_a[:, :, None], (DEPTH, HQ_A, LANES))
    w_router_p = jnp.pad(w_router, ((0, 0), (0, 0), (0, LANES - N_EXP)))
    b_router_p = jnp.pad(b_router, ((0, 0), (0, LANES - N_EXP)), constant_values=-jnp.inf).reshape(DEPTH, 1, LANES)
    b_up_p = b_up.reshape(DEPTH, N_EXP, 2 * D_FF // PAIR_TILE, LANES, 2).transpose(0, 1, 2, 4, 3)
    b_up_p = b_up_p.reshape(DEPTH, N_EXP, 1, 2 * D_FF)
    b_down_r = b_down.reshape(DEPTH, N_EXP, 1, D_MODEL)
    kr_cache = jnp.pad(cache_mla_krope, ((0, 0), (0, 0), (0, 0), (NOPE_C, LANES - QK_C)))
    zeros_state = jnp.zeros((2 * BATCH, G_B * P_B), F32)

    new_k, new_v, new_ckv, new_kr, new_sre, new_sim = [], [], [], [], [], []
    for l in range(DEPTH):
        mod = mods[l]
        proj = inproj(x, norm_mix_g[l], mod, w_in_r[l])
        qa, ka, qc, kc, vc, ckvn = prep(proj, tab, gqa[l], gka[l], q_a_norm_c[l].reshape(1, Q_LORA),
                                         kv_a_norm_c[l].reshape(1, KV_LORA), gqc[l], gkc[l], wuq[l], wukv[l])
        kc_cache, vc_cache = cache_keys(cache_mla_ckv[:, l].reshape(DEC_BATCH * PAST_LEN, KV_LORA),
                                        kr_cache[:, l].reshape(DEC_BATCH * PAST_LEN, LANES), gkc[l], wukv[l])
        oa = attn_a_lat(qa, ka, proj, cache_attn_k[:, l].reshape(DEC_BATCH, PAST_LEN, LANES),
                        cache_attn_v[:, l].reshape(DEC_BATCH, PAST_LEN, LANES), sink[l],
                        attn_a_ctx(qa, ka, proj, sink[l]))
        oc = attn_c_lat(qc, kc, vc, kc_cache, vc_cache, attn_c_ctx(qc, kc, vc))
        wb, wc, a_re, a_im = _s5_params(ssm_lam_re[l], ssm_lam_im[l], ssm_log_dt[l], ssm_b_re[l], ssm_b_im[l],
                                        ssm_c_re[l], ssm_c_im[l])
        ub = proj[:, C_UB:C_UB + D_B]
        a8 = lambda a: jnp.broadcast_to(a[:, None, :], (2, SUBLANES, G_B * P_B))
        a_mixed = lambda a: jnp.repeat(a, SUBLANES // 2, axis=0)[None]
        y_ctx, f_re, f_im = s5_scan(_to_time_major(ub[:N_CTX].reshape(BATCH, SEQ, D_B)), wb, wc, a8(a_re), a8(a_im),
                                    zeros_state, zeros_state, mixed=False)
        h0 = lambda s: s[:, l].transpose(1, 0, 2, 3).reshape(2 * DEC_BATCH, G_B * P_B)
        y_lat, _, _ = s5_scan(_to_time_major(ub[N_CTX:].reshape(DEC_BATCH, DEC_SEQ, D_B)), wb, wc,
                              a_mixed(a_re), a_mixed(a_im), h0(state_ssm_re), h0(state_ssm_im), mixed=True)
        y = jnp.concatenate([_from_time_major(y_ctx, BATCH).reshape(N_CTX, D_B),
                             _from_time_major(y_lat, DEC_BATCH).reshape(N_LAT, D_B)], axis=0)
        ob = s5_finish(proj, y, ssm_d[l].reshape(1, D_B), w_glu[l].astype(BF16), b_glu[l].reshape(1, D_B))
        x1, h2, h2row = merge(x, proj, oa, ob, oc, mod, norm_ffn_g[l].reshape(1, D_MODEL), w_br_a[l].astype(BF16),
                       w_br_b[l].astype(BF16), w_br_c[l].astype(BF16), w_out[l].astype(BF16))
        idx, rank, wgt, cnt = router(h2, w_router_p[l], b_router_p[l])
        counts = cnt[0, :N_EXP].astype(jnp.int32)
        padded = (counts + BLK - 1) // BLK * BLK
        pad_end = jnp.cumsum(padded)
        pad_start = pad_end - padded
        dest = (BLK + pad_start[idx[:, :TOP_K]] + rank[:, :TOP_K]).reshape(-1).astype(jnp.int32)
        block_start = jnp.arange(N_BLOCKS, dtype=jnp.int32) * BLK
        block_e = jnp.minimum(jnp.sum(pad_end[None, :] <= block_start[:, None], axis=1), N_EXP - 1)
        block_e = block_e.astype(jnp.int32)
        first = jnp.concatenate([jnp.ones((1,), jnp.int32), (block_e[1:] != block_e[:-1]).astype(jnp.int32)])
        comb = experts(l, block_e, first, invert_plan(dest), h2row, w_up, b_up_p, w_down, b_down_r)
        x = combine(comb, x1, wgt, mod)

        new_k.append(ka[:N_CTX].reshape(BATCH, SEQ, HKV_A, HD_A))
        new_v.append(proj[:N_CTX, C_VA:C_VA + LANES].reshape(BATCH, SEQ, HKV_A, HD_A))
        new_ckv.append(ckvn[:N_CTX].reshape(BATCH, SEQ, KV_LORA))
        new_kr.append(proj[:N_CTX, C_KR + NOPE_C:C_KR + QK_C].reshape(BATCH, SEQ, ROPE_C))
        new_sre.append(f_re.reshape(2, BATCH, G_B, P_B).transpose(1, 0, 2, 3))
        new_sim.append(f_im.reshape(2, BATCH, G_B, P_B).transpose(1, 0, 2, 3))

    y_prompt = x[:N_CTX].reshape(BATCH, SEQ, D_MODEL)
    y_sample = x[N_CTX:].reshape(DEC_BATCH, DEC_SEQ, D_MODEL)
    return (y_prompt, y_sample, jnp.stack(new_k, axis=1), jnp.stack(new_v, axis=1), jnp.stack(new_ckv, axis=1),
            jnp.stack(new_kr, axis=1), jnp.stack(new_sre, axis=1), jnp.stack(new_sim, axis=1))
```

```python
import functools
import math

import jax
import jax.numpy as jnp
from jax import lax
from jax.experimental import pallas as pl
from jax.experimental.pallas import tpu as pltpu

D_MODEL = 1024
BATCH = 32
SEQ = 256
DEPTH = 2
DEC_BATCH = 4
DEC_SEQ = 2048
PAST_LEN = 256
GRID_W = 64
ROPE_BASE = 10000.0
EPS = 1e-6
NEG_INF = -1e30
BLK = 128
HQ_A, HKV_A, HD_A = 8, 2, 64
G_A = HQ_A // HKV_A
WINDOW = 128
D_B, GS_B, P_B = 512, 16, 64
G_B = D_B // GS_B
H_C, Q_LORA, KV_LORA, NOPE_C, ROPE_C, V_C = 8, 256, 128, 64, 32, 64
QK_C = NOPE_C + ROPE_C
N_EXP, TOP_K, D_FF = 32, 4, 1024
SWIGLU_ALPHA, SWIGLU_LIMIT = 1.702, 7.0

N_CTX = BATCH * SEQ
N_LAT = DEC_BATCH * DEC_SEQ
T_ALL = N_CTX + N_LAT
N_GROUPS = 8

LANES = 128
SUBLANES = 8

C_GL = 0
C_QA = 3 * D_MODEL
C_UB = C_QA + HQ_A * LANES
C_CQ = C_UB + D_B
C_KA = C_CQ + Q_LORA
C_VA = C_KA + LANES
C_CKV = C_VA + LANES
C_KR = C_CKV + LANES
N_PROJ = C_KR + LANES

N_ASSIGN = T_ALL * TOP_K
N_BLOCKS = N_ASSIGN // BLK + N_EXP
BUF_LEN = N_BLOCKS * BLK

F32 = jnp.float32
BF16 = jnp.bfloat16
ROW3 = (D_MODEL // LANES, LANES)
VMEM_LIMIT = 56 * 1024 * 1024


def _cp(sem, vmem=VMEM_LIMIT):
    return pltpu.CompilerParams(dimension_semantics=sem, vmem_limit_bytes=vmem)


def _group_of_tile(i, tm):
    n_ctx_tiles = N_CTX // tm
    per_batch = DEC_SEQ // tm
    return jnp.where(i < n_ctx_tiles, 0, 1 + (i - n_ctx_tiles) // per_batch)


def _sigmoid(x):
    return 1.0 / (1.0 + jnp.exp(-x))


def _adaln_kernel(c_ref, w_ref, b_ref, o_ref):
    c = c_ref[...]
    s = c * _sigmoid(c)
    o_ref[...] = jnp.dot(s.astype(BF16), w_ref[...].astype(BF16), preferred_element_type=F32) + b_ref[...]


def adaln(cvecs, w_ada, b_ada):
    tn = 1536
    out = pl.pallas_call(
        _adaln_kernel,
        out_shape=jax.ShapeDtypeStruct((DEPTH, N_GROUPS, 6 * D_MODEL), F32),
        grid=(DEPTH, 6 * D_MODEL // tn),
        in_specs=[
            pl.BlockSpec((N_GROUPS, D_MODEL), lambda l, j: (0, 0)),
            pl.BlockSpec((None, D_MODEL, tn), lambda l, j: (l, 0, j)),
            pl.BlockSpec((None, 1, tn), lambda l, j: (l, 0, j)),
        ],
        out_specs=pl.BlockSpec((None, N_GROUPS, tn), lambda l, j: (l, 0, j)),
        compiler_params=_cp(("arbitrary", "arbitrary")),
        name="adaln",
    )(cvecs, w_ada, b_ada.reshape(DEPTH, 1, 6 * D_MODEL))
    return out.reshape(DEPTH, N_GROUPS, 6, D_MODEL)


def _rms(x, g):
    ms = jnp.mean(x * x, axis=-1, keepdims=True)
    return x * lax.rsqrt(ms + EPS) * g


def _inproj_kernel(x_ref, g_ref, mod_ref, w_ref, o_ref):
    h = _rms(x_ref[...], g_ref[...])
    h = h * (1.0 + mod_ref[1:2, :]) + mod_ref[0:1, :]
    o_ref[...] = jnp.dot(h.astype(BF16), w_ref[...], preferred_element_type=F32)


def inproj(x, g, mod, w):
    tm, tn = 512, 1792
    return pl.pallas_call(
        _inproj_kernel,
        out_shape=jax.ShapeDtypeStruct((T_ALL, N_PROJ), F32),
        grid=(N_PROJ // tn, T_ALL // tm),
        in_specs=[
            pl.BlockSpec((tm, D_MODEL), lambda j, i: (i, 0)),
            pl.BlockSpec((1, D_MODEL), lambda j, i: (0, 0)),
            pl.BlockSpec((None, 6, D_MODEL), lambda j, i: (_group_of_tile(i, tm), 0, 0)),
            pl.BlockSpec((D_MODEL, tn), lambda j, i: (0, j)),
        ],
        out_specs=pl.BlockSpec((tm, tn), lambda j, i: (i, j)),
        compiler_params=_cp(("arbitrary", "arbitrary")),
        name="inproj",
    )(x, g.reshape(1, D_MODEL), mod, w)


def _rope(slab, cos, sin_lo, sin_hi, half):
    up = pltpu.roll(slab, LANES - half, axis=1)
    dn = pltpu.roll(slab, half, axis=1)
    return slab * cos + up * sin_lo + dn * sin_hi


def _slot_norm(slab, gain, n_real):
    ms = jnp.sum(slab * slab, axis=-1, keepdims=True) * (1.0 / n_real)
    return slab * lax.rsqrt(ms + EPS) * gain


def _mla_keys(ckvn, kr_blk, wukv_ref, gk, rope_c):
    kv = jnp.dot(ckvn.astype(BF16), wukv_ref[...], preferred_element_type=F32)
    ks = []
    for h in range(H_C):
        slab = kv[:, h * LANES:(h + 1) * LANES] + kr_blk
        slab = _slot_norm(slab, gk, QK_C)
        if rope_c is not None:
            slab = _rope(slab, *rope_c, ROPE_C // 4)
        ks.append(slab.astype(BF16))
    return jnp.concatenate(ks, axis=1), kv[:, H_C * LANES:].astype(BF16)


PREP_TM = 256


def _prep_kernel(qa_ref, cq_ref, ka_ref, ckv_ref, kr_ref, tab_ref,
                 gqa_ref, gka_ref, gcq_ref, gckv_ref, gqc_ref, gkc_ref, wuq_ref, wukv_ref,
                 qa_o, ka_o, qc_o, kc_o, vc_o, ckvn_o):
    def body(rotate):
        if rotate:
            tab = tab_ref[...]
            rope_a = (tab[:, 0:128], tab[:, 128:256], tab[:, 256:384])
            rope_c = (tab[:, 384:512], tab[:, 512:640], tab[:, 640:768])
            rot_a = lambda x: _rope(x, *rope_a, HD_A // 4)
            rot_c = lambda x: _rope(x, *rope_c, ROPE_C // 4)
        else:
            rope_c = None
            rot_a = rot_c = lambda x: x
        lane = lax.broadcasted_iota(jnp.int32, (1, LANES), 1)

        gqa = gqa_ref[...]
        qa = qa_ref[...]
        outs = []
        for h in range(HQ_A):
            slab = _slot_norm(qa[:, h * LANES:(h + 1) * LANES], gqa, HD_A)
            outs.append((rot_a(slab) * (HD_A ** -0.5)).astype(BF16))
        qa_o[...] = jnp.concatenate(outs, axis=1)

        ka = ka_ref[...]
        sq = ka * ka
        lo = lane < HD_A
        ms_lo = jnp.sum(jnp.where(lo, sq, 0.0), axis=-1, keepdims=True)
        ms_hi = jnp.sum(jnp.where(lo, 0.0, sq), axis=-1, keepdims=True)
        rs = jnp.where(lo, lax.rsqrt(ms_lo * (1.0 / HD_A) + EPS), lax.rsqrt(ms_hi * (1.0 / HD_A) + EPS))
        ka_o[...] = rot_a(ka * rs * gka_ref[...])

        cqn = _rms(cq_ref[...], gcq_ref[...])
        q = jnp.dot(cqn.astype(BF16), wuq_ref[...], preferred_element_type=F32)
        gqc = gqc_ref[...]
        outs = []
        for h in range(H_C):
            slab = _slot_norm(q[:, h * LANES:(h + 1) * LANES], gqc, QK_C)
            outs.append((rot_c(slab) * (QK_C ** -0.5)).astype(BF16))
        qc_o[...] = jnp.concatenate(outs, axis=1)

        ckvn = _rms(ckv_ref[...], gckv_ref[...])
        ckvn_o[...] = ckvn
        kc, vc = _mla_keys(ckvn, kr_ref[...], wukv_ref, gkc_ref[...], rope_c)
        kc_o[...] = kc
        vc_o[...] = vc

    is_latent = pl.program_id(0) >= N_CTX // PREP_TM
    pl.when(is_latent)(lambda: body(True))
    pl.when(jnp.logical_not(is_latent))(lambda: body(False))


def prep(proj, tab, gqa, gka, gcq, gckv, gqc, gkc, wuq, wukv):
    tm = PREP_TM
    n_ctx_tiles = N_CTX // tm
    per_batch = DEC_SEQ // tm

    def tab_map(i):
        return (jnp.where(i < n_ctx_tiles, 0, per_batch + (i - n_ctx_tiles) % per_batch), 0)

    def col(width, off):
        return pl.BlockSpec((tm, width), lambda i: (i, off // width))

    def full(shape):
        return pl.BlockSpec(shape, lambda i: (0,) * len(shape))

    def row_out(width):
        return pl.BlockSpec((tm, width), lambda i: (i, 0))

    return pl.pallas_call(
        _prep_kernel,
        out_shape=(
            jax.ShapeDtypeStruct((T_ALL, HQ_A * LANES), BF16),
            jax.ShapeDtypeStruct((T_ALL, LANES), F32),
            jax.ShapeDtypeStruct((T_ALL, H_C * LANES), BF16),
            jax.ShapeDtypeStruct((T_ALL, H_C * LANES), BF16),
            jax.ShapeDtypeStruct((T_ALL, H_C * V_C), BF16),
            jax.ShapeDtypeStruct((T_ALL, KV_LORA), F32),
        ),
        grid=(T_ALL // tm,),
        in_specs=[
            col(HQ_A * LANES, C_QA), col(Q_LORA, C_CQ), col(LANES, C_KA), col(LANES, C_CKV), col(LANES, C_KR),
            pl.BlockSpec((tm, 6 * LANES), tab_map),
            full((1, LANES)), full((1, LANES)), full((1, Q_LORA)), full((1, KV_LORA)),
            full((1, LANES)), full((1, LANES)),
            full((Q_LORA, H_C * LANES)), full((KV_LORA, H_C * LANES + H_C * V_C)),
        ],
        out_specs=(row_out(HQ_A * LANES), row_out(LANES), row_out(H_C * LANES), row_out(H_C * LANES),
                   row_out(H_C * V_C), row_out(KV_LORA)),
        compiler_params=_cp(("arbitrary",)),
        name="prep",
    )(proj, proj, proj, proj, proj, tab, gqa, gka, gcq, gckv, gqc, gkc, wuq, wukv)


def _cachekeys_kernel(ckv_ref, kr_ref, gkc_ref, wukv_ref, kc_o, vc_o):
    kc, vc = _mla_keys(ckv_ref[...], kr_ref[...], wukv_ref, gkc_ref[...], None)
    kc_o[...] = kc
    vc_o[...] = vc


def cache_keys(ckv, kr_blk, gkc, wukv):
    r = ckv.shape[0]
    tm = 256
    return pl.pallas_call(
        _cachekeys_kernel,
        out_shape=(jax.ShapeDtypeStruct((r, H_C * LANES), BF16), jax.ShapeDtypeStruct((r, H_C * V_C), BF16)),
        grid=(r // tm,),
        in_specs=[
            pl.BlockSpec((tm, LANES), lambda i: (i, 0)),
            pl.BlockSpec((tm, LANES), lambda i: (i, 0)),
            pl.BlockSpec((1, LANES), lambda i: (0, 0)),
            pl.BlockSpec((KV_LORA, H_C * LANES + H_C * V_C), lambda i: (0, 0)),
        ],
        out_specs=(pl.BlockSpec((tm, H_C * LANES), lambda i: (i, 0)), pl.BlockSpec((tm, H_C * V_C), lambda i: (i, 0))),
        compiler_params=_cp(("arbitrary",)),
        name="cache_keys",
    )(ckv, kr_blk, gkc, wukv)


def _attn_body(q_ref, segs, sink_ref, o_ref, *, n_heads, k_slot, v_slab, v_half, tq, band_qi=None):
    outs = []
    for h in range(n_heads):
        qh = q_ref[:, h * LANES:(h + 1) * LANES]
        scores = []
        for k_ref, _, off in segs:
            kh = k_ref[:, k_slot(h) * LANES:(k_slot(h) + 1) * LANES].astype(BF16)
            s = lax.dot_general(qh, kh, (((1,), (1,)), ((), ())), preferred_element_type=F32)
            if off is not None:
                tk = s.shape[1]
                blk = band_qi + off
                q_pos = band_qi * tq + lax.broadcasted_iota(jnp.int32, (tq, tk), 0)
                k_pos = blk * tk + lax.broadcasted_iota(jnp.int32, (tq, tk), 1)
                ok = (jnp.abs(k_pos - q_pos) <= WINDOW) & (blk >= 0) & (blk < DEC_SEQ // tk)
                s = jnp.where(ok, s, NEG_INF)
            scores.append(s)
        m = scores[0].max(axis=-1, keepdims=True)
        for s in scores[1:]:
            m = jnp.maximum(m, s.max(axis=-1, keepdims=True))
        if sink_ref is not None:
            sink = sink_ref[h:h + 1, 0:1]
            m = jnp.maximum(m, sink)
            denom = jnp.exp(sink - m)
        else:
            denom = jnp.zeros_like(m)
        acc = None
        for s, (_, v_ref, _) in zip(scores, segs):
            p = jnp.exp(s - m)
            denom = denom + p.sum(axis=-1, keepdims=True)
            vs = v_ref[:, v_slab(h) * LANES:(v_slab(h) + 1) * LANES].astype(BF16)
            pv = jnp.dot(p.astype(BF16), vs, preferred_element_type=F32)
            acc = pv if acc is None else acc + pv
        half = v_half(h)
        outs.append(acc[:, half * 64:(half + 1) * 64] * (1.0 / denom))
    o_ref[...] = jnp.concatenate(outs, axis=1).astype(o_ref.dtype)


_C_CFG = dict(n_heads=H_C, k_slot=lambda h: h, v_slab=lambda h: h // 2, v_half=lambda h: h % 2)


def _attn_a_body(q_ref, segs, sink_ref, o_ref, *, tq, band_qi=None):
    rows = HQ_A * tq
    q = jnp.concatenate([q_ref[:, h * LANES:(h + 1) * LANES] for h in range(HQ_A)], axis=0)
    sink = jnp.concatenate([jnp.broadcast_to(sink_ref[h:h + 1, 0:1], (tq, 1)) for h in range(HQ_A)], axis=0)
    scores = []
    for k_ref, _, off in segs:
        s = lax.dot_general(q, k_ref[...].astype(BF16), (((1,), (1,)), ((), ())), preferred_element_type=F32)
        if off is not None:
            tk = s.shape[1]
            blk = band_qi + off
            q_pos = band_qi * tq + (lax.broadcasted_iota(jnp.int32, (rows, tk), 0) & (tq - 1))
            k_pos = blk * tk + lax.broadcasted_iota(jnp.int32, (rows, tk), 1)
            ok = (jnp.abs(k_pos - q_pos) <= WINDOW) & (blk >= 0) & (blk < DEC_SEQ // tk)
            s = jnp.where(ok, s, NEG_INF)
        scores.append(s)
    m = sink
    for s in scores:
        m = jnp.maximum(m, s.max(axis=-1, keepdims=True))
    denom = jnp.exp(sink - m)
    acc = None
    for s, (_, v_ref, _) in zip(scores, segs):
        p = jnp.exp(s - m)
        denom = denom + p.sum(axis=-1, keepdims=True)
        pv = jnp.dot(p.astype(BF16), v_ref[...].astype(BF16), preferred_element_type=F32)
        acc = pv if acc is None else acc + pv
    acc = acc * (1.0 / denom)
    outs = []
    for h in range(HQ_A):
        kv = h // G_A
        outs.append(acc[h * tq:(h + 1) * tq, kv * HD_A:(kv + 1) * HD_A])
    o_ref[...] = jnp.concatenate(outs, axis=1).astype(o_ref.dtype)


def _attn_a_ctx_kernel(q_ref, k_ref, v_ref, sink_ref, o_init, o_ref):
    del o_init
    _attn_a_body(q_ref, [(k_ref, v_ref, None)], sink_ref, o_ref, tq=SEQ)


def _attn_a_lat_kernel(q_ref, k0, k1, k2, v0, v1, v2, kc_ref, vc_ref, sink_ref, o_ctx, o_ref):
    del o_ctx
    qi = pl.program_id(1)
    segs = [(k0, v0, -1), (k1, v1, 0), (k2, v2, 1), (kc_ref, vc_ref, None)]
    _attn_a_body(q_ref, segs, sink_ref, o_ref, tq=BLK, band_qi=qi)


def _attn_c_ctx_kernel(q_ref, k_ref, v_ref, o_init, o_ref):
    del o_init
    _attn_body(q_ref, [(k_ref, v_ref, None)], None, o_ref, tq=SEQ, **_C_CFG)


def _attn_c_lat_kernel(q_ref, k_ref, v_ref, kc_ref, vc_ref, o_ctx, o_ref):
    del o_ctx
    _attn_body(q_ref, [(k_ref, v_ref, None), (kc_ref, vc_ref, None)], None, o_ref, tq=256, **_C_CFG)


def attn_a_ctx(qa, ka, proj, sink):
    return pl.pallas_call(
        _attn_a_ctx_kernel,
        out_shape=jax.ShapeDtypeStruct((T_ALL, HQ_A * HD_A), BF16),
        grid=(BATCH,),
        in_specs=[
            pl.BlockSpec((SEQ, HQ_A * LANES), lambda b: (b, 0)),
            pl.BlockSpec((SEQ, LANES), lambda b: (b, 0)),
            pl.BlockSpec((SEQ, LANES), lambda b: (b, C_VA // LANES)),
            pl.BlockSpec((SUBLANES, LANES), lambda b: (0, 0)),
            pl.BlockSpec(memory_space=pl.ANY),
        ],
        out_specs=pl.BlockSpec((SEQ, HQ_A * HD_A), lambda b: (b, 0)),
        input_output_aliases={4: 0},
        compiler_params=_cp(("arbitrary",)),
        name="attn_a_ctx",
    )(qa, ka, proj, sink, jnp.zeros((T_ALL, HQ_A * HD_A), BF16))


def attn_a_lat(qa, ka, proj, k_cache, v_cache, sink, o_all):
    nb = DEC_SEQ // BLK
    base = N_CTX // BLK

    def band(off, colblk):
        return pl.BlockSpec((BLK, LANES), lambda b, i: (base + b * nb + jnp.clip(i + off, 0, nb - 1), colblk))

    return pl.pallas_call(
        _attn_a_lat_kernel,
        out_shape=jax.ShapeDtypeStruct((T_ALL, HQ_A * HD_A), BF16),
        grid=(DEC_BATCH, nb),
        in_specs=[
            pl.BlockSpec((BLK, HQ_A * LANES), lambda b, i: (base + b * nb + i, 0)),
            band(-1, 0), band(0, 0), band(1, 0),
            band(-1, C_VA // LANES), band(0, C_VA // LANES), band(1, C_VA // LANES),
            pl.BlockSpec((None, PAST_LEN, LANES), lambda b, i: (b, 0, 0)),
            pl.BlockSpec((None, PAST_LEN, LANES), lambda b, i: (b, 0, 0)),
            pl.BlockSpec((SUBLANES, LANES), lambda b, i: (0, 0)),
            pl.BlockSpec(memory_space=pl.ANY),
        ],
        out_specs=pl.BlockSpec((BLK, HQ_A * HD_A), lambda b, i: (base + b * nb + i, 0)),
        input_output_aliases={10: 0},
        compiler_params=_cp(("arbitrary", "arbitrary")),
        name="attn_a_lat",
    )(qa, ka, ka, ka, proj, proj, proj, k_cache, v_cache, sink, o_all)


def attn_c_ctx(qc, kc, vc):
    return pl.pallas_call(
        _attn_c_ctx_kernel,
        out_shape=jax.ShapeDtypeStruct((T_ALL, H_C * V_C), BF16),
        grid=(BATCH,),
        in_specs=[
            pl.BlockSpec((SEQ, H_C * LANES), lambda b: (b, 0)),
            pl.BlockSpec((SEQ, H_C * LANES), lambda b: (b, 0)),
            pl.BlockSpec((SEQ, H_C * V_C), lambda b: (b, 0)),
            pl.BlockSpec(memory_space=pl.ANY),
        ],
        out_specs=pl.BlockSpec((SEQ, H_C * V_C), lambda b: (b, 0)),
        input_output_aliases={3: 0},
        compiler_params=_cp(("arbitrary",)),
        name="attn_c_ctx",
    )(qc, kc, vc, jnp.zeros((T_ALL, H_C * V_C), BF16))


def attn_c_lat(qc, kc, vc, kc_cache, vc_cache, o_all):
    tq = 256
    nq = DEC_SEQ // tq
    qbase = N_CTX // tq
    kbase = N_CTX // DEC_SEQ
    return pl.pallas_call(
        _attn_c_lat_kernel,
        out_shape=jax.ShapeDtypeStruct((T_ALL, H_C * V_C), BF16),
        grid=(DEC_BATCH, nq),
        in_specs=[
            pl.BlockSpec((tq, H_C * LANES), lambda b, i: (qbase + b * nq + i, 0)),
            pl.BlockSpec((DEC_SEQ, H_C * LANES), lambda b, i: (kbase + b, 0)),
            pl.BlockSpec((DEC_SEQ, H_C * V_C), lambda b, i: (kbase + b, 0)),
            pl.BlockSpec((PAST_LEN, H_C * LANES), lambda b, i: (b, 0)),
            pl.BlockSpec((PAST_LEN, H_C * V_C), lambda b, i: (b, 0)),
            pl.BlockSpec(memory_space=pl.ANY),
        ],
        out_specs=pl.BlockSpec((tq, H_C * V_C), lambda b, i: (qbase + b * nq + i, 0)),
        input_output_aliases={5: 0},
        compiler_params=_cp(("arbitrary", "arbitrary")),
        name="attn_c_lat",
    )(qc, kc, vc, kc_cache, vc_cache, o_all)


S5_TC = 256
S5_CH = 512
S5_UNROLL = 8
TOK_CHUNKS = T_ALL // S5_TC
CHUNK_GROUP = 8
assert SEQ == S5_TC and DEC_SEQ == CHUNK_GROUP * S5_TC and BATCH == 4 * CHUNK_GROUP


def _s5_bproj(u2, wb_ref, pick_first):
    def part(lo, hi):
        r = jnp.dot(u2, wb_ref[0, :, lo:hi], preferred_element_type=F32)
        if pick_first is not None:
            r = jnp.where(pick_first, r, jnp.dot(u2, wb_ref[1, :, lo:hi], preferred_element_type=F32))
        return r.reshape(S5_TC, SUBLANES, S5_CH)
    return part(0, S5_CH), part(S5_CH, 2 * S5_CH)


def _s5_cproj(xre, xim, wc_ref, pick_first):
    rows = S5_TC * SUBLANES
    xr = xre[...].reshape(rows, S5_CH).astype(BF16)
    xi = xim[...].reshape(rows, S5_CH).astype(BF16)

    def part(d):
        return (jnp.dot(xr, wc_ref[d, 0:S5_CH, :], preferred_element_type=F32)
                + jnp.dot(xi, wc_ref[d, S5_CH:, :], preferred_element_type=F32))

    y = part(0)
    if pick_first is not None:
        y = jnp.where(pick_first, y, part(1))
    return jnp.swapaxes(y.reshape(S5_TC, SUBLANES, LANES), 0, 1)


def _s5_ctx_kernel(u_ref, wb_ref, wc_ref, are_ref, aim_ref, y_init, y_ref, fre_ref, fim_ref, bre, bim):
    del y_init
    d = pl.program_id(2)
    rows = S5_TC * SUBLANES
    u2 = jnp.swapaxes(u_ref[...], 0, 1).reshape(rows, LANES).astype(BF16)
    b_re, b_im = _s5_bproj(u2, wb_ref, None)
    bre[...] = b_re
    bim[...] = b_im
    a_re = are_ref[...]
    a_im = aim_ref[...]

    def outer(i, carry):
        sr, si = carry
        for j in range(S5_UNROLL):
            t0 = i * S5_UNROLL + j
            t = jnp.where(d == 1, S5_TC - 1 - t0, t0)
            nr = a_re * sr - a_im * si + bre[t]
            ni = a_re * si + a_im * sr + bim[t]
            bre[t] = nr
            bim[t] = ni
            sr, si = nr, ni
        return sr, si

    zero = jnp.zeros((SUBLANES, S5_CH), F32)
    sr, si = lax.fori_loop(0, S5_TC // S5_UNROLL, outer, (zero, zero))
    fre_ref[...] = sr
    fim_ref[...] = si
    y = _s5_cproj(bre, bim, wc_ref, None)

    @pl.when(d == 0)
    def _():
        y_ref[...] = y

    @pl.when(d == 1)
    def _():
        y_ref[...] += y


def s5_ctx(proj4, wb, wc, a_re, a_im):
    nj = D_B // LANES
    ucol = C_UB // LANES
    return pl.pallas_call(
        _s5_ctx_kernel,
        out_shape=(
            jax.ShapeDtypeStruct((TOK_CHUNKS // CHUNK_GROUP, CHUNK_GROUP, S5_TC, D_B), F32),
            jax.ShapeDtypeStruct((2, BATCH, G_B * P_B), F32),
            jax.ShapeDtypeStruct((2, BATCH, G_B * P_B), F32),
        ),
        grid=(BATCH // CHUNK_GROUP, nj, 2),
        in_specs=[
            pl.BlockSpec((None, CHUNK_GROUP, S5_TC, LANES), lambda sb, j, d: (sb, 0, 0, ucol + j)),
            pl.BlockSpec((1, None, LANES, 2 * S5_CH), lambda sb, j, d: (d, j, 0, 0)),
            pl.BlockSpec((1, None, 2 * S5_CH, LANES), lambda sb, j, d: (d, j, 0, 0)),
            pl.BlockSpec((None, SUBLANES, S5_CH), lambda sb, j, d: (d, 0, j)),
            pl.BlockSpec((None, SUBLANES, S5_CH), lambda sb, j, d: (d, 0, j)),
            pl.BlockSpec(memory_space=pl.ANY),
        ],
        out_specs=(
            pl.BlockSpec((None, CHUNK_GROUP, S5_TC, LANES), lambda sb, j, d: (sb, 0, 0, j)),
            pl.BlockSpec((None, SUBLANES, S5_CH), lambda sb, j, d: (d, sb, j)),
            pl.BlockSpec((None, SUBLANES, S5_CH), lambda sb, j, d: (d, sb, j)),
        ),
        scratch_shapes=[pltpu.VMEM((S5_TC, SUBLANES, S5_CH), F32), pltpu.VMEM((S5_TC, SUBLANES, S5_CH), F32)],
        input_output_aliases={5: 0},
        compiler_params=_cp(("arbitrary", "arbitrary", "arbitrary")),
        name="s5_ctx",
    )(proj4, wb, wc, a_re, a_im, jnp.zeros((TOK_CHUNKS // CHUNK_GROUP, CHUNK_GROUP, S5_TC, D_B), F32))


def _s5_lat_kernel(uf_ref, ub_ref, wb_ref, wc_ref, are_ref, aim_ref, h0re_ref, h0im_ref, y_ctx,
                   yf_ref, yb_ref, bre, bim, xre, xim, sre, sim):
    del y_ctx
    k = pl.program_id(1)

    @pl.when(k == 0)
    def _():
        sre[...] = h0re_ref[...]
        sim[...] = h0im_ref[...]

    rows = S5_TC * SUBLANES
    half = SUBLANES // 2
    u8 = jnp.concatenate([uf_ref[...], ub_ref[...]], axis=0)
    u2 = jnp.swapaxes(u8, 0, 1).reshape(rows, LANES).astype(BF16)
    fwd_rows = (lax.broadcasted_iota(jnp.int32, (rows, 1), 0) % SUBLANES) < half
    b_re, b_im = _s5_bproj(u2, wb_ref, fwd_rows)
    bre[...] = b_re
    bim[...] = b_im
    a_re = are_ref[...]
    a_im = aim_ref[...]
    fwd8 = lax.broadcasted_iota(jnp.int32, (SUBLANES, 1), 0) < half

    def outer(i, carry):
        sr, si = carry
        for j in range(S5_UNROLL):
            t = i * S5_UNROLL + j
            tb = S5_TC - 1 - t
            nr = a_re * sr - a_im * si + jnp.where(fwd8, bre[t], bre[tb])
            ni = a_re * si + a_im * sr + jnp.where(fwd8, bim[t], bim[tb])
            xre[t, 0:half, :] = nr[0:half]
            xre[tb, half:, :] = nr[half:]
            xim[t, 0:half, :] = ni[0:half]
            xim[tb, half:, :] = ni[half:]
            sr, si = nr, ni
        return sr, si

    sr, si = lax.fori_loop(0, S5_TC // S5_UNROLL, outer, (sre[...], sim[...]))
    sre[...] = sr
    sim[...] = si
    y = _s5_cproj(xre, xim, wc_ref, fwd_rows)
    yf_ref[...] = y[0:half]
    yb_ref[...] = y[half:]


def s5_lat(proj4, y_all, wb, wc, a_re, a_im, h0_re, h0_im):
    nj = D_B // LANES
    nk = DEC_SEQ // S5_TC
    ucol = C_UB // LANES
    lat = lambda col0, rev: pl.BlockSpec(
        (DEC_BATCH, None, S5_TC, LANES), lambda j, k: (1, (nk - 1 - k) if rev else k, 0, col0 + j))
    vec = lambda: pl.BlockSpec((SUBLANES, S5_CH), lambda j, k: (0, j))
    buf = lambda: pltpu.VMEM((S5_TC, SUBLANES, S5_CH), F32)
    return pl.pallas_call(
        _s5_lat_kernel,
        out_shape=(
            jax.ShapeDtypeStruct(y_all.shape, F32),
            jax.ShapeDtypeStruct((DEC_BATCH, nk, S5_TC, D_B), F32),
        ),
        grid=(nj, nk),
        in_specs=[
            lat(ucol, False), lat(ucol, True),
            pl.BlockSpec((2, None, LANES, 2 * S5_CH), lambda j, k: (0, j, 0, 0)),
            pl.BlockSpec((2, None, 2 * S5_CH, LANES), lambda j, k: (0, j, 0, 0)),
            vec(), vec(), vec(), vec(),
            pl.BlockSpec(memory_space=pl.ANY),
        ],
        out_specs=(
            lat(0, False),
            pl.BlockSpec((DEC_BATCH, None, S5_TC, LANES), lambda j, k: (0, nk - 1 - k, 0, j)),
        ),
        scratch_shapes=[buf(), buf(), buf(), buf(),
                        pltpu.VMEM((SUBLANES, S5_CH), F32), pltpu.VMEM((SUBLANES, S5_CH), F32)],
        input_output_aliases={8: 0},
        compiler_params=_cp(("arbitrary", "arbitrary")),
        name="s5_lat",
    )(proj4, proj4, wb, wc, a_re, a_im, h0_re, h0_im, y_all)


S5FIN_TM = 512


def _s5fin_kernel(u_ref, y_ref, yb_ref, d_ref, w_ref, b_ref, o_ref):
    def body(latent):
        y = d_ref[...] * u_ref[...] + y_ref[...]
        if latent:
            y = y + yb_ref[...]
        y = 0.5 * y * (1.0 + jnp.tanh(math.sqrt(2.0 / math.pi) * (y + 0.044715 * (y * y * y))))
        z = jnp.dot(y.astype(BF16), w_ref[...], preferred_element_type=F32) + b_ref[...]
        o_ref[...] = (y * _sigmoid(z)).astype(o_ref.dtype)

    is_latent = pl.program_id(0) >= N_CTX // S5FIN_TM
    pl.when(is_latent)(lambda: body(True))
    pl.when(jnp.logical_not(is_latent))(lambda: body(False))


def s5_finish(proj, y, y_bwd_lat, d, w_glu, b_glu):
    tm = S5FIN_TM
    n_ctx_tiles = N_CTX // tm
    return pl.pallas_call(
        _s5fin_kernel,
        out_shape=jax.ShapeDtypeStruct((T_ALL, D_B), BF16),
        grid=(T_ALL // tm,),
        in_specs=[
            pl.BlockSpec((tm, D_B), lambda i: (i, C_UB // D_B)),
            pl.BlockSpec((tm, D_B), lambda i: (i, 0)),
            pl.BlockSpec((tm, D_B), lambda i: (jnp.maximum(i - n_ctx_tiles, 0), 0)),
            pl.BlockSpec((1, D_B), lambda i: (0, 0)),
            pl.BlockSpec((D_B, D_B), lambda i: (0, 0)),
            pl.BlockSpec((1, D_B), lambda i: (0, 0)),
        ],
        out_specs=pl.BlockSpec((tm, D_B), lambda i: (i, 0)),
        compiler_params=_cp(("arbitrary",)),
        name="s5_finish",
    )(proj, y, y_bwd_lat, d, w_glu, b_glu)


def _merge_kernel(x_ref, gl_ref, oa_ref, ob_ref, oc_ref, mod_ref, g_ref, wa_ref, wb_ref, wc_ref, wo_ref,
                  x1_ref, h2_ref, h2row_ref):
    m = None
    for br, (o_ref, w_ref) in enumerate(((oa_ref, wa_ref), (ob_ref, wb_ref), (oc_ref, wc_ref))):
        gate = _sigmoid(gl_ref[:, br * D_MODEL:(br + 1) * D_MODEL])
        t = gate * jnp.dot(o_ref[...], w_ref[...], preferred_element_type=F32)
        m = t if m is None else m + t
    x1 = x_ref[...] + mod_ref[2:3, :] * jnp.dot(m.astype(BF16), wo_ref[...], preferred_element_type=F32)
    x1_ref[...] = x1
    h2 = _rms(x1, g_ref[...])
    h2 = h2 * (1.0 + mod_ref[4:5, :]) + mod_ref[3:4, :]
    h2_ref[...] = h2
    h2row_ref[...] = h2.reshape((h2.shape[0],) + ROW3)


def merge(x, proj, oa, ob, oc, mod, g, wa, wb, wc, wo):
    tm = 256
    full = lambda shape: pl.BlockSpec(shape, lambda i: (0,) * len(shape))
    return pl.pallas_call(
        _merge_kernel,
        out_shape=(jax.ShapeDtypeStruct((T_ALL, D_MODEL), F32), jax.ShapeDtypeStruct((T_ALL, D_MODEL), F32),
                   jax.ShapeDtypeStruct((T_ALL,) + ROW3, F32)),
        grid=(T_ALL // tm,),
        in_specs=[
            pl.BlockSpec((tm, D_MODEL), lambda i: (i, 0)),
            pl.BlockSpec((tm, 3 * D_MODEL), lambda i: (i, 0)),
            pl.BlockSpec((tm, 512), lambda i: (i, 0)),
            pl.BlockSpec((tm, 512), lambda i: (i, 0)),
            pl.BlockSpec((tm, 512), lambda i: (i, 0)),
            pl.BlockSpec((None, 6, D_MODEL), lambda i: (_group_of_tile(i, tm), 0, 0)),
            full((1, D_MODEL)),
            full((512, D_MODEL)), full((512, D_MODEL)), full((512, D_MODEL)), full((D_MODEL, D_MODEL)),
        ],
        out_specs=(pl.BlockSpec((tm, D_MODEL), lambda i: (i, 0)), pl.BlockSpec((tm, D_MODEL), lambda i: (i, 0)),
                   pl.BlockSpec((tm,) + ROW3, lambda i: (i, 0, 0))),
        compiler_params=_cp(("arbitrary",)),
        name="merge",
    )(x, proj, oa, ob, oc, mod, g, wa, wb, wc, wo)


ROUTE_TM = 256


def _router_kernel(h_ref, w_ref, b_ref, idx_ref, rank_ref, wgt_ref, cnt_ref, cnt_acc):
    i = pl.program_id(0)

    @pl.when(i == 0)
    def _():
        cnt_acc[...] = jnp.zeros_like(cnt_acc)

    h = h_ref[...]
    w = w_ref[...]
    h_hi = h.astype(BF16)
    h_lo = (h - h_hi.astype(F32)).astype(BF16)
    w_hi = w.astype(BF16)
    w_lo = (w - w_hi.astype(F32)).astype(BF16)
    dot = functools.partial(jnp.dot, preferred_element_type=F32)
    logits = dot(h_hi, w_hi) + (dot(h_hi, w_lo) + dot(h_lo, w_hi)) + b_ref[...]
    lane_i = lax.broadcasted_iota(jnp.int32, (ROUTE_TM, LANES), 1)
    lane = lane_i.astype(F32)
    r_i = lax.broadcasted_iota(jnp.int32, (ROUTE_TM, ROUTE_TM), 0)
    c_i = lax.broadcasted_iota(jnp.int32, (ROUTE_TM, ROUTE_TM), 1)
    earlier = jnp.where(c_i < r_i, 1.0, 0.0).astype(BF16)

    cnt = cnt_acc[...]
    idx_out = jnp.zeros((ROUTE_TM, LANES), F32)
    rank_out = jnp.zeros((ROUTE_TM, LANES), F32)
    val_out = jnp.zeros((ROUTE_TM, LANES), F32)
    v0 = None
    esum = None
    for k in range(TOP_K):
        m = logits.max(axis=-1, keepdims=True)
        sel = jnp.min(jnp.where(logits == m, lane, float(LANES)), axis=-1, keepdims=True)
        hit = lane == sel
        logits = jnp.where(hit, -jnp.inf, logits)
        onehot = jnp.where(hit, 1.0, 0.0)
        within = jnp.dot(earlier, onehot.astype(BF16), preferred_element_type=F32)
        rank = jnp.sum(onehot * (within + cnt), axis=-1, keepdims=True)
        cnt = cnt + jnp.sum(onehot, axis=0, keepdims=True)
        if k == 0:
            v0 = m
        e = jnp.exp(m - v0)
        esum = e if esum is None else esum + e
        idx_out = jnp.where(lane_i == k, sel, idx_out)
        rank_out = jnp.where(lane_i == k, rank, rank_out)
        val_out = jnp.where(lane_i == k, e, val_out)
    cnt_acc[...] = cnt
    cnt_ref[...] = cnt
    idx_ref[...] = idx_out.astype(jnp.int32)
    rank_ref[...] = rank_out.astype(jnp.int32)
    wgt_ref[...] = val_out * (1.0 / esum)


def router(h2, w_router, b_router):
    tile = lambda: pl.BlockSpec((ROUTE_TM, LANES), lambda i: (i, 0))
    return pl.pallas_call(
        _router_kernel,
        out_shape=(
            jax.ShapeDtypeStruct((T_ALL, LANES), jnp.int32),
            jax.ShapeDtypeStruct((T_ALL, LANES), jnp.int32),
            jax.ShapeDtypeStruct((T_ALL, LANES), F32),
            jax.ShapeDtypeStruct((1, LANES), F32),
        ),
        grid=(T_ALL // ROUTE_TM,),
        in_specs=[
            pl.BlockSpec((ROUTE_TM, D_MODEL), lambda i: (i, 0)),
            pl.BlockSpec((D_MODEL, LANES), lambda i: (0, 0)),
            pl.BlockSpec((1, LANES), lambda i: (0, 0)),
        ],
        out_specs=(tile(), tile(), tile(), pl.BlockSpec((1, LANES), lambda i: (0, 0))),
        scratch_shapes=[pltpu.VMEM((1, LANES), F32)],
        compiler_params=_cp(("arbitrary",)),
        name="router",
    )(h2, w_router, b_router)


PLAN_UNROLL = 8
N_DUMP = 5 * BLK
PLAN_LEN = BUF_LEN + BLK
assert TOP_K == 4 and T_ALL & (T_ALL - 1) == 0


def _invert_kernel(dest_ref, fill_hbm, inv_ref):
    pltpu.sync_copy(fill_hbm, inv_ref)
    tok_step = PLAN_UNROLL // TOP_K

    def put(i, c):
        for j in range(PLAN_UNROLL):
            inv_ref[dest_ref[i * PLAN_UNROLL + j]] = i * tok_step + ((j % TOP_K) * T_ALL + j // TOP_K)
        return c

    lax.fori_loop(0, N_ASSIGN // PLAN_UNROLL, put, 0)


def invert_plan(dest):
    r = jnp.arange(PLAN_LEN, dtype=jnp.int32)
    fill = N_ASSIGN + jnp.where(r < BLK, 2 * BLK + r, (r - BLK) & (2 * BLK - 1))
    return pl.pallas_call(
        _invert_kernel,
        out_shape=jax.ShapeDtypeStruct((PLAN_LEN,), jnp.int32),
        in_specs=[pl.BlockSpec(memory_space=pltpu.SMEM), pl.BlockSpec(memory_space=pl.ANY)],
        out_specs=pl.BlockSpec(memory_space=pltpu.SMEM),
        name="invert_plan",
    )(dest, fill)


PAIR_TILE = 2 * LANES


def _pair_selection():
    r = lax.broadcasted_iota(jnp.int32, (PAIR_TILE, PAIR_TILE), 0)
    c = lax.broadcasted_iota(jnp.int32, (PAIR_TILE, PAIR_TILE), 1)
    return (r == jnp.where(c < LANES, 2 * c, 2 * (c - LANES) + 1)).astype(BF16)


EXP_NBUF = 3


def _expert_kernel(be_ref, first_ref, inv_ref, h_hbm, wu32_ref, bu_ref, wd32_ref, bd_ref, sel_ref, comb_hbm,
                   xbuf, ybuf, wu_ref, wd_ref, gsem, ssem):
    del be_ref
    i = pl.program_id(0)
    n = pl.num_programs(0)
    cur = i % EXP_NBUF
    nxt = (i + 2) % EXP_NBUF
    prv = nxt

    def gather(block, s):
        base = (block + 1) * BLK
        for r in range(BLK):
            tok = inv_ref[base + r] & (T_ALL - 1)
            pltpu.make_async_copy(h_hbm.at[tok], xbuf.at[s, r], gsem.at[s]).start()

    def scatter(block, s):
        base = (block + 1) * BLK
        for r in range(BLK):
            a = inv_ref[base + r]
            pltpu.make_async_copy(ybuf.at[s, r], comb_hbm.at[a], ssem.at[s]).start()

    def wait_block(sem):
        pltpu.make_async_copy(h_hbm.at[pl.ds(0, BLK)], xbuf.at[0], sem).wait()

    @pl.when(i == 0)
    def _():
        ybuf[...] = jnp.zeros_like(ybuf)
        for s in range(EXP_NBUF - 1):
            for r in range(BLK):
                dump = N_ASSIGN + (3 + s) * BLK + r
                pltpu.make_async_copy(ybuf.at[s, r], comb_hbm.at[dump], ssem.at[s]).start()
        gather(0, 0)
        gather(1, 1)

    wait_block(gsem.at[cur])
    wait_block(ssem.at[cur])

    @pl.when(first_ref[i] == 1)
    def _():
        sel = sel_ref[...]
        for j in range(2 * D_FF // PAIR_TILE):
            cols = slice(j * PAIR_TILE, (j + 1) * PAIR_TILE)
            wu_ref[:, cols] = jnp.dot(wu32_ref[:, cols].astype(BF16), sel, preferred_element_type=F32).astype(BF16)
        wd_ref[...] = wd32_ref[...].astype(BF16)

    gather(jnp.minimum(i + 2, n - 1), nxt)
    scatter(i - 1, prv)

    x = xbuf[cur].reshape(BLK, D_MODEL).astype(BF16)
    h = jnp.dot(x, wu_ref[...], preferred_element_type=F32) + bu_ref[...]
    acts = []
    for j in range(2 * D_FF // PAIR_TILE):
        glu = jnp.minimum(h[:, j * PAIR_TILE:j * PAIR_TILE + LANES], SWIGLU_LIMIT)
        lin = jnp.clip(h[:, j * PAIR_TILE + LANES:(j + 1) * PAIR_TILE], -SWIGLU_LIMIT, SWIGLU_LIMIT)
        acts.append((glu * _sigmoid(SWIGLU_ALPHA * glu) * (lin + 1.0)).astype(BF16))
    act = jnp.concatenate(acts, axis=1)
    y = jnp.dot(act, wd_ref[...], preferred_element_type=F32) + bd_ref[...]
    ybuf[cur] = y.reshape((BLK,) + ROW3)

    @pl.when(i == n - 1)
    def _():
        for s in range(EXP_NBUF):
            @pl.when(s != cur)
            def _():
                wait_block(gsem.at[s])
                wait_block(ssem.at[s])
        scatter(i, cur)
        wait_block(ssem.at[cur])


def experts(layer, block_e, first, inv, h2row, wu, bu, wd, bd):
    per_expert = lambda *blk: pl.BlockSpec((None, None) + blk, lambda i, be, first, inv: (layer, be[i], 0, 0))
    return pl.pallas_call(
        _expert_kernel,
        out_shape=jax.ShapeDtypeStruct((N_ASSIGN + N_DUMP,) + ROW3, F32),
        grid_spec=pltpu.PrefetchScalarGridSpec(
            num_scalar_prefetch=3,
            grid=(N_BLOCKS,),
            in_specs=[
                pl.BlockSpec(memory_space=pl.ANY),
                per_expert(D_MODEL, 2 * D_FF), per_expert(1, 2 * D_FF), per_expert(D_FF, D_MODEL),
                per_expert(1, D_MODEL),
                pl.BlockSpec((PAIR_TILE, PAIR_TILE), lambda i, be, first, inv: (0, 0)),
            ],
            out_specs=pl.BlockSpec(memory_space=pl.ANY),
            scratch_shapes=[
                pltpu.VMEM((EXP_NBUF, BLK) + ROW3, F32), pltpu.VMEM((EXP_NBUF, BLK) + ROW3, F32),
                pltpu.VMEM((D_MODEL, 2 * D_FF), BF16), pltpu.VMEM((D_FF, D_MODEL), BF16),
                pltpu.SemaphoreType.DMA((EXP_NBUF,)), pltpu.SemaphoreType.DMA((EXP_NBUF,)),
            ],
        ),
        compiler_params=_cp(("arbitrary",)),
        name="experts",
    )(block_e, first, inv, h2row, wu, bu, wd, bd, _pair_selection())


COMB_TM = 256


def _combine_kernel(c0_ref, c1_ref, c2_ref, c3_ref, x_ref, w_ref, mod_ref, o_ref):
    w = w_ref[...]
    acc = None
    for k, c_ref in enumerate((c0_ref, c1_ref, c2_ref, c3_ref)):
        t = w[:, k:k + 1] * c_ref[...].reshape(COMB_TM, D_MODEL)
        acc = t if acc is None else acc + t
    o_ref[...] = x_ref[...] + mod_ref[5:6, :] * acc


def combine(comb, x1, wgt, mod):
    tiles = T_ALL // COMB_TM
    kth = lambda k: pl.BlockSpec((COMB_TM,) + ROW3, lambda i: (k * tiles + i, 0, 0))
    return pl.pallas_call(
        _combine_kernel,
        out_shape=jax.ShapeDtypeStruct((T_ALL, D_MODEL), F32),
        grid=(tiles,),
        in_specs=[
            kth(0), kth(1), kth(2), kth(3),
            pl.BlockSpec((COMB_TM, D_MODEL), lambda i: (i, 0)),
            pl.BlockSpec((COMB_TM, LANES), lambda i: (i, 0)),
            pl.BlockSpec((None, 6, D_MODEL), lambda i: (_group_of_tile(i, COMB_TM), 0, 0)),
        ],
        out_specs=pl.BlockSpec((COMB_TM, D_MODEL), lambda i: (i, 0)),
        compiler_params=_cp(("arbitrary",)),
        name="combine",
    )(comb, comb, comb, comb, x1, wgt, mod)


def _rope_table():
    pos = jnp.arange(DEC_SEQ)
    row = (pos // GRID_W).astype(F32)[:, None]
    col = (pos % GRID_W).astype(F32)[:, None]

    def parts(rot_dim):
        nf = rot_dim // 4
        inv = ROPE_BASE ** (-jnp.arange(nf, dtype=F32) / nf)
        cr, sr, cc, sc = jnp.cos(row * inv), jnp.sin(row * inv), jnp.cos(col * inv), jnp.sin(col * inv)
        z = jnp.zeros_like(sr)
        cos = jnp.concatenate([cr, cr, cc, cc], axis=1)
        lo = jnp.concatenate([-sr, z, -sc, z], axis=1)
        hi = jnp.concatenate([z, sr, z, sc], axis=1)
        return cos, lo, hi

    ca, la, ha = (jnp.tile(t, (1, 2)) for t in parts(HD_A))
    cc, lc, hc = parts(ROPE_C)
    pad = lambda t, fill: jnp.pad(t, ((0, 0), (NOPE_C, LANES - QK_C)), constant_values=fill)
    lat = jnp.concatenate([ca, la, ha, pad(cc, 1.0), pad(lc, 0.0), pad(hc, 0.0)], axis=1)
    ones, zeros = jnp.ones((DEC_SEQ, LANES), F32), jnp.zeros((DEC_SEQ, LANES), F32)
    ident = jnp.concatenate([ones, zeros, zeros, ones, zeros, zeros], axis=1)
    return jnp.concatenate([ident, lat], axis=0)


def _arrange_w_in(w_in):
    o = 0
    parts = {}
    for name, n in (("qa", HQ_A * HD_A), ("ka", HKV_A * HD_A), ("va", HKV_A * HD_A), ("ub", D_B), ("cq", Q_LORA),
                    ("ckv", KV_LORA), ("kr", ROPE_C), ("gl", 3 * D_MODEL)):
        parts[name] = w_in[..., o:o + n]
        o += n
    qa = parts["qa"].reshape(DEPTH, D_MODEL, HKV_A, G_A, 1, HD_A)
    eye = jnp.eye(HKV_A, dtype=F32).reshape(1, 1, HKV_A, 1, HKV_A, 1)
    qa_slots = (qa * eye).reshape(DEPTH, D_MODEL, HQ_A * LANES)
    kr = jnp.pad(parts["kr"], ((0, 0), (0, 0), (NOPE_C, LANES - QK_C)))
    w = jnp.concatenate([parts["gl"], qa_slots, parts["ub"], parts["cq"], parts["ka"], parts["va"], parts["ckv"], kr],
                        axis=-1)
    return w.astype(BF16)


def _s5_params(lam_re, lam_im, log_dt, b_re, b_im, c_re, c_im):
    dt = jnp.exp(log_dt)[..., None]
    decay = jnp.exp(lam_re * dt)
    ab_re, ab_im = decay * jnp.cos(lam_im * dt), decay * jnp.sin(lam_im * dt)
    den = lam_re * lam_re + lam_im * lam_im
    f_re = ((ab_re - 1) * lam_re + ab_im * lam_im) / den
    f_im = (ab_im * lam_re - (ab_re - 1) * lam_im) / den
    bb_re = f_re[..., None] * b_re - f_im[..., None] * b_im
    bb_im = f_re[..., None] * b_im + f_im[..., None] * b_re
    nj, gpb = D_B // LANES, LANES // GS_B
    eye = jnp.eye(gpb, dtype=F32)

    def blockdiag_b(bb):
        t = bb.transpose(0, 1, 3, 2).reshape(2, nj, gpb, GS_B, P_B)
        return (t[:, :, :, :, None, :] * eye[None, None, :, None, :, None]).reshape(2, nj, LANES, gpb * P_B)

    def blockdiag_c(cc):
        t = cc.transpose(0, 1, 3, 2).reshape(2, nj, gpb, P_B, GS_B)
        return (t[:, :, :, :, None, :] * eye[None, None, :, None, :, None]).reshape(2, nj, gpb * P_B, LANES)

    wb = jnp.concatenate([blockdiag_b(bb_re), blockdiag_b(bb_im)], axis=-1).astype(BF16)
    wc = jnp.concatenate([blockdiag_c(c_re), -blockdiag_c(c_im)], axis=-2).astype(BF16)
    return wb, wc, ab_re.reshape(2, G_B * P_B), ab_im.reshape(2, G_B * P_B)


def kernel(x_prompt, x_sample, c, cache_attn_k, cache_attn_v, cache_mla_ckv, cache_mla_krope, state_ssm_re, state_ssm_im, c_ctx, w_ada, b_ada, norm_mix_g, norm_ffn_g, w_in, q_norm_a, k_norm_a, sink_a, q_a_norm_c, kv_a_norm_c, w_uq_c, w_ukv_c, q_norm_c, k_norm_c, ssm_lam_re, ssm_lam_im, ssm_log_dt, ssm_b_re, ssm_b_im, ssm_c_re, ssm_c_im, ssm_d, w_glu, b_glu, w_br_a, w_br_b, w_br_c, w_out, w_router, b_router, w_up, b_up, w_down, b_down):
    x = jnp.concatenate([x_prompt.reshape(N_CTX, D_MODEL), x_sample.reshape(N_LAT, D_MODEL)], axis=0)
    cvecs = jnp.concatenate([c_ctx[None], c, jnp.zeros((N_GROUPS - 1 - DEC_BATCH, D_MODEL), F32)], axis=0)
    mods = adaln(cvecs, w_ada, b_ada)

    tab = _rope_table()
    w_in_r = _arrange_w_in(w_in)
    pad_slot = lambda g: jnp.pad(g, ((0, 0), (0, LANES - QK_C))).reshape(DEPTH, 1, LANES)
    gqa = jnp.tile(q_norm_a, (1, 2)).reshape(DEPTH, 1, LANES)
    gka = jnp.tile(k_norm_a, (1, 2)).reshape(DEPTH, 1, LANES)
    gqc, gkc = pad_slot(q_norm_c), pad_slot(k_norm_c)
    wuq = jnp.pad(w_uq_c.reshape(DEPTH, Q_LORA, H_C, QK_C), ((0, 0), (0, 0), (0, 0), (0, LANES - QK_C)))
    wuq = wuq.reshape(DEPTH, Q_LORA, H_C * LANES).astype(BF16)
    wukv4 = w_ukv_c.reshape(DEPTH, KV_LORA, H_C, NOPE_C + V_C)
    wuk = jnp.pad(wukv4[..., :NOPE_C], ((0, 0), (0, 0), (0, 0), (0, LANES - NOPE_C))).reshape(DEPTH, KV_LORA, H_C * LANES)
    wuv = wukv4[..., NOPE_C:].reshape(DEPTH, KV_LORA, H_C * V_C)
    wukv = jnp.concatenate([wuk, wuv], axis=-1).astype(BF16)
    sink = jnp.broadcast_to(sink_a[:, :, None], (DEPTH, HQ_A, LANES))
    w_router_p = jnp.pad(w_router, ((0, 0), (0, 0), (0, LANES - N_EXP)))
    b_router_p = jnp.pad(b_router, ((0, 0), (0, LANES - N_EXP)), constant_values=-jnp.inf).reshape(DEPTH, 1, LANES)
    b_up_p = b_up.reshape(DEPTH, N_EXP, 2 * D_FF // PAIR_TILE, LANES, 2).transpose(0, 1, 2, 4, 3)
    b_up_p = b_up_p.reshape(DEPTH, N_EXP, 1, 2 * D_FF)
    b_down_r = b_down.reshape(DEPTH, N_EXP, 1, D_MODEL)
    kr_cache = jnp.pad(cache_mla_krope, ((0, 0), (0, 0), (0, 0), (NOPE_C, LANES - QK_C)))

    new_k, new_v, new_ckv, new_kr, new_sre, new_sim = [], [], [], [], [], []
    for l in range(DEPTH):
        mod = mods[l]
        proj = inproj(x, norm_mix_g[l], mod, w_in_r[l])
        qa, ka, qc, kc, vc, ckvn = prep(proj, tab, gqa[l], gka[l], q_a_norm_c[l].reshape(1, Q_LORA),
                                         kv_a_norm_c[l].reshape(1, KV_LORA), gqc[l], gkc[l], wuq[l], wukv[l])
        kc_cache, vc_cache = cache_keys(cache_mla_ckv[:, l].reshape(DEC_BATCH * PAST_LEN, KV_LORA),
                                        kr_cache[:, l].reshape(DEC_BATCH * PAST_LEN, LANES), gkc[l], wukv[l])
        oa = attn_a_lat(qa, ka, proj, cache_attn_k[:, l].reshape(DEC_BATCH, PAST_LEN, LANES),
                        cache_attn_v[:, l].reshape(DEC_BATCH, PAST_LEN, LANES), sink[l],
                        attn_a_ctx(qa, ka, proj, sink[l]))
        oc = attn_c_lat(qc, kc, vc, kc_cache, vc_cache, attn_c_ctx(qc, kc, vc))
        wb, wc, a_re, a_im = _s5_params(ssm_lam_re[l], ssm_lam_im[l], ssm_log_dt[l], ssm_b_re[l], ssm_b_im[l],
                                        ssm_c_re[l], ssm_c_im[l])
        proj4 = proj.reshape(TOK_CHUNKS // CHUNK_GROUP, CHUNK_GROUP, S5_TC, N_PROJ)
        per_row = lambda a, rep: jnp.repeat(a, rep, axis=0)
        y_all, f_re, f_im = s5_ctx(proj4, wb, wc, per_row(a_re, SUBLANES).reshape(2, SUBLANES, -1),
                                   per_row(a_im, SUBLANES).reshape(2, SUBLANES, -1))
        h0 = lambda st: st[:, l].transpose(1, 0, 2, 3).reshape(2 * DEC_BATCH, G_B * P_B)
        y_all, y_bwd = s5_lat(proj4, y_all, wb, wc, per_row(a_re, DEC_BATCH), per_row(a_im, DEC_BATCH),
                              h0(state_ssm_re), h0(state_ssm_im))
        ob = s5_finish(proj, y_all.reshape(T_ALL, D_B), y_bwd.reshape(N_LAT, D_B), ssm_d[l].reshape(1, D_B),
                       w_glu[l].astype(BF16), b_glu[l].reshape(1, D_B))
        x1, h2, h2row = merge(x, proj, oa, ob, oc, mod, norm_ffn_g[l].reshape(1, D_MODEL), w_br_a[l].astype(BF16),
                       w_br_b[l].astype(BF16), w_br_c[l].astype(BF16), w_out[l].astype(BF16))
        idx, rank, wgt, cnt = router(h2, w_router_p[l], b_router_p[l])
        counts = cnt[0, :N_EXP].astype(jnp.int32)
        padded = (counts + BLK - 1) // BLK * BLK
        pad_end = jnp.cumsum(padded)
        pad_start = pad_end - padded
        dest = (BLK + pad_start[idx[:, :TOP_K]] + rank[:, :TOP_K]).reshape(-1).astype(jnp.int32)
        block_start = jnp.arange(N_BLOCKS, dtype=jnp.int32) * BLK
        block_e = jnp.minimum(jnp.sum(pad_end[None, :] <= block_start[:, None], axis=1), N_EXP - 1)
        block_e = block_e.astype(jnp.int32)
        first = jnp.concatenate([jnp.ones((1,), jnp.int32), (block_e[1:] != block_e[:-1]).astype(jnp.int32)])
        comb = experts(l, block_e, first, invert_plan(dest), h2row, w_up, b_up_p, w_down, b_down_r)
        x = combine(comb, x1, wgt, mod)

        new_k.append(ka[:N_CTX].reshape(BATCH, SEQ, HKV_A, HD_A))
        new_v.append(proj[:N_CTX, C_VA:C_VA + LANES].reshape(BATCH, SEQ, HKV_A, HD_A))
        new_ckv.append(ckvn[:N_CTX].reshape(BATCH, SEQ, KV_LORA))
        new_kr.append(proj[:N_CTX, C_KR + NOPE_C:C_KR + QK_C].reshape(BATCH, SEQ, ROPE_C))
        new_sre.append(f_re.reshape(2, BATCH, G_B, P_B).transpose(1, 0, 2, 3))
        new_sim.append(f_im.reshape(2, BATCH, G_B, P_B).transpose(1, 0, 2, 3))

    y_prompt = x[:N_CTX].reshape(BATCH, SEQ, D_MODEL)
    y_sample = x[N_CTX:].reshape(DEC_BATCH, DEC_SEQ, D_MODEL)
    return (y_prompt, y_sample, jnp.stack(new_k, axis=1), jnp.stack(new_v, axis=1), jnp.stack(new_ckv, axis=1),
            jnp.stack(new_kr, axis=1), jnp.stack(new_sre, axis=1), jnp.stack(new_sim, axis=1))
```

```python
import functools
import math

import jax
import jax.numpy as jnp
from jax import lax
from jax.experimental import pallas as pl
from jax.experimental.pallas import tpu as pltpu

D_MODEL = 1024
BATCH = 32
SEQ = 256
DEPTH = 2
DEC_BATCH = 4
DEC_SEQ = 2048
PAST_LEN = 256
GRID_W = 64
ROPE_BASE = 10000.0
EPS = 1e-6
NEG_INF = -1e30
BAND = 128
BLK = 128
HQ_A, HKV_A, HD_A = 8, 2, 64
G_A = HQ_A // HKV_A
WINDOW = 128
D_B, GS_B, P_B = 512, 16, 64
G_B = D_B // GS_B
H_C, Q_LORA, KV_LORA, NOPE_C, ROPE_C, V_C = 8, 256, 128, 64, 32, 64
QK_C = NOPE_C + ROPE_C
N_EXP, TOP_K, D_FF = 32, 4, 1024
SWIGLU_ALPHA, SWIGLU_LIMIT = 1.702, 7.0

N_CTX = BATCH * SEQ
N_LAT = DEC_BATCH * DEC_SEQ
T_ALL = N_CTX + N_LAT
N_GROUPS = 8

LANES = 128
SUBLANES = 8

C_GL = 0
C_QA = 3 * D_MODEL
C_UB = C_QA + HQ_A * LANES
C_CQ = C_UB + D_B
C_KA = C_CQ + Q_LORA
C_VA = C_KA + LANES
C_CKV = C_VA + LANES
C_KR = C_CKV + LANES
N_PROJ = C_KR + LANES

N_ASSIGN = T_ALL * TOP_K
N_BLOCKS = N_ASSIGN // BLK + N_EXP
BUF_LEN = N_BLOCKS * BLK

F32 = jnp.float32
BF16 = jnp.bfloat16
ROW3 = (D_MODEL // LANES, LANES)
VMEM_LIMIT = 56 * 1024 * 1024


def _cp(sem, vmem=VMEM_LIMIT):
    return pltpu.CompilerParams(dimension_semantics=sem, vmem_limit_bytes=vmem)


def _group_of_tile(i, tm):
    n_ctx_tiles = N_CTX // tm
    per_batch = DEC_SEQ // tm
    return jnp.where(i < n_ctx_tiles, 0, 1 + (i - n_ctx_tiles) // per_batch)


def _sigmoid(x):
    return 1.0 / (1.0 + jnp.exp(-x))


def _adaln_kernel(c_ref, w_ref, b_ref, o_ref):
    c = c_ref[...]
    s = c * _sigmoid(c)
    o_ref[...] = jnp.dot(s.astype(BF16), w_ref[...].astype(BF16), preferred_element_type=F32) + b_ref[...]


def adaln(cvecs, w_ada, b_ada):
    tn = 1536
    out = pl.pallas_call(
        _adaln_kernel,
        out_shape=jax.ShapeDtypeStruct((DEPTH, N_GROUPS, 6 * D_MODEL), F32),
        grid=(DEPTH, 6 * D_MODEL // tn),
        in_specs=[
            pl.BlockSpec((N_GROUPS, D_MODEL), lambda l, j: (0, 0)),
            pl.BlockSpec((None, D_MODEL, tn), lambda l, j: (l, 0, j)),
            pl.BlockSpec((None, 1, tn), lambda l, j: (l, 0, j)),
        ],
        out_specs=pl.BlockSpec((None, N_GROUPS, tn), lambda l, j: (l, 0, j)),
        compiler_params=_cp(("arbitrary", "arbitrary")),
        name="adaln",
    )(cvecs, w_ada, b_ada.reshape(DEPTH, 1, 6 * D_MODEL))
    return out.reshape(DEPTH, N_GROUPS, 6, D_MODEL)


def _rms(x, g):
    ms = jnp.mean(x * x, axis=-1, keepdims=True)
    return x * lax.rsqrt(ms + EPS) * g


def _inproj_kernel(x_ref, g_ref, mod_ref, w_ref, o_ref):
    h = _rms(x_ref[...], g_ref[...])
    h = h * (1.0 + mod_ref[1:2, :]) + mod_ref[0:1, :]
    o_ref[...] = jnp.dot(h.astype(BF16), w_ref[...], preferred_element_type=F32)


def inproj(x, g, mod, w):
    tm, tn = 512, 1792
    return pl.pallas_call(
        _inproj_kernel,
        out_shape=jax.ShapeDtypeStruct((T_ALL, N_PROJ), F32),
        grid=(N_PROJ // tn, T_ALL // tm),
        in_specs=[
            pl.BlockSpec((tm, D_MODEL), lambda j, i: (i, 0)),
            pl.BlockSpec((1, D_MODEL), lambda j, i: (0, 0)),
            pl.BlockSpec((None, 6, D_MODEL), lambda j, i: (_group_of_tile(i, tm), 0, 0)),
            pl.BlockSpec((D_MODEL, tn), lambda j, i: (0, j)),
        ],
        out_specs=pl.BlockSpec((tm, tn), lambda j, i: (i, j)),
        compiler_params=_cp(("arbitrary", "arbitrary")),
        name="inproj",
    )(x, g.reshape(1, D_MODEL), mod, w)


def _rope(slab, cos, sin, swap):
    hi = slab.astype(BF16)
    lo = (slab - hi.astype(F32)).astype(BF16)
    partner = jnp.dot(hi, swap, preferred_element_type=F32) + jnp.dot(lo, swap, preferred_element_type=F32)
    return slab * cos + partner * sin


def _slot_norm(slab, gain, n_real):
    ms = jnp.sum(slab * slab, axis=-1, keepdims=True) * (1.0 / n_real)
    return slab * lax.rsqrt(ms + EPS) * gain


def _mla_keys(ckvn, kr_blk, wukv_ref, gk, rope_c):
    kv = jnp.dot(ckvn.astype(BF16), wukv_ref[...], preferred_element_type=F32)
    ks = []
    for h in range(H_C):
        slab = kv[:, h * LANES:(h + 1) * LANES] + kr_blk
        slab = _slot_norm(slab, gk, QK_C)
        if rope_c is not None:
            slab = _rope(slab, *rope_c)
        ks.append(slab.astype(BF16))
    return jnp.concatenate(ks, axis=1), kv[:, H_C * LANES:].astype(BF16)


PREP_TM = 256


def _prep_kernel(qa_ref, cq_ref, ka_ref, ckv_ref, kr_ref, tab_ref,
                 gqa_ref, gka_ref, gcq_ref, gckv_ref, gqc_ref, gkc_ref, wuq_ref, wukv_ref, swap_ref,
                 qa_o, ka_o, qc_o, kc_o, vc_o, ckvn_o):
    def body(rotate):
        if rotate:
            tab = tab_ref[...]
            rope_a = (tab[:, 0:128], tab[:, 128:256], swap_ref[0])
            rope_c = (tab[:, 256:384], tab[:, 384:512], swap_ref[1])
            rot_a = lambda x: _rope(x, *rope_a)
            rot_c = lambda x: _rope(x, *rope_c)
        else:
            rope_c = None
            rot_a = rot_c = lambda x: x
        lane = lax.broadcasted_iota(jnp.int32, (1, LANES), 1)

        gqa = gqa_ref[...]
        qa = qa_ref[...]
        outs = []
        for h in range(HQ_A):
            slab = _slot_norm(qa[:, h * LANES:(h + 1) * LANES], gqa, HD_A)
            outs.append((rot_a(slab) * (HD_A ** -0.5)).astype(BF16))
        qa_o[...] = jnp.concatenate(outs, axis=1)

        ka = ka_ref[...]
        sq = ka * ka
        lo = lane < HD_A
        ms_lo = jnp.sum(jnp.where(lo, sq, 0.0), axis=-1, keepdims=True)
        ms_hi = jnp.sum(jnp.where(lo, 0.0, sq), axis=-1, keepdims=True)
        rs = jnp.where(lo, lax.rsqrt(ms_lo * (1.0 / HD_A) + EPS), lax.rsqrt(ms_hi * (1.0 / HD_A) + EPS))
        ka_o[...] = rot_a(ka * rs * gka_ref[...])

        cqn = _rms(cq_ref[...], gcq_ref[...])
        q = jnp.dot(cqn.astype(BF16), wuq_ref[...], preferred_element_type=F32)
        gqc = gqc_ref[...]
        outs = []
        for h in range(H_C):
            slab = _slot_norm(q[:, h * LANES:(h + 1) * LANES], gqc, QK_C)
            outs.append((rot_c(slab) * (QK_C ** -0.5)).astype(BF16))
        qc_o[...] = jnp.concatenate(outs, axis=1)

        ckvn = _rms(ckv_ref[...], gckv_ref[...])
        ckvn_o[...] = ckvn
        kc, vc = _mla_keys(ckvn, kr_ref[...], wukv_ref, gkc_ref[...], rope_c)
        kc_o[...] = kc
        vc_o[...] = vc

    is_latent = pl.program_id(0) >= N_CTX // PREP_TM
    pl.when(is_latent)(lambda: body(True))
    pl.when(jnp.logical_not(is_latent))(lambda: body(False))


def prep(proj, tab, gqa, gka, gcq, gckv, gqc, gkc, wuq, wukv):
    tm = PREP_TM
    n_ctx_tiles = N_CTX // tm
    per_batch = DEC_SEQ // tm

    def tab_map(i):
        return (jnp.where(i < n_ctx_tiles, 0, per_batch + (i - n_ctx_tiles) % per_batch), 0)

    def col(width, off):
        return pl.BlockSpec((tm, width), lambda i: (i, off // width))

    def full(shape):
        return pl.BlockSpec(shape, lambda i: (0,) * len(shape))

    def row_out(width):
        return pl.BlockSpec((tm, width), lambda i: (i, 0))

    return pl.pallas_call(
        _prep_kernel,
        out_shape=(
            jax.ShapeDtypeStruct((T_ALL, HQ_A * LANES), BF16),
            jax.ShapeDtypeStruct((T_ALL, LANES), F32),
            jax.ShapeDtypeStruct((T_ALL, H_C * LANES), BF16),
            jax.ShapeDtypeStruct((T_ALL, H_C * LANES), BF16),
            jax.ShapeDtypeStruct((T_ALL, H_C * V_C), BF16),
            jax.ShapeDtypeStruct((T_ALL, KV_LORA), F32),
        ),
        grid=(T_ALL // tm,),
        in_specs=[
            col(HQ_A * LANES, C_QA), col(Q_LORA, C_CQ), col(LANES, C_KA), col(LANES, C_CKV), col(LANES, C_KR),
            pl.BlockSpec((tm, 4 * LANES), tab_map),
            full((1, LANES)), full((1, LANES)), full((1, Q_LORA)), full((1, KV_LORA)),
            full((1, LANES)), full((1, LANES)),
            full((Q_LORA, H_C * LANES)), full((KV_LORA, H_C * LANES + H_C * V_C)), full((2, LANES, LANES)),
        ],
        out_specs=(row_out(HQ_A * LANES), row_out(LANES), row_out(H_C * LANES), row_out(H_C * LANES),
                   row_out(H_C * V_C), row_out(KV_LORA)),
        compiler_params=_cp(("arbitrary",)),
        name="prep",
    )(proj, proj, proj, proj, proj, tab, gqa, gka, gcq, gckv, gqc, gkc, wuq, wukv, _pair_swaps())


def _cachekeys_kernel(ckv_ref, kr_ref, gkc_ref, wukv_ref, kc_o, vc_o):
    kc, vc = _mla_keys(ckv_ref[...], kr_ref[...], wukv_ref, gkc_ref[...], None)
    kc_o[...] = kc
    vc_o[...] = vc


def cache_keys(ckv, kr_blk, gkc, wukv):
    r = ckv.shape[0]
    tm = 256
    return pl.pallas_call(
        _cachekeys_kernel,
        out_shape=(jax.ShapeDtypeStruct((r, H_C * LANES), BF16), jax.ShapeDtypeStruct((r, H_C * V_C), BF16)),
        grid=(r // tm,),
        in_specs=[
            pl.BlockSpec((tm, LANES), lambda i: (i, 0)),
            pl.BlockSpec((tm, LANES), lambda i: (i, 0)),
            pl.BlockSpec((1, LANES), lambda i: (0, 0)),
            pl.BlockSpec((KV_LORA, H_C * LANES + H_C * V_C), lambda i: (0, 0)),
        ],
        out_specs=(pl.BlockSpec((tm, H_C * LANES), lambda i: (i, 0)), pl.BlockSpec((tm, H_C * V_C), lambda i: (i, 0))),
        compiler_params=_cp(("arbitrary",)),
        name="cache_keys",
    )(ckv, kr_blk, gkc, wukv)


def _attn_body(q_ref, segs, sink_ref, o_ref, *, n_heads, k_slot, v_slab, v_half, tq, band_qi=None):
    outs = []
    for h in range(n_heads):
        qh = q_ref[:, h * LANES:(h + 1) * LANES]
        scores = []
        for k_ref, _, off in segs:
            kh = k_ref[:, k_slot(h) * LANES:(k_slot(h) + 1) * LANES].astype(BF16)
            s = lax.dot_general(qh, kh, (((1,), (1,)), ((), ())), preferred_element_type=F32)
            if off is not None:
                tk = s.shape[1]
                blk = band_qi + off
                q_pos = band_qi * tq + lax.broadcasted_iota(jnp.int32, (tq, tk), 0)
                k_pos = blk * tk + lax.broadcasted_iota(jnp.int32, (tq, tk), 1)
                ok = (jnp.abs(k_pos - q_pos) <= WINDOW) & (blk >= 0) & (blk < DEC_SEQ // tk)
                s = jnp.where(ok, s, NEG_INF)
            scores.append(s)
        m = scores[0].max(axis=-1, keepdims=True)
        for s in scores[1:]:
            m = jnp.maximum(m, s.max(axis=-1, keepdims=True))
        if sink_ref is not None:
            sink = sink_ref[h:h + 1, 0:1]
            m = jnp.maximum(m, sink)
            denom = jnp.exp(sink - m)
        else:
            denom = jnp.zeros_like(m)
        acc = None
        for s, (_, v_ref, _) in zip(scores, segs):
            p = jnp.exp(s - m)
            denom = denom + p.sum(axis=-1, keepdims=True)
            vs = v_ref[:, v_slab(h) * LANES:(v_slab(h) + 1) * LANES].astype(BF16)
            pv = jnp.dot(p.astype(BF16), vs, preferred_element_type=F32)
            acc = pv if acc is None else acc + pv
        half = v_half(h)
        outs.append(acc[:, half * 64:(half + 1) * 64] * (1.0 / denom))
    o_ref[...] = jnp.concatenate(outs, axis=1).astype(o_ref.dtype)


_C_CFG = dict(n_heads=H_C, k_slot=lambda h: h, v_slab=lambda h: h // 2, v_half=lambda h: h % 2)


def _attn_a_body(q_ref, segs, sink_ref, o_ref, *, tq, band_qi=None):
    rows = HQ_A * tq
    q = jnp.concatenate([q_ref[:, h * LANES:(h + 1) * LANES] for h in range(HQ_A)], axis=0)
    sink = jnp.concatenate([jnp.broadcast_to(sink_ref[h:h + 1, 0:1], (tq, 1)) for h in range(HQ_A)], axis=0)
    scores = []
    for k_ref, _, off in segs:
        s = lax.dot_general(q, k_ref[...].astype(BF16), (((1,), (1,)), ((), ())), preferred_element_type=F32)
        if off is not None:
            tk = s.shape[1]
            blk = band_qi + off
            q_pos = band_qi * tq + (lax.broadcasted_iota(jnp.int32, (rows, tk), 0) & (tq - 1))
            k_pos = blk * tk + lax.broadcasted_iota(jnp.int32, (rows, tk), 1)
            ok = (jnp.abs(k_pos - q_pos) <= WINDOW) & (blk >= 0) & (blk < DEC_SEQ // tk)
            s = jnp.where(ok, s, NEG_INF)
        scores.append(s)
    m = sink
    for s in scores:
        m = jnp.maximum(m, s.max(axis=-1, keepdims=True))
    denom = jnp.exp(sink - m)
    acc = None
    for s, (_, v_ref, _) in zip(scores, segs):
        p = jnp.exp(s - m)
        denom = denom + p.sum(axis=-1, keepdims=True)
        pv = jnp.dot(p.astype(BF16), v_ref[...].astype(BF16), preferred_element_type=F32)
        acc = pv if acc is None else acc + pv
    acc = acc * (1.0 / denom)
    outs = []
    for h in range(HQ_A):
        kv = h // G_A
        outs.append(acc[h * tq:(h + 1) * tq, kv * HD_A:(kv + 1) * HD_A])
    o_ref[...] = jnp.concatenate(outs, axis=1).astype(o_ref.dtype)


def _attn_a_ctx_kernel(q_ref, k_ref, v_ref, sink_ref, o_init, o_ref):
    del o_init
    _attn_a_body(q_ref, [(k_ref, v_ref, None)], sink_ref, o_ref, tq=SEQ)


def _attn_a_lat_kernel(q_ref, k0, k1, k2, v0, v1, v2, kc_ref, vc_ref, sink_ref, o_ctx, o_ref):
    del o_ctx
    qi = pl.program_id(1)
    segs = [(k0, v0, -1), (k1, v1, 0), (k2, v2, 1), (kc_ref, vc_ref, None)]
    _attn_a_body(q_ref, segs, sink_ref, o_ref, tq=BAND, band_qi=qi)


def _attn_c_ctx_kernel(q_ref, k_ref, v_ref, o_init, o_ref):
    del o_init
    _attn_body(q_ref, [(k_ref, v_ref, None)], None, o_ref, tq=SEQ, **_C_CFG)


def _attn_c_lat_kernel(q_ref, k_ref, v_ref, kc_ref, vc_ref, o_ctx, o_ref):
    del o_ctx
    _attn_body(q_ref, [(k_ref, v_ref, None), (kc_ref, vc_ref, None)], None, o_ref, tq=256, **_C_CFG)


def attn_a_ctx(qa, ka, proj, sink):
    return pl.pallas_call(
        _attn_a_ctx_kernel,
        out_shape=jax.ShapeDtypeStruct((T_ALL, HQ_A * HD_A), BF16),
        grid=(BATCH,),
        in_specs=[
            pl.BlockSpec((SEQ, HQ_A * LANES), lambda b: (b, 0)),
            pl.BlockSpec((SEQ, LANES), lambda b: (b, 0)),
            pl.BlockSpec((SEQ, LANES), lambda b: (b, C_VA // LANES)),
            pl.BlockSpec((SUBLANES, LANES), lambda b: (0, 0)),
            pl.BlockSpec(memory_space=pl.ANY),
        ],
        out_specs=pl.BlockSpec((SEQ, HQ_A * HD_A), lambda b: (b, 0)),
        input_output_aliases={4: 0},
        compiler_params=_cp(("arbitrary",)),
        name="attn_a_ctx",
    )(qa, ka, proj, sink, jnp.zeros((T_ALL, HQ_A * HD_A), BF16))


def attn_a_lat(qa, ka, proj, k_cache, v_cache, sink, o_all):
    nb = DEC_SEQ // BAND
    base = N_CTX // BAND

    def band(off, colblk):
        return pl.BlockSpec((BAND, LANES), lambda b, i: (base + b * nb + jnp.clip(i + off, 0, nb - 1), colblk))

    return pl.pallas_call(
        _attn_a_lat_kernel,
        out_shape=jax.ShapeDtypeStruct((T_ALL, HQ_A * HD_A), BF16),
        grid=(DEC_BATCH, nb),
        in_specs=[
            pl.BlockSpec((BAND, HQ_A * LANES), lambda b, i: (base + b * nb + i, 0)),
            band(-1, 0), band(0, 0), band(1, 0),
            band(-1, C_VA // LANES), band(0, C_VA // LANES), band(1, C_VA // LANES),
            pl.BlockSpec((None, PAST_LEN, LANES), lambda b, i: (b, 0, 0)),
            pl.BlockSpec((None, PAST_LEN, LANES), lambda b, i: (b, 0, 0)),
            pl.BlockSpec((SUBLANES, LANES), lambda b, i: (0, 0)),
            pl.BlockSpec(memory_space=pl.ANY),
        ],
        out_specs=pl.BlockSpec((BAND, HQ_A * HD_A), lambda b, i: (base + b * nb + i, 0)),
        input_output_aliases={10: 0},
        compiler_params=_cp(("arbitrary", "arbitrary")),
        name="attn_a_lat",
    )(qa, ka, ka, ka, proj, proj, proj, k_cache, v_cache, sink, o_all)


def attn_c_ctx(qc, kc, vc):
    return pl.pallas_call(
        _attn_c_ctx_kernel,
        out_shape=jax.ShapeDtypeStruct((T_ALL, H_C * V_C), BF16),
        grid=(BATCH,),
        in_specs=[
            pl.BlockSpec((SEQ, H_C * LANES), lambda b: (b, 0)),
            pl.BlockSpec((SEQ, H_C * LANES), lambda b: (b, 0)),
            pl.BlockSpec((SEQ, H_C * V_C), lambda b: (b, 0)),
            pl.BlockSpec(memory_space=pl.ANY),
        ],
        out_specs=pl.BlockSpec((SEQ, H_C * V_C), lambda b: (b, 0)),
        input_output_aliases={3: 0},
        compiler_params=_cp(("arbitrary",)),
        name="attn_c_ctx",
    )(qc, kc, vc, jnp.zeros((T_ALL, H_C * V_C), BF16))


def attn_c_lat(qc, kc, vc, kc_cache, vc_cache, o_all):
    tq = 256
    nq = DEC_SEQ // tq
    qbase = N_CTX // tq
    kbase = N_CTX // DEC_SEQ
    return pl.pallas_call(
        _attn_c_lat_kernel,
        out_shape=jax.ShapeDtypeStruct((T_ALL, H_C * V_C), BF16),
        grid=(DEC_BATCH, nq),
        in_specs=[
            pl.BlockSpec((tq, H_C * LANES), lambda b, i: (qbase + b * nq + i, 0)),
            pl.BlockSpec((DEC_SEQ, H_C * LANES), lambda b, i: (kbase + b, 0)),
            pl.BlockSpec((DEC_SEQ, H_C * V_C), lambda b, i: (kbase + b, 0)),
            pl.BlockSpec((PAST_LEN, H_C * LANES), lambda b, i: (b, 0)),
            pl.BlockSpec((PAST_LEN, H_C * V_C), lambda b, i: (b, 0)),
            pl.BlockSpec(memory_space=pl.ANY),
        ],
        out_specs=pl.BlockSpec((tq, H_C * V_C), lambda b, i: (qbase + b * nq + i, 0)),
        input_output_aliases={5: 0},
        compiler_params=_cp(("arbitrary", "arbitrary")),
        name="attn_c_lat",
    )(qc, kc, vc, kc_cache, vc_cache, o_all)


S5_TC = 256
S5_CH = 512
S5_UNROLL = 8
TOK_CHUNKS = T_ALL // S5_TC
CHUNK_GROUP = 8
assert SEQ == S5_TC and DEC_SEQ == CHUNK_GROUP * S5_TC and BATCH == 4 * CHUNK_GROUP


def _s5_bproj(u2, wb_ref, pick_first):
    def part(lo, hi):
        r = jnp.dot(u2, wb_ref[0, :, lo:hi], preferred_element_type=F32)
        if pick_first is not None:
            r = jnp.where(pick_first, r, jnp.dot(u2, wb_ref[1, :, lo:hi], preferred_element_type=F32))
        return r.reshape(S5_TC, SUBLANES, S5_CH)
    return part(0, S5_CH), part(S5_CH, 2 * S5_CH)


def _s5_cproj(xre, xim, wc_ref, pick_first):
    rows = S5_TC * SUBLANES
    xr = xre[...].reshape(rows, S5_CH).astype(BF16)
    xi = xim[...].reshape(rows, S5_CH).astype(BF16)

    def part(d):
        return (jnp.dot(xr, wc_ref[d, 0:S5_CH, :], preferred_element_type=F32)
                + jnp.dot(xi, wc_ref[d, S5_CH:, :], preferred_element_type=F32))

    y = part(0)
    if pick_first is not None:
        y = jnp.where(pick_first, y, part(1))
    return jnp.swapaxes(y.reshape(S5_TC, SUBLANES, LANES), 0, 1)


def _s5_ctx_kernel(u_ref, wb_ref, wc_ref, are_ref, aim_ref, y_init, y_ref, fre_ref, fim_ref, bre, bim):
    del y_init
    d = pl.program_id(2)
    rows = S5_TC * SUBLANES
    u2 = jnp.swapaxes(u_ref[...], 0, 1).reshape(rows, LANES).astype(BF16)
    b_re, b_im = _s5_bproj(u2, wb_ref, None)
    bre[...] = b_re
    bim[...] = b_im
    a_re = are_ref[...]
    a_im = aim_ref[...]

    def outer(i, carry):
        sr, si = carry
        for j in range(S5_UNROLL):
            t0 = i * S5_UNROLL + j
            t = jnp.where(d == 1, S5_TC - 1 - t0, t0)
            nr = a_re * sr - a_im * si + bre[t]
            ni = a_re * si + a_im * sr + bim[t]
            bre[t] = nr
            bim[t] = ni
            sr, si = nr, ni
        return sr, si

    zero = jnp.zeros((SUBLANES, S5_CH), F32)
    sr, si = lax.fori_loop(0, S5_TC // S5_UNROLL, outer, (zero, zero))
    fre_ref[...] = sr
    fim_ref[...] = si
    y = _s5_cproj(bre, bim, wc_ref, None)

    @pl.when(d == 0)
    def _():
        y_ref[...] = y

    @pl.when(d == 1)
    def _():
        y_ref[...] += y


def s5_ctx(proj4, wb, wc, a_re, a_im):
    nj = D_B // LANES
    ucol = C_UB // LANES
    return pl.pallas_call(
        _s5_ctx_kernel,
        out_shape=(
            jax.ShapeDtypeStruct((TOK_CHUNKS // CHUNK_GROUP, CHUNK_GROUP, S5_TC, D_B), F32),
            jax.ShapeDtypeStruct((2, BATCH, G_B * P_B), F32),
            jax.ShapeDtypeStruct((2, BATCH, G_B * P_B), F32),
        ),
        grid=(BATCH // CHUNK_GROUP, nj, 2),
        in_specs=[
            pl.BlockSpec((None, CHUNK_GROUP, S5_TC, LANES), lambda sb, j, d: (sb, 0, 0, ucol + j)),
            pl.BlockSpec((1, None, LANES, 2 * S5_CH), lambda sb, j, d: (d, j, 0, 0)),
            pl.BlockSpec((1, None, 2 * S5_CH, LANES), lambda sb, j, d: (d, j, 0, 0)),
            pl.BlockSpec((None, SUBLANES, S5_CH), lambda sb, j, d: (d, 0, j)),
            pl.BlockSpec((None, SUBLANES, S5_CH), lambda sb, j, d: (d, 0, j)),
            pl.BlockSpec(memory_space=pl.ANY),
        ],
        out_specs=(
            pl.BlockSpec((None, CHUNK_GROUP, S5_TC, LANES), lambda sb, j, d: (sb, 0, 0, j)),
            pl.BlockSpec((None, SUBLANES, S5_CH), lambda sb, j, d: (d, sb, j)),
            pl.BlockSpec((None, SUBLANES, S5_CH), lambda sb, j, d: (d, sb, j)),
        ),
        scratch_shapes=[pltpu.VMEM((S5_TC, SUBLANES, S5_CH), F32), pltpu.VMEM((S5_TC, SUBLANES, S5_CH), F32)],
        input_output_aliases={5: 0},
        compiler_params=_cp(("arbitrary", "arbitrary", "arbitrary")),
        name="s5_ctx",
    )(proj4, wb, wc, a_re, a_im, jnp.zeros((TOK_CHUNKS // CHUNK_GROUP, CHUNK_GROUP, S5_TC, D_B), F32))


def _s5_lat_kernel(uf_ref, ub_ref, wb_ref, wc_ref, are_ref, aim_ref, h0re_ref, h0im_ref, y_ctx,
                   yf_ref, yb_ref, bre, bim, xre, xim, sre, sim):
    del y_ctx
    k = pl.program_id(1)

    @pl.when(k == 0)
    def _():
        sre[...] = h0re_ref[...]
        sim[...] = h0im_ref[...]

    rows = S5_TC * SUBLANES
    half = SUBLANES // 2
    u8 = jnp.concatenate([uf_ref[...], ub_ref[...]], axis=0)
    u2 = jnp.swapaxes(u8, 0, 1).reshape(rows, LANES).astype(BF16)
    fwd_rows = (lax.broadcasted_iota(jnp.int32, (rows, 1), 0) % SUBLANES) < half
    b_re, b_im = _s5_bproj(u2, wb_ref, fwd_rows)
    bre[...] = b_re
    bim[...] = b_im
    a_re = are_ref[...]
    a_im = aim_ref[...]
    fwd8 = lax.broadcasted_iota(jnp.int32, (SUBLANES, 1), 0) < half

    def outer(i, carry):
        sr, si = carry
        for j in range(S5_UNROLL):
            t = i * S5_UNROLL + j
            tb = S5_TC - 1 - t
            nr = a_re * sr - a_im * si + jnp.where(fwd8, bre[t], bre[tb])
            ni = a_re * si + a_im * sr + jnp.where(fwd8, bim[t], bim[tb])
            xre[t, 0:half, :] = nr[0:half]
            xre[tb, half:, :] = nr[half:]
            xim[t, 0:half, :] = ni[0:half]
            xim[tb, half:, :] = ni[half:]
            sr, si = nr, ni
        return sr, si

    sr, si = lax.fori_loop(0, S5_TC // S5_UNROLL, outer, (sre[...], sim[...]))
    sre[...] = sr
    sim[...] = si
    y = _s5_cproj(xre, xim, wc_ref, fwd_rows)
    yf_ref[...] = y[0:half]
    yb_ref[...] = y[half:]


def s5_lat(proj4, y_all, wb, wc, a_re, a_im, h0_re, h0_im):
    nj = D_B // LANES
    nk = DEC_SEQ // S5_TC
    ucol = C_UB // LANES
    lat = lambda col0, rev: pl.BlockSpec(
        (DEC_BATCH, None, S5_TC, LANES), lambda j, k: (1, (nk - 1 - k) if rev else k, 0, col0 + j))
    vec = lambda: pl.BlockSpec((SUBLANES, S5_CH), lambda j, k: (0, j))
    buf = lambda: pltpu.VMEM((S5_TC, SUBLANES, S5_CH), F32)
    return pl.pallas_call(
        _s5_lat_kernel,
        out_shape=(
            jax.ShapeDtypeStruct(y_all.shape, F32),
            jax.ShapeDtypeStruct((DEC_BATCH, nk, S5_TC, D_B), F32),
        ),
        grid=(nj, nk),
        in_specs=[
            lat(ucol, False), lat(ucol, True),
            pl.BlockSpec((2, None, LANES, 2 * S5_CH), lambda j, k: (0, j, 0, 0)),
            pl.BlockSpec((2, None, 2 * S5_CH, LANES), lambda j, k: (0, j, 0, 0)),
            vec(), vec(), vec(), vec(),
            pl.BlockSpec(memory_space=pl.ANY),
        ],
        out_specs=(
            lat(0, False),
            pl.BlockSpec((DEC_BATCH, None, S5_TC, LANES), lambda j, k: (0, nk - 1 - k, 0, j)),
        ),
        scratch_shapes=[buf(), buf(), buf(), buf(),
                        pltpu.VMEM((SUBLANES, S5_CH), F32), pltpu.VMEM((SUBLANES, S5_CH), F32)],
        input_output_aliases={8: 0},
        compiler_params=_cp(("arbitrary", "arbitrary")),
        name="s5_lat",
    )(proj4, proj4, wb, wc, a_re, a_im, h0_re, h0_im, y_all)


S5FIN_TM = 512


def _s5fin_kernel(u_ref, y_ref, yb_ref, d_ref, w_ref, b_ref, o_ref):
    def body(latent):
        y = d_ref[...] * u_ref[...] + y_ref[...]
        if latent:
            y = y + yb_ref[...]
        y = 0.5 * y * (1.0 + jnp.tanh(math.sqrt(2.0 / math.pi) * (y + 0.044715 * (y * y * y))))
        z = jnp.dot(y.astype(BF16), w_ref[...], preferred_element_type=F32) + b_ref[...]
        o_ref[...] = (y * _sigmoid(z)).astype(o_ref.dtype)

    is_latent = pl.program_id(0) >= N_CTX // S5FIN_TM
    pl.when(is_latent)(lambda: body(True))
    pl.when(jnp.logical_not(is_latent))(lambda: body(False))


def s5_finish(proj, y, y_bwd_lat, d, w_glu, b_glu):
    tm = S5FIN_TM
    n_ctx_tiles = N_CTX // tm
    return pl.pallas_call(
        _s5fin_kernel,
        out_shape=jax.ShapeDtypeStruct((T_ALL, D_B), BF16),
        grid=(T_ALL // tm,),
        in_specs=[
            pl.BlockSpec((tm, D_B), lambda i: (i, C_UB // D_B)),
            pl.BlockSpec((tm, D_B), lambda i: (i, 0)),
            pl.BlockSpec((tm, D_B), lambda i: (jnp.maximum(i - n_ctx_tiles, 0), 0)),
            pl.BlockSpec((1, D_B), lambda i: (0, 0)),
            pl.BlockSpec((D_B, D_B), lambda i: (0, 0)),
            pl.BlockSpec((1, D_B), lambda i: (0, 0)),
        ],
        out_specs=pl.BlockSpec((tm, D_B), lambda i: (i, 0)),
        compiler_params=_cp(("arbitrary",)),
        name="s5_finish",
    )(proj, y, y_bwd_lat, d, w_glu, b_glu)


def _merge_kernel(x_ref, gl_ref, oa_ref, ob_ref, oc_ref, mod_ref, g_ref, wa_ref, wb_ref, wc_ref, wo_ref,
                  x1_ref, h2_ref, h2row_ref):
    m = None
    for br, (o_ref, w_ref) in enumerate(((oa_ref, wa_ref), (ob_ref, wb_ref), (oc_ref, wc_ref))):
        gate = _sigmoid(gl_ref[:, br * D_MODEL:(br + 1) * D_MODEL])
        t = gate * jnp.dot(o_ref[...], w_ref[...], preferred_element_type=F32)
        m = t if m is None else m + t
    x1 = x_ref[...] + mod_ref[2:3, :] * jnp.dot(m.astype(BF16), wo_ref[...], preferred_element_type=F32)
    x1_ref[...] = x1
    h2 = _rms(x1, g_ref[...])
    h2 = h2 * (1.0 + mod_ref[4:5, :]) + mod_ref[3:4, :]
    h2_ref[...] = h2
    h2row_ref[...] = h2.reshape((h2.shape[0],) + ROW3)


def merge(x, proj, oa, ob, oc, mod, g, wa, wb, wc, wo):
    tm = 256
    full = lambda shape: pl.BlockSpec(shape, lambda i: (0,) * len(shape))
    return pl.pallas_call(
        _merge_kernel,
        out_shape=(jax.ShapeDtypeStruct((T_ALL, D_MODEL), F32), jax.ShapeDtypeStruct((T_ALL, D_MODEL), F32),
                   jax.ShapeDtypeStruct((T_ALL,) + ROW3, F32)),
        grid=(T_ALL // tm,),
        in_specs=[
            pl.BlockSpec((tm, D_MODEL), lambda i: (i, 0)),
            pl.BlockSpec((tm, 3 * D_MODEL), lambda i: (i, 0)),
            pl.BlockSpec((tm, 512), lambda i: (i, 0)),
            pl.BlockSpec((tm, 512), lambda i: (i, 0)),
            pl.BlockSpec((tm, 512), lambda i: (i, 0)),
            pl.BlockSpec((None, 6, D_MODEL), lambda i: (_group_of_tile(i, tm), 0, 0)),
            full((1, D_MODEL)),
            full((512, D_MODEL)), full((512, D_MODEL)), full((512, D_MODEL)), full((D_MODEL, D_MODEL)),
        ],
        out_specs=(pl.BlockSpec((tm, D_MODEL), lambda i: (i, 0)), pl.BlockSpec((tm, D_MODEL), lambda i: (i, 0)),
                   pl.BlockSpec((tm,) + ROW3, lambda i: (i, 0, 0))),
        compiler_params=_cp(("arbitrary",)),
        name="merge",
    )(x, proj, oa, ob, oc, mod, g, wa, wb, wc, wo)


ROUTE_TM = 256


def _router_kernel(h_ref, w_ref, b_ref, idx_ref, rank_ref, wgt_ref, cnt_ref, cnt_acc):
    i = pl.program_id(0)

    @pl.when(i == 0)
    def _():
        cnt_acc[...] = jnp.zeros_like(cnt_acc)

    h = h_ref[...]
    w = w_ref[...]
    h_hi = h.astype(BF16)
    h_lo = (h - h_hi.astype(F32)).astype(BF16)
    w_hi = w.astype(BF16)
    w_lo = (w - w_hi.astype(F32)).astype(BF16)
    dot = functools.partial(jnp.dot, preferred_element_type=F32)
    logits = dot(h_hi, w_hi) + (dot(h_hi, w_lo) + dot(h_lo, w_hi)) + b_ref[...]
    lane_i = lax.broadcasted_iota(jnp.int32, (ROUTE_TM, LANES), 1)
    lane = lane_i.astype(F32)
    r_i = lax.broadcasted_iota(jnp.int32, (ROUTE_TM, ROUTE_TM), 0)
    c_i = lax.broadcasted_iota(jnp.int32, (ROUTE_TM, ROUTE_TM), 1)
    earlier = jnp.where(c_i < r_i, 1.0, 0.0).astype(BF16)

    cnt = cnt_acc[...]
    idx_out = jnp.zeros((ROUTE_TM, LANES), F32)
    rank_out = jnp.zeros((ROUTE_TM, LANES), F32)
    val_out = jnp.zeros((ROUTE_TM, LANES), F32)
    v0 = None
    esum = None
    for k in range(TOP_K):
        m = logits.max(axis=-1, keepdims=True)
        sel = jnp.min(jnp.where(logits == m, lane, float(LANES)), axis=-1, keepdims=True)
        hit = lane == sel
        logits = jnp.where(hit, -jnp.inf, logits)
        onehot = jnp.where(hit, 1.0, 0.0)
        within = jnp.dot(earlier, onehot.astype(BF16), preferred_element_type=F32)
        rank = jnp.sum(onehot * (within + cnt), axis=-1, keepdims=True)
        cnt = cnt + jnp.sum(onehot, axis=0, keepdims=True)
        if k == 0:
            v0 = m
        e = jnp.exp(m - v0)
        esum = e if esum is None else esum + e
        idx_out = jnp.where(lane_i == k, sel, idx_out)
        rank_out = jnp.where(lane_i == k, rank, rank_out)
        val_out = jnp.where(lane_i == k, e, val_out)
    cnt_acc[...] = cnt
    cnt_ref[...] = cnt
    idx_ref[...] = idx_out.astype(jnp.int32)
    rank_ref[...] = rank_out.astype(jnp.int32)
    wgt_ref[...] = val_out * (1.0 / esum)


def router(h2, w_router, b_router):
    tile = lambda: pl.BlockSpec((ROUTE_TM, LANES), lambda i: (i, 0))
    return pl.pallas_call(
        _router_kernel,
        out_shape=(
            jax.ShapeDtypeStruct((T_ALL, LANES), jnp.int32),
            jax.ShapeDtypeStruct((T_ALL, LANES), jnp.int32),
            jax.ShapeDtypeStruct((T_ALL, LANES), F32),
            jax.ShapeDtypeStruct((1, LANES), F32),
        ),
        grid=(T_ALL // ROUTE_TM,),
        in_specs=[
            pl.BlockSpec((ROUTE_TM, D_MODEL), lambda i: (i, 0)),
            pl.BlockSpec((D_MODEL, LANES), lambda i: (0, 0)),
            pl.BlockSpec((1, LANES), lambda i: (0, 0)),
        ],
        out_specs=(tile(), tile(), tile(), pl.BlockSpec((1, LANES), lambda i: (0, 0))),
        scratch_shapes=[pltpu.VMEM((1, LANES), F32)],
        compiler_params=_cp(("arbitrary",)),
        name="router",
    )(h2, w_router, b_router)


PLAN_UNROLL = 8
N_DUMP = 5 * BLK
PLAN_LEN = BUF_LEN + BLK
assert TOP_K == 4 and T_ALL & (T_ALL - 1) == 0


def _invert_kernel(dest_ref, fill_hbm, inv_ref):
    pltpu.sync_copy(fill_hbm, inv_ref)
    tok_step = PLAN_UNROLL // TOP_K

    def put(i, c):
        for j in range(PLAN_UNROLL):
            inv_ref[dest_ref[i * PLAN_UNROLL + j]] = i * tok_step + ((j % TOP_K) * T_ALL + j // TOP_K)
        return c

    lax.fori_loop(0, N_ASSIGN // PLAN_UNROLL, put, 0)


def invert_plan(dest):
    r = jnp.arange(PLAN_LEN, dtype=jnp.int32)
    fill = N_ASSIGN + jnp.where(r < BLK, 2 * BLK + r, (r - BLK) & (2 * BLK - 1))
    return pl.pallas_call(
        _invert_kernel,
        out_shape=jax.ShapeDtypeStruct((PLAN_LEN,), jnp.int32),
        in_specs=[pl.BlockSpec(memory_space=pltpu.SMEM), pl.BlockSpec(memory_space=pl.ANY)],
        out_specs=pl.BlockSpec(memory_space=pltpu.SMEM),
        name="invert_plan",
    )(dest, fill)


PAIR_TILE = 2 * LANES


def _pair_selection():
    r = lax.broadcasted_iota(jnp.int32, (PAIR_TILE, PAIR_TILE), 0)
    c = lax.broadcasted_iota(jnp.int32, (PAIR_TILE, PAIR_TILE), 1)
    return (r == jnp.where(c < LANES, 2 * c, 2 * (c - LANES) + 1)).astype(BF16)


EXP_NBUF = 3


def _expert_kernel(be_ref, first_ref, par_ref, nxt_ref, inv_ref, h_hbm, wu_hbm, bu_ref, wd_hbm, bd_ref, sel_ref,
                   comb_hbm, xbuf, ybuf, wu_ref, wd_ref, wu_st, wd_st, gsem, ssem, wsem, *, layer):
    i = pl.program_id(0)
    n = pl.num_programs(0)
    cur = i % EXP_NBUF
    nxt = (i + 2) % EXP_NBUF
    prv = nxt

    def gather(block, s):
        base = (block + 1) * BLK
        for r in range(BLK):
            tok = inv_ref[base + r] & (T_ALL - 1)
            pltpu.make_async_copy(h_hbm.at[tok], xbuf.at[s, r], gsem.at[s]).start()

    def scatter(block, s):
        base = (block + 1) * BLK
        for r in range(BLK):
            a = inv_ref[base + r]
            pltpu.make_async_copy(ybuf.at[s, r], comb_hbm.at[a], ssem.at[s]).start()

    def wait_block(sem):
        pltpu.make_async_copy(h_hbm.at[pl.ds(0, BLK)], xbuf.at[0], sem).wait()

    def weight_copies(e, p):
        return (pltpu.make_async_copy(wu_hbm.at[layer, e], wu_st.at[p], wsem.at[p]),
                pltpu.make_async_copy(wd_hbm.at[layer, e], wd_st.at[p], wsem.at[p]))

    @pl.when(i == 0)
    def _():
        for cp in weight_copies(be_ref[0], 0):
            cp.start()
        ybuf[...] = jnp.zeros_like(ybuf)
        for s in range(EXP_NBUF - 1):
            for r in range(BLK):
                dump = N_ASSIGN + (3 + s) * BLK + r
                pltpu.make_async_copy(ybuf.at[s, r], comb_hbm.at[dump], ssem.at[s]).start()
        gather(0, 0)
        gather(1, 1)

    wait_block(gsem.at[cur])
    wait_block(ssem.at[cur])

    @pl.when(first_ref[i] == 1)
    def _():
        p = par_ref[i]
        for cp in weight_copies(be_ref[i], p):
            cp.wait()

        @pl.when(nxt_ref[i] >= 0)
        def _():
            for cp in weight_copies(nxt_ref[i], 1 - p):
                cp.start()

        sel = sel_ref[...]
        for j in range(2 * D_FF // PAIR_TILE):
            cols = slice(j * PAIR_TILE, (j + 1) * PAIR_TILE)
            wu_ref[:, cols] = jnp.dot(wu_st[p, :, cols].astype(BF16), sel, preferred_element_type=F32).astype(BF16)
        wd_ref[...] = wd_st[p].astype(BF16)

    gather(jnp.minimum(i + 2, n - 1), nxt)
    scatter(i - 1, prv)

    x = xbuf[cur].reshape(BLK, D_MODEL).astype(BF16)
    h = jnp.dot(x, wu_ref[...], preferred_element_type=F32) + bu_ref[...]
    acts = []
    for j in range(2 * D_FF // PAIR_TILE):
        glu = jnp.minimum(h[:, j * PAIR_TILE:j * PAIR_TILE + LANES], SWIGLU_LIMIT)
        lin = jnp.clip(h[:, j * PAIR_TILE + LANES:(j + 1) * PAIR_TILE], -SWIGLU_LIMIT, SWIGLU_LIMIT)
        acts.append((glu * _sigmoid(SWIGLU_ALPHA * glu) * (lin + 1.0)).astype(BF16))
    act = jnp.concatenate(acts, axis=1)
    y = jnp.dot(act, wd_ref[...], preferred_element_type=F32) + bd_ref[...]
    ybuf[cur] = y.reshape((BLK,) + ROW3)

    @pl.when(i == n - 1)
    def _():
        for s in range(EXP_NBUF):
            @pl.when(s != cur)
            def _():
                wait_block(gsem.at[s])
                wait_block(ssem.at[s])
        scatter(i, cur)
        wait_block(ssem.at[cur])


def experts(layer, block_e, inv, h2row, wu, bu, wd, bd):
    first = jnp.concatenate([jnp.ones((1,), jnp.int32), (block_e[1:] != block_e[:-1]).astype(jnp.int32)])
    slot = (jnp.cumsum(first) - 1) & 1
    later = jnp.where(block_e[None, :] > block_e[:, None], block_e[None, :], N_EXP).min(axis=1)
    nxt = jnp.where(later == N_EXP, -1, later).astype(jnp.int32)
    per_expert = lambda *blk: pl.BlockSpec((None, None) + blk, lambda i, be, *_: (layer, be[i], 0, 0))
    return pl.pallas_call(
        functools.partial(_expert_kernel, layer=layer),
        out_shape=jax.ShapeDtypeStruct((N_ASSIGN + N_DUMP,) + ROW3, F32),
        grid_spec=pltpu.PrefetchScalarGridSpec(
            num_scalar_prefetch=5,
            grid=(N_BLOCKS,),
            in_specs=[
                pl.BlockSpec(memory_space=pl.ANY),
                pl.BlockSpec(memory_space=pl.ANY), per_expert(1, 2 * D_FF),
                pl.BlockSpec(memory_space=pl.ANY), per_expert(1, D_MODEL),
                pl.BlockSpec((PAIR_TILE, PAIR_TILE), lambda i, *_: (0, 0)),
            ],
            out_specs=pl.BlockSpec(memory_space=pl.ANY),
            scratch_shapes=[
                pltpu.VMEM((EXP_NBUF, BLK) + ROW3, F32), pltpu.VMEM((EXP_NBUF, BLK) + ROW3, F32),
                pltpu.VMEM((D_MODEL, 2 * D_FF), BF16), pltpu.VMEM((D_FF, D_MODEL), BF16),
                pltpu.VMEM((2, D_MODEL, 2 * D_FF), F32), pltpu.VMEM((2, D_FF, D_MODEL), F32),
                pltpu.SemaphoreType.DMA((EXP_NBUF,)), pltpu.SemaphoreType.DMA((EXP_NBUF,)),
                pltpu.SemaphoreType.DMA((2,)),
            ],
        ),
        compiler_params=_cp(("arbitrary",)),
        name="experts",
    )(block_e, first, slot.astype(jnp.int32), nxt, inv, h2row, wu, bu, wd, bd, _pair_selection())


COMB_TM = 256


def _combine_kernel(c0_ref, c1_ref, c2_ref, c3_ref, x_ref, w_ref, mod_ref, o_ref):
    w = w_ref[...]
    acc = None
    for k, c_ref in enumerate((c0_ref, c1_ref, c2_ref, c3_ref)):
        t = w[:, k:k + 1] * c_ref[...].reshape(COMB_TM, D_MODEL)
        acc = t if acc is None else acc + t
    o_ref[...] = x_ref[...] + mod_ref[5:6, :] * acc


def combine(comb, x1, wgt, mod):
    tiles = T_ALL // COMB_TM
    kth = lambda k: pl.BlockSpec((COMB_TM,) + ROW3, lambda i: (k * tiles + i, 0, 0))
    return pl.pallas_call(
        _combine_kernel,
        out_shape=jax.ShapeDtypeStruct((T_ALL, D_MODEL), F32),
        grid=(tiles,),
        in_specs=[
            kth(0), kth(1), kth(2), kth(3),
            pl.BlockSpec((COMB_TM, D_MODEL), lambda i: (i, 0)),
            pl.BlockSpec((COMB_TM, LANES), lambda i: (i, 0)),
            pl.BlockSpec((None, 6, D_MODEL), lambda i: (_group_of_tile(i, COMB_TM), 0, 0)),
        ],
        out_specs=pl.BlockSpec((COMB_TM, D_MODEL), lambda i: (i, 0)),
        compiler_params=_cp(("arbitrary",)),
        name="combine",
    )(comb, comb, comb, comb, x1, wgt, mod)


def _rope_table():
    pos = jnp.arange(DEC_SEQ)
    row = (pos // GRID_W).astype(F32)[:, None]
    col = (pos % GRID_W).astype(F32)[:, None]

    def parts(rot_dim):
        nf = rot_dim // 4
        inv = ROPE_BASE ** (-jnp.arange(nf, dtype=F32) / nf)
        cr, sr, cc, sc = jnp.cos(row * inv), jnp.sin(row * inv), jnp.cos(col * inv), jnp.sin(col * inv)
        return jnp.concatenate([cr, cr, cc, cc], axis=1), jnp.concatenate([-sr, sr, -sc, sc], axis=1)

    ca, sa = (jnp.tile(t, (1, 2)) for t in parts(HD_A))
    cc, sc = parts(ROPE_C)
    pad = lambda t, fill: jnp.pad(t, ((0, 0), (NOPE_C, LANES - QK_C)), constant_values=fill)
    lat = jnp.concatenate([ca, sa, pad(cc, 1.0), pad(sc, 0.0)], axis=1)
    ones, zeros = jnp.ones((DEC_SEQ, LANES), F32), jnp.zeros((DEC_SEQ, LANES), F32)
    ident = jnp.concatenate([ones, zeros, ones, zeros], axis=1)
    return jnp.concatenate([ident, lat], axis=0)


def _pair_swaps():
    j = lax.broadcasted_iota(jnp.int32, (LANES, LANES), 0)
    i = lax.broadcasted_iota(jnp.int32, (LANES, LANES), 1)

    def swap(half):
        first = (i % (2 * half)) < half
        return (j == jnp.where(first, i + half, i - half)).astype(BF16)

    return jnp.stack([swap(HD_A // 4), swap(ROPE_C // 4)])


def _arrange_w_in(w_in):
    o = 0
    parts = {}
    for name, n in (("qa", HQ_A * HD_A), ("ka", HKV_A * HD_A), ("va", HKV_A * HD_A), ("ub", D_B), ("cq", Q_LORA),
                    ("ckv", KV_LORA), ("kr", ROPE_C), ("gl", 3 * D_MODEL)):
        parts[name] = w_in[..., o:o + n]
        o += n
    qa = parts["qa"].reshape(DEPTH, D_MODEL, HKV_A, G_A, 1, HD_A)
    eye = jnp.eye(HKV_A, dtype=F32).reshape(1, 1, HKV_A, 1, HKV_A, 1)
    qa_slots = (qa * eye).reshape(DEPTH, D_MODEL, HQ_A * LANES)
    kr = jnp.pad(parts["kr"], ((0, 0), (0, 0), (NOPE_C, LANES - QK_C)))
    w = jnp.concatenate([parts["gl"], qa_slots, parts["ub"], parts["cq"], parts["ka"], parts["va"], parts["ckv"], kr],
                        axis=-1)
    return w.astype(BF16)


def _s5_params(lam_re, lam_im, log_dt, b_re, b_im, c_re, c_im):
    dt = jnp.exp(log_dt)[..., None]
    decay = jnp.exp(lam_re * dt)
    ab_re, ab_im = decay * jnp.cos(lam_im * dt), decay * jnp.sin(lam_im * dt)
    den = lam_re * lam_re + lam_im * lam_im
    f_re = ((ab_re - 1) * lam_re + ab_im * lam_im) / den
    f_im = (ab_im * lam_re - (ab_re - 1) * lam_im) / den
    bb_re = f_re[..., None] * b_re - f_im[..., None] * b_im
    bb_im = f_re[..., None] * b_im + f_im[..., None] * b_re
    nj, gpb = D_B // LANES, LANES // GS_B
    eye = jnp.eye(gpb, dtype=F32)

    def blockdiag_b(bb):
        t = bb.transpose(0, 1, 3, 2).reshape(2, nj, gpb, GS_B, P_B)
        return (t[:, :, :, :, None, :] * eye[None, None, :, None, :, None]).reshape(2, nj, LANES, gpb * P_B)

    def blockdiag_c(cc):
        t = cc.transpose(0, 1, 3, 2).reshape(2, nj, gpb, P_B, GS_B)
        return (t[:, :, :, :, None, :] * eye[None, None, :, None, :, None]).reshape(2, nj, gpb * P_B, LANES)

    wb = jnp.concatenate([blockdiag_b(bb_re), blockdiag_b(bb_im)], axis=-1).astype(BF16)
    wc = jnp.concatenate([blockdiag_c(c_re), -blockdiag_c(c_im)], axis=-2).astype(BF16)
    return wb, wc, ab_re.reshape(2, G_B * P_B), ab_im.reshape(2, G_B * P_B)


def kernel(x_prompt, x_sample, c, cache_attn_k, cache_attn_v, cache_mla_ckv, cache_mla_krope, state_ssm_re, state_ssm_im, c_ctx, w_ada, b_ada, norm_mix_g, norm_ffn_g, w_in, q_norm_a, k_norm_a, sink_a, q_a_norm_c, kv_a_norm_c, w_uq_c, w_ukv_c, q_norm_c, k_norm_c, ssm_lam_re, ssm_lam_im, ssm_log_dt, ssm_b_re, ssm_b_im, ssm_c_re, ssm_c_im, ssm_d, w_glu, b_glu, w_br_a, w_br_b, w_br_c, w_out, w_router, b_router, w_up, b_up, w_down, b_down):
    x = jnp.concatenate([x_prompt.reshape(N_CTX, D_MODEL), x_sample.reshape(N_LAT, D_MODEL)], axis=0)
    cvecs = jnp.concatenate([c_ctx[None], c, jnp.zeros((N_GROUPS - 1 - DEC_BATCH, D_MODEL), F32)], axis=0)
    mods = adaln(cvecs, w_ada, b_ada)

    tab = _rope_table()
    w_in_r = _arrange_w_in(w_in)
    pad_slot = lambda g: jnp.pad(g, ((0, 0), (0, LANES - QK_C))).reshape(DEPTH, 1, LANES)
    gqa = jnp.tile(q_norm_a, (1, 2)).reshape(DEPTH, 1, LANES)
    gka = jnp.tile(k_norm_a, (1, 2)).reshape(DEPTH, 1, LANES)
    gqc, gkc = pad_slot(q_norm_c), pad_slot(k_norm_c)
    wuq = jnp.pad(w_uq_c.reshape(DEPTH, Q_LORA, H_C, QK_C), ((0, 0), (0, 0), (0, 0), (0, LANES - QK_C)))
    wuq = wuq.reshape(DEPTH, Q_LORA, H_C * LANES).astype(BF16)
    wukv4 = w_ukv_c.reshape(DEPTH, KV_LORA, H_C, NOPE_C + V_C)
    wuk = jnp.pad(wukv4[..., :NOPE_C], ((0, 0), (0, 0), (0, 0), (0, LANES - NOPE_C))).reshape(DEPTH, KV_LORA, H_C * LANES)
    wuv = wukv4[..., NOPE_C:].reshape(DEPTH, KV_LORA, H_C * V_C)
    wukv = jnp.concatenate([wuk, wuv], axis=-1).astype(BF16)
    sink = jnp.broadcast_to(sink_a[:, :, None], (DEPTH, HQ_A, LANES))
    w_router_p = jnp.pad(w_router, ((0, 0), (0, 0), (0, LANES - N_EXP)))
    b_router_p = jnp.pad(b_router, ((0, 0), (0, LANES - N_EXP)), constant_values=-jnp.inf).reshape(DEPTH, 1, LANES)
    b_up_p = b_up.reshape(DEPTH, N_EXP, 2 * D_FF // PAIR_TILE, LANES, 2).transpose(0, 1, 2, 4, 3)
    b_up_p = b_up_p.reshape(DEPTH, N_EXP, 1, 2 * D_FF)
    b_down_r = b_down.reshape(DEPTH, N_EXP, 1, D_MODEL)
    kr_cache = jnp.pad(cache_mla_krope, ((0, 0), (0, 0), (0, 0), (NOPE_C, LANES - QK_C)))

    new_k, new_v, new_ckv, new_kr, new_sre, new_sim = [], [], [], [], [], []
    for l in range(DEPTH):
        mod = mods[l]
        proj = inproj(x, norm_mix_g[l], mod, w_in_r[l])
        qa, ka, qc, kc, vc, ckvn = prep(proj, tab, gqa[l], gka[l], q_a_norm_c[l].reshape(1, Q_LORA),
                                         kv_a_norm_c[l].reshape(1, KV_LORA), gqc[l], gkc[l], wuq[l], wukv[l])
        kc_cache, vc_cache = cache_keys(cache_mla_ckv[:, l].reshape(DEC_BATCH * PAST_LEN, KV_LORA),
                                        kr_cache[:, l].reshape(DEC_BATCH * PAST_LEN, LANES), gkc[l], wukv[l])
        oa = attn_a_lat(qa, ka, proj, cache_attn_k[:, l].reshape(DEC_BATCH, PAST_LEN, LANES),
                        cache_attn_v[:, l].reshape(DEC_BATCH, PAST_LEN, LANES), sink[l],
                        attn_a_ctx(qa, ka, proj, sink[l]))
        oc = attn_c_lat(qc, kc, vc, kc_cache, vc_cache, attn_c_ctx(qc, kc, vc))
        wb, wc, a_re, a_im = _s5_params(ssm_lam_re[l], ssm_lam_im[l], ssm_log_dt[l], ssm_b_re[l], ssm_b_im[l],
                                        ssm_c_re[l], ssm_c_im[l])
        proj4 = proj.reshape(TOK_CHUNKS // CHUNK_GROUP, CHUNK_GROUP, S5_TC, N_PROJ)
        per_row = lambda a, rep: jnp.repeat(a, rep, axis=0)
        y_all, f_re, f_im = s5_ctx(proj4, wb, wc, per_row(a_re, SUBLANES).reshape(2, SUBLANES, -1),
                                   per_row(a_im, SUBLANES).reshape(2, SUBLANES, -1))
        h0 = lambda st: st[:, l].transpose(1, 0, 2, 3).reshape(2 * DEC_BATCH, G_B * P_B)
        y_all, y_bwd = s5_lat(proj4, y_all, wb, wc, per_row(a_re, DEC_BATCH), per_row(a_im, DEC_BATCH),
                              h0(state_ssm_re), h0(state_ssm_im))
        ob = s5_finish(proj, y_all.reshape(T_ALL, D_B), y_bwd.reshape(N_LAT, D_B), ssm_d[l].reshape(1, D_B),
                       w_glu[l].astype(BF16), b_glu[l].reshape(1, D_B))
        x1, h2, h2row = merge(x, proj, oa, ob, oc, mod, norm_ffn_g[l].reshape(1, D_MODEL), w_br_a[l].astype(BF16),
                       w_br_b[l].astype(BF16), w_br_c[l].astype(BF16), w_out[l].astype(BF16))
        idx, rank, wgt, cnt = router(h2, w_router_p[l], b_router_p[l])
        counts = cnt[0, :N_EXP].astype(jnp.int32)
        padded = (counts + BLK - 1) // BLK * BLK
        pad_end = jnp.cumsum(padded)
        pad_start = pad_end - padded
        dest = (BLK + pad_start[idx[:, :TOP_K]] + rank[:, :TOP_K]).reshape(-1).astype(jnp.int32)
        block_start = jnp.arange(N_BLOCKS, dtype=jnp.int32) * BLK
        block_e = jnp.minimum(jnp.sum(pad_end[None, :] <= block_start[:, None], axis=1), N_EXP - 1)
        block_e = block_e.astype(jnp.int32)
        comb = experts(l, block_e, invert_plan(dest), h2row, w_up, b_up_p, w_down, b_down_r)
        x = combine(comb, x1, wgt, mod)

        new_k.append(ka[:N_CTX].reshape(BATCH, SEQ, HKV_A, HD_A))
        new_v.append(proj[:N_CTX, C_VA:C_VA + LANES].reshape(BATCH, SEQ, HKV_A, HD_A))
        new_ckv.append(ckvn[:N_CTX].reshape(BATCH, SEQ, KV_LORA))
        new_kr.append(proj[:N_CTX, C_KR + NOPE_C:C_KR + QK_C].reshape(BATCH, SEQ, ROPE_C))
        new_sre.append(f_re.reshape(2, BATCH, G_B, P_B).transpose(1, 0, 2, 3))
        new_sim.append(f_im.reshape(2, BATCH, G_B, P_B).transpose(1, 0, 2, 3))

    y_prompt = x[:N_CTX].reshape(BATCH, SEQ, D_MODEL)
    y_sample = x[N_CTX:].reshape(DEC_BATCH, DEC_SEQ, D_MODEL)
    return (y_prompt, y_sample, jnp.stack(new_k, axis=1), jnp.stack(new_v, axis=1), jnp.stack(new_ckv, axis=1),
            jnp.stack(new_kr, axis=1), jnp.stack(new_sre, axis=1), jnp.stack(new_sim, axis=1))
```

```python
import functools
import math

import jax
import jax.numpy as jnp
from jax import lax
from jax.experimental import pallas as pl
from jax.experimental.pallas import tpu as pltpu

D_MODEL = 1024
BATCH = 32
SEQ = 256
DEPTH = 2
DEC_BATCH = 4
DEC_SEQ = 2048
PAST_LEN = 256
GRID_W = 64
ROPE_BASE = 10000.0
EPS = 1e-6
NEG_INF = -1e30
BAND = 128
BLK = 128
HQ_A, HKV_A, HD_A = 8, 2, 64
G_A = HQ_A // HKV_A
WINDOW = 128
D_B, GS_B, P_B = 512, 16, 64
G_B = D_B // GS_B
H_C, Q_LORA, KV_LORA, NOPE_C, ROPE_C, V_C = 8, 256, 128, 64, 32, 64
QK_C = NOPE_C + ROPE_C
N_EXP, TOP_K, D_FF = 32, 4, 1024
SWIGLU_ALPHA, SWIGLU_LIMIT = 1.702, 7.0

N_CTX = BATCH * SEQ
N_LAT = DEC_BATCH * DEC_SEQ
T_ALL = N_CTX + N_LAT
N_GROUPS = 8

LANES = 128
SUBLANES = 8

C_GL = 0
C_QA = 3 * D_MODEL
C_UB = C_QA + HQ_A * LANES
C_CQ = C_UB + D_B
C_KA = C_CQ + Q_LORA
C_VA = C_KA + LANES
C_CKV = C_VA + LANES
C_KR = C_CKV + LANES
N_PROJ = C_KR + LANES

N_ASSIGN = T_ALL * TOP_K
N_BLOCKS = N_ASSIGN // BLK + N_EXP
BUF_LEN = N_BLOCKS * BLK

F32 = jnp.float32
BF16 = jnp.bfloat16
ROW3 = (D_MODEL // LANES, LANES)
VMEM_LIMIT = 56 * 1024 * 1024


def _cp(sem, vmem=VMEM_LIMIT):
    return pltpu.CompilerParams(dimension_semantics=sem, vmem_limit_bytes=vmem)


def _group_of_tile(i, tm):
    n_ctx_tiles = N_CTX // tm
    per_batch = DEC_SEQ // tm
    return jnp.where(i < n_ctx_tiles, 0, 1 + (i - n_ctx_tiles) // per_batch)


def _sigmoid(x):
    return 1.0 / (1.0 + jnp.exp(-x))


def _adaln_kernel(c_ref, w_ref, b_ref, o_ref):
    c = c_ref[...]
    s = c * _sigmoid(c)
    o_ref[...] = jnp.dot(s.astype(BF16), w_ref[...].astype(BF16), preferred_element_type=F32) + b_ref[...]


def adaln(cvecs, w_ada, b_ada):
    tn = 1536
    out = pl.pallas_call(
        _adaln_kernel,
        out_shape=jax.ShapeDtypeStruct((DEPTH, N_GROUPS, 6 * D_MODEL), F32),
        grid=(DEPTH, 6 * D_MODEL // tn),
        in_specs=[
            pl.BlockSpec((N_GROUPS, D_MODEL), lambda l, j: (0, 0)),
            pl.BlockSpec((None, D_MODEL, tn), lambda l, j: (l, 0, j)),
            pl.BlockSpec((None, 1, tn), lambda l, j: (l, 0, j)),
        ],
        out_specs=pl.BlockSpec((None, N_GROUPS, tn), lambda l, j: (l, 0, j)),
        compiler_params=_cp(("arbitrary", "arbitrary")),
        name="adaln",
    )(cvecs, w_ada, b_ada.reshape(DEPTH, 1, 6 * D_MODEL))
    return out.reshape(DEPTH, N_GROUPS, 6, D_MODEL)


def _rms(x, g):
    ms = jnp.mean(x * x, axis=-1, keepdims=True)
    return x * lax.rsqrt(ms + EPS) * g


def _inproj_kernel(x_ref, g_ref, mod_ref, w_ref, o_ref):
    h = _rms(x_ref[...], g_ref[...])
    h = h * (1.0 + mod_ref[1:2, :]) + mod_ref[0:1, :]
    o_ref[...] = jnp.dot(h.astype(BF16), w_ref[...], preferred_element_type=F32)


def inproj(layer, x, g, mod, w):
    tm, tn = 512, 1792
    return pl.pallas_call(
        _inproj_kernel,
        out_shape=jax.ShapeDtypeStruct((T_ALL, N_PROJ), F32),
        grid=(N_PROJ // tn, T_ALL // tm),
        in_specs=[
            pl.BlockSpec((tm, D_MODEL), lambda j, i: (i, 0)),
            pl.BlockSpec((1, D_MODEL), lambda j, i: (0, 0)),
            pl.BlockSpec((None, 6, D_MODEL), lambda j, i: (_group_of_tile(i, tm), 0, 0)),
            pl.BlockSpec((None, D_MODEL, tn), lambda j, i: (layer, 0, j)),
        ],
        out_specs=pl.BlockSpec((tm, tn), lambda j, i: (i, j)),
        compiler_params=_cp(("arbitrary", "arbitrary")),
        name="inproj",
    )(x, g.reshape(1, D_MODEL), mod, w)


def _rope(slab, cos, sin, swap):
    hi = slab.astype(BF16)
    lo = (slab - hi.astype(F32)).astype(BF16)
    partner = jnp.dot(hi, swap, preferred_element_type=F32) + jnp.dot(lo, swap, preferred_element_type=F32)
    return slab * cos + partner * sin


def _slot_norm(slab, gain, n_real):
    ms = jnp.sum(slab * slab, axis=-1, keepdims=True) * (1.0 / n_real)
    return slab * lax.rsqrt(ms + EPS) * gain


def _mla_keys(ckvn, kr_blk, wukv_ref, gk, rope_c):
    kv = jnp.dot(ckvn.astype(BF16), wukv_ref[...], preferred_element_type=F32)
    ks = []
    for h in range(H_C):
        slab = kv[:, h * LANES:(h + 1) * LANES] + kr_blk
        slab = _slot_norm(slab, gk, QK_C)
        if rope_c is not None:
            slab = _rope(slab, *rope_c)
        ks.append(slab.astype(BF16))
    return jnp.concatenate(ks, axis=1), kv[:, H_C * LANES:].astype(BF16)


PREP_TM = 256


def _prep_kernel(qa_ref, cq_ref, ka_ref, ckv_ref, kr_ref, tab_ref,
                 gqa_ref, gka_ref, gcq_ref, gckv_ref, gqc_ref, gkc_ref, wuq_ref, wukv_ref, swap_ref,
                 qa_o, ka_o, qc_o, kc_o, vc_o, ckvn_o):
    def body(rotate):
        if rotate:
            tab = tab_ref[...]
            rope_a = (tab[:, 0:128], tab[:, 128:256], swap_ref[0])
            rope_c = (tab[:, 256:384], tab[:, 384:512], swap_ref[1])
            rot_a = lambda x: _rope(x, *rope_a)
            rot_c = lambda x: _rope(x, *rope_c)
        else:
            rope_c = None
            rot_a = rot_c = lambda x: x
        lane = lax.broadcasted_iota(jnp.int32, (1, LANES), 1)

        gqa = gqa_ref[...]
        qa = qa_ref[...]
        outs = []
        for h in range(HQ_A):
            slab = _slot_norm(qa[:, h * LANES:(h + 1) * LANES], gqa, HD_A)
            outs.append((rot_a(slab) * (HD_A ** -0.5)).astype(BF16))
        qa_o[...] = jnp.concatenate(outs, axis=1)

        ka = ka_ref[...]
        sq = ka * ka
        lo = lane < HD_A
        ms_lo = jnp.sum(jnp.where(lo, sq, 0.0), axis=-1, keepdims=True)
        ms_hi = jnp.sum(jnp.where(lo, 0.0, sq), axis=-1, keepdims=True)
        rs = jnp.where(lo, lax.rsqrt(ms_lo * (1.0 / HD_A) + EPS), lax.rsqrt(ms_hi * (1.0 / HD_A) + EPS))
        ka_o[...] = rot_a(ka * rs * gka_ref[...])

        cqn = _rms(cq_ref[...], gcq_ref[...])
        q = jnp.dot(cqn.astype(BF16), wuq_ref[...], preferred_element_type=F32)
        gqc = gqc_ref[...]
        outs = []
        for h in range(H_C):
            slab = _slot_norm(q[:, h * LANES:(h + 1) * LANES], gqc, QK_C)
            outs.append((rot_c(slab) * (QK_C ** -0.5)).astype(BF16))
        qc_o[...] = jnp.concatenate(outs, axis=1)

        ckvn = _rms(ckv_ref[...], gckv_ref[...])
        ckvn_o[...] = ckvn
        kc, vc = _mla_keys(ckvn, kr_ref[...], wukv_ref, gkc_ref[...], rope_c)
        kc_o[...] = kc
        vc_o[...] = vc

    is_latent = pl.program_id(0) >= N_CTX // PREP_TM
    pl.when(is_latent)(lambda: body(True))
    pl.when(jnp.logical_not(is_latent))(lambda: body(False))


def prep(proj, tab, gqa, gka, gcq, gckv, gqc, gkc, wuq, wukv):
    tm = PREP_TM
    n_ctx_tiles = N_CTX // tm
    per_batch = DEC_SEQ // tm

    def tab_map(i):
        return (jnp.where(i < n_ctx_tiles, 0, per_batch + (i - n_ctx_tiles) % per_batch), 0)

    def col(width, off):
        return pl.BlockSpec((tm, width), lambda i: (i, off // width))

    def full(shape):
        return pl.BlockSpec(shape, lambda i: (0,) * len(shape))

    def row_out(width):
        return pl.BlockSpec((tm, width), lambda i: (i, 0))

    return pl.pallas_call(
        _prep_kernel,
        out_shape=(
            jax.ShapeDtypeStruct((T_ALL, HQ_A * LANES), BF16),
            jax.ShapeDtypeStruct((T_ALL, LANES), F32),
            jax.ShapeDtypeStruct((T_ALL, H_C * LANES), BF16),
            jax.ShapeDtypeStruct((T_ALL, H_C * LANES), BF16),
            jax.ShapeDtypeStruct((T_ALL, H_C * V_C), BF16),
            jax.ShapeDtypeStruct((T_ALL, KV_LORA), F32),
        ),
        grid=(T_ALL // tm,),
        in_specs=[
            col(HQ_A * LANES, C_QA), col(Q_LORA, C_CQ), col(LANES, C_KA), col(LANES, C_CKV), col(LANES, C_KR),
            pl.BlockSpec((tm, 4 * LANES), tab_map),
            full((1, LANES)), full((1, LANES)), full((1, Q_LORA)), full((1, KV_LORA)),
            full((1, LANES)), full((1, LANES)),
            full((Q_LORA, H_C * LANES)), full((KV_LORA, H_C * LANES + H_C * V_C)), full((2, LANES, LANES)),
        ],
        out_specs=(row_out(HQ_A * LANES), row_out(LANES), row_out(H_C * LANES), row_out(H_C * LANES),
                   row_out(H_C * V_C), row_out(KV_LORA)),
        compiler_params=_cp(("arbitrary",)),
        name="prep",
    )(proj, proj, proj, proj, proj, tab, gqa, gka, gcq, gckv, gqc, gkc, wuq, wukv, _pair_swaps())


def _cachekeys_kernel(ckv_ref, kr_ref, gkc_ref, wukv_ref, kc_o, vc_o):
    kc, vc = _mla_keys(ckv_ref[...], kr_ref[...], wukv_ref, gkc_ref[...], None)
    kc_o[...] = kc
    vc_o[...] = vc


def cache_keys(ckv, kr_blk, gkc, wukv):
    r = ckv.shape[0]
    tm = 256
    return pl.pallas_call(
        _cachekeys_kernel,
        out_shape=(jax.ShapeDtypeStruct((r, H_C * LANES), BF16), jax.ShapeDtypeStruct((r, H_C * V_C), BF16)),
        grid=(r // tm,),
        in_specs=[
            pl.BlockSpec((tm, LANES), lambda i: (i, 0)),
            pl.BlockSpec((tm, LANES), lambda i: (i, 0)),
            pl.BlockSpec((1, LANES), lambda i: (0, 0)),
            pl.BlockSpec((KV_LORA, H_C * LANES + H_C * V_C), lambda i: (0, 0)),
        ],
        out_specs=(pl.BlockSpec((tm, H_C * LANES), lambda i: (i, 0)), pl.BlockSpec((tm, H_C * V_C), lambda i: (i, 0))),
        compiler_params=_cp(("arbitrary",)),
        name="cache_keys",
    )(ckv, kr_blk, gkc, wukv)


def _attn_body(q_ref, segs, sink_ref, o_ref, *, n_heads, k_slot, v_slab, v_half, tq, band_qi=None):
    outs = []
    for h in range(n_heads):
        qh = q_ref[:, h * LANES:(h + 1) * LANES]
        scores = []
        for k_ref, _, off in segs:
            kh = k_ref[:, k_slot(h) * LANES:(k_slot(h) + 1) * LANES].astype(BF16)
            s = lax.dot_general(qh, kh, (((1,), (1,)), ((), ())), preferred_element_type=F32)
            if off is not None:
                tk = s.shape[1]
                blk = band_qi + off
                q_pos = band_qi * tq + lax.broadcasted_iota(jnp.int32, (tq, tk), 0)
                k_pos = blk * tk + lax.broadcasted_iota(jnp.int32, (tq, tk), 1)
                ok = (jnp.abs(k_pos - q_pos) <= WINDOW) & (blk >= 0) & (blk < DEC_SEQ // tk)
                s = jnp.where(ok, s, NEG_INF)
            scores.append(s)
        m = scores[0].max(axis=-1, keepdims=True)
        for s in scores[1:]:
            m = jnp.maximum(m, s.max(axis=-1, keepdims=True))
        if sink_ref is not None:
            sink = sink_ref[h:h + 1, 0:1]
            m = jnp.maximum(m, sink)
            denom = jnp.exp(sink - m)
        else:
            denom = jnp.zeros_like(m)
        acc = None
        for s, (_, v_ref, _) in zip(scores, segs):
            p = jnp.exp(s - m)
            denom = denom + p.sum(axis=-1, keepdims=True)
            vs = v_ref[:, v_slab(h) * LANES:(v_slab(h) + 1) * LANES].astype(BF16)
            pv = jnp.dot(p.astype(BF16), vs, preferred_element_type=F32)
            acc = pv if acc is None else acc + pv
        half = v_half(h)
        outs.append(acc[:, half * 64:(half + 1) * 64] * (1.0 / denom))
    o_ref[...] = jnp.concatenate(outs, axis=1).astype(o_ref.dtype)


_C_CFG = dict(n_heads=H_C, k_slot=lambda h: h, v_slab=lambda h: h // 2, v_half=lambda h: h % 2)


def _attn_a_body(q_ref, segs, sink_ref, o_ref, *, tq, band_qi=None):
    rows = HQ_A * tq
    q = jnp.concatenate([q_ref[:, h * LANES:(h + 1) * LANES] for h in range(HQ_A)], axis=0)
    sink = jnp.concatenate([jnp.broadcast_to(sink_ref[h:h + 1, 0:1], (tq, 1)) for h in range(HQ_A)], axis=0)
    scores = []
    for k_ref, _, off in segs:
        s = lax.dot_general(q, k_ref[...].astype(BF16), (((1,), (1,)), ((), ())), preferred_element_type=F32)
        if off is not None:
            tk = s.shape[1]
            blk = band_qi + off
            q_pos = band_qi * tq + (lax.broadcasted_iota(jnp.int32, (rows, tk), 0) & (tq - 1))
            k_pos = blk * tk + lax.broadcasted_iota(jnp.int32, (rows, tk), 1)
            ok = (jnp.abs(k_pos - q_pos) <= WINDOW) & (blk >= 0) & (blk < DEC_SEQ // tk)
            s = jnp.where(ok, s, NEG_INF)
        scores.append(s)
    m = sink
    for s in scores:
        m = jnp.maximum(m, s.max(axis=-1, keepdims=True))
    denom = jnp.exp(sink - m)
    acc = None
    for s, (_, v_ref, _) in zip(scores, segs):
        p = jnp.exp(s - m)
        denom = denom + p.sum(axis=-1, keepdims=True)
        pv = jnp.dot(p.astype(BF16), v_ref[...].astype(BF16), preferred_element_type=F32)
        acc = pv if acc is None else acc + pv
    acc = acc * (1.0 / denom)
    outs = []
    for h in range(HQ_A):
        kv = h // G_A
        outs.append(acc[h * tq:(h + 1) * tq, kv * HD_A:(kv + 1) * HD_A])
    o_ref[...] = jnp.concatenate(outs, axis=1).astype(o_ref.dtype)


def _attn_a_ctx_kernel(q_ref, k_ref, v_ref, sink_ref, o_init, o_ref):
    del o_init
    _attn_a_body(q_ref, [(k_ref, v_ref, None)], sink_ref, o_ref, tq=SEQ)


def _attn_a_lat_kernel(q_ref, k0, k1, k2, v0, v1, v2, kc_ref, vc_ref, sink_ref, o_ctx, o_ref):
    del o_ctx
    qi = pl.program_id(1)
    segs = [(k0, v0, -1), (k1, v1, 0), (k2, v2, 1), (kc_ref, vc_ref, None)]
    _attn_a_body(q_ref, segs, sink_ref, o_ref, tq=BAND, band_qi=qi)


def _attn_c_ctx_kernel(q_ref, k_ref, v_ref, o_init, o_ref):
    del o_init
    _attn_body(q_ref, [(k_ref, v_ref, None)], None, o_ref, tq=SEQ, **_C_CFG)


def _attn_c_lat_kernel(q_ref, k_ref, v_ref, kc_ref, vc_ref, o_ctx, o_ref):
    del o_ctx
    _attn_body(q_ref, [(k_ref, v_ref, None), (kc_ref, vc_ref, None)], None, o_ref, tq=256, **_C_CFG)


def attn_a_ctx(qa, ka, proj, sink):
    return pl.pallas_call(
        _attn_a_ctx_kernel,
        out_shape=jax.ShapeDtypeStruct((T_ALL, HQ_A * HD_A), BF16),
        grid=(BATCH,),
        in_specs=[
            pl.BlockSpec((SEQ, HQ_A * LANES), lambda b: (b, 0)),
            pl.BlockSpec((SEQ, LANES), lambda b: (b, 0)),
            pl.BlockSpec((SEQ, LANES), lambda b: (b, C_VA // LANES)),
            pl.BlockSpec((SUBLANES, LANES), lambda b: (0, 0)),
            pl.BlockSpec(memory_space=pl.ANY),
        ],
        out_specs=pl.BlockSpec((SEQ, HQ_A * HD_A), lambda b: (b, 0)),
        input_output_aliases={4: 0},
        compiler_params=_cp(("arbitrary",)),
        name="attn_a_ctx",
    )(qa, ka, proj, sink, jnp.zeros((T_ALL, HQ_A * HD_A), BF16))


def attn_a_lat(qa, ka, proj, k_cache, v_cache, sink, o_all):
    nb = DEC_SEQ // BAND
    base = N_CTX // BAND

    def band(off, colblk):
        return pl.BlockSpec((BAND, LANES), lambda b, i: (base + b * nb + jnp.clip(i + off, 0, nb - 1), colblk))

    return pl.pallas_call(
        _attn_a_lat_kernel,
        out_shape=jax.ShapeDtypeStruct((T_ALL, HQ_A * HD_A), BF16),
        grid=(DEC_BATCH, nb),
        in_specs=[
            pl.BlockSpec((BAND, HQ_A * LANES), lambda b, i: (base + b * nb + i, 0)),
            band(-1, 0), band(0, 0), band(1, 0),
            band(-1, C_VA // LANES), band(0, C_VA // LANES), band(1, C_VA // LANES),
            pl.BlockSpec((None, PAST_LEN, LANES), lambda b, i: (b, 0, 0)),
            pl.BlockSpec((None, PAST_LEN, LANES), lambda b, i: (b, 0, 0)),
            pl.BlockSpec((SUBLANES, LANES), lambda b, i: (0, 0)),
            pl.BlockSpec(memory_space=pl.ANY),
        ],
        out_specs=pl.BlockSpec((BAND, HQ_A * HD_A), lambda b, i: (base + b * nb + i, 0)),
        input_output_aliases={10: 0},
        compiler_params=_cp(("arbitrary", "arbitrary")),
        name="attn_a_lat",
    )(qa, ka, ka, ka, proj, proj, proj, k_cache, v_cache, sink, o_all)


def attn_c_ctx(qc, kc, vc):
    return pl.pallas_call(
        _attn_c_ctx_kernel,
        out_shape=jax.ShapeDtypeStruct((T_ALL, H_C * V_C), BF16),
        grid=(BATCH,),
        in_specs=[
            pl.BlockSpec((SEQ, H_C * LANES), lambda b: (b, 0)),
            pl.BlockSpec((SEQ, H_C * LANES), lambda b: (b, 0)),
            pl.BlockSpec((SEQ, H_C * V_C), lambda b: (b, 0)),
            pl.BlockSpec(memory_space=pl.ANY),
        ],
        out_specs=pl.BlockSpec((SEQ, H_C * V_C), lambda b: (b, 0)),
        input_output_aliases={3: 0},
        compiler_params=_cp(("arbitrary",)),
        name="attn_c_ctx",
    )(qc, kc, vc, jnp.zeros((T_ALL, H_C * V_C), BF16))


def attn_c_lat(qc, kc, vc, kc_cache, vc_cache, o_all):
    tq = 256
    nq = DEC_SEQ // tq
    qbase = N_CTX // tq
    kbase = N_CTX // DEC_SEQ
    return pl.pallas_call(
        _attn_c_lat_kernel,
        out_shape=jax.ShapeDtypeStruct((T_ALL, H_C * V_C), BF16),
        grid=(DEC_BATCH, nq),
        in_specs=[
            pl.BlockSpec((tq, H_C * LANES), lambda b, i: (qbase + b * nq + i, 0)),
            pl.BlockSpec((DEC_SEQ, H_C * LANES), lambda b, i: (kbase + b, 0)),
            pl.BlockSpec((DEC_SEQ, H_C * V_C), lambda b, i: (kbase + b, 0)),
            pl.BlockSpec((PAST_LEN, H_C * LANES), lambda b, i: (b, 0)),
            pl.BlockSpec((PAST_LEN, H_C * V_C), lambda b, i: (b, 0)),
            pl.BlockSpec(memory_space=pl.ANY),
        ],
        out_specs=pl.BlockSpec((tq, H_C * V_C), lambda b, i: (qbase + b * nq + i, 0)),
        input_output_aliases={5: 0},
        compiler_params=_cp(("arbitrary", "arbitrary")),
        name="attn_c_lat",
    )(qc, kc, vc, kc_cache, vc_cache, o_all)


S5_TC = 256
S5_CH = 512
S5_UNROLL = 8
TOK_CHUNKS = T_ALL // S5_TC
CHUNK_GROUP = 8
assert SEQ == S5_TC and DEC_SEQ == CHUNK_GROUP * S5_TC and BATCH == 4 * CHUNK_GROUP


def _s5_bproj(u2, wb_ref, pick_first):
    def part(lo, hi):
        r = jnp.dot(u2, wb_ref[0, :, lo:hi], preferred_element_type=F32)
        if pick_first is not None:
            r = jnp.where(pick_first, r, jnp.dot(u2, wb_ref[1, :, lo:hi], preferred_element_type=F32))
        return r.reshape(S5_TC, SUBLANES, S5_CH)
    return part(0, S5_CH), part(S5_CH, 2 * S5_CH)


def _s5_cproj(xre, xim, wc_ref, pick_first):
    rows = S5_TC * SUBLANES
    xr = xre[...].reshape(rows, S5_CH).astype(BF16)
    xi = xim[...].reshape(rows, S5_CH).astype(BF16)

    def part(d):
        return (jnp.dot(xr, wc_ref[d, 0:S5_CH, :], preferred_element_type=F32)
                + jnp.dot(xi, wc_ref[d, S5_CH:, :], preferred_element_type=F32))

    y = part(0)
    if pick_first is not None:
        y = jnp.where(pick_first, y, part(1))
    return jnp.swapaxes(y.reshape(S5_TC, SUBLANES, LANES), 0, 1)


def _s5_ctx_kernel(u_ref, wb_ref, wc_ref, are_ref, aim_ref, y_init, y_ref, fre_ref, fim_ref, bre, bim):
    del y_init
    d = pl.program_id(2)
    rows = S5_TC * SUBLANES
    u2 = jnp.swapaxes(u_ref[...], 0, 1).reshape(rows, LANES).astype(BF16)
    b_re, b_im = _s5_bproj(u2, wb_ref, None)
    bre[...] = b_re
    bim[...] = b_im
    a_re = are_ref[...]
    a_im = aim_ref[...]

    def outer(i, carry):
        sr, si = carry
        for j in range(S5_UNROLL):
            t0 = i * S5_UNROLL + j
            t = jnp.where(d == 1, S5_TC - 1 - t0, t0)
            nr = a_re * sr - a_im * si + bre[t]
            ni = a_re * si + a_im * sr + bim[t]
            bre[t] = nr
            bim[t] = ni
            sr, si = nr, ni
        return sr, si

    zero = jnp.zeros((SUBLANES, S5_CH), F32)
    sr, si = lax.fori_loop(0, S5_TC // S5_UNROLL, outer, (zero, zero))
    fre_ref[...] = sr
    fim_ref[...] = si
    y = _s5_cproj(bre, bim, wc_ref, None)

    @pl.when(d == 0)
    def _():
        y_ref[...] = y

    @pl.when(d == 1)
    def _():
        y_ref[...] += y


def s5_ctx(proj4, wb, wc, a_re, a_im):
    nj = D_B // LANES
    ucol = C_UB // LANES
    return pl.pallas_call(
        _s5_ctx_kernel,
        out_shape=(
            jax.ShapeDtypeStruct((TOK_CHUNKS // CHUNK_GROUP, CHUNK_GROUP, S5_TC, D_B), F32),
            jax.ShapeDtypeStruct((2, BATCH, G_B * P_B), F32),
            jax.ShapeDtypeStruct((2, BATCH, G_B * P_B), F32),
        ),
        grid=(BATCH // CHUNK_GROUP, nj, 2),
        in_specs=[
            pl.BlockSpec((None, CHUNK_GROUP, S5_TC, LANES), lambda sb, j, d: (sb, 0, 0, ucol + j)),
            pl.BlockSpec((1, None, LANES, 2 * S5_CH), lambda sb, j, d: (d, j, 0, 0)),
            pl.BlockSpec((1, None, 2 * S5_CH, LANES), lambda sb, j, d: (d, j, 0, 0)),
            pl.BlockSpec((None, SUBLANES, S5_CH), lambda sb, j, d: (d, 0, j)),
            pl.BlockSpec((None, SUBLANES, S5_CH), lambda sb, j, d: (d, 0, j)),
            pl.BlockSpec(memory_space=pl.ANY),
        ],
        out_specs=(
            pl.BlockSpec((None, CHUNK_GROUP, S5_TC, LANES), lambda sb, j, d: (sb, 0, 0, j)),
            pl.BlockSpec((None, SUBLANES, S5_CH), lambda sb, j, d: (d, sb, j)),
            pl.BlockSpec((None, SUBLANES, S5_CH), lambda sb, j, d: (d, sb, j)),
        ),
        scratch_shapes=[pltpu.VMEM((S5_TC, SUBLANES, S5_CH), F32), pltpu.VMEM((S5_TC, SUBLANES, S5_CH), F32)],
        input_output_aliases={5: 0},
        compiler_params=_cp(("arbitrary", "arbitrary", "arbitrary")),
        name="s5_ctx",
    )(proj4, wb, wc, a_re, a_im, jnp.zeros((TOK_CHUNKS // CHUNK_GROUP, CHUNK_GROUP, S5_TC, D_B), F32))


def _s5_lat_kernel(uf_ref, ub_ref, wb_ref, wc_ref, are_ref, aim_ref, h0re_ref, h0im_ref, y_ctx,
                   yf_ref, yb_ref, bre, bim, xre, xim, sre, sim):
    del y_ctx
    k = pl.program_id(1)

    @pl.when(k == 0)
    def _():
        sre[...] = h0re_ref[...]
        sim[...] = h0im_ref[...]

    rows = S5_TC * SUBLANES
    half = SUBLANES // 2
    u8 = jnp.concatenate([uf_ref[...], ub_ref[...]], axis=0)
    u2 = jnp.swapaxes(u8, 0, 1).reshape(rows, LANES).astype(BF16)
    fwd_rows = (lax.broadcasted_iota(jnp.int32, (rows, 1), 0) % SUBLANES) < half
    b_re, b_im = _s5_bproj(u2, wb_ref, fwd_rows)
    bre[...] = b_re
    bim[...] = b_im
    a_re = are_ref[...]
    a_im = aim_ref[...]
    fwd8 = lax.broadcasted_iota(jnp.int32, (SUBLANES, 1), 0) < half

    def outer(i, carry):
        sr, si = carry
        for j in range(S5_UNROLL):
            t = i * S5_UNROLL + j
            tb = S5_TC - 1 - t
            nr = a_re * sr - a_im * si + jnp.where(fwd8, bre[t], bre[tb])
            ni = a_re * si + a_im * sr + jnp.where(fwd8, bim[t], bim[tb])
            xre[t, 0:half, :] = nr[0:half]
            xre[tb, half:, :] = nr[half:]
            xim[t, 0:half, :] = ni[0:half]
            xim[tb, half:, :] = ni[half:]
            sr, si = nr, ni
        return sr, si

    sr, si = lax.fori_loop(0, S5_TC // S5_UNROLL, outer, (sre[...], sim[...]))
    sre[...] = sr
    sim[...] = si
    y = _s5_cproj(xre, xim, wc_ref, fwd_rows)
    yf_ref[...] = y[0:half]
    yb_ref[...] = y[half:]


def s5_lat(proj4, y_all, wb, wc, a_re, a_im, h0_re, h0_im):
    nj = D_B // LANES
    nk = DEC_SEQ // S5_TC
    ucol = C_UB // LANES
    lat = lambda col0, rev: pl.BlockSpec(
        (DEC_BATCH, None, S5_TC, LANES), lambda j, k: (1, (nk - 1 - k) if rev else k, 0, col0 + j))
    vec = lambda: pl.BlockSpec((SUBLANES, S5_CH), lambda j, k: (0, j))
    buf = lambda: pltpu.VMEM((S5_TC, SUBLANES, S5_CH), F32)
    return pl.pallas_call(
        _s5_lat_kernel,
        out_shape=(
            jax.ShapeDtypeStruct(y_all.shape, F32),
            jax.ShapeDtypeStruct((DEC_BATCH, nk, S5_TC, D_B), F32),
        ),
        grid=(nj, nk),
        in_specs=[
            lat(ucol, False), lat(ucol, True),
            pl.BlockSpec((2, None, LANES, 2 * S5_CH), lambda j, k: (0, j, 0, 0)),
            pl.BlockSpec((2, None, 2 * S5_CH, LANES), lambda j, k: (0, j, 0, 0)),
            vec(), vec(), vec(), vec(),
            pl.BlockSpec(memory_space=pl.ANY),
        ],
        out_specs=(
            lat(0, False),
            pl.BlockSpec((DEC_BATCH, None, S5_TC, LANES), lambda j, k: (0, nk - 1 - k, 0, j)),
        ),
        scratch_shapes=[buf(), buf(), buf(), buf(),
                        pltpu.VMEM((SUBLANES, S5_CH), F32), pltpu.VMEM((SUBLANES, S5_CH), F32)],
        input_output_aliases={8: 0},
        compiler_params=_cp(("arbitrary", "arbitrary")),
        name="s5_lat",
    )(proj4, proj4, wb, wc, a_re, a_im, h0_re, h0_im, y_all)


S5FIN_TM = 512


def _s5fin_kernel(u_ref, y_ref, yb_ref, d_ref, w_ref, b_ref, o_ref):
    def body(latent):
        y = d_ref[...] * u_ref[...] + y_ref[...]
        if latent:
            y = y + yb_ref[...]
        y = 0.5 * y * (1.0 + jnp.tanh(math.sqrt(2.0 / math.pi) * (y + 0.044715 * (y * y * y))))
        z = jnp.dot(y.astype(BF16), w_ref[...], preferred_element_type=F32) + b_ref[...]
        o_ref[...] = (y * _sigmoid(z)).astype(o_ref.dtype)

    is_latent = pl.program_id(0) >= N_CTX // S5FIN_TM
    pl.when(is_latent)(lambda: body(True))
    pl.when(jnp.logical_not(is_latent))(lambda: body(False))


def s5_finish(proj, y, y_bwd_lat, d, w_glu, b_glu):
    tm = S5FIN_TM
    n_ctx_tiles = N_CTX // tm
    return pl.pallas_call(
        _s5fin_kernel,
        out_shape=jax.ShapeDtypeStruct((T_ALL, D_B), BF16),
        grid=(T_ALL // tm,),
        in_specs=[
            pl.BlockSpec((tm, D_B), lambda i: (i, C_UB // D_B)),
            pl.BlockSpec((tm, D_B), lambda i: (i, 0)),
            pl.BlockSpec((tm, D_B), lambda i: (jnp.maximum(i - n_ctx_tiles, 0), 0)),
            pl.BlockSpec((1, D_B), lambda i: (0, 0)),
            pl.BlockSpec((D_B, D_B), lambda i: (0, 0)),
            pl.BlockSpec((1, D_B), lambda i: (0, 0)),
        ],
        out_specs=pl.BlockSpec((tm, D_B), lambda i: (i, 0)),
        compiler_params=_cp(("arbitrary",)),
        name="s5_finish",
    )(proj, y, y_bwd_lat, d, w_glu, b_glu)


def _merge_kernel(x_ref, gl_ref, oa_ref, ob_ref, oc_ref, mod_ref, g_ref, wa_ref, wb_ref, wc_ref, wo_ref,
                  wr_ref, br_ref, x1_ref, h2row_ref, idx_ref, rank_ref, wgt_ref, cnt_ref, cnt_acc):
    @pl.when(pl.program_id(0) == 0)
    def _():
        cnt_acc[...] = jnp.zeros_like(cnt_acc)

    m = None
    for br, (o_ref, w_ref) in enumerate(((oa_ref, wa_ref), (ob_ref, wb_ref), (oc_ref, wc_ref))):
        gate = _sigmoid(gl_ref[:, br * D_MODEL:(br + 1) * D_MODEL])
        t = gate * jnp.dot(o_ref[...], w_ref[...], preferred_element_type=F32)
        m = t if m is None else m + t
    x1 = x_ref[...] + mod_ref[2:3, :] * jnp.dot(m.astype(BF16), wo_ref[...], preferred_element_type=F32)
    x1_ref[...] = x1
    h2 = _rms(x1, g_ref[...])
    h2 = h2 * (1.0 + mod_ref[4:5, :]) + mod_ref[3:4, :]
    h2row_ref[...] = h2.reshape((h2.shape[0],) + ROW3)
    idx, rank, wgt, cnt = _route(h2, wr_ref, br_ref, cnt_acc[...])
    idx_ref[...] = idx
    rank_ref[...] = rank
    wgt_ref[...] = wgt
    cnt_acc[...] = cnt
    cnt_ref[...] = cnt


def merge(x, proj, oa, ob, oc, mod, g, wa, wb, wc, wo, w_router, b_router):
    tm = ROUTE_TM
    full = lambda shape: pl.BlockSpec(shape, lambda i: (0,) * len(shape))
    tile = lambda: pl.BlockSpec((tm, LANES), lambda i: (i, 0))
    return pl.pallas_call(
        _merge_kernel,
        out_shape=(jax.ShapeDtypeStruct((T_ALL, D_MODEL), F32), jax.ShapeDtypeStruct((T_ALL,) + ROW3, F32),
                   jax.ShapeDtypeStruct((T_ALL, LANES), jnp.int32), jax.ShapeDtypeStruct((T_ALL, LANES), jnp.int32),
                   jax.ShapeDtypeStruct((T_ALL, LANES), F32), jax.ShapeDtypeStruct((1, LANES), F32)),
        grid=(T_ALL // tm,),
        in_specs=[
            pl.BlockSpec((tm, D_MODEL), lambda i: (i, 0)),
            pl.BlockSpec((tm, 3 * D_MODEL), lambda i: (i, 0)),
            pl.BlockSpec((tm, 512), lambda i: (i, 0)),
            pl.BlockSpec((tm, 512), lambda i: (i, 0)),
            pl.BlockSpec((tm, 512), lambda i: (i, 0)),
            pl.BlockSpec((None, 6, D_MODEL), lambda i: (_group_of_tile(i, tm), 0, 0)),
            full((1, D_MODEL)),
            full((512, D_MODEL)), full((512, D_MODEL)), full((512, D_MODEL)), full((D_MODEL, D_MODEL)),
            full((D_MODEL, LANES)), full((1, LANES)),
        ],
        out_specs=(pl.BlockSpec((tm, D_MODEL), lambda i: (i, 0)), pl.BlockSpec((tm,) + ROW3, lambda i: (i, 0, 0)),
                   tile(), tile(), tile(), full((1, LANES))),
        scratch_shapes=[pltpu.VMEM((1, LANES), F32)],
        compiler_params=_cp(("arbitrary",)),
        name="merge",
    )(x, proj, oa, ob, oc, mod, g, wa, wb, wc, wo, w_router, b_router)


ROUTE_TM = 256


def _route(h, w_ref, b_ref, cnt):
    w = w_ref[...]
    h_hi = h.astype(BF16)
    h_lo = (h - h_hi.astype(F32)).astype(BF16)
    w_hi = w.astype(BF16)
    w_lo = (w - w_hi.astype(F32)).astype(BF16)
    dot = functools.partial(jnp.dot, preferred_element_type=F32)
    logits = dot(h_hi, w_hi) + (dot(h_hi, w_lo) + dot(h_lo, w_hi)) + b_ref[...]
    lane_i = lax.broadcasted_iota(jnp.int32, (ROUTE_TM, LANES), 1)
    lane = lane_i.astype(F32)
    r_i = lax.broadcasted_iota(jnp.int32, (ROUTE_TM, ROUTE_TM), 0)
    c_i = lax.broadcasted_iota(jnp.int32, (ROUTE_TM, ROUTE_TM), 1)
    earlier = jnp.where(c_i < r_i, 1.0, 0.0).astype(BF16)

    idx_out = jnp.zeros((ROUTE_TM, LANES), F32)
    rank_out = jnp.zeros((ROUTE_TM, LANES), F32)
    val_out = jnp.zeros((ROUTE_TM, LANES), F32)
    v0 = None
    esum = None
    for k in range(TOP_K):
        m = logits.max(axis=-1, keepdims=True)
        sel = jnp.min(jnp.where(logits == m, lane, float(LANES)), axis=-1, keepdims=True)
        hit = lane == sel
        logits = jnp.where(hit, -jnp.inf, logits)
        onehot = jnp.where(hit, 1.0, 0.0)
        within = jnp.dot(earlier, onehot.astype(BF16), preferred_element_type=F32)
        rank = jnp.sum(onehot * (within + cnt), axis=-1, keepdims=True)
        cnt = cnt + jnp.sum(onehot, axis=0, keepdims=True)
        if k == 0:
            v0 = m
        e = jnp.exp(m - v0)
        esum = e if esum is None else esum + e
        idx_out = jnp.where(lane_i == k, sel, idx_out)
        rank_out = jnp.where(lane_i == k, rank, rank_out)
        val_out = jnp.where(lane_i == k, e, val_out)
    return idx_out.astype(jnp.int32), rank_out.astype(jnp.int32), val_out * (1.0 / esum), cnt


PLAN_UNROLL = 16
N_DUMP = 5 * BLK
PLAN_LEN = BUF_LEN + BLK
assert TOP_K == 4 and T_ALL & (T_ALL - 1) == 0


def _invert_kernel(dest_ref, fill_hbm, inv_ref):
    pltpu.sync_copy(fill_hbm, inv_ref)
    tok_step = PLAN_UNROLL // TOP_K

    def put(i, c):
        for j in range(PLAN_UNROLL):
            inv_ref[dest_ref[i * PLAN_UNROLL + j]] = i * tok_step + ((j % TOP_K) * T_ALL + j // TOP_K)
        return c

    lax.fori_loop(0, N_ASSIGN // PLAN_UNROLL, put, 0)


def invert_plan(dest):
    r = jnp.arange(PLAN_LEN, dtype=jnp.int32)
    fill = N_ASSIGN + jnp.where(r < BLK, 2 * BLK + r, (r - BLK) & (2 * BLK - 1))
    return pl.pallas_call(
        _invert_kernel,
        out_shape=jax.ShapeDtypeStruct((PLAN_LEN,), jnp.int32),
        in_specs=[pl.BlockSpec(memory_space=pltpu.SMEM), pl.BlockSpec(memory_space=pl.ANY)],
        out_specs=pl.BlockSpec(memory_space=pltpu.SMEM),
        name="invert_plan",
    )(dest, fill)


PAIR_TILE = 2 * LANES


def _pair_selection():
    r = lax.broadcasted_iota(jnp.int32, (PAIR_TILE, PAIR_TILE), 0)
    c = lax.broadcasted_iota(jnp.int32, (PAIR_TILE, PAIR_TILE), 1)
    return (r == jnp.where(c < LANES, 2 * c, 2 * (c - LANES) + 1)).astype(BF16)


EXP_NBUF = 3


def _expert_kernel(be_ref, first_ref, par_ref, nxt_ref, inv_ref, h_hbm, wu_hbm, bu_ref, wd_hbm, bd_ref, sel_ref,
                   comb_hbm, xbuf, ybuf, wu_ref, wd_ref, wu_st, wd_st, gsem, ssem, wsem, *, layer):
    i = pl.program_id(0)
    n = pl.num_programs(0)
    cur = i % EXP_NBUF
    nxt = (i + 2) % EXP_NBUF
    prv = nxt

    def gather(block, s):
        base = (block + 1) * BLK
        for r in range(BLK):
            tok = inv_ref[base + r] & (T_ALL - 1)
            pltpu.make_async_copy(h_hbm.at[tok], xbuf.at[s, r], gsem.at[s]).start()

    def scatter(block, s):
        base = (block + 1) * BLK
        for r in range(BLK):
            a = inv_ref[base + r]
            pltpu.make_async_copy(ybuf.at[s, r], comb_hbm.at[a], ssem.at[s]).start()

    def wait_block(sem):
        pltpu.make_async_copy(h_hbm.at[pl.ds(0, BLK)], xbuf.at[0], sem).wait()

    def weight_copies(e, p):
        return (pltpu.make_async_copy(wu_hbm.at[layer, e], wu_st.at[p], wsem.at[p]),
                pltpu.make_async_copy(wd_hbm.at[layer, e], wd_st.at[p], wsem.at[p]))

    @pl.when(i == 0)
    def _():
        for cp in weight_copies(be_ref[0], 0):
            cp.start()
        ybuf[...] = jnp.zeros_like(ybuf)
        for s in range(EXP_NBUF - 1):
            for r in range(BLK):
                dump = N_ASSIGN + (3 + s) * BLK + r
                pltpu.make_async_copy(ybuf.at[s, r], comb_hbm.at[dump], ssem.at[s]).start()
        gather(0, 0)
        gather(1, 1)

    wait_block(gsem.at[cur])
    wait_block(ssem.at[cur])

    @pl.when(first_ref[i] == 1)
    def _():
        p = par_ref[i]
        for cp in weight_copies(be_ref[i], p):
            cp.wait()

        @pl.when(nxt_ref[i] >= 0)
        def _():
            for cp in weight_copies(nxt_ref[i], 1 - p):
                cp.start()

        sel = sel_ref[...]
        for j in range(2 * D_FF // PAIR_TILE):
            cols = slice(j * PAIR_TILE, (j + 1) * PAIR_TILE)
            wu_ref[:, cols] = jnp.dot(wu_st[p, :, cols].astype(BF16), sel, preferred_element_type=F32).astype(BF16)
        wd_ref[...] = wd_st[p].astype(BF16)

    gather(jnp.minimum(i + 2, n - 1), nxt)
    scatter(i - 1, prv)

    x = xbuf[cur].reshape(BLK, D_MODEL).astype(BF16)
    h = jnp.dot(x, wu_ref[...], preferred_element_type=F32) + bu_ref[...]
    acts = []
    for j in range(2 * D_FF // PAIR_TILE):
        glu = jnp.minimum(h[:, j * PAIR_TILE:j * PAIR_TILE + LANES], SWIGLU_LIMIT)
        lin = jnp.clip(h[:, j * PAIR_TILE + LANES:(j + 1) * PAIR_TILE], -SWIGLU_LIMIT, SWIGLU_LIMIT)
        acts.append((glu * _sigmoid(SWIGLU_ALPHA * glu) * (lin + 1.0)).astype(BF16))
    act = jnp.concatenate(acts, axis=1)
    y = jnp.dot(act, wd_ref[...], preferred_element_type=F32) + bd_ref[...]
    ybuf[cur] = y.reshape((BLK,) + ROW3)

    @pl.when(i == n - 1)
    def _():
        for s in range(EXP_NBUF):
            @pl.when(s != cur)
            def _():
                wait_block(gsem.at[s])
                wait_block(ssem.at[s])
        scatter(i, cur)
        wait_block(ssem.at[cur])


def experts(layer, block_e, inv, h2row, wu, bu, wd, bd):
    first = jnp.concatenate([jnp.ones((1,), jnp.int32), (block_e[1:] != block_e[:-1]).astype(jnp.int32)])
    slot = (jnp.cumsum(first) - 1) & 1
    later = jnp.where(block_e[None, :] > block_e[:, None], block_e[None, :], N_EXP).min(axis=1)
    nxt = jnp.where(later == N_EXP, -1, later).astype(jnp.int32)
    per_expert = lambda *blk: pl.BlockSpec((None, None) + blk, lambda i, be, *_: (layer, be[i], 0, 0))
    return pl.pallas_call(
        functools.partial(_expert_kernel, layer=layer),
        out_shape=jax.ShapeDtypeStruct((N_ASSIGN + N_DUMP,) + ROW3, F32),
        grid_spec=pltpu.PrefetchScalarGridSpec(
            num_scalar_prefetch=5,
            grid=(N_BLOCKS,),
            in_specs=[
                pl.BlockSpec(memory_space=pl.ANY),
                pl.BlockSpec(memory_space=pl.ANY), per_expert(1, 2 * D_FF),
                pl.BlockSpec(memory_space=pl.ANY), per_expert(1, D_MODEL),
                pl.BlockSpec((PAIR_TILE, PAIR_TILE), lambda i, *_: (0, 0)),
            ],
            out_specs=pl.BlockSpec(memory_space=pl.ANY),
            scratch_shapes=[
                pltpu.VMEM((EXP_NBUF, BLK) + ROW3, F32), pltpu.VMEM((EXP_NBUF, BLK) + ROW3, F32),
                pltpu.VMEM((D_MODEL, 2 * D_FF), BF16), pltpu.VMEM((D_FF, D_MODEL), BF16),
                pltpu.VMEM((2, D_MODEL, 2 * D_FF), F32), pltpu.VMEM((2, D_FF, D_MODEL), F32),
                pltpu.SemaphoreType.DMA((EXP_NBUF,)), pltpu.SemaphoreType.DMA((EXP_NBUF,)),
                pltpu.SemaphoreType.DMA((2,)),
            ],
        ),
        compiler_params=_cp(("arbitrary",)),
        name="experts",
    )(block_e, first, slot.astype(jnp.int32), nxt, inv, h2row, wu, bu, wd, bd, _pair_selection())


COMB_TM = 256


def _combine_kernel(c0_ref, c1_ref, c2_ref, c3_ref, x_ref, w_ref, mod_ref, o_ref):
    w = w_ref[...]
    acc = None
    for k, c_ref in enumerate((c0_ref, c1_ref, c2_ref, c3_ref)):
        t = w[:, k:k + 1] * c_ref[...].reshape(COMB_TM, D_MODEL)
        acc = t if acc is None else acc + t
    o_ref[...] = x_ref[...] + mod_ref[5:6, :] * acc


def combine(comb, x1, wgt, mod):
    tiles = T_ALL // COMB_TM
    kth = lambda k: pl.BlockSpec((COMB_TM,) + ROW3, lambda i: (k * tiles + i, 0, 0))
    return pl.pallas_call(
        _combine_kernel,
        out_shape=jax.ShapeDtypeStruct((T_ALL, D_MODEL), F32),
        grid=(tiles,),
        in_specs=[
            kth(0), kth(1), kth(2), kth(3),
            pl.BlockSpec((COMB_TM, D_MODEL), lambda i: (i, 0)),
            pl.BlockSpec((COMB_TM, LANES), lambda i: (i, 0)),
            pl.BlockSpec((None, 6, D_MODEL), lambda i: (_group_of_tile(i, COMB_TM), 0, 0)),
        ],
        out_specs=pl.BlockSpec((COMB_TM, D_MODEL), lambda i: (i, 0)),
        compiler_params=_cp(("arbitrary",)),
        name="combine",
    )(comb, comb, comb, comb, x1, wgt, mod)


def _rope_table():
    pos = jnp.arange(DEC_SEQ)
    row = (pos // GRID_W).astype(F32)[:, None]
    col = (pos % GRID_W).astype(F32)[:, None]

    def parts(rot_dim):
        nf = rot_dim // 4
        inv = ROPE_BASE ** (-jnp.arange(nf, dtype=F32) / nf)
        cr, sr, cc, sc = jnp.cos(row * inv), jnp.sin(row * inv), jnp.cos(col * inv), jnp.sin(col * inv)
        return jnp.concatenate([cr, cr, cc, cc], axis=1), jnp.concatenate([-sr, sr, -sc, sc], axis=1)

    ca, sa = (jnp.tile(t, (1, 2)) for t in parts(HD_A))
    cc, sc = parts(ROPE_C)
    pad = lambda t, fill: jnp.pad(t, ((0, 0), (NOPE_C, LANES - QK_C)), constant_values=fill)
    lat = jnp.concatenate([ca, sa, pad(cc, 1.0), pad(sc, 0.0)], axis=1)
    ones, zeros = jnp.ones((DEC_SEQ, LANES), F32), jnp.zeros((DEC_SEQ, LANES), F32)
    ident = jnp.concatenate([ones, zeros, ones, zeros], axis=1)
    return jnp.concatenate([ident, lat], axis=0)


def _pair_swaps():
    j = lax.broadcasted_iota(jnp.int32, (LANES, LANES), 0)
    i = lax.broadcasted_iota(jnp.int32, (LANES, LANES), 1)

    def swap(half):
        first = (i % (2 * half)) < half
        return (j == jnp.where(first, i + half, i - half)).astype(BF16)

    return jnp.stack([swap(HD_A // 4), swap(ROPE_C // 4)])


def _arrange_w_in(w_in):
    o = 0
    parts = {}
    for name, n in (("qa", HQ_A * HD_A), ("ka", HKV_A * HD_A), ("va", HKV_A * HD_A), ("ub", D_B), ("cq", Q_LORA),
                    ("ckv", KV_LORA), ("kr", ROPE_C), ("gl", 3 * D_MODEL)):
        parts[name] = w_in[..., o:o + n]
        o += n
    qa = parts["qa"].reshape(DEPTH, D_MODEL, HKV_A, G_A, 1, HD_A)
    eye = jnp.eye(HKV_A, dtype=F32).reshape(1, 1, HKV_A, 1, HKV_A, 1)
    qa_slots = (qa * eye).reshape(DEPTH, D_MODEL, HQ_A * LANES)
    kr = jnp.pad(parts["kr"], ((0, 0), (0, 0), (NOPE_C, LANES - QK_C)))
    w = jnp.concatenate([parts["gl"], qa_slots, parts["ub"], parts["cq"], parts["ka"], parts["va"], parts["ckv"], kr],
                        axis=-1)
    return w.astype(BF16)


def _s5_params(lam_re, lam_im, log_dt, b_re, b_im, c_re, c_im):
    dt = jnp.exp(log_dt)[..., None]
    decay = jnp.exp(lam_re * dt)
    ab_re, ab_im = decay * jnp.cos(lam_im * dt), decay * jnp.sin(lam_im * dt)
    den = lam_re * lam_re + lam_im * lam_im
    f_re = ((ab_re - 1) * lam_re + ab_im * lam_im) / den
    f_im = (ab_im * lam_re - (ab_re - 1) * lam_im) / den
    bb_re = f_re[..., None] * b_re - f_im[..., None] * b_im
    bb_im = f_re[..., None] * b_im + f_im[..., None] * b_re
    nj, gpb = D_B // LANES, LANES // GS_B
    eye = jnp.eye(gpb, dtype=F32)

    def blockdiag_b(bb):
        t = bb.transpose(0, 1, 3, 2).reshape(2, nj, gpb, GS_B, P_B)
        return (t[:, :, :, :, None, :] * eye[None, None, :, None, :, None]).reshape(2, nj, LANES, gpb * P_B)

    def blockdiag_c(cc):
        t = cc.transpose(0, 1, 3, 2).reshape(2, nj, gpb, P_B, GS_B)
        return (t[:, :, :, :, None, :] * eye[None, None, :, None, :, None]).reshape(2, nj, gpb * P_B, LANES)

    wb = jnp.concatenate([blockdiag_b(bb_re), blockdiag_b(bb_im)], axis=-1).astype(BF16)
    wc = jnp.concatenate([blockdiag_c(c_re), -blockdiag_c(c_im)], axis=-2).astype(BF16)
    return wb, wc, ab_re.reshape(2, G_B * P_B), ab_im.reshape(2, G_B * P_B)


def kernel(x_prompt, x_sample, c, cache_attn_k, cache_attn_v, cache_mla_ckv, cache_mla_krope, state_ssm_re, state_ssm_im, c_ctx, w_ada, b_ada, norm_mix_g, norm_ffn_g, w_in, q_norm_a, k_norm_a, sink_a, q_a_norm_c, kv_a_norm_c, w_uq_c, w_ukv_c, q_norm_c, k_norm_c, ssm_lam_re, ssm_lam_im, ssm_log_dt, ssm_b_re, ssm_b_im, ssm_c_re, ssm_c_im, ssm_d, w_glu, b_glu, w_br_a, w_br_b, w_br_c, w_out, w_router, b_router, w_up, b_up, w_down, b_down):
    x = jnp.concatenate([x_prompt.reshape(N_CTX, D_MODEL), x_sample.reshape(N_LAT, D_MODEL)], axis=0)
    cvecs = jnp.concatenate([c_ctx[None], c, jnp.zeros((N_GROUPS - 1 - DEC_BATCH, D_MODEL), F32)], axis=0)
    mods = adaln(cvecs, w_ada, b_ada)

    tab = _rope_table()
    w_in_r = _arrange_w_in(w_in)
    pad_slot = lambda g: jnp.pad(g, ((0, 0), (0, LANES - QK_C))).reshape(DEPTH, 1, LANES)
    gqa = jnp.tile(q_norm_a, (1, 2)).reshape(DEPTH, 1, LANES)
    gka = jnp.tile(k_norm_a, (1, 2)).reshape(DEPTH, 1, LANES)
    gqc, gkc = pad_slot(q_norm_c), pad_slot(k_norm_c)
    wuq = jnp.pad(w_uq_c.reshape(DEPTH, Q_LORA, H_C, QK_C), ((0, 0), (0, 0), (0, 0), (0, LANES - QK_C)))
    wuq = wuq.reshape(DEPTH, Q_LORA, H_C * LANES).astype(BF16)
    wukv4 = w_ukv_c.reshape(DEPTH, KV_LORA, H_C, NOPE_C + V_C)
    wuk = jnp.pad(wukv4[..., :NOPE_C], ((0, 0), (0, 0), (0, 0), (0, LANES - NOPE_C))).reshape(DEPTH, KV_LORA, H_C * LANES)
    wuv = wukv4[..., NOPE_C:].reshape(DEPTH, KV_LORA, H_C * V_C)
    wukv = jnp.concatenate([wuk, wuv], axis=-1).astype(BF16)
    sink = jnp.broadcast_to(sink_a[:, :, None], (DEPTH, HQ_A, LANES))
    w_router_p = jnp.pad(w_router, ((0, 0), (0, 0), (0, LANES - N_EXP)))
    b_router_p = jnp.pad(b_router, ((0, 0), (0, LANES - N_EXP)), constant_values=-jnp.inf).reshape(DEPTH, 1, LANES)
    b_up_p = b_up.reshape(DEPTH, N_EXP, 2 * D_FF // PAIR_TILE, LANES, 2).transpose(0, 1, 2, 4, 3)
    b_up_p = b_up_p.reshape(DEPTH, N_EXP, 1, 2 * D_FF)
    b_down_r = b_down.reshape(DEPTH, N_EXP, 1, D_MODEL)
    kr_cache = jnp.pad(cache_mla_krope, ((0, 0), (0, 0), (0, 0), (NOPE_C, LANES - QK_C)))

    new_k, new_v, new_ckv, new_kr, new_sre, new_sim = [], [], [], [], [], []
    for l in range(DEPTH):
        mod = mods[l]
        proj = inproj(l, x, norm_mix_g[l], mod, w_in_r)
        qa, ka, qc, kc, vc, ckvn = prep(proj, tab, gqa[l], gka[l], q_a_norm_c[l].reshape(1, Q_LORA),
                                         kv_a_norm_c[l].reshape(1, KV_LORA), gqc[l], gkc[l], wuq[l], wukv[l])
        kc_cache, vc_cache = cache_keys(cache_mla_ckv[:, l].reshape(DEC_BATCH * PAST_LEN, KV_LORA),
                                        kr_cache[:, l].reshape(DEC_BATCH * PAST_LEN, LANES), gkc[l], wukv[l])
        oa = attn_a_lat(qa, ka, proj, cache_attn_k[:, l].reshape(DEC_BATCH, PAST_LEN, LANES),
                        cache_attn_v[:, l].reshape(DEC_BATCH, PAST_LEN, LANES), sink[l],
                        attn_a_ctx(qa, ka, proj, sink[l]))
        oc = attn_c_lat(qc, kc, vc, kc_cache, vc_cache, attn_c_ctx(qc, kc, vc))
        wb, wc, a_re, a_im = _s5_params(ssm_lam_re[l], ssm_lam_im[l], ssm_log_dt[l], ssm_b_re[l], ssm_b_im[l],
                                        ssm_c_re[l], ssm_c_im[l])
        proj4 = proj.reshape(TOK_CHUNKS // CHUNK_GROUP, CHUNK_GROUP, S5_TC, N_PROJ)
        per_row = lambda a, rep: jnp.repeat(a, rep, axis=0)
        y_all, f_re, f_im = s5_ctx(proj4, wb, wc, per_row(a_re, SUBLANES).reshape(2, SUBLANES, -1),
                                   per_row(a_im, SUBLANES).reshape(2, SUBLANES, -1))
        h0 = lambda st: st[:, l].transpose(1, 0, 2, 3).reshape(2 * DEC_BATCH, G_B * P_B)
        y_all, y_bwd = s5_lat(proj4, y_all, wb, wc, per_row(a_re, DEC_BATCH), per_row(a_im, DEC_BATCH),
                              h0(state_ssm_re), h0(state_ssm_im))
        ob = s5_finish(proj, y_all.reshape(T_ALL, D_B), y_bwd.reshape(N_LAT, D_B), ssm_d[l].reshape(1, D_B),
                       w_glu[l].astype(BF16), b_glu[l].reshape(1, D_B))
        x1, h2row, idx, rank, wgt, cnt = merge(
            x, proj, oa, ob, oc, mod, norm_ffn_g[l].reshape(1, D_MODEL), w_br_a[l].astype(BF16),
            w_br_b[l].astype(BF16), w_br_c[l].astype(BF16), w_out[l].astype(BF16), w_router_p[l], b_router_p[l])
        counts = cnt[0, :N_EXP].astype(jnp.int32)
        padded = (counts + BLK - 1) // BLK * BLK
        pad_end = jnp.cumsum(padded)
        pad_start = pad_end - padded
        dest = (BLK + pad_start[idx[:, :TOP_K]] + rank[:, :TOP_K]).reshape(-1).astype(jnp.int32)
        block_start = jnp.arange(N_BLOCKS, dtype=jnp.int32) * BLK
        block_e = jnp.minimum(jnp.sum(pad_end[None, :] <= block_start[:, None], axis=1), N_EXP - 1)
        block_e = block_e.astype(jnp.int32)
        comb = experts(l, block_e, invert_plan(dest), h2row, w_up, b_up_p, w_down, b_down_r)
        x = combine(comb, x1, wgt, mod)

        new_k.append(ka[:N_CTX].reshape(BATCH, SEQ, HKV_A, HD_A))
        new_v.append(proj[:N_CTX, C_VA:C_VA + LANES].reshape(BATCH, SEQ, HKV_A, HD_A))
        new_ckv.append(ckvn[:N_CTX].reshape(BATCH, SEQ, KV_LORA))
        new_kr.append(proj[:N_CTX, C_KR + NOPE_C:C_KR + QK_C].reshape(BATCH, SEQ, ROPE_C))
        new_sre.append(f_re.reshape(2, BATCH, G_B, P_B).transpose(1, 0, 2, 3))
        new_sim.append(f_im.reshape(2, BATCH, G_B, P_B).transpose(1, 0, 2, 3))

    y_prompt = x[:N_CTX].reshape(BATCH, SEQ, D_MODEL)
    y_sample = x[N_CTX:].reshape(DEC_BATCH, DEC_SEQ, D_MODEL)
    return (y_prompt, y_sample, jnp.stack(new_k, axis=1), jnp.stack(new_v, axis=1), jnp.stack(new_ckv, axis=1),
            jnp.stack(new_kr, axis=1), jnp.stack(new_sre, axis=1), jnp.stack(new_sim, axis=1))
```

```python
import functools
import math

import jax
import jax.numpy as jnp
from jax import lax
from jax.experimental import pallas as pl
from jax.experimental.pallas import tpu as pltpu

D_MODEL = 1024
BATCH = 32
SEQ = 256
DEPTH = 2
DEC_BATCH = 4
DEC_SEQ = 2048
PAST_LEN = 256
GRID_W = 64
ROPE_BASE = 10000.0
EPS = 1e-6
NEG_INF = -1e30
BAND = 128
BLK = 128
HQ_A, HKV_A, HD_A = 8, 2, 64
G_A = HQ_A // HKV_A
WINDOW = 128
D_B, GS_B, P_B = 512, 16, 64
G_B = D_B // GS_B
H_C, Q_LORA, KV_LORA, NOPE_C, ROPE_C, V_C = 8, 256, 128, 64, 32, 64
QK_C = NOPE_C + ROPE_C
N_EXP, TOP_K, D_FF = 32, 4, 1024
SWIGLU_ALPHA, SWIGLU_LIMIT = 1.702, 7.0

N_CTX = BATCH * SEQ
N_LAT = DEC_BATCH * DEC_SEQ
T_ALL = N_CTX + N_LAT
N_GROUPS = 8

LANES = 128
SUBLANES = 8

C_GL = 0
C_QA = 3 * D_MODEL
C_UB = C_QA + HQ_A * LANES
C_CQ = C_UB + D_B
C_KA = C_CQ + Q_LORA
C_VA = C_KA + LANES
C_CKV = C_VA + LANES
C_KR = C_CKV + LANES
N_PROJ = C_KR + LANES

N_ASSIGN = T_ALL * TOP_K
N_BLOCKS = N_ASSIGN // BLK + N_EXP
BUF_LEN = N_BLOCKS * BLK

F32 = jnp.float32
BF16 = jnp.bfloat16
ROW3 = (D_MODEL // LANES, LANES)
VMEM_LIMIT = 56 * 1024 * 1024


def _cp(sem, vmem=VMEM_LIMIT):
    return pltpu.CompilerParams(dimension_semantics=sem, vmem_limit_bytes=vmem)


def _group_of_tile(i, tm):
    n_ctx_tiles = N_CTX // tm
    per_batch = DEC_SEQ // tm
    return jnp.where(i < n_ctx_tiles, 0, 1 + (i - n_ctx_tiles) // per_batch)


def _sigmoid(x):
    return 1.0 / (1.0 + jnp.exp(-x))


def _x_specs(tm, tile_of):
    n_ctx_tiles = N_CTX // tm
    return [pl.BlockSpec((tm, D_MODEL), lambda *g: (jnp.minimum(tile_of(*g), n_ctx_tiles - 1), 0)),
            pl.BlockSpec((tm, D_MODEL), lambda *g: (jnp.maximum(tile_of(*g) - n_ctx_tiles, 0), 0))]


def _x_tile(xc_ref, xl_ref, tile, tm):
    return jnp.where(tile >= N_CTX // tm, xl_ref[...], xc_ref[...])


def _adaln_kernel(c_ref, w_ref, b_ref, o_ref):
    c = c_ref[...]
    s = c * _sigmoid(c)
    o_ref[...] = jnp.dot(s.astype(BF16), w_ref[...].astype(BF16), preferred_element_type=F32) + b_ref[...]


def adaln(cvecs, w_ada, b_ada):
    tn = 1536
    out = pl.pallas_call(
        _adaln_kernel,
        out_shape=jax.ShapeDtypeStruct((DEPTH, N_GROUPS, 6 * D_MODEL), F32),
        grid=(DEPTH, 6 * D_MODEL // tn),
        in_specs=[
            pl.BlockSpec((N_GROUPS, D_MODEL), lambda l, j: (0, 0)),
            pl.BlockSpec((None, D_MODEL, tn), lambda l, j: (l, 0, j)),
            pl.BlockSpec((None, 1, tn), lambda l, j: (l, 0, j)),
        ],
        out_specs=pl.BlockSpec((None, N_GROUPS, tn), lambda l, j: (l, 0, j)),
        compiler_params=_cp(("arbitrary", "arbitrary")),
        name="adaln",
    )(cvecs, w_ada, b_ada.reshape(DEPTH, 1, 6 * D_MODEL))
    return out.reshape(DEPTH, N_GROUPS, 6, D_MODEL)


def _rms(x, g):
    ms = jnp.mean(x * x, axis=-1, keepdims=True)
    return x * lax.rsqrt(ms + EPS) * g


INPROJ_TM = 512


def _inproj_kernel(xc_ref, xl_ref, g_ref, mod_ref, w_ref, o_ref):
    h = _rms(_x_tile(xc_ref, xl_ref, pl.program_id(1), INPROJ_TM), g_ref[...])
    h = h * (1.0 + mod_ref[1:2, :]) + mod_ref[0:1, :]
    o_ref[...] = jnp.dot(h.astype(BF16), w_ref[...], preferred_element_type=F32)


def inproj(layer, x_ctx, x_lat, g, mod, w):
    tm, tn = INPROJ_TM, 1792
    return pl.pallas_call(
        _inproj_kernel,
        out_shape=jax.ShapeDtypeStruct((T_ALL, N_PROJ), F32),
        grid=(N_PROJ // tn, T_ALL // tm),
        in_specs=_x_specs(tm, lambda j, i: i) + [
            pl.BlockSpec((1, D_MODEL), lambda j, i: (0, 0)),
            pl.BlockSpec((None, 6, D_MODEL), lambda j, i: (_group_of_tile(i, tm), 0, 0)),
            pl.BlockSpec((None, D_MODEL, tn), lambda j, i: (layer, 0, j)),
        ],
        out_specs=pl.BlockSpec((tm, tn), lambda j, i: (i, j)),
        compiler_params=_cp(("arbitrary", "arbitrary")),
        name="inproj",
    )(x_ctx, x_lat, g.reshape(1, D_MODEL), mod, w)


def _rope(slab, cos, sin, swap):
    hi = slab.astype(BF16)
    lo = (slab - hi.astype(F32)).astype(BF16)
    partner = jnp.dot(hi, swap, preferred_element_type=F32) + jnp.dot(lo, swap, preferred_element_type=F32)
    return slab * cos + partner * sin


def _slot_norm(slab, gain, n_real):
    ms = jnp.sum(slab * slab, axis=-1, keepdims=True) * (1.0 / n_real)
    return slab * lax.rsqrt(ms + EPS) * gain


def _mla_keys(ckvn, kr_blk, wukv_ref, gk, rope_c):
    kv = jnp.dot(ckvn.astype(BF16), wukv_ref[...], preferred_element_type=F32)
    ks = []
    for h in range(H_C):
        slab = kv[:, h * LANES:(h + 1) * LANES] + kr_blk
        slab = _slot_norm(slab, gk, QK_C)
        if rope_c is not None:
            slab = _rope(slab, *rope_c)
        ks.append(slab.astype(BF16))
    return jnp.concatenate(ks, axis=1), kv[:, H_C * LANES:].astype(BF16)


PREP_TM = 256


def _prep_kernel(qa_ref, cq_ref, ka_ref, ckv_ref, kr_ref, tab_ref,
                 gqa_ref, gka_ref, gcq_ref, gckv_ref, gqc_ref, gkc_ref, wuq_ref, wukv_ref, swap_ref,
                 qa_o, ka_o, qc_o, kc_o, vc_o, ckvn_o):
    def body(rotate):
        if rotate:
            tab = tab_ref[...]
            rope_a = (tab[:, 0:128], tab[:, 128:256], swap_ref[0])
            rope_c = (tab[:, 256:384], tab[:, 384:512], swap_ref[1])
            rot_a = lambda x: _rope(x, *rope_a)
            rot_c = lambda x: _rope(x, *rope_c)
        else:
            rope_c = None
            rot_a = rot_c = lambda x: x
        lane = lax.broadcasted_iota(jnp.int32, (1, LANES), 1)

        gqa = gqa_ref[...]
        qa = qa_ref[...]
        outs = []
        for h in range(HQ_A):
            slab = _slot_norm(qa[:, h * LANES:(h + 1) * LANES], gqa, HD_A)
            outs.append((rot_a(slab) * (HD_A ** -0.5)).astype(BF16))
        qa_o[...] = jnp.concatenate(outs, axis=1)

        ka = ka_ref[...]
        sq = ka * ka
        lo = lane < HD_A
        ms_lo = jnp.sum(jnp.where(lo, sq, 0.0), axis=-1, keepdims=True)
        ms_hi = jnp.sum(jnp.where(lo, 0.0, sq), axis=-1, keepdims=True)
        rs = jnp.where(lo, lax.rsqrt(ms_lo * (1.0 / HD_A) + EPS), lax.rsqrt(ms_hi * (1.0 / HD_A) + EPS))
        ka_o[...] = rot_a(ka * rs * gka_ref[...])

        cqn = _rms(cq_ref[...], gcq_ref[...])
        q = jnp.dot(cqn.astype(BF16), wuq_ref[...], preferred_element_type=F32)
        gqc = gqc_ref[...]
        outs = []
        for h in range(H_C):
            slab = _slot_norm(q[:, h * LANES:(h + 1) * LANES], gqc, QK_C)
            outs.append((rot_c(slab) * (QK_C ** -0.5)).astype(BF16))
        qc_o[...] = jnp.concatenate(outs, axis=1)

        ckvn = _rms(ckv_ref[...], gckv_ref[...])
        ckvn_o[...] = ckvn
        kc, vc = _mla_keys(ckvn, kr_ref[...], wukv_ref, gkc_ref[...], rope_c)
        kc_o[...] = kc
        vc_o[...] = vc

    is_latent = pl.program_id(0) >= N_CTX // PREP_TM
    pl.when(is_latent)(lambda: body(True))
    pl.when(jnp.logical_not(is_latent))(lambda: body(False))


def prep(proj, tab, gqa, gka, gcq, gckv, gqc, gkc, wuq, wukv):
    tm = PREP_TM
    n_ctx_tiles = N_CTX // tm
    per_batch = DEC_SEQ // tm

    def tab_map(i):
        return (jnp.where(i < n_ctx_tiles, 0, per_batch + (i - n_ctx_tiles) % per_batch), 0)

    def col(width, off):
        return pl.BlockSpec((tm, width), lambda i: (i, off // width))

    def full(shape):
        return pl.BlockSpec(shape, lambda i: (0,) * len(shape))

    def row_out(width):
        return pl.BlockSpec((tm, width), lambda i: (i, 0))

    return pl.pallas_call(
        _prep_kernel,
        out_shape=(
            jax.ShapeDtypeStruct((T_ALL, HQ_A * LANES), BF16),
            jax.ShapeDtypeStruct((T_ALL, LANES), F32),
            jax.ShapeDtypeStruct((T_ALL, H_C * LANES), BF16),
            jax.ShapeDtypeStruct((T_ALL, H_C * LANES), BF16),
            jax.ShapeDtypeStruct((T_ALL, H_C * V_C), BF16),
            jax.ShapeDtypeStruct((T_ALL, KV_LORA), F32),
        ),
        grid=(T_ALL // tm,),
        in_specs=[
            col(HQ_A * LANES, C_QA), col(Q_LORA, C_CQ), col(LANES, C_KA), col(LANES, C_CKV), col(LANES, C_KR),
            pl.BlockSpec((tm, 4 * LANES), tab_map),
            full((1, LANES)), full((1, LANES)), full((1, Q_LORA)), full((1, KV_LORA)),
            full((1, LANES)), full((1, LANES)),
            full((Q_LORA, H_C * LANES)), full((KV_LORA, H_C * LANES + H_C * V_C)), full((2, LANES, LANES)),
        ],
        out_specs=(row_out(HQ_A * LANES), row_out(LANES), row_out(H_C * LANES), row_out(H_C * LANES),
                   row_out(H_C * V_C), row_out(KV_LORA)),
        compiler_params=_cp(("arbitrary",)),
        name="prep",
    )(proj, proj, proj, proj, proj, tab, gqa, gka, gcq, gckv, gqc, gkc, wuq, wukv, _pair_swaps())


def _cachekeys_kernel(ckv_ref, kr_ref, gkc_ref, wukv_ref, kc_o, vc_o):
    kc, vc = _mla_keys(ckv_ref[...], kr_ref[...], wukv_ref, gkc_ref[...], None)
    kc_o[...] = kc
    vc_o[...] = vc


def cache_keys(ckv, kr_blk, gkc, wukv):
    r = ckv.shape[0]
    tm = 256
    return pl.pallas_call(
        _cachekeys_kernel,
        out_shape=(jax.ShapeDtypeStruct((r, H_C * LANES), BF16), jax.ShapeDtypeStruct((r, H_C * V_C), BF16)),
        grid=(r // tm,),
        in_specs=[
            pl.BlockSpec((tm, LANES), lambda i: (i, 0)),
            pl.BlockSpec((tm, LANES), lambda i: (i, 0)),
            pl.BlockSpec((1, LANES), lambda i: (0, 0)),
            pl.BlockSpec((KV_LORA, H_C * LANES + H_C * V_C), lambda i: (0, 0)),
        ],
        out_specs=(pl.BlockSpec((tm, H_C * LANES), lambda i: (i, 0)), pl.BlockSpec((tm, H_C * V_C), lambda i: (i, 0))),
        compiler_params=_cp(("arbitrary",)),
        name="cache_keys",
    )(ckv, kr_blk, gkc, wukv)


def _attn_body(q_ref, segs, sink_ref, o_ref, *, n_heads, k_slot, v_slab, v_half, tq, band_qi=None):
    outs = []
    for h in range(n_heads):
        qh = q_ref[:, h * LANES:(h + 1) * LANES]
        scores = []
        for k_ref, _, off in segs:
            kh = k_ref[:, k_slot(h) * LANES:(k_slot(h) + 1) * LANES].astype(BF16)
            s = lax.dot_general(qh, kh, (((1,), (1,)), ((), ())), preferred_element_type=F32)
            if off is not None:
                tk = s.shape[1]
                blk = band_qi + off
                q_pos = band_qi * tq + lax.broadcasted_iota(jnp.int32, (tq, tk), 0)
                k_pos = blk * tk + lax.broadcasted_iota(jnp.int32, (tq, tk), 1)
                ok = (jnp.abs(k_pos - q_pos) <= WINDOW) & (blk >= 0) & (blk < DEC_SEQ // tk)
                s = jnp.where(ok, s, NEG_INF)
            scores.append(s)
        m = scores[0].max(axis=-1, keepdims=True)
        for s in scores[1:]:
            m = jnp.maximum(m, s.max(axis=-1, keepdims=True))
        if sink_ref is not None:
            sink = sink_ref[h:h + 1, 0:1]
            m = jnp.maximum(m, sink)
            denom = jnp.exp(sink - m)
        else:
            denom = jnp.zeros_like(m)
        acc = None
        for s, (_, v_ref, _) in zip(scores, segs):
            p = jnp.exp(s - m)
            denom = denom + p.sum(axis=-1, keepdims=True)
            vs = v_ref[:, v_slab(h) * LANES:(v_slab(h) + 1) * LANES].astype(BF16)
            pv = jnp.dot(p.astype(BF16), vs, preferred_element_type=F32)
            acc = pv if acc is None else acc + pv
        half = v_half(h)
        outs.append(acc[:, half * 64:(half + 1) * 64] * (1.0 / denom))
    o_ref[...] = jnp.concatenate(outs, axis=1).astype(o_ref.dtype)


_C_CFG = dict(n_heads=H_C, k_slot=lambda h: h, v_slab=lambda h: h // 2, v_half=lambda h: h % 2)


def _attn_a_body(q_ref, segs, sink_ref, o_ref, *, tq, band_qi=None):
    rows = HQ_A * tq
    q = jnp.concatenate([q_ref[:, h * LANES:(h + 1) * LANES] for h in range(HQ_A)], axis=0)
    sink = jnp.concatenate([jnp.broadcast_to(sink_ref[h:h + 1, 0:1], (tq, 1)) for h in range(HQ_A)], axis=0)
    scores = []
    for k_ref, _, off in segs:
        s = lax.dot_general(q, k_ref[...].astype(BF16), (((1,), (1,)), ((), ())), preferred_element_type=F32)
        if off is not None:
            tk = s.shape[1]
            blk = band_qi + off
            q_pos = band_qi * tq + (lax.broadcasted_iota(jnp.int32, (rows, tk), 0) & (tq - 1))
            k_pos = blk * tk + lax.broadcasted_iota(jnp.int32, (rows, tk), 1)
            ok = (jnp.abs(k_pos - q_pos) <= WINDOW) & (blk >= 0) & (blk < DEC_SEQ // tk)
            s = jnp.where(ok, s, NEG_INF)
        scores.append(s)
    m = sink
    for s in scores:
        m = jnp.maximum(m, s.max(axis=-1, keepdims=True))
    denom = jnp.exp(sink - m)
    acc = None
    for s, (_, v_ref, _) in zip(scores, segs):
        p = jnp.exp(s - m)
        denom = denom + p.sum(axis=-1, keepdims=True)
        pv = jnp.dot(p.astype(BF16), v_ref[...].astype(BF16), preferred_element_type=F32)
        acc = pv if acc is None else acc + pv
    acc = acc * (1.0 / denom)
    outs = []
    for h in range(HQ_A):
        kv = h // G_A
        outs.append(acc[h * tq:(h + 1) * tq, kv * HD_A:(kv + 1) * HD_A])
    o_ref[...] = jnp.concatenate(outs, axis=1).astype(o_ref.dtype)


def _attn_a_ctx_kernel(q_ref, k_ref, v_ref, sink_ref, o_init, o_ref):
    del o_init
    _attn_a_body(q_ref, [(k_ref, v_ref, None)], sink_ref, o_ref, tq=SEQ)


def _attn_a_lat_kernel(q_ref, k0, k1, k2, v0, v1, v2, kc_ref, vc_ref, sink_ref, o_ctx, o_ref):
    del o_ctx
    qi = pl.program_id(1)
    segs = [(k0, v0, -1), (k1, v1, 0), (k2, v2, 1), (kc_ref, vc_ref, None)]
    _attn_a_body(q_ref, segs, sink_ref, o_ref, tq=BAND, band_qi=qi)


def _attn_c_ctx_kernel(q_ref, k_ref, v_ref, o_init, o_ref):
    del o_init
    _attn_body(q_ref, [(k_ref, v_ref, None)], None, o_ref, tq=SEQ, **_C_CFG)


def _attn_c_lat_kernel(q_ref, k_ref, v_ref, kc_ref, vc_ref, o_ctx, o_ref):
    del o_ctx
    _attn_body(q_ref, [(k_ref, v_ref, None), (kc_ref, vc_ref, None)], None, o_ref, tq=256, **_C_CFG)


def attn_a_ctx(qa, ka, proj, sink):
    return pl.pallas_call(
        _attn_a_ctx_kernel,
        out_shape=jax.ShapeDtypeStruct((T_ALL, HQ_A * HD_A), BF16),
        grid=(BATCH,),
        in_specs=[
            pl.BlockSpec((SEQ, HQ_A * LANES), lambda b: (b, 0)),
            pl.BlockSpec((SEQ, LANES), lambda b: (b, 0)),
            pl.BlockSpec((SEQ, LANES), lambda b: (b, C_VA // LANES)),
            pl.BlockSpec((SUBLANES, LANES), lambda b: (0, 0)),
            pl.BlockSpec(memory_space=pl.ANY),
        ],
        out_specs=pl.BlockSpec((SEQ, HQ_A * HD_A), lambda b: (b, 0)),
        input_output_aliases={4: 0},
        compiler_params=_cp(("arbitrary",)),
        name="attn_a_ctx",
    )(qa, ka, proj, sink, jnp.zeros((T_ALL, HQ_A * HD_A), BF16))


def attn_a_lat(qa, ka, proj, k_cache, v_cache, sink, o_all):
    nb = DEC_SEQ // BAND
    base = N_CTX // BAND

    def band(off, colblk):
        return pl.BlockSpec((BAND, LANES), lambda b, i: (base + b * nb + jnp.clip(i + off, 0, nb - 1), colblk))

    return pl.pallas_call(
        _attn_a_lat_kernel,
        out_shape=jax.ShapeDtypeStruct((T_ALL, HQ_A * HD_A), BF16),
        grid=(DEC_BATCH, nb),
        in_specs=[
            pl.BlockSpec((BAND, HQ_A * LANES), lambda b, i: (base + b * nb + i, 0)),
            band(-1, 0), band(0, 0), band(1, 0),
            band(-1, C_VA // LANES), band(0, C_VA // LANES), band(1, C_VA // LANES),
            pl.BlockSpec((None, PAST_LEN, LANES), lambda b, i: (b, 0, 0)),
            pl.BlockSpec((None, PAST_LEN, LANES), lambda b, i: (b, 0, 0)),
            pl.BlockSpec((SUBLANES, LANES), lambda b, i: (0, 0)),
            pl.BlockSpec(memory_space=pl.ANY),
        ],
        out_specs=pl.BlockSpec((BAND, HQ_A * HD_A), lambda b, i: (base + b * nb + i, 0)),
        input_output_aliases={10: 0},
        compiler_params=_cp(("arbitrary", "arbitrary")),
        name="attn_a_lat",
    )(qa, ka, ka, ka, proj, proj, proj, k_cache, v_cache, sink, o_all)


def attn_c_ctx(qc, kc, vc):
    return pl.pallas_call(
        _attn_c_ctx_kernel,
        out_shape=jax.ShapeDtypeStruct((T_ALL, H_C * V_C), BF16),
        grid=(BATCH,),
        in_specs=[
            pl.BlockSpec((SEQ, H_C * LANES), lambda b: (b, 0)),
            pl.BlockSpec((SEQ, H_C * LANES), lambda b: (b, 0)),
            pl.BlockSpec((SEQ, H_C * V_C), lambda b: (b, 0)),
            pl.BlockSpec(memory_space=pl.ANY),
        ],
        out_specs=pl.BlockSpec((SEQ, H_C * V_C), lambda b: (b, 0)),
        input_output_aliases={3: 0},
        compiler_params=_cp(("arbitrary",)),
        name="attn_c_ctx",
    )(qc, kc, vc, jnp.zeros((T_ALL, H_C * V_C), BF16))


def attn_c_lat(qc, kc, vc, kc_cache, vc_cache, o_all):
    tq = 256
    nq = DEC_SEQ // tq
    qbase = N_CTX // tq
    kbase = N_CTX // DEC_SEQ
    return pl.pallas_call(
        _attn_c_lat_kernel,
        out_shape=jax.ShapeDtypeStruct((T_ALL, H_C * V_C), BF16),
        grid=(DEC_BATCH, nq),
        in_specs=[
            pl.BlockSpec((tq, H_C * LANES), lambda b, i: (qbase + b * nq + i, 0)),
            pl.BlockSpec((DEC_SEQ, H_C * LANES), lambda b, i: (kbase + b, 0)),
            pl.BlockSpec((DEC_SEQ, H_C * V_C), lambda b, i: (kbase + b, 0)),
            pl.BlockSpec((PAST_LEN, H_C * LANES), lambda b, i: (b, 0)),
            pl.BlockSpec((PAST_LEN, H_C * V_C), lambda b, i: (b, 0)),
            pl.BlockSpec(memory_space=pl.ANY),
        ],
        out_specs=pl.BlockSpec((tq, H_C * V_C), lambda b, i: (qbase + b * nq + i, 0)),
        input_output_aliases={5: 0},
        compiler_params=_cp(("arbitrary", "arbitrary")),
        name="attn_c_lat",
    )(qc, kc, vc, kc_cache, vc_cache, o_all)


S5_TC = 256
S5_CH = 512
S5_UNROLL = 8
TOK_CHUNKS = T_ALL // S5_TC
CHUNK_GROUP = 8
assert SEQ == S5_TC and DEC_SEQ == CHUNK_GROUP * S5_TC and BATCH == 4 * CHUNK_GROUP


def _s5_bproj(u2, wb_ref, pick_first):
    def part(lo, hi):
        r = jnp.dot(u2, wb_ref[0, :, lo:hi], preferred_element_type=F32)
        if pick_first is not None:
            r = jnp.where(pick_first, r, jnp.dot(u2, wb_ref[1, :, lo:hi], preferred_element_type=F32))
        return r.reshape(S5_TC, SUBLANES, S5_CH)
    return part(0, S5_CH), part(S5_CH, 2 * S5_CH)


def _s5_cproj(xre, xim, wc_ref, pick_first):
    rows = S5_TC * SUBLANES
    xr = xre[...].reshape(rows, S5_CH).astype(BF16)
    xi = xim[...].reshape(rows, S5_CH).astype(BF16)

    def part(d):
        return (jnp.dot(xr, wc_ref[d, 0:S5_CH, :], preferred_element_type=F32)
                + jnp.dot(xi, wc_ref[d, S5_CH:, :], preferred_element_type=F32))

    y = part(0)
    if pick_first is not None:
        y = jnp.where(pick_first, y, part(1))
    return jnp.swapaxes(y.reshape(S5_TC, SUBLANES, LANES), 0, 1)


def _s5_ctx_kernel(u_ref, wb_ref, wc_ref, are_ref, aim_ref, y_init, y_ref, fre_ref, fim_ref, bre, bim):
    del y_init
    d = pl.program_id(2)
    rows = S5_TC * SUBLANES
    u2 = jnp.swapaxes(u_ref[...], 0, 1).reshape(rows, LANES).astype(BF16)
    b_re, b_im = _s5_bproj(u2, wb_ref, None)
    bre[...] = b_re
    bim[...] = b_im
    a_re = are_ref[...]
    a_im = aim_ref[...]

    def outer(i, carry):
        sr, si = carry
        for j in range(S5_UNROLL):
            t0 = i * S5_UNROLL + j
            t = jnp.where(d == 1, S5_TC - 1 - t0, t0)
            nr = a_re * sr - a_im * si + bre[t]
            ni = a_re * si + a_im * sr + bim[t]
            bre[t] = nr
            bim[t] = ni
            sr, si = nr, ni
        return sr, si

    zero = jnp.zeros((SUBLANES, S5_CH), F32)
    sr, si = lax.fori_loop(0, S5_TC // S5_UNROLL, outer, (zero, zero))
    fre_ref[...] = sr
    fim_ref[...] = si
    y = _s5_cproj(bre, bim, wc_ref, None)

    @pl.when(d == 0)
    def _():
        y_ref[...] = y

    @pl.when(d == 1)
    def _():
        y_ref[...] += y


def s5_ctx(proj4, wb, wc, a_re, a_im):
    nj = D_B // LANES
    ucol = C_UB // LANES
    return pl.pallas_call(
        _s5_ctx_kernel,
        out_shape=(
            jax.ShapeDtypeStruct((TOK_CHUNKS // CHUNK_GROUP, CHUNK_GROUP, S5_TC, D_B), F32),
            jax.ShapeDtypeStruct((2, BATCH, G_B * P_B), F32),
            jax.ShapeDtypeStruct((2, BATCH, G_B * P_B), F32),
        ),
        grid=(BATCH // CHUNK_GROUP, nj, 2),
        in_specs=[
            pl.BlockSpec((None, CHUNK_GROUP, S5_TC, LANES), lambda sb, j, d: (sb, 0, 0, ucol + j)),
            pl.BlockSpec((1, None, LANES, 2 * S5_CH), lambda sb, j, d: (d, j, 0, 0)),
            pl.BlockSpec((1, None, 2 * S5_CH, LANES), lambda sb, j, d: (d, j, 0, 0)),
            pl.BlockSpec((None, SUBLANES, S5_CH), lambda sb, j, d: (d, 0, j)),
            pl.BlockSpec((None, SUBLANES, S5_CH), lambda sb, j, d: (d, 0, j)),
            pl.BlockSpec(memory_space=pl.ANY),
        ],
        out_specs=(
            pl.BlockSpec((None, CHUNK_GROUP, S5_TC, LANES), lambda sb, j, d: (sb, 0, 0, j)),
            pl.BlockSpec((None, SUBLANES, S5_CH), lambda sb, j, d: (d, sb, j)),
            pl.BlockSpec((None, SUBLANES, S5_CH), lambda sb, j, d: (d, sb, j)),
        ),
        scratch_shapes=[pltpu.VMEM((S5_TC, SUBLANES, S5_CH), F32), pltpu.VMEM((S5_TC, SUBLANES, S5_CH), F32)],
        input_output_aliases={5: 0},
        compiler_params=_cp(("arbitrary", "arbitrary", "arbitrary")),
        name="s5_ctx",
    )(proj4, wb, wc, a_re, a_im, jnp.zeros((TOK_CHUNKS // CHUNK_GROUP, CHUNK_GROUP, S5_TC, D_B), F32))


def _s5_lat_kernel(uf_ref, ub_ref, wb_ref, wc_ref, are_ref, aim_ref, h0re_ref, h0im_ref, y_ctx,
                   yf_ref, yb_ref, bre, bim, xre, xim, sre, sim):
    del y_ctx
    k = pl.program_id(1)

    @pl.when(k == 0)
    def _():
        sre[...] = h0re_ref[...]
        sim[...] = h0im_ref[...]

    rows = S5_TC * SUBLANES
    half = SUBLANES // 2
    u8 = jnp.concatenate([uf_ref[...], ub_ref[...]], axis=0)
    u2 = jnp.swapaxes(u8, 0, 1).reshape(rows, LANES).astype(BF16)
    fwd_rows = (lax.broadcasted_iota(jnp.int32, (rows, 1), 0) % SUBLANES) < half
    b_re, b_im = _s5_bproj(u2, wb_ref, fwd_rows)
    bre[...] = b_re
    bim[...] = b_im
    a_re = are_ref[...]
    a_im = aim_ref[...]
    fwd8 = lax.broadcasted_iota(jnp.int32, (SUBLANES, 1), 0) < half

    def outer(i, carry):
        sr, si = carry
        for j in range(S5_UNROLL):
            t = i * S5_UNROLL + j
            tb = S5_TC - 1 - t
            nr = a_re * sr - a_im * si + jnp.where(fwd8, bre[t], bre[tb])
            ni = a_re * si + a_im * sr + jnp.where(fwd8, bim[t], bim[tb])
            xre[t, 0:half, :] = nr[0:half]
            xre[tb, half:, :] = nr[half:]
            xim[t, 0:half, :] = ni[0:half]
            xim[tb, half:, :] = ni[half:]
            sr, si = nr, ni
        return sr, si

    sr, si = lax.fori_loop(0, S5_TC // S5_UNROLL, outer, (sre[...], sim[...]))
    sre[...] = sr
    sim[...] = si
    y = _s5_cproj(xre, xim, wc_ref, fwd_rows)
    yf_ref[...] = y[0:half]
    yb_ref[...] = y[half:]


def s5_lat(proj4, y_all, wb, wc, a_re, a_im, h0_re, h0_im):
    nj = D_B // LANES
    nk = DEC_SEQ // S5_TC
    ucol = C_UB // LANES
    lat = lambda col0, rev: pl.BlockSpec(
        (DEC_BATCH, None, S5_TC, LANES), lambda j, k: (1, (nk - 1 - k) if rev else k, 0, col0 + j))
    vec = lambda: pl.BlockSpec((SUBLANES, S5_CH), lambda j, k: (0, j))
    buf = lambda: pltpu.VMEM((S5_TC, SUBLANES, S5_CH), F32)
    return pl.pallas_call(
        _s5_lat_kernel,
        out_shape=(
            jax.ShapeDtypeStruct(y_all.shape, F32),
            jax.ShapeDtypeStruct((DEC_BATCH, nk, S5_TC, D_B), F32),
        ),
        grid=(nj, nk),
        in_specs=[
            lat(ucol, False), lat(ucol, True),
            pl.BlockSpec((2, None, LANES, 2 * S5_CH), lambda j, k: (0, j, 0, 0)),
            pl.BlockSpec((2, None, 2 * S5_CH, LANES), lambda j, k: (0, j, 0, 0)),
            vec(), vec(), vec(), vec(),
            pl.BlockSpec(memory_space=pl.ANY),
        ],
        out_specs=(
            lat(0, False),
            pl.BlockSpec((DEC_BATCH, None, S5_TC, LANES), lambda j, k: (0, nk - 1 - k, 0, j)),
        ),
        scratch_shapes=[buf(), buf(), buf(), buf(),
                        pltpu.VMEM((SUBLANES, S5_CH), F32), pltpu.VMEM((SUBLANES, S5_CH), F32)],
        input_output_aliases={8: 0},
        compiler_params=_cp(("arbitrary", "arbitrary")),
        name="s5_lat",
    )(proj4, proj4, wb, wc, a_re, a_im, h0_re, h0_im, y_all)


S5FIN_TM = 512


def _s5fin_kernel(u_ref, y_ref, yb_ref, d_ref, w_ref, b_ref, o_ref):
    def body(latent):
        y = d_ref[...] * u_ref[...] + y_ref[...]
        if latent:
            y = y + yb_ref[...]
        y = 0.5 * y * (1.0 + jnp.tanh(math.sqrt(2.0 / math.pi) * (y + 0.044715 * (y * y * y))))
        z = jnp.dot(y.astype(BF16), w_ref[...], preferred_element_type=F32) + b_ref[...]
        o_ref[...] = (y * _sigmoid(z)).astype(o_ref.dtype)

    is_latent = pl.program_id(0) >= N_CTX // S5FIN_TM
    pl.when(is_latent)(lambda: body(True))
    pl.when(jnp.logical_not(is_latent))(lambda: body(False))


def s5_finish(proj, y, y_bwd_lat, d, w_glu, b_glu):
    tm = S5FIN_TM
    n_ctx_tiles = N_CTX // tm
    return pl.pallas_call(
        _s5fin_kernel,
        out_shape=jax.ShapeDtypeStruct((T_ALL, D_B), BF16),
        grid=(T_ALL // tm,),
        in_specs=[
            pl.BlockSpec((tm, D_B), lambda i: (i, C_UB // D_B)),
            pl.BlockSpec((tm, D_B), lambda i: (i, 0)),
            pl.BlockSpec((tm, D_B), lambda i: (jnp.maximum(i - n_ctx_tiles, 0), 0)),
            pl.BlockSpec((1, D_B), lambda i: (0, 0)),
            pl.BlockSpec((D_B, D_B), lambda i: (0, 0)),
            pl.BlockSpec((1, D_B), lambda i: (0, 0)),
        ],
        out_specs=pl.BlockSpec((tm, D_B), lambda i: (i, 0)),
        compiler_params=_cp(("arbitrary",)),
        name="s5_finish",
    )(proj, y, y_bwd_lat, d, w_glu, b_glu)


def _merge_kernel(xc_ref, xl_ref, gl_ref, oa_ref, ob_ref, oc_ref, mod_ref, g_ref, wa_ref, wb_ref, wc_ref, wo_ref,
                  wr_ref, br_ref, x1_ref, h2row_ref, idx_ref, rank_ref, wgt_ref, cnt_ref, cnt_acc):
    @pl.when(pl.program_id(0) == 0)
    def _():
        cnt_acc[...] = jnp.zeros_like(cnt_acc)

    m = None
    for br, (o_ref, w_ref) in enumerate(((oa_ref, wa_ref), (ob_ref, wb_ref), (oc_ref, wc_ref))):
        gate = _sigmoid(gl_ref[:, br * D_MODEL:(br + 1) * D_MODEL])
        t = gate * jnp.dot(o_ref[...], w_ref[...], preferred_element_type=F32)
        m = t if m is None else m + t
    x = _x_tile(xc_ref, xl_ref, pl.program_id(0), ROUTE_TM)
    x1 = x + mod_ref[2:3, :] * jnp.dot(m.astype(BF16), wo_ref[...], preferred_element_type=F32)
    x1_ref[...] = x1
    h2 = _rms(x1, g_ref[...])
    h2 = h2 * (1.0 + mod_ref[4:5, :]) + mod_ref[3:4, :]
    h2row_ref[...] = h2.reshape((h2.shape[0],) + ROW3)
    idx, rank, wgt, cnt = _route(h2, wr_ref, br_ref, cnt_acc[...])
    idx_ref[...] = idx
    rank_ref[...] = rank
    wgt_ref[...] = wgt
    cnt_acc[...] = cnt
    cnt_ref[...] = cnt


def merge(x_ctx, x_lat, proj, oa, ob, oc, mod, g, wa, wb, wc, wo, w_router, b_router):
    tm = ROUTE_TM
    full = lambda shape: pl.BlockSpec(shape, lambda i: (0,) * len(shape))
    tile = lambda: pl.BlockSpec((tm, LANES), lambda i: (i, 0))
    return pl.pallas_call(
        _merge_kernel,
        out_shape=(jax.ShapeDtypeStruct((T_ALL, D_MODEL), F32), jax.ShapeDtypeStruct((T_ALL,) + ROW3, F32),
                   jax.ShapeDtypeStruct((T_ALL, LANES), jnp.int32), jax.ShapeDtypeStruct((T_ALL, LANES), jnp.int32),
                   jax.ShapeDtypeStruct((T_ALL, LANES), F32), jax.ShapeDtypeStruct((1, LANES), F32)),
        grid=(T_ALL // tm,),
        in_specs=_x_specs(tm, lambda i: i) + [
            pl.BlockSpec((tm, 3 * D_MODEL), lambda i: (i, 0)),
            pl.BlockSpec((tm, 512), lambda i: (i, 0)),
            pl.BlockSpec((tm, 512), lambda i: (i, 0)),
            pl.BlockSpec((tm, 512), lambda i: (i, 0)),
            pl.BlockSpec((None, 6, D_MODEL), lambda i: (_group_of_tile(i, tm), 0, 0)),
            full((1, D_MODEL)),
            full((512, D_MODEL)), full((512, D_MODEL)), full((512, D_MODEL)), full((D_MODEL, D_MODEL)),
            full((D_MODEL, LANES)), full((1, LANES)),
        ],
        out_specs=(pl.BlockSpec((tm, D_MODEL), lambda i: (i, 0)), pl.BlockSpec((tm,) + ROW3, lambda i: (i, 0, 0)),
                   tile(), tile(), tile(), full((1, LANES))),
        scratch_shapes=[pltpu.VMEM((1, LANES), F32)],
        compiler_params=_cp(("arbitrary",)),
        name="merge",
    )(x_ctx, x_lat, proj, oa, ob, oc, mod, g, wa, wb, wc, wo, w_router, b_router)


ROUTE_TM = 256


def _route(h, w_ref, b_ref, cnt):
    w = w_ref[...]
    h_hi = h.astype(BF16)
    h_lo = (h - h_hi.astype(F32)).astype(BF16)
    w_hi = w.astype(BF16)
    w_lo = (w - w_hi.astype(F32)).astype(BF16)
    dot = functools.partial(jnp.dot, preferred_element_type=F32)
    logits = dot(h_hi, w_hi) + (dot(h_hi, w_lo) + dot(h_lo, w_hi)) + b_ref[...]
    lane_i = lax.broadcasted_iota(jnp.int32, (ROUTE_TM, LANES), 1)
    lane = lane_i.astype(F32)
    r_i = lax.broadcasted_iota(jnp.int32, (ROUTE_TM, ROUTE_TM), 0)
    c_i = lax.broadcasted_iota(jnp.int32, (ROUTE_TM, ROUTE_TM), 1)
    earlier = jnp.where(c_i < r_i, 1.0, 0.0).astype(BF16)

    idx_out = jnp.zeros((ROUTE_TM, LANES), F32)
    rank_out = jnp.zeros((ROUTE_TM, LANES), F32)
    val_out = jnp.zeros((ROUTE_TM, LANES), F32)
    v0 = None
    esum = None
    for k in range(TOP_K):
        m = logits.max(axis=-1, keepdims=True)
        sel = jnp.min(jnp.where(logits == m, lane, float(LANES)), axis=-1, keepdims=True)
        hit = lane == sel
        logits = jnp.where(hit, -jnp.inf, logits)
        onehot = jnp.where(hit, 1.0, 0.0)
        within = jnp.dot(earlier, onehot.astype(BF16), preferred_element_type=F32)
        rank = jnp.sum(onehot * (within + cnt), axis=-1, keepdims=True)
        cnt = cnt + jnp.sum(onehot, axis=0, keepdims=True)
        if k == 0:
            v0 = m
        e = jnp.exp(m - v0)
        esum = e if esum is None else esum + e
        idx_out = jnp.where(lane_i == k, sel, idx_out)
        rank_out = jnp.where(lane_i == k, rank, rank_out)
        val_out = jnp.where(lane_i == k, e, val_out)
    return idx_out.astype(jnp.int32), rank_out.astype(jnp.int32), val_out * (1.0 / esum), cnt


PLAN_UNROLL = 16
N_DUMP = 5 * BLK
PLAN_LEN = BUF_LEN + BLK
assert TOP_K == 4 and T_ALL & (T_ALL - 1) == 0


def _invert_kernel(dest_ref, fill_hbm, inv_ref):
    pltpu.sync_copy(fill_hbm, inv_ref)
    tok_step = PLAN_UNROLL // TOP_K

    def put(i, c):
        for j in range(PLAN_UNROLL):
            inv_ref[dest_ref[i * PLAN_UNROLL + j]] = i * tok_step + ((j % TOP_K) * T_ALL + j // TOP_K)
        return c

    lax.fori_loop(0, N_ASSIGN // PLAN_UNROLL, put, 0)


def invert_plan(dest):
    r = jnp.arange(PLAN_LEN, dtype=jnp.int32)
    fill = N_ASSIGN + jnp.where(r < BLK, 2 * BLK + r, (r - BLK) & (2 * BLK - 1))
    return pl.pallas_call(
        _invert_kernel,
        out_shape=jax.ShapeDtypeStruct((PLAN_LEN,), jnp.int32),
        in_specs=[pl.BlockSpec(memory_space=pltpu.SMEM), pl.BlockSpec(memory_space=pl.ANY)],
        out_specs=pl.BlockSpec(memory_space=pltpu.SMEM),
        name="invert_plan",
    )(dest, fill)


PAIR_TILE = 2 * LANES


def _pair_selection():
    r = lax.broadcasted_iota(jnp.int32, (PAIR_TILE, PAIR_TILE), 0)
    c = lax.broadcasted_iota(jnp.int32, (PAIR_TILE, PAIR_TILE), 1)
    return (r == jnp.where(c < LANES, 2 * c, 2 * (c - LANES) + 1)).astype(BF16)


EXP_NBUF = 3


def _expert_kernel(be_ref, first_ref, par_ref, nxt_ref, inv_ref, h_hbm, wu_hbm, bu_ref, wd_hbm, bd_ref, sel_ref,
                   comb_hbm, xbuf, ybuf, wu_ref, wd_ref, wu_st, wd_st, gsem, ssem, wsem, *, layer):
    i = pl.program_id(0)
    n = pl.num_programs(0)
    cur = i % EXP_NBUF
    nxt = (i + 2) % EXP_NBUF
    prv = nxt

    def gather(block, s):
        base = (block + 1) * BLK
        for r in range(BLK):
            tok = inv_ref[base + r] & (T_ALL - 1)
            pltpu.make_async_copy(h_hbm.at[tok], xbuf.at[s, r], gsem.at[s]).start()

    def scatter(block, s):
        base = (block + 1) * BLK
        for r in range(BLK):
            a = inv_ref[base + r]
            pltpu.make_async_copy(ybuf.at[s, r], comb_hbm.at[a], ssem.at[s]).start()

    def wait_block(sem):
        pltpu.make_async_copy(h_hbm.at[pl.ds(0, BLK)], xbuf.at[0], sem).wait()

    def weight_copies(e, p):
        return (pltpu.make_async_copy(wu_hbm.at[layer, e], wu_st.at[p], wsem.at[p]),
                pltpu.make_async_copy(wd_hbm.at[layer, e], wd_st.at[p], wsem.at[p]))

    @pl.when(i == 0)
    def _():
        for cp in weight_copies(be_ref[0], 0):
            cp.start()
        ybuf[...] = jnp.zeros_like(ybuf)
        for s in range(EXP_NBUF - 1):
            for r in range(BLK):
                dump = N_ASSIGN + (3 + s) * BLK + r
                pltpu.make_async_copy(ybuf.at[s, r], comb_hbm.at[dump], ssem.at[s]).start()
        gather(0, 0)
        gather(1, 1)

    wait_block(gsem.at[cur])
    wait_block(ssem.at[cur])

    @pl.when(first_ref[i] == 1)
    def _():
        p = par_ref[i]
        for cp in weight_copies(be_ref[i], p):
            cp.wait()

        @pl.when(nxt_ref[i] >= 0)
        def _():
            for cp in weight_copies(nxt_ref[i], 1 - p):
                cp.start()

        sel = sel_ref[...]
        for j in range(2 * D_FF // PAIR_TILE):
            cols = slice(j * PAIR_TILE, (j + 1) * PAIR_TILE)
            wu_ref[:, cols] = jnp.dot(wu_st[p, :, cols].astype(BF16), sel, preferred_element_type=F32).astype(BF16)
        wd_ref[...] = wd_st[p].astype(BF16)

    gather(jnp.minimum(i + 2, n - 1), nxt)
    scatter(i - 1, prv)

    x = xbuf[cur].reshape(BLK, D_MODEL).astype(BF16)
    h = jnp.dot(x, wu_ref[...], preferred_element_type=F32) + bu_ref[...]
    acts = []
    for j in range(2 * D_FF // PAIR_TILE):
        glu = jnp.minimum(h[:, j * PAIR_TILE:j * PAIR_TILE + LANES], SWIGLU_LIMIT)
        lin = jnp.clip(h[:, j * PAIR_TILE + LANES:(j + 1) * PAIR_TILE], -SWIGLU_LIMIT, SWIGLU_LIMIT)
        acts.append((glu * _sigmoid(SWIGLU_ALPHA * glu) * (lin + 1.0)).astype(BF16))
    act = jnp.concatenate(acts, axis=1)
    y = jnp.dot(act, wd_ref[...], preferred_element_type=F32) + bd_ref[...]
    ybuf[cur] = y.reshape((BLK,) + ROW3)

    @pl.when(i == n - 1)
    def _():
        for s in range(EXP_NBUF):
            @pl.when(s != cur)
            def _():
                wait_block(gsem.at[s])
                wait_block(ssem.at[s])
        scatter(i, cur)
        wait_block(ssem.at[cur])


def experts(layer, block_e, inv, h2row, wu, bu, wd, bd):
    first = jnp.concatenate([jnp.ones((1,), jnp.int32), (block_e[1:] != block_e[:-1]).astype(jnp.int32)])
    slot = (jnp.cumsum(first) - 1) & 1
    later = jnp.where(block_e[None, :] > block_e[:, None], block_e[None, :], N_EXP).min(axis=1)
    nxt = jnp.where(later == N_EXP, -1, later).astype(jnp.int32)
    per_expert = lambda *blk: pl.BlockSpec((None, None) + blk, lambda i, be, *_: (layer, be[i], 0, 0))
    return pl.pallas_call(
        functools.partial(_expert_kernel, layer=layer),
        out_shape=jax.ShapeDtypeStruct((N_ASSIGN + N_DUMP,) + ROW3, F32),
        grid_spec=pltpu.PrefetchScalarGridSpec(
            num_scalar_prefetch=5,
            grid=(N_BLOCKS,),
            in_specs=[
                pl.BlockSpec(memory_space=pl.ANY),
                pl.BlockSpec(memory_space=pl.ANY), per_expert(1, 2 * D_FF),
                pl.BlockSpec(memory_space=pl.ANY), per_expert(1, D_MODEL),
                pl.BlockSpec((PAIR_TILE, PAIR_TILE), lambda i, *_: (0, 0)),
            ],
            out_specs=pl.BlockSpec(memory_space=pl.ANY),
            scratch_shapes=[
                pltpu.VMEM((EXP_NBUF, BLK) + ROW3, F32), pltpu.VMEM((EXP_NBUF, BLK) + ROW3, F32),
                pltpu.VMEM((D_MODEL, 2 * D_FF), BF16), pltpu.VMEM((D_FF, D_MODEL), BF16),
                pltpu.VMEM((2, D_MODEL, 2 * D_FF), F32), pltpu.VMEM((2, D_FF, D_MODEL), F32),
                pltpu.SemaphoreType.DMA((EXP_NBUF,)), pltpu.SemaphoreType.DMA((EXP_NBUF,)),
                pltpu.SemaphoreType.DMA((2,)),
            ],
        ),
        compiler_params=_cp(("arbitrary",)),
        name="experts",
    )(block_e, first, slot.astype(jnp.int32), nxt, inv, h2row, wu, bu, wd, bd, _pair_selection())


COMB_TM = 256


def _combine_kernel(c0_ref, c1_ref, c2_ref, c3_ref, x_ref, w_ref, mod_ref, oc_ref, ol_ref):
    w = w_ref[...]
    acc = None
    for k, c_ref in enumerate((c0_ref, c1_ref, c2_ref, c3_ref)):
        t = w[:, k:k + 1] * c_ref[...].reshape(COMB_TM, D_MODEL)
        acc = t if acc is None else acc + t
    out = x_ref[...] + mod_ref[5:6, :] * acc
    is_latent = pl.program_id(0) >= N_CTX // COMB_TM

    @pl.when(is_latent)
    def _():
        ol_ref[...] = out

    @pl.when(jnp.logical_not(is_latent))
    def _():
        oc_ref[...] = out


def combine(comb, x1, wgt, mod):
    tiles = T_ALL // COMB_TM
    kth = lambda k: pl.BlockSpec((COMB_TM,) + ROW3, lambda i: (k * tiles + i, 0, 0))
    return pl.pallas_call(
        _combine_kernel,
        out_shape=(jax.ShapeDtypeStruct((N_CTX, D_MODEL), F32), jax.ShapeDtypeStruct((N_LAT, D_MODEL), F32)),
        grid=(tiles,),
        in_specs=[
            kth(0), kth(1), kth(2), kth(3),
            pl.BlockSpec((COMB_TM, D_MODEL), lambda i: (i, 0)),
            pl.BlockSpec((COMB_TM, LANES), lambda i: (i, 0)),
            pl.BlockSpec((None, 6, D_MODEL), lambda i: (_group_of_tile(i, COMB_TM), 0, 0)),
        ],
        out_specs=tuple(_x_specs(COMB_TM, lambda i: i)),
        compiler_params=_cp(("arbitrary",)),
        name="combine",
    )(comb, comb, comb, comb, x1, wgt, mod)


def _rope_table():
    pos = jnp.arange(DEC_SEQ)
    row = (pos // GRID_W).astype(F32)[:, None]
    col = (pos % GRID_W).astype(F32)[:, None]

    def parts(rot_dim):
        nf = rot_dim // 4
        inv = ROPE_BASE ** (-jnp.arange(nf, dtype=F32) / nf)
        cr, sr, cc, sc = jnp.cos(row * inv), jnp.sin(row * inv), jnp.cos(col * inv), jnp.sin(col * inv)
        return jnp.concatenate([cr, cr, cc, cc], axis=1), jnp.concatenate([-sr, sr, -sc, sc], axis=1)

    ca, sa = (jnp.tile(t, (1, 2)) for t in parts(HD_A))
    cc, sc = parts(ROPE_C)
    pad = lambda t, fill: jnp.pad(t, ((0, 0), (NOPE_C, LANES - QK_C)), constant_values=fill)
    lat = jnp.concatenate([ca, sa, pad(cc, 1.0), pad(sc, 0.0)], axis=1)
    ones, zeros = jnp.ones((DEC_SEQ, LANES), F32), jnp.zeros((DEC_SEQ, LANES), F32)
    ident = jnp.concatenate([ones, zeros, ones, zeros], axis=1)
    return jnp.concatenate([ident, lat], axis=0)


def _pair_swaps():
    j = lax.broadcasted_iota(jnp.int32, (LANES, LANES), 0)
    i = lax.broadcasted_iota(jnp.int32, (LANES, LANES), 1)

    def swap(half):
        first = (i % (2 * half)) < half
        return (j == jnp.where(first, i + half, i - half)).astype(BF16)

    return jnp.stack([swap(HD_A // 4), swap(ROPE_C // 4)])


def _arrange_w_in(w_in):
    o = 0
    parts = {}
    for name, n in (("qa", HQ_A * HD_A), ("ka", HKV_A * HD_A), ("va", HKV_A * HD_A), ("ub", D_B), ("cq", Q_LORA),
                    ("ckv", KV_LORA), ("kr", ROPE_C), ("gl", 3 * D_MODEL)):
        parts[name] = w_in[..., o:o + n]
        o += n
    qa = parts["qa"].reshape(DEPTH, D_MODEL, HKV_A, G_A, 1, HD_A)
    eye = jnp.eye(HKV_A, dtype=F32).reshape(1, 1, HKV_A, 1, HKV_A, 1)
    qa_slots = (qa * eye).reshape(DEPTH, D_MODEL, HQ_A * LANES)
    kr = jnp.pad(parts["kr"], ((0, 0), (0, 0), (NOPE_C, LANES - QK_C)))
    w = jnp.concatenate([parts["gl"], qa_slots, parts["ub"], parts["cq"], parts["ka"], parts["va"], parts["ckv"], kr],
                        axis=-1)
    return w.astype(BF16)


def _s5_params(lam_re, lam_im, log_dt, b_re, b_im, c_re, c_im):
    dt = jnp.exp(log_dt)[..., None]
    decay = jnp.exp(lam_re * dt)
    ab_re, ab_im = decay * jnp.cos(lam_im * dt), decay * jnp.sin(lam_im * dt)
    den = lam_re * lam_re + lam_im * lam_im
    f_re = ((ab_re - 1) * lam_re + ab_im * lam_im) / den
    f_im = (ab_im * lam_re - (ab_re - 1) * lam_im) / den
    bb_re = f_re[..., None] * b_re - f_im[..., None] * b_im
    bb_im = f_re[..., None] * b_im + f_im[..., None] * b_re
    nj, gpb = D_B // LANES, LANES // GS_B
    eye = jnp.eye(gpb, dtype=F32)

    def blockdiag_b(bb):
        t = bb.transpose(0, 1, 3, 2).reshape(2, nj, gpb, GS_B, P_B)
        return (t[:, :, :, :, None, :] * eye[None, None, :, None, :, None]).reshape(2, nj, LANES, gpb * P_B)

    def blockdiag_c(cc):
        t = cc.transpose(0, 1, 3, 2).reshape(2, nj, gpb, P_B, GS_B)
        return (t[:, :, :, :, None, :] * eye[None, None, :, None, :, None]).reshape(2, nj, gpb * P_B, LANES)

    wb = jnp.concatenate([blockdiag_b(bb_re), blockdiag_b(bb_im)], axis=-1).astype(BF16)
    wc = jnp.concatenate([blockdiag_c(c_re), -blockdiag_c(c_im)], axis=-2).astype(BF16)
    return wb, wc, ab_re.reshape(2, G_B * P_B), ab_im.reshape(2, G_B * P_B)


def kernel(x_prompt, x_sample, c, cache_attn_k, cache_attn_v, cache_mla_ckv, cache_mla_krope, state_ssm_re, state_ssm_im, c_ctx, w_ada, b_ada, norm_mix_g, norm_ffn_g, w_in, q_norm_a, k_norm_a, sink_a, q_a_norm_c, kv_a_norm_c, w_uq_c, w_ukv_c, q_norm_c, k_norm_c, ssm_lam_re, ssm_lam_im, ssm_log_dt, ssm_b_re, ssm_b_im, ssm_c_re, ssm_c_im, ssm_d, w_glu, b_glu, w_br_a, w_br_b, w_br_c, w_out, w_router, b_router, w_up, b_up, w_down, b_down):
    x_ctx, x_lat = x_prompt.reshape(N_CTX, D_MODEL), x_sample.reshape(N_LAT, D_MODEL)
    cvecs = jnp.concatenate([c_ctx[None], c, jnp.zeros((N_GROUPS - 1 - DEC_BATCH, D_MODEL), F32)], axis=0)
    mods = adaln(cvecs, w_ada, b_ada)

    tab = _rope_table()
    w_in_r = _arrange_w_in(w_in)
    pad_slot = lambda g: jnp.pad(g, ((0, 0), (0, LANES - QK_C))).reshape(DEPTH, 1, LANES)
    gqa = jnp.tile(q_norm_a, (1, 2)).reshape(DEPTH, 1, LANES)
    gka = jnp.tile(k_norm_a, (1, 2)).reshape(DEPTH, 1, LANES)
    gqc, gkc = pad_slot(q_norm_c), pad_slot(k_norm_c)
    wuq = jnp.pad(w_uq_c.reshape(DEPTH, Q_LORA, H_C, QK_C), ((0, 0), (0, 0), (0, 0), (0, LANES - QK_C)))
    wuq = wuq.reshape(DEPTH, Q_LORA, H_C * LANES).astype(BF16)
    wukv4 = w_ukv_c.reshape(DEPTH, KV_LORA, H_C, NOPE_C + V_C)
    wuk = jnp.pad(wukv4[..., :NOPE_C], ((0, 0), (0, 0), (0, 0), (0, LANES - NOPE_C))).reshape(DEPTH, KV_LORA, H_C * LANES)
    wuv = wukv4[..., NOPE_C:].reshape(DEPTH, KV_LORA, H_C * V_C)
    wukv = jnp.concatenate([wuk, wuv], axis=-1).astype(BF16)
    sink = jnp.broadcast_to(sink_a[:, :, None], (DEPTH, HQ_A, LANES))
    w_router_p = jnp.pad(w_router, ((0, 0), (0, 0), (0, LANES - N_EXP)))
    b_router_p = jnp.pad(b_router, ((0, 0), (0, LANES - N_EXP)), constant_values=-jnp.inf).reshape(DEPTH, 1, LANES)
    b_up_p = b_up.reshape(DEPTH, N_EXP, 2 * D_FF // PAIR_TILE, LANES, 2).transpose(0, 1, 2, 4, 3)
    b_up_p = b_up_p.reshape(DEPTH, N_EXP, 1, 2 * D_FF)
    b_down_r = b_down.reshape(DEPTH, N_EXP, 1, D_MODEL)
    kr_cache = jnp.pad(cache_mla_krope, ((0, 0), (0, 0), (0, 0), (NOPE_C, LANES - QK_C)))

    new_k, new_v, new_ckv, new_kr, new_sre, new_sim = [], [], [], [], [], []
    for l in range(DEPTH):
        mod = mods[l]
        proj = inproj(l, x_ctx, x_lat, norm_mix_g[l], mod, w_in_r)
        qa, ka, qc, kc, vc, ckvn = prep(proj, tab, gqa[l], gka[l], q_a_norm_c[l].reshape(1, Q_LORA),
                                         kv_a_norm_c[l].reshape(1, KV_LORA), gqc[l], gkc[l], wuq[l], wukv[l])
        kc_cache, vc_cache = cache_keys(cache_mla_ckv[:, l].reshape(DEC_BATCH * PAST_LEN, KV_LORA),
                                        kr_cache[:, l].reshape(DEC_BATCH * PAST_LEN, LANES), gkc[l], wukv[l])
        oa = attn_a_lat(qa, ka, proj, cache_attn_k[:, l].reshape(DEC_BATCH, PAST_LEN, LANES),
                        cache_attn_v[:, l].reshape(DEC_BATCH, PAST_LEN, LANES), sink[l],
                        attn_a_ctx(qa, ka, proj, sink[l]))
        oc = attn_c_lat(qc, kc, vc, kc_cache, vc_cache, attn_c_ctx(qc, kc, vc))
        wb, wc, a_re, a_im = _s5_params(ssm_lam_re[l], ssm_lam_im[l], ssm_log_dt[l], ssm_b_re[l], ssm_b_im[l],
                                        ssm_c_re[l], ssm_c_im[l])
        proj4 = proj.reshape(TOK_CHUNKS // CHUNK_GROUP, CHUNK_GROUP, S5_TC, N_PROJ)
        per_row = lambda a, rep: jnp.repeat(a, rep, axis=0)
        y_all, f_re, f_im = s5_ctx(proj4, wb, wc, per_row(a_re, SUBLANES).reshape(2, SUBLANES, -1),
                                   per_row(a_im, SUBLANES).reshape(2, SUBLANES, -1))
        h0 = lambda st: st[:, l].transpose(1, 0, 2, 3).reshape(2 * DEC_BATCH, G_B * P_B)
        y_all, y_bwd = s5_lat(proj4, y_all, wb, wc, per_row(a_re, DEC_BATCH), per_row(a_im, DEC_BATCH),
                              h0(state_ssm_re), h0(state_ssm_im))
        ob = s5_finish(proj, y_all.reshape(T_ALL, D_B), y_bwd.reshape(N_LAT, D_B), ssm_d[l].reshape(1, D_B),
                       w_glu[l].astype(BF16), b_glu[l].reshape(1, D_B))
        x1, h2row, idx, rank, wgt, cnt = merge(
            x_ctx, x_lat, proj, oa, ob, oc, mod, norm_ffn_g[l].reshape(1, D_MODEL), w_br_a[l].astype(BF16),
            w_br_b[l].astype(BF16), w_br_c[l].astype(BF16), w_out[l].astype(BF16), w_router_p[l], b_router_p[l])
        counts = cnt[0, :N_EXP].astype(jnp.int32)
        padded = (counts + BLK - 1) // BLK * BLK
        pad_end = jnp.cumsum(padded)
        pad_start = pad_end - padded
        dest = (BLK + pad_start[idx[:, :TOP_K]] + rank[:, :TOP_K]).reshape(-1).astype(jnp.int32)
        block_start = jnp.arange(N_BLOCKS, dtype=jnp.int32) * BLK
        block_e = jnp.minimum(jnp.sum(pad_end[None, :] <= block_start[:, None], axis=1), N_EXP - 1)
        block_e = block_e.astype(jnp.int32)
        comb = experts(l, block_e, invert_plan(dest), h2row, w_up, b_up_p, w_down, b_down_r)
        x_ctx, x_lat = combine(comb, x1, wgt, mod)

        new_k.append(ka[:N_CTX].reshape(BATCH, SEQ, HKV_A, HD_A))
        new_v.append(proj[:N_CTX, C_VA:C_VA + LANES].reshape(BATCH, SEQ, HKV_A, HD_A))
        new_ckv.append(ckvn[:N_CTX].reshape(BATCH, SEQ, KV_LORA))
        new_kr.append(proj[:N_CTX, C_KR + NOPE_C:C_KR + QK_C].reshape(BATCH, SEQ, ROPE_C))
        new_sre.append(f_re.reshape(2, BATCH, G_B, P_B).transpose(1, 0, 2, 3))
        new_sim.append(f_im.reshape(2, BATCH, G_B, P_B).transpose(1, 0, 2, 3))

    y_prompt = x_ctx.reshape(BATCH, SEQ, D_MODEL)
    y_sample = x_lat.reshape(DEC_BATCH, DEC_SEQ, D_MODEL)
    return (y_prompt, y_sample, jnp.stack(new_k, axis=1), jnp.stack(new_v, axis=1), jnp.stack(new_ckv, axis=1),
            jnp.stack(new_kr, axis=1), jnp.stack(new_sre, axis=1), jnp.stack(new_sim, axis=1))
```

```python
import functools
import math

import jax
import jax.numpy as jnp
from jax import lax
from jax.experimental import pallas as pl
from jax.experimental.pallas import tpu as pltpu

D_MODEL = 1024
BATCH = 32
SEQ = 256
DEPTH = 2
DEC_BATCH = 4
DEC_SEQ = 2048
PAST_LEN = 256
GRID_W = 64
ROPE_BASE = 10000.0
EPS = 1e-6
NEG_INF = -1e30
BAND = 128
BLK = 128
HQ_A, HKV_A, HD_A = 8, 2, 64
G_A = HQ_A // HKV_A
WINDOW = 128
D_B, GS_B, P_B = 512, 16, 64
G_B = D_B // GS_B
H_C, Q_LORA, KV_LORA, NOPE_C, ROPE_C, V_C = 8, 256, 128, 64, 32, 64
QK_C = NOPE_C + ROPE_C
N_EXP, TOP_K, D_FF = 32, 4, 1024
SWIGLU_ALPHA, SWIGLU_LIMIT = 1.702, 7.0

N_CTX = BATCH * SEQ
N_LAT = DEC_BATCH * DEC_SEQ
T_ALL = N_CTX + N_LAT
N_GROUPS = 8

LANES = 128
SUBLANES = 8

C_GL = 0
C_QA = 3 * D_MODEL
C_UB = C_QA + HQ_A * LANES
C_CQ = C_UB + D_B
C_KA = C_CQ + Q_LORA
C_VA = C_KA + LANES
C_CKV = C_VA + LANES
C_KR = C_CKV + LANES
N_PROJ = C_KR + LANES

N_ASSIGN = T_ALL * TOP_K
N_BLOCKS = N_ASSIGN // BLK + N_EXP
BUF_LEN = N_BLOCKS * BLK

F32 = jnp.float32
BF16 = jnp.bfloat16
ROW3 = (D_MODEL // LANES, LANES)
VMEM_LIMIT = 56 * 1024 * 1024


def _cp(sem, vmem=VMEM_LIMIT):
    return pltpu.CompilerParams(dimension_semantics=sem, vmem_limit_bytes=vmem)


def _group_of_tile(i, tm):
    n_ctx_tiles = N_CTX // tm
    per_batch = DEC_SEQ // tm
    return jnp.where(i < n_ctx_tiles, 0, 1 + (i - n_ctx_tiles) // per_batch)


def _sigmoid(x):
    return 1.0 / (1.0 + jnp.exp(-x))


def _x_specs(tm, tile_of):
    n_ctx_tiles = N_CTX // tm
    return [pl.BlockSpec((tm, D_MODEL), lambda *g: (jnp.minimum(tile_of(*g), n_ctx_tiles - 1), 0)),
            pl.BlockSpec((tm, D_MODEL), lambda *g: (jnp.maximum(tile_of(*g) - n_ctx_tiles, 0), 0))]


def _x_tile(xc_ref, xl_ref, tile, tm):
    return jnp.where(tile >= N_CTX // tm, xl_ref[...], xc_ref[...])


def _adaln_kernel(c_ref, w_ref, b_ref, o_ref):
    c = c_ref[...]
    s = c * _sigmoid(c)
    o_ref[...] = jnp.dot(s.astype(BF16), w_ref[...].astype(BF16), preferred_element_type=F32) + b_ref[...]


def adaln(cvecs, w_ada, b_ada):
    tn = 1536
    out = pl.pallas_call(
        _adaln_kernel,
        out_shape=jax.ShapeDtypeStruct((DEPTH, N_GROUPS, 6 * D_MODEL), F32),
        grid=(DEPTH, 6 * D_MODEL // tn),
        in_specs=[
            pl.BlockSpec((N_GROUPS, D_MODEL), lambda l, j: (0, 0)),
            pl.BlockSpec((None, D_MODEL, tn), lambda l, j: (l, 0, j)),
            pl.BlockSpec((None, 1, tn), lambda l, j: (l, 0, j)),
        ],
        out_specs=pl.BlockSpec((None, N_GROUPS, tn), lambda l, j: (l, 0, j)),
        compiler_params=_cp(("arbitrary", "arbitrary")),
        name="adaln",
    )(cvecs, w_ada, b_ada.reshape(DEPTH, 1, 6 * D_MODEL))
    return out.reshape(DEPTH, N_GROUPS, 6, D_MODEL)


def _rms(x, g):
    ms = jnp.mean(x * x, axis=-1, keepdims=True)
    return x * lax.rsqrt(ms + EPS) * g


INPROJ_TM = 512


def _inproj_kernel(xc_ref, xl_ref, g_ref, mod_ref, w_ref, o_ref):
    h = _rms(_x_tile(xc_ref, xl_ref, pl.program_id(1), INPROJ_TM), g_ref[...])
    h = h * (1.0 + mod_ref[1:2, :]) + mod_ref[0:1, :]
    o_ref[...] = jnp.dot(h.astype(BF16), w_ref[...], preferred_element_type=F32)


def inproj(layer, x_ctx, x_lat, g, mod, w):
    tm, tn = INPROJ_TM, 1792
    return pl.pallas_call(
        _inproj_kernel,
        out_shape=jax.ShapeDtypeStruct((T_ALL, N_PROJ), F32),
        grid=(N_PROJ // tn, T_ALL // tm),
        in_specs=_x_specs(tm, lambda j, i: i) + [
            pl.BlockSpec((1, D_MODEL), lambda j, i: (0, 0)),
            pl.BlockSpec((None, 6, D_MODEL), lambda j, i: (_group_of_tile(i, tm), 0, 0)),
            pl.BlockSpec((None, D_MODEL, tn), lambda j, i: (layer, 0, j)),
        ],
        out_specs=pl.BlockSpec((tm, tn), lambda j, i: (i, j)),
        compiler_params=_cp(("arbitrary", "arbitrary")),
        name="inproj",
    )(x_ctx, x_lat, g.reshape(1, D_MODEL), mod, w)


def _rope(slab, cos, sin, swap):
    hi = slab.astype(BF16)
    lo = (slab - hi.astype(F32)).astype(BF16)
    partner = jnp.dot(hi, swap, preferred_element_type=F32) + jnp.dot(lo, swap, preferred_element_type=F32)
    return slab * cos + partner * sin


def _slot_norm(slab, gain, n_real):
    ms = jnp.sum(slab * slab, axis=-1, keepdims=True) * (1.0 / n_real)
    return slab * lax.rsqrt(ms + EPS) * gain


def _mla_keys(ckvn, kr_blk, wukv_ref, gk, rope_c):
    kv = jnp.dot(ckvn.astype(BF16), wukv_ref[...], preferred_element_type=F32)
    ks = []
    for h in range(H_C):
        slab = kv[:, h * LANES:(h + 1) * LANES] + kr_blk
        slab = _slot_norm(slab, gk, QK_C)
        if rope_c is not None:
            slab = _rope(slab, *rope_c)
        ks.append(slab.astype(BF16))
    return jnp.concatenate(ks, axis=1), kv[:, H_C * LANES:].astype(BF16)


PREP_TM = 256


def _prep_kernel(qa_ref, cq_ref, ka_ref, ckv_ref, kr_ref, tab_ref,
                 gqa_ref, gka_ref, gcq_ref, gckv_ref, gqc_ref, gkc_ref, wuq_ref, wukv_ref, swap_ref,
                 qa_o, ka_o, qc_o, kc_o, vc_o, ckvn_o):
    def body(rotate):
        if rotate:
            tab = tab_ref[...]
            rope_a = (tab[:, 0:128], tab[:, 128:256], swap_ref[0])
            rope_c = (tab[:, 256:384], tab[:, 384:512], swap_ref[1])
            rot_a = lambda x: _rope(x, *rope_a)
            rot_c = lambda x: _rope(x, *rope_c)
        else:
            rope_c = None
            rot_a = rot_c = lambda x: x
        lane = lax.broadcasted_iota(jnp.int32, (1, LANES), 1)

        gqa = gqa_ref[...]
        qa = qa_ref[...]
        outs = []
        for h in range(HQ_A):
            slab = _slot_norm(qa[:, h * LANES:(h + 1) * LANES], gqa, HD_A)
            outs.append((rot_a(slab) * (HD_A ** -0.5)).astype(BF16))
        qa_o[...] = jnp.concatenate(outs, axis=1)

        ka = ka_ref[...]
        sq = ka * ka
        lo = lane < HD_A
        ms_lo = jnp.sum(jnp.where(lo, sq, 0.0), axis=-1, keepdims=True)
        ms_hi = jnp.sum(jnp.where(lo, 0.0, sq), axis=-1, keepdims=True)
        rs = jnp.where(lo, lax.rsqrt(ms_lo * (1.0 / HD_A) + EPS), lax.rsqrt(ms_hi * (1.0 / HD_A) + EPS))
        ka_o[...] = rot_a(ka * rs * gka_ref[...])

        cqn = _rms(cq_ref[...], gcq_ref[...])
        q = jnp.dot(cqn.astype(BF16), wuq_ref[...], preferred_element_type=F32)
        gqc = gqc_ref[...]
        outs = []
        for h in range(H_C):
            slab = _slot_norm(q[:, h * LANES:(h + 1) * LANES], gqc, QK_C)
            outs.append((rot_c(slab) * (QK_C ** -0.5)).astype(BF16))
        qc_o[...] = jnp.concatenate(outs, axis=1)

        ckvn = _rms(ckv_ref[...], gckv_ref[...])
        ckvn_o[...] = ckvn
        kc, vc = _mla_keys(ckvn, kr_ref[...], wukv_ref, gkc_ref[...], rope_c)
        kc_o[...] = kc
        vc_o[...] = vc

    is_latent = pl.program_id(0) >= N_CTX // PREP_TM
    pl.when(is_latent)(lambda: body(True))
    pl.when(jnp.logical_not(is_latent))(lambda: body(False))


def prep(proj, tab, gqa, gka, gcq, gckv, gqc, gkc, wuq, wukv):
    tm = PREP_TM
    n_ctx_tiles = N_CTX // tm
    per_batch = DEC_SEQ // tm

    def tab_map(i):
        return (jnp.where(i < n_ctx_tiles, 0, per_batch + (i - n_ctx_tiles) % per_batch), 0)

    def col(width, off):
        return pl.BlockSpec((tm, width), lambda i: (i, off // width))

    def full(shape):
        return pl.BlockSpec(shape, lambda i: (0,) * len(shape))

    def row_out(width):
        return pl.BlockSpec((tm, width), lambda i: (i, 0))

    return pl.pallas_call(
        _prep_kernel,
        out_shape=(
            jax.ShapeDtypeStruct((T_ALL, HQ_A * LANES), BF16),
            jax.ShapeDtypeStruct((T_ALL, LANES), F32),
            jax.ShapeDtypeStruct((T_ALL, H_C * LANES), BF16),
            jax.ShapeDtypeStruct((T_ALL, H_C * LANES), BF16),
            jax.ShapeDtypeStruct((T_ALL, H_C * V_C), BF16),
            jax.ShapeDtypeStruct((T_ALL, KV_LORA), F32),
        ),
        grid=(T_ALL // tm,),
        in_specs=[
            col(HQ_A * LANES, C_QA), col(Q_LORA, C_CQ), col(LANES, C_KA), col(LANES, C_CKV), col(LANES, C_KR),
            pl.BlockSpec((tm, 4 * LANES), tab_map),
            full((1, LANES)), full((1, LANES)), full((1, Q_LORA)), full((1, KV_LORA)),
            full((1, LANES)), full((1, LANES)),
            full((Q_LORA, H_C * LANES)), full((KV_LORA, H_C * LANES + H_C * V_C)), full((2, LANES, LANES)),
        ],
        out_specs=(row_out(HQ_A * LANES), row_out(LANES), row_out(H_C * LANES), row_out(H_C * LANES),
                   row_out(H_C * V_C), row_out(KV_LORA)),
        compiler_params=_cp(("arbitrary",)),
        name="prep",
    )(proj, proj, proj, proj, proj, tab, gqa, gka, gcq, gckv, gqc, gkc, wuq, wukv, _pair_swaps())


def _cachekeys_kernel(ckv_ref, kr_ref, gkc_ref, wukv_ref, kc_o, vc_o):
    kc, vc = _mla_keys(ckv_ref[...], kr_ref[...], wukv_ref, gkc_ref[...], None)
    kc_o[...] = kc
    vc_o[...] = vc


def cache_keys(ckv, kr_blk, gkc, wukv):
    r = ckv.shape[0]
    tm = 256
    return pl.pallas_call(
        _cachekeys_kernel,
        out_shape=(jax.ShapeDtypeStruct((r, H_C * LANES), BF16), jax.ShapeDtypeStruct((r, H_C * V_C), BF16)),
        grid=(r // tm,),
        in_specs=[
            pl.BlockSpec((tm, LANES), lambda i: (i, 0)),
            pl.BlockSpec((tm, LANES), lambda i: (i, 0)),
            pl.BlockSpec((1, LANES), lambda i: (0, 0)),
            pl.BlockSpec((KV_LORA, H_C * LANES + H_C * V_C), lambda i: (0, 0)),
        ],
        out_specs=(pl.BlockSpec((tm, H_C * LANES), lambda i: (i, 0)), pl.BlockSpec((tm, H_C * V_C), lambda i: (i, 0))),
        compiler_params=_cp(("arbitrary",)),
        name="cache_keys",
    )(ckv, kr_blk, gkc, wukv)


def _attn_body(q_ref, segs, sink_ref, o_ref, *, n_heads, k_slot, v_slab, v_half, tq, band_qi=None):
    outs = []
    for h in range(n_heads):
        qh = q_ref[:, h * LANES:(h + 1) * LANES]
        scores = []
        for k_ref, _, off in segs:
            kh = k_ref[:, k_slot(h) * LANES:(k_slot(h) + 1) * LANES].astype(BF16)
            s = lax.dot_general(qh, kh, (((1,), (1,)), ((), ())), preferred_element_type=F32)
            if off is not None:
                tk = s.shape[1]
                blk = band_qi + off
                q_pos = band_qi * tq + lax.broadcasted_iota(jnp.int32, (tq, tk), 0)
                k_pos = blk * tk + lax.broadcasted_iota(jnp.int32, (tq, tk), 1)
                ok = (jnp.abs(k_pos - q_pos) <= WINDOW) & (blk >= 0) & (blk < DEC_SEQ // tk)
                s = jnp.where(ok, s, NEG_INF)
            scores.append(s)
        m = scores[0].max(axis=-1, keepdims=True)
        for s in scores[1:]:
            m = jnp.maximum(m, s.max(axis=-1, keepdims=True))
        if sink_ref is not None:
            sink = sink_ref[h:h + 1, 0:1]
            m = jnp.maximum(m, sink)
            denom = jnp.exp(sink - m)
        else:
            denom = jnp.zeros_like(m)
        acc = None
        for s, (_, v_ref, _) in zip(scores, segs):
            p = jnp.exp(s - m)
            denom = denom + p.sum(axis=-1, keepdims=True)
            vs = v_ref[:, v_slab(h) * LANES:(v_slab(h) + 1) * LANES].astype(BF16)
            pv = jnp.dot(p.astype(BF16), vs, preferred_element_type=F32)
            acc = pv if acc is None else acc + pv
        half = v_half(h)
        outs.append(acc[:, half * 64:(half + 1) * 64] * (1.0 / denom))
    o_ref[...] = jnp.concatenate(outs, axis=1).astype(o_ref.dtype)


ATT_CTX_SEQS = 4
_C_CFG = dict(n_heads=H_C, k_slot=lambda h: h, v_slab=lambda h: h // 2, v_half=lambda h: h % 2)


def _attn_a_body(q_ref, segs, sink_ref, o_ref, *, tq, band_qi=None):
    rows = HQ_A * tq
    q = jnp.concatenate([q_ref[:, h * LANES:(h + 1) * LANES] for h in range(HQ_A)], axis=0)
    sink = jnp.concatenate([jnp.broadcast_to(sink_ref[h:h + 1, 0:1], (tq, 1)) for h in range(HQ_A)], axis=0)
    scores = []
    for k_ref, _, off in segs:
        s = lax.dot_general(q, k_ref[...].astype(BF16), (((1,), (1,)), ((), ())), preferred_element_type=F32)
        if off is not None:
            tk = s.shape[1]
            blk = band_qi + off
            q_pos = band_qi * tq + (lax.broadcasted_iota(jnp.int32, (rows, tk), 0) & (tq - 1))
            k_pos = blk * tk + lax.broadcasted_iota(jnp.int32, (rows, tk), 1)
            ok = (jnp.abs(k_pos - q_pos) <= WINDOW) & (blk >= 0) & (blk < DEC_SEQ // tk)
            s = jnp.where(ok, s, NEG_INF)
        scores.append(s)
    m = sink
    for s in scores:
        m = jnp.maximum(m, s.max(axis=-1, keepdims=True))
    denom = jnp.exp(sink - m)
    acc = None
    for s, (_, v_ref, _) in zip(scores, segs):
        p = jnp.exp(s - m)
        denom = denom + p.sum(axis=-1, keepdims=True)
        pv = jnp.dot(p.astype(BF16), v_ref[...].astype(BF16), preferred_element_type=F32)
        acc = pv if acc is None else acc + pv
    acc = acc * (1.0 / denom)
    outs = []
    for h in range(HQ_A):
        kv = h // G_A
        outs.append(acc[h * tq:(h + 1) * tq, kv * HD_A:(kv + 1) * HD_A])
    o_ref[...] = jnp.concatenate(outs, axis=1).astype(o_ref.dtype)


def _attn_a_ctx_kernel(q_ref, k_ref, v_ref, sink_ref, o_init, o_ref):
    del o_init
    for b in range(ATT_CTX_SEQS):
        rows = pl.ds(b * SEQ, SEQ)
        _attn_a_body(q_ref.at[rows], [(k_ref.at[rows], v_ref.at[rows], None)], sink_ref, o_ref.at[rows], tq=SEQ)


def _attn_a_lat_kernel(q_ref, k0, k1, k2, v0, v1, v2, kc_ref, vc_ref, sink_ref, o_ctx, o_ref):
    del o_ctx
    qi = pl.program_id(1)
    segs = [(k0, v0, -1), (k1, v1, 0), (k2, v2, 1), (kc_ref, vc_ref, None)]
    _attn_a_body(q_ref, segs, sink_ref, o_ref, tq=BAND, band_qi=qi)


def _attn_c_ctx_kernel(q_ref, k_ref, v_ref, o_init, o_ref):
    del o_init
    for b in range(ATT_CTX_SEQS):
        rows = pl.ds(b * SEQ, SEQ)
        _attn_body(q_ref.at[rows], [(k_ref.at[rows], v_ref.at[rows], None)], None, o_ref.at[rows], tq=SEQ, **_C_CFG)


def _attn_c_lat_kernel(q_ref, k_ref, v_ref, kc_ref, vc_ref, o_ctx, o_ref):
    del o_ctx
    _attn_body(q_ref, [(k_ref, v_ref, None), (kc_ref, vc_ref, None)], None, o_ref, tq=256, **_C_CFG)


def attn_a_ctx(qa, ka, proj, sink):
    return pl.pallas_call(
        _attn_a_ctx_kernel,
        out_shape=jax.ShapeDtypeStruct((T_ALL, HQ_A * HD_A), BF16),
        grid=(BATCH // ATT_CTX_SEQS,),
        in_specs=[
            pl.BlockSpec((ATT_CTX_SEQS * SEQ, HQ_A * LANES), lambda b: (b, 0)),
            pl.BlockSpec((ATT_CTX_SEQS * SEQ, LANES), lambda b: (b, 0)),
            pl.BlockSpec((ATT_CTX_SEQS * SEQ, LANES), lambda b: (b, C_VA // LANES)),
            pl.BlockSpec((SUBLANES, LANES), lambda b: (0, 0)),
            pl.BlockSpec(memory_space=pl.ANY),
        ],
        out_specs=pl.BlockSpec((ATT_CTX_SEQS * SEQ, HQ_A * HD_A), lambda b: (b, 0)),
        input_output_aliases={4: 0},
        compiler_params=_cp(("arbitrary",)),
        name="attn_a_ctx",
    )(qa, ka, proj, sink, jnp.zeros((T_ALL, HQ_A * HD_A), BF16))


def attn_a_lat(qa, ka, proj, k_cache, v_cache, sink, o_all):
    nb = DEC_SEQ // BAND
    base = N_CTX // BAND

    def band(off, colblk):
        return pl.BlockSpec((BAND, LANES), lambda b, i: (base + b * nb + jnp.clip(i + off, 0, nb - 1), colblk))

    return pl.pallas_call(
        _attn_a_lat_kernel,
        out_shape=jax.ShapeDtypeStruct((T_ALL, HQ_A * HD_A), BF16),
        grid=(DEC_BATCH, nb),
        in_specs=[
            pl.BlockSpec((BAND, HQ_A * LANES), lambda b, i: (base + b * nb + i, 0)),
            band(-1, 0), band(0, 0), band(1, 0),
            band(-1, C_VA // LANES), band(0, C_VA // LANES), band(1, C_VA // LANES),
            pl.BlockSpec((None, PAST_LEN, LANES), lambda b, i: (b, 0, 0)),
            pl.BlockSpec((None, PAST_LEN, LANES), lambda b, i: (b, 0, 0)),
            pl.BlockSpec((SUBLANES, LANES), lambda b, i: (0, 0)),
            pl.BlockSpec(memory_space=pl.ANY),
        ],
        out_specs=pl.BlockSpec((BAND, HQ_A * HD_A), lambda b, i: (base + b * nb + i, 0)),
        input_output_aliases={10: 0},
        compiler_params=_cp(("arbitrary", "arbitrary")),
        name="attn_a_lat",
    )(qa, ka, ka, ka, proj, proj, proj, k_cache, v_cache, sink, o_all)


def attn_c_ctx(qc, kc, vc):
    return pl.pallas_call(
        _attn_c_ctx_kernel,
        out_shape=jax.ShapeDtypeStruct((T_ALL, H_C * V_C), BF16),
        grid=(BATCH // ATT_CTX_SEQS,),
        in_specs=[
            pl.BlockSpec((ATT_CTX_SEQS * SEQ, H_C * LANES), lambda b: (b, 0)),
            pl.BlockSpec((ATT_CTX_SEQS * SEQ, H_C * LANES), lambda b: (b, 0)),
            pl.BlockSpec((ATT_CTX_SEQS * SEQ, H_C * V_C), lambda b: (b, 0)),
            pl.BlockSpec(memory_space=pl.ANY),
        ],
        out_specs=pl.BlockSpec((ATT_CTX_SEQS * SEQ, H_C * V_C), lambda b: (b, 0)),
        input_output_aliases={3: 0},
        compiler_params=_cp(("arbitrary",)),
        name="attn_c_ctx",
    )(qc, kc, vc, jnp.zeros((T_ALL, H_C * V_C), BF16))


def attn_c_lat(qc, kc, vc, kc_cache, vc_cache, o_all):
    tq = 256
    nq = DEC_SEQ // tq
    qbase = N_CTX // tq
    kbase = N_CTX // DEC_SEQ
    return pl.pallas_call(
        _attn_c_lat_kernel,
        out_shape=jax.ShapeDtypeStruct((T_ALL, H_C * V_C), BF16),
        grid=(DEC_BATCH, nq),
        in_specs=[
            pl.BlockSpec((tq, H_C * LANES), lambda b, i: (qbase + b * nq + i, 0)),
            pl.BlockSpec((DEC_SEQ, H_C * LANES), lambda b, i: (kbase + b, 0)),
            pl.BlockSpec((DEC_SEQ, H_C * V_C), lambda b, i: (kbase + b, 0)),
            pl.BlockSpec((PAST_LEN, H_C * LANES), lambda b, i: (b, 0)),
            pl.BlockSpec((PAST_LEN, H_C * V_C), lambda b, i: (b, 0)),
            pl.BlockSpec(memory_space=pl.ANY),
        ],
        out_specs=pl.BlockSpec((tq, H_C * V_C), lambda b, i: (qbase + b * nq + i, 0)),
        input_output_aliases={5: 0},
        compiler_params=_cp(("arbitrary", "arbitrary")),
        name="attn_c_lat",
    )(qc, kc, vc, kc_cache, vc_cache, o_all)


S5_TC = 256
S5_CH = 512
S5_UNROLL = 8
TOK_CHUNKS = T_ALL // S5_TC
CHUNK_GROUP = 8
assert SEQ == S5_TC and DEC_SEQ == CHUNK_GROUP * S5_TC and BATCH == 4 * CHUNK_GROUP


def _s5_bproj(u2, wb_ref, pick_first):
    def part(lo, hi):
        r = jnp.dot(u2, wb_ref[0, :, lo:hi], preferred_element_type=F32)
        if pick_first is not None:
            r = jnp.where(pick_first, r, jnp.dot(u2, wb_ref[1, :, lo:hi], preferred_element_type=F32))
        return r.reshape(S5_TC, SUBLANES, S5_CH)
    return part(0, S5_CH), part(S5_CH, 2 * S5_CH)


def _s5_cproj(xre, xim, wc_ref, pick_first):
    rows = S5_TC * SUBLANES
    xr = xre[...].reshape(rows, S5_CH).astype(BF16)
    xi = xim[...].reshape(rows, S5_CH).astype(BF16)

    def part(d):
        return (jnp.dot(xr, wc_ref[d, 0:S5_CH, :], preferred_element_type=F32)
                + jnp.dot(xi, wc_ref[d, S5_CH:, :], preferred_element_type=F32))

    y = part(0)
    if pick_first is not None:
        y = jnp.where(pick_first, y, part(1))
    return jnp.swapaxes(y.reshape(S5_TC, SUBLANES, LANES), 0, 1)


def _s5_ctx_kernel(u_ref, wb_ref, wc_ref, are_ref, aim_ref, y_init, y_ref, fre_ref, fim_ref, bre, bim):
    del y_init
    d = pl.program_id(2)
    rows = S5_TC * SUBLANES
    u2 = jnp.swapaxes(u_ref[...], 0, 1).reshape(rows, LANES).astype(BF16)
    b_re, b_im = _s5_bproj(u2, wb_ref, None)
    bre[...] = b_re
    bim[...] = b_im
    a_re = are_ref[...]
    a_im = aim_ref[...]

    def outer(i, carry):
        sr, si = carry
        for j in range(S5_UNROLL):
            t0 = i * S5_UNROLL + j
            t = jnp.where(d == 1, S5_TC - 1 - t0, t0)
            nr = a_re * sr - a_im * si + bre[t]
            ni = a_re * si + a_im * sr + bim[t]
            bre[t] = nr
            bim[t] = ni
            sr, si = nr, ni
        return sr, si

    zero = jnp.zeros((SUBLANES, S5_CH), F32)
    sr, si = lax.fori_loop(0, S5_TC // S5_UNROLL, outer, (zero, zero))
    fre_ref[...] = sr
    fim_ref[...] = si
    y = _s5_cproj(bre, bim, wc_ref, None)

    @pl.when(d == 0)
    def _():
        y_ref[...] = y

    @pl.when(d == 1)
    def _():
        y_ref[...] += y


def s5_ctx(proj4, wb, wc, a_re, a_im):
    nj = D_B // LANES
    ucol = C_UB // LANES
    return pl.pallas_call(
        _s5_ctx_kernel,
        out_shape=(
            jax.ShapeDtypeStruct((TOK_CHUNKS // CHUNK_GROUP, CHUNK_GROUP, S5_TC, D_B), F32),
            jax.ShapeDtypeStruct((2, BATCH, G_B * P_B), F32),
            jax.ShapeDtypeStruct((2, BATCH, G_B * P_B), F32),
        ),
        grid=(BATCH // CHUNK_GROUP, nj, 2),
        in_specs=[
            pl.BlockSpec((None, CHUNK_GROUP, S5_TC, LANES), lambda sb, j, d: (sb, 0, 0, ucol + j)),
            pl.BlockSpec((1, None, LANES, 2 * S5_CH), lambda sb, j, d: (d, j, 0, 0)),
            pl.BlockSpec((1, None, 2 * S5_CH, LANES), lambda sb, j, d: (d, j, 0, 0)),
            pl.BlockSpec((None, SUBLANES, S5_CH), lambda sb, j, d: (d, 0, j)),
            pl.BlockSpec((None, SUBLANES, S5_CH), lambda sb, j, d: (d, 0, j)),
            pl.BlockSpec(memory_space=pl.ANY),
        ],
        out_specs=(
            pl.BlockSpec((None, CHUNK_GROUP, S5_TC, LANES), lambda sb, j, d: (sb, 0, 0, j)),
            pl.BlockSpec((None, SUBLANES, S5_CH), lambda sb, j, d: (d, sb, j)),
            pl.BlockSpec((None, SUBLANES, S5_CH), lambda sb, j, d: (d, sb, j)),
        ),
        scratch_shapes=[pltpu.VMEM((S5_TC, SUBLANES, S5_CH), F32), pltpu.VMEM((S5_TC, SUBLANES, S5_CH), F32)],
        input_output_aliases={5: 0},
        compiler_params=_cp(("arbitrary", "arbitrary", "arbitrary")),
        name="s5_ctx",
    )(proj4, wb, wc, a_re, a_im, jnp.zeros((TOK_CHUNKS // CHUNK_GROUP, CHUNK_GROUP, S5_TC, D_B), F32))


def _s5_lat_kernel(uf_ref, ub_ref, wb_ref, wc_ref, are_ref, aim_ref, h0re_ref, h0im_ref, y_ctx,
                   yf_ref, yb_ref, bre, bim, xre, xim, sre, sim):
    del y_ctx
    k = pl.program_id(1)

    @pl.when(k == 0)
    def _():
        sre[...] = h0re_ref[...]
        sim[...] = h0im_ref[...]

    rows = S5_TC * SUBLANES
    half = SUBLANES // 2
    u8 = jnp.concatenate([uf_ref[...], ub_ref[...]], axis=0)
    u2 = jnp.swapaxes(u8, 0, 1).reshape(rows, LANES).astype(BF16)
    fwd_rows = (lax.broadcasted_iota(jnp.int32, (rows, 1), 0) % SUBLANES) < half
    b_re, b_im = _s5_bproj(u2, wb_ref, fwd_rows)
    bre[...] = b_re
    bim[...] = b_im
    a_re = are_ref[...]
    a_im = aim_ref[...]
    fwd8 = lax.broadcasted_iota(jnp.int32, (SUBLANES, 1), 0) < half

    def outer(i, carry):
        sr, si = carry
        for j in range(S5_UNROLL):
            t = i * S5_UNROLL + j
            tb = S5_TC - 1 - t
            nr = a_re * sr - a_im * si + jnp.where(fwd8, bre[t], bre[tb])
            ni = a_re * si + a_im * sr + jnp.where(fwd8, bim[t], bim[tb])
            xre[t, 0:half, :] = nr[0:half]
            xre[tb, half:, :] = nr[half:]
            xim[t, 0:half, :] = ni[0:half]
            xim[tb, half:, :] = ni[half:]
            sr, si = nr, ni
        return sr, si

    sr, si = lax.fori_loop(0, S5_TC // S5_UNROLL, outer, (sre[...], sim[...]))
    sre[...] = sr
    sim[...] = si
    y = _s5_cproj(xre, xim, wc_ref, fwd_rows)
    yf_ref[...] = y[0:half]
    yb_ref[...] = y[half:]


def s5_lat(proj4, y_all, wb, wc, a_re, a_im, h0_re, h0_im):
    nj = D_B // LANES
    nk = DEC_SEQ // S5_TC
    ucol = C_UB // LANES
    lat = lambda col0, rev: pl.BlockSpec(
        (DEC_BATCH, None, S5_TC, LANES), lambda j, k: (1, (nk - 1 - k) if rev else k, 0, col0 + j))
    vec = lambda: pl.BlockSpec((SUBLANES, S5_CH), lambda j, k: (0, j))
    buf = lambda: pltpu.VMEM((S5_TC, SUBLANES, S5_CH), F32)
    return pl.pallas_call(
        _s5_lat_kernel,
        out_shape=(
            jax.ShapeDtypeStruct(y_all.shape, F32),
            jax.ShapeDtypeStruct((DEC_BATCH, nk, S5_TC, D_B), F32),
        ),
        grid=(nj, nk),
        in_specs=[
            lat(ucol, False), lat(ucol, True),
            pl.BlockSpec((2, None, LANES, 2 * S5_CH), lambda j, k: (0, j, 0, 0)),
            pl.BlockSpec((2, None, 2 * S5_CH, LANES), lambda j, k: (0, j, 0, 0)),
            vec(), vec(), vec(), vec(),
            pl.BlockSpec(memory_space=pl.ANY),
        ],
        out_specs=(
            lat(0, False),
            pl.BlockSpec((DEC_BATCH, None, S5_TC, LANES), lambda j, k: (0, nk - 1 - k, 0, j)),
        ),
        scratch_shapes=[buf(), buf(), buf(), buf(),
                        pltpu.VMEM((SUBLANES, S5_CH), F32), pltpu.VMEM((SUBLANES, S5_CH), F32)],
        input_output_aliases={8: 0},
        compiler_params=_cp(("arbitrary", "arbitrary")),
        name="s5_lat",
    )(proj4, proj4, wb, wc, a_re, a_im, h0_re, h0_im, y_all)


S5FIN_TM = 512


def _s5fin_kernel(u_ref, y_ref, yb_ref, d_ref, w_ref, b_ref, o_ref):
    def body(latent):
        y = d_ref[...] * u_ref[...] + y_ref[...]
        if latent:
            y = y + yb_ref[...]
        y = 0.5 * y * (1.0 + jnp.tanh(math.sqrt(2.0 / math.pi) * (y + 0.044715 * (y * y * y))))
        z = jnp.dot(y.astype(BF16), w_ref[...], preferred_element_type=F32) + b_ref[...]
        o_ref[...] = (y * _sigmoid(z)).astype(o_ref.dtype)

    is_latent = pl.program_id(0) >= N_CTX // S5FIN_TM
    pl.when(is_latent)(lambda: body(True))
    pl.when(jnp.logical_not(is_latent))(lambda: body(False))


def s5_finish(proj, y, y_bwd_lat, d, w_glu, b_glu):
    tm = S5FIN_TM
    n_ctx_tiles = N_CTX // tm
    return pl.pallas_call(
        _s5fin_kernel,
        out_shape=jax.ShapeDtypeStruct((T_ALL, D_B), BF16),
        grid=(T_ALL // tm,),
        in_specs=[
            pl.BlockSpec((tm, D_B), lambda i: (i, C_UB // D_B)),
            pl.BlockSpec((tm, D_B), lambda i: (i, 0)),
            pl.BlockSpec((tm, D_B), lambda i: (jnp.maximum(i - n_ctx_tiles, 0), 0)),
            pl.BlockSpec((1, D_B), lambda i: (0, 0)),
            pl.BlockSpec((D_B, D_B), lambda i: (0, 0)),
            pl.BlockSpec((1, D_B), lambda i: (0, 0)),
        ],
        out_specs=pl.BlockSpec((tm, D_B), lambda i: (i, 0)),
        compiler_params=_cp(("arbitrary",)),
        name="s5_finish",
    )(proj, y, y_bwd_lat, d, w_glu, b_glu)


def _merge_kernel(xc_ref, xl_ref, gl_ref, oa_ref, ob_ref, oc_ref, mod_ref, g_ref, wa_ref, wb_ref, wc_ref, wo_ref,
                  wr_ref, br_ref, x1_ref, h2row_ref, idx_ref, rank_ref, wgt_ref, cnt_ref, cnt_acc):
    @pl.when(pl.program_id(0) == 0)
    def _():
        cnt_acc[...] = jnp.zeros_like(cnt_acc)

    m = None
    for br, (o_ref, w_ref) in enumerate(((oa_ref, wa_ref), (ob_ref, wb_ref), (oc_ref, wc_ref))):
        gate = _sigmoid(gl_ref[:, br * D_MODEL:(br + 1) * D_MODEL])
        t = gate * jnp.dot(o_ref[...], w_ref[...], preferred_element_type=F32)
        m = t if m is None else m + t
    x = _x_tile(xc_ref, xl_ref, pl.program_id(0), ROUTE_TM)
    x1 = x + mod_ref[2:3, :] * jnp.dot(m.astype(BF16), wo_ref[...], preferred_element_type=F32)
    x1_ref[...] = x1
    h2 = _rms(x1, g_ref[...])
    h2 = h2 * (1.0 + mod_ref[4:5, :]) + mod_ref[3:4, :]
    h2row_ref[...] = h2.reshape((h2.shape[0],) + ROW3)
    idx, rank, wgt, cnt = _route(h2, wr_ref, br_ref, cnt_acc[...])
    idx_ref[...] = idx
    rank_ref[...] = rank
    wgt_ref[...] = wgt
    cnt_acc[...] = cnt
    cnt_ref[...] = cnt


def merge(x_ctx, x_lat, proj, oa, ob, oc, mod, g, wa, wb, wc, wo, w_router, b_router):
    tm = ROUTE_TM
    full = lambda shape: pl.BlockSpec(shape, lambda i: (0,) * len(shape))
    tile = lambda: pl.BlockSpec((tm, LANES), lambda i: (i, 0))
    return pl.pallas_call(
        _merge_kernel,
        out_shape=(jax.ShapeDtypeStruct((T_ALL, D_MODEL), F32), jax.ShapeDtypeStruct((T_ALL,) + ROW3, F32),
                   jax.ShapeDtypeStruct((T_ALL, LANES), jnp.int32), jax.ShapeDtypeStruct((T_ALL, LANES), jnp.int32),
                   jax.ShapeDtypeStruct((T_ALL, LANES), F32), jax.ShapeDtypeStruct((1, LANES), F32)),
        grid=(T_ALL // tm,),
        in_specs=_x_specs(tm, lambda i: i) + [
            pl.BlockSpec((tm, 3 * D_MODEL), lambda i: (i, 0)),
            pl.BlockSpec((tm, 512), lambda i: (i, 0)),
            pl.BlockSpec((tm, 512), lambda i: (i, 0)),
            pl.BlockSpec((tm, 512), lambda i: (i, 0)),
            pl.BlockSpec((None, 6, D_MODEL), lambda i: (_group_of_tile(i, tm), 0, 0)),
            full((1, D_MODEL)),
            full((512, D_MODEL)), full((512, D_MODEL)), full((512, D_MODEL)), full((D_MODEL, D_MODEL)),
            full((D_MODEL, LANES)), full((1, LANES)),
        ],
        out_specs=(pl.BlockSpec((tm, D_MODEL), lambda i: (i, 0)), pl.BlockSpec((tm,) + ROW3, lambda i: (i, 0, 0)),
                   tile(), tile(), tile(), full((1, LANES))),
        scratch_shapes=[pltpu.VMEM((1, LANES), F32)],
        compiler_params=_cp(("arbitrary",)),
        name="merge",
    )(x_ctx, x_lat, proj, oa, ob, oc, mod, g, wa, wb, wc, wo, w_router, b_router)


ROUTE_TM = 256


def _route(h, w_ref, b_ref, cnt):
    w = w_ref[...]
    h_hi = h.astype(BF16)
    h_lo = (h - h_hi.astype(F32)).astype(BF16)
    w_hi = w.astype(BF16)
    w_lo = (w - w_hi.astype(F32)).astype(BF16)
    dot = functools.partial(jnp.dot, preferred_element_type=F32)
    logits = dot(h_hi, w_hi) + (dot(h_hi, w_lo) + dot(h_lo, w_hi)) + b_ref[...]
    lane_i = lax.broadcasted_iota(jnp.int32, (ROUTE_TM, LANES), 1)
    lane = lane_i.astype(F32)
    r_i = lax.broadcasted_iota(jnp.int32, (ROUTE_TM, ROUTE_TM), 0)
    c_i = lax.broadcasted_iota(jnp.int32, (ROUTE_TM, ROUTE_TM), 1)
    earlier = jnp.where(c_i < r_i, 1.0, 0.0).astype(BF16)

    idx_out = jnp.zeros((ROUTE_TM, LANES), F32)
    rank_out = jnp.zeros((ROUTE_TM, LANES), F32)
    val_out = jnp.zeros((ROUTE_TM, LANES), F32)
    v0 = None
    esum = None
    for k in range(TOP_K):
        m = logits.max(axis=-1, keepdims=True)
        sel = jnp.min(jnp.where(logits == m, lane, float(LANES)), axis=-1, keepdims=True)
        hit = lane == sel
        logits = jnp.where(hit, -jnp.inf, logits)
        onehot = jnp.where(hit, 1.0, 0.0)
        within = jnp.dot(earlier, onehot.astype(BF16), preferred_element_type=F32)
        rank = jnp.sum(onehot * (within + cnt), axis=-1, keepdims=True)
        cnt = cnt + jnp.sum(onehot, axis=0, keepdims=True)
        if k == 0:
            v0 = m
        e = jnp.exp(m - v0)
        esum = e if esum is None else esum + e
        idx_out = jnp.where(lane_i == k, sel, idx_out)
        rank_out = jnp.where(lane_i == k, rank, rank_out)
        val_out = jnp.where(lane_i == k, e, val_out)
    return idx_out.astype(jnp.int32), rank_out.astype(jnp.int32), val_out * (1.0 / esum), cnt


PLAN_UNROLL = 16
N_DUMP = 5 * BLK
PLAN_LEN = BUF_LEN + BLK
assert TOP_K == 4 and T_ALL & (T_ALL - 1) == 0


def _invert_kernel(dest_ref, fill_hbm, inv_ref):
    pltpu.sync_copy(fill_hbm, inv_ref)
    tok_step = PLAN_UNROLL // TOP_K

    def put(i, c):
        for j in range(PLAN_UNROLL):
            inv_ref[dest_ref[i * PLAN_UNROLL + j]] = i * tok_step + ((j % TOP_K) * T_ALL + j // TOP_K)
        return c

    lax.fori_loop(0, N_ASSIGN // PLAN_UNROLL, put, 0)


def invert_plan(dest):
    r = jnp.arange(PLAN_LEN, dtype=jnp.int32)
    fill = N_ASSIGN + jnp.where(r < BLK, 2 * BLK + r, (r - BLK) & (2 * BLK - 1))
    return pl.pallas_call(
        _invert_kernel,
        out_shape=jax.ShapeDtypeStruct((PLAN_LEN,), jnp.int32),
        in_specs=[pl.BlockSpec(memory_space=pltpu.SMEM), pl.BlockSpec(memory_space=pl.ANY)],
        out_specs=pl.BlockSpec(memory_space=pltpu.SMEM),
        name="invert_plan",
    )(dest, fill)


PAIR_TILE = 2 * LANES


def _pair_selection():
    r = lax.broadcasted_iota(jnp.int32, (PAIR_TILE, PAIR_TILE), 0)
    c = lax.broadcasted_iota(jnp.int32, (PAIR_TILE, PAIR_TILE), 1)
    return (r == jnp.where(c < LANES, 2 * c, 2 * (c - LANES) + 1)).astype(BF16)


EXP_NBUF = 3


def _expert_kernel(be_ref, first_ref, par_ref, nxt_ref, inv_ref, h_hbm, wu_hbm, bu_ref, wd_hbm, bd_ref, sel_ref,
                   comb_hbm, xbuf, ybuf, wu_ref, wd_ref, wu_st, wd_st, gsem, ssem, wsem, *, layer):
    i = pl.program_id(0)
    n = pl.num_programs(0)
    cur = i % EXP_NBUF
    nxt = (i + 2) % EXP_NBUF
    prv = nxt

    def gather(block, s):
        base = (block + 1) * BLK
        for r in range(BLK):
            tok = inv_ref[base + r] & (T_ALL - 1)
            pltpu.make_async_copy(h_hbm.at[tok], xbuf.at[s, r], gsem.at[s]).start()

    def scatter(block, s):
        base = (block + 1) * BLK
        for r in range(BLK):
            a = inv_ref[base + r]
            pltpu.make_async_copy(ybuf.at[s, r], comb_hbm.at[a], ssem.at[s]).start()

    def wait_block(sem):
        pltpu.make_async_copy(h_hbm.at[pl.ds(0, BLK)], xbuf.at[0], sem).wait()

    def weight_copies(e, p):
        return (pltpu.make_async_copy(wu_hbm.at[layer, e], wu_st.at[p], wsem.at[p]),
                pltpu.make_async_copy(wd_hbm.at[layer, e], wd_st.at[p], wsem.at[p]))

    @pl.when(i == 0)
    def _():
        for cp in weight_copies(be_ref[0], 0):
            cp.start()
        ybuf[...] = jnp.zeros_like(ybuf)
        for s in range(EXP_NBUF - 1):
            for r in range(BLK):
                dump = N_ASSIGN + (3 + s) * BLK + r
                pltpu.make_async_copy(ybuf.at[s, r], comb_hbm.at[dump], ssem.at[s]).start()
        gather(0, 0)
        gather(1, 1)

    wait_block(gsem.at[cur])
    wait_block(ssem.at[cur])

    @pl.when(first_ref[i] == 1)
    def _():
        p = par_ref[i]
        for cp in weight_copies(be_ref[i], p):
            cp.wait()

        @pl.when(nxt_ref[i] >= 0)
        def _():
            for cp in weight_copies(nxt_ref[i], 1 - p):
                cp.start()

        sel = sel_ref[...]
        for j in range(2 * D_FF // PAIR_TILE):
            cols = slice(j * PAIR_TILE, (j + 1) * PAIR_TILE)
            wu_ref[:, cols] = jnp.dot(wu_st[p, :, cols].astype(BF16), sel, preferred_element_type=F32).astype(BF16)
        wd_ref[...] = wd_st[p].astype(BF16)

    gather(jnp.minimum(i + 2, n - 1), nxt)
    scatter(i - 1, prv)

    x = xbuf[cur].reshape(BLK, D_MODEL).astype(BF16)
    h = jnp.dot(x, wu_ref[...], preferred_element_type=F32) + bu_ref[...]
    acts = []
    for j in range(2 * D_FF // PAIR_TILE):
        glu = jnp.minimum(h[:, j * PAIR_TILE:j * PAIR_TILE + LANES], SWIGLU_LIMIT)
        lin = jnp.clip(h[:, j * PAIR_TILE + LANES:(j + 1) * PAIR_TILE], -SWIGLU_LIMIT, SWIGLU_LIMIT)
        acts.append((glu * _sigmoid(SWIGLU_ALPHA * glu) * (lin + 1.0)).astype(BF16))
    act = jnp.concatenate(acts, axis=1)
    y = jnp.dot(act, wd_ref[...], preferred_element_type=F32) + bd_ref[...]
    ybuf[cur] = y.reshape((BLK,) + ROW3)

    @pl.when(i == n - 1)
    def _():
        for s in range(EXP_NBUF):
            @pl.when(s != cur)
            def _():
                wait_block(gsem.at[s])
                wait_block(ssem.at[s])
        scatter(i, cur)
        wait_block(ssem.at[cur])


def experts(layer, block_e, inv, h2row, wu, bu, wd, bd):
    first = jnp.concatenate([jnp.ones((1,), jnp.int32), (block_e[1:] != block_e[:-1]).astype(jnp.int32)])
    slot = (jnp.cumsum(first) - 1) & 1
    later = jnp.where(block_e[None, :] > block_e[:, None], block_e[None, :], N_EXP).min(axis=1)
    nxt = jnp.where(later == N_EXP, -1, later).astype(jnp.int32)
    per_expert = lambda *blk: pl.BlockSpec((None, None) + blk, lambda i, be, *_: (layer, be[i], 0, 0))
    return pl.pallas_call(
        functools.partial(_expert_kernel, layer=layer),
        out_shape=jax.ShapeDtypeStruct((N_ASSIGN + N_DUMP,) + ROW3, F32),
        grid_spec=pltpu.PrefetchScalarGridSpec(
            num_scalar_prefetch=5,
            grid=(N_BLOCKS,),
            in_specs=[
                pl.BlockSpec(memory_space=pl.ANY),
                pl.BlockSpec(memory_space=pl.ANY), per_expert(1, 2 * D_FF),
                pl.BlockSpec(memory_space=pl.ANY), per_expert(1, D_MODEL),
                pl.BlockSpec((PAIR_TILE, PAIR_TILE), lambda i, *_: (0, 0)),
            ],
            out_specs=pl.BlockSpec(memory_space=pl.ANY),
            scratch_shapes=[
                pltpu.VMEM((EXP_NBUF, BLK) + ROW3, F32), pltpu.VMEM((EXP_NBUF, BLK) + ROW3, F32),
                pltpu.VMEM((D_MODEL, 2 * D_FF), BF16), pltpu.VMEM((D_FF, D_MODEL), BF16),
                pltpu.VMEM((2, D_MODEL, 2 * D_FF), F32), pltpu.VMEM((2, D_FF, D_MODEL), F32),
                pltpu.SemaphoreType.DMA((EXP_NBUF,)), pltpu.SemaphoreType.DMA((EXP_NBUF,)),
                pltpu.SemaphoreType.DMA((2,)),
            ],
        ),
        compiler_params=_cp(("arbitrary",)),
        name="experts",
    )(block_e, first, slot.astype(jnp.int32), nxt, inv, h2row, wu, bu, wd, bd, _pair_selection())


COMB_TM = 256


def _combine_kernel(c0_ref, c1_ref, c2_ref, c3_ref, x_ref, w_ref, mod_ref, oc_ref, ol_ref):
    w = w_ref[...]
    acc = None
    for k, c_ref in enumerate((c0_ref, c1_ref, c2_ref, c3_ref)):
        t = w[:, k:k + 1] * c_ref[...].reshape(COMB_TM, D_MODEL)
        acc = t if acc is None else acc + t
    out = x_ref[...] + mod_ref[5:6, :] * acc
    is_latent = pl.program_id(0) >= N_CTX // COMB_TM

    @pl.when(is_latent)
    def _():
        ol_ref[...] = out

    @pl.when(jnp.logical_not(is_latent))
    def _():
        oc_ref[...] = out


def combine(comb, x1, wgt, mod):
    tiles = T_ALL // COMB_TM
    kth = lambda k: pl.BlockSpec((COMB_TM,) + ROW3, lambda i: (k * tiles + i, 0, 0))
    return pl.pallas_call(
        _combine_kernel,
        out_shape=(jax.ShapeDtypeStruct((N_CTX, D_MODEL), F32), jax.ShapeDtypeStruct((N_LAT, D_MODEL), F32)),
        grid=(tiles,),
        in_specs=[
            kth(0), kth(1), kth(2), kth(3),
            pl.BlockSpec((COMB_TM, D_MODEL), lambda i: (i, 0)),
            pl.BlockSpec((COMB_TM, LANES), lambda i: (i, 0)),
            pl.BlockSpec((None, 6, D_MODEL), lambda i: (_group_of_tile(i, COMB_TM), 0, 0)),
        ],
        out_specs=tuple(_x_specs(COMB_TM, lambda i: i)),
        compiler_params=_cp(("arbitrary",)),
        name="combine",
    )(comb, comb, comb, comb, x1, wgt, mod)


def _rope_table():
    pos = jnp.arange(DEC_SEQ)
    row = (pos // GRID_W).astype(F32)[:, None]
    col = (pos % GRID_W).astype(F32)[:, None]

    def parts(rot_dim):
        nf = rot_dim // 4
        inv = ROPE_BASE ** (-jnp.arange(nf, dtype=F32) / nf)
        cr, sr, cc, sc = jnp.cos(row * inv), jnp.sin(row * inv), jnp.cos(col * inv), jnp.sin(col * inv)
        return jnp.concatenate([cr, cr, cc, cc], axis=1), jnp.concatenate([-sr, sr, -sc, sc], axis=1)

    ca, sa = (jnp.tile(t, (1, 2)) for t in parts(HD_A))
    cc, sc = parts(ROPE_C)
    pad = lambda t, fill: jnp.pad(t, ((0, 0), (NOPE_C, LANES - QK_C)), constant_values=fill)
    lat = jnp.concatenate([ca, sa, pad(cc, 1.0), pad(sc, 0.0)], axis=1)
    ones, zeros = jnp.ones((DEC_SEQ, LANES), F32), jnp.zeros((DEC_SEQ, LANES), F32)
    ident = jnp.concatenate([ones, zeros, ones, zeros], axis=1)
    return jnp.concatenate([ident, lat], axis=0)


def _pair_swaps():
    j = lax.broadcasted_iota(jnp.int32, (LANES, LANES), 0)
    i = lax.broadcasted_iota(jnp.int32, (LANES, LANES), 1)

    def swap(half):
        first = (i % (2 * half)) < half
        return (j == jnp.where(first, i + half, i - half)).astype(BF16)

    return jnp.stack([swap(HD_A // 4), swap(ROPE_C // 4)])


def _arrange_w_in(w_in):
    o = 0
    parts = {}
    for name, n in (("qa", HQ_A * HD_A), ("ka", HKV_A * HD_A), ("va", HKV_A * HD_A), ("ub", D_B), ("cq", Q_LORA),
                    ("ckv", KV_LORA), ("kr", ROPE_C), ("gl", 3 * D_MODEL)):
        parts[name] = w_in[..., o:o + n]
        o += n
    qa = parts["qa"].reshape(DEPTH, D_MODEL, HKV_A, G_A, 1, HD_A)
    eye = jnp.eye(HKV_A, dtype=F32).reshape(1, 1, HKV_A, 1, HKV_A, 1)
    qa_slots = (qa * eye).reshape(DEPTH, D_MODEL, HQ_A * LANES)
    kr = jnp.pad(parts["kr"], ((0, 0), (0, 0), (NOPE_C, LANES - QK_C)))
    w = jnp.concatenate([parts["gl"], qa_slots, parts["ub"], parts["cq"], parts["ka"], parts["va"], parts["ckv"], kr],
                        axis=-1)
    return w.astype(BF16)


def _s5_params(lam_re, lam_im, log_dt, b_re, b_im, c_re, c_im):
    dt = jnp.exp(log_dt)[..., None]
    decay = jnp.exp(lam_re * dt)
    ab_re, ab_im = decay * jnp.cos(lam_im * dt), decay * jnp.sin(lam_im * dt)
    den = lam_re * lam_re + lam_im * lam_im
    f_re = ((ab_re - 1) * lam_re + ab_im * lam_im) / den
    f_im = (ab_im * lam_re - (ab_re - 1) * lam_im) / den
    bb_re = f_re[..., None] * b_re - f_im[..., None] * b_im
    bb_im = f_re[..., None] * b_im + f_im[..., None] * b_re
    nj, gpb = D_B // LANES, LANES // GS_B
    eye = jnp.eye(gpb, dtype=F32)

    def blockdiag_b(bb):
        t = bb.transpose(0, 1, 3, 2).reshape(2, nj, gpb, GS_B, P_B)
        return (t[:, :, :, :, None, :] * eye[None, None, :, None, :, None]).reshape(2, nj, LANES, gpb * P_B)

    def blockdiag_c(cc):
        t = cc.transpose(0, 1, 3, 2).reshape(2, nj, gpb, P_B, GS_B)
        return (t[:, :, :, :, None, :] * eye[None, None, :, None, :, None]).reshape(2, nj, gpb * P_B, LANES)

    wb = jnp.concatenate([blockdiag_b(bb_re), blockdiag_b(bb_im)], axis=-1).astype(BF16)
    wc = jnp.concatenate([blockdiag_c(c_re), -blockdiag_c(c_im)], axis=-2).astype(BF16)
    return wb, wc, ab_re.reshape(2, G_B * P_B), ab_im.reshape(2, G_B * P_B)


def kernel(x_prompt, x_sample, c, cache_attn_k, cache_attn_v, cache_mla_ckv, cache_mla_krope, state_ssm_re, state_ssm_im, c_ctx, w_ada, b_ada, norm_mix_g, norm_ffn_g, w_in, q_norm_a, k_norm_a, sink_a, q_a_norm_c, kv_a_norm_c, w_uq_c, w_ukv_c, q_norm_c, k_norm_c, ssm_lam_re, ssm_lam_im, ssm_log_dt, ssm_b_re, ssm_b_im, ssm_c_re, ssm_c_im, ssm_d, w_glu, b_glu, w_br_a, w_br_b, w_br_c, w_out, w_router, b_router, w_up, b_up, w_down, b_down):
    x_ctx, x_lat = x_prompt.reshape(N_CTX, D_MODEL), x_sample.reshape(N_LAT, D_MODEL)
    cvecs = jnp.concatenate([c_ctx[None], c, jnp.zeros((N_GROUPS - 1 - DEC_BATCH, D_MODEL), F32)], axis=0)
    mods = adaln(cvecs, w_ada, b_ada)

    tab = _rope_table()
    w_in_r = _arrange_w_in(w_in)
    pad_slot = lambda g: jnp.pad(g, ((0, 0), (0, LANES - QK_C))).reshape(DEPTH, 1, LANES)
    gqa = jnp.tile(q_norm_a, (1, 2)).reshape(DEPTH, 1, LANES)
    gka = jnp.tile(k_norm_a, (1, 2)).reshape(DEPTH, 1, LANES)
    gqc, gkc = pad_slot(q_norm_c), pad_slot(k_norm_c)
    wuq = jnp.pad(w_uq_c.reshape(DEPTH, Q_LORA, H_C, QK_C), ((0, 0), (0, 0), (0, 0), (0, LANES - QK_C)))
    wuq = wuq.reshape(DEPTH, Q_LORA, H_C * LANES).astype(BF16)
    wukv4 = w_ukv_c.reshape(DEPTH, KV_LORA, H_C, NOPE_C + V_C)
    wuk = jnp.pad(wukv4[..., :NOPE_C], ((0, 0), (0, 0), (0, 0), (0, LANES - NOPE_C))).reshape(DEPTH, KV_LORA, H_C * LANES)
    wuv = wukv4[..., NOPE_C:].reshape(DEPTH, KV_LORA, H_C * V_C)
    wukv = jnp.concatenate([wuk, wuv], axis=-1).astype(BF16)
    sink = jnp.broadcast_to(sink_a[:, :, None], (DEPTH, HQ_A, LANES))
    w_router_p = jnp.pad(w_router, ((0, 0), (0, 0), (0, LANES - N_EXP)))
    b_router_p = jnp.pad(b_router, ((0, 0), (0, LANES - N_EXP)), constant_values=-jnp.inf).reshape(DEPTH, 1, LANES)
    b_up_p = b_up.reshape(DEPTH, N_EXP, 2 * D_FF // PAIR_TILE, LANES, 2).transpose(0, 1, 2, 4, 3)
    b_up_p = b_up_p.reshape(DEPTH, N_EXP, 1, 2 * D_FF)
    b_down_r = b_down.reshape(DEPTH, N_EXP, 1, D_MODEL)
    kr_cache = jnp.pad(cache_mla_krope, ((0, 0), (0, 0), (0, 0), (NOPE_C, LANES - QK_C)))

    new_k, new_v, new_ckv, new_kr, new_sre, new_sim = [], [], [], [], [], []
    for l in range(DEPTH):
        mod = mods[l]
        proj = inproj(l, x_ctx, x_lat, norm_mix_g[l], mod, w_in_r)
        qa, ka, qc, kc, vc, ckvn = prep(proj, tab, gqa[l], gka[l], q_a_norm_c[l].reshape(1, Q_LORA),
                                         kv_a_norm_c[l].reshape(1, KV_LORA), gqc[l], gkc[l], wuq[l], wukv[l])
        kc_cache, vc_cache = cache_keys(cache_mla_ckv[:, l].reshape(DEC_BATCH * PAST_LEN, KV_LORA),
                                        kr_cache[:, l].reshape(DEC_BATCH * PAST_LEN, LANES), gkc[l], wukv[l])
        oa = attn_a_lat(qa, ka, proj, cache_attn_k[:, l].reshape(DEC_BATCH, PAST_LEN, LANES),
                        cache_attn_v[:, l].reshape(DEC_BATCH, PAST_LEN, LANES), sink[l],
                        attn_a_ctx(qa, ka, proj, sink[l]))
        oc = attn_c_lat(qc, kc, vc, kc_cache, vc_cache, attn_c_ctx(qc, kc, vc))
        wb, wc, a_re, a_im = _s5_params(ssm_lam_re[l], ssm_lam_im[l], ssm_log_dt[l], ssm_b_re[l], ssm_b_im[l],
                                        ssm_c_re[l], ssm_c_im[l])
        proj4 = proj.reshape(TOK_CHUNKS // CHUNK_GROUP, CHUNK_GROUP, S5_TC, N_PROJ)
        per_row = lambda a, rep: jnp.repeat(a, rep, axis=0)
        y_all, f_re, f_im = s5_ctx(proj4, wb, wc, per_row(a_re, SUBLANES).reshape(2, SUBLANES, -1),
                                   per_row(a_im, SUBLANES).reshape(2, SUBLANES, -1))
        h0 = lambda st: st[:, l].transpose(1, 0, 2, 3).reshape(2 * DEC_BATCH, G_B * P_B)
        y_all, y_bwd = s5_lat(proj4, y_all, wb, wc, per_row(a_re, DEC_BATCH), per_row(a_im, DEC_BATCH),
                              h0(state_ssm_re), h0(state_ssm_im))
        ob = s5_finish(proj, y_all.reshape(T_ALL, D_B), y_bwd.reshape(N_LAT, D_B), ssm_d[l].reshape(1, D_B),
                       w_glu[l].astype(BF16), b_glu[l].reshape(1, D_B))
        x1, h2row, idx, rank, wgt, cnt = merge(
            x_ctx, x_lat, proj, oa, ob, oc, mod, norm_ffn_g[l].reshape(1, D_MODEL), w_br_a[l].astype(BF16),
            w_br_b[l].astype(BF16), w_br_c[l].astype(BF16), w_out[l].astype(BF16), w_router_p[l], b_router_p[l])
        counts = cnt[0, :N_EXP].astype(jnp.int32)
        padded = (counts + BLK - 1) // BLK * BLK
        pad_end = jnp.cumsum(padded)
        pad_start = pad_end - padded
        dest = (BLK + pad_start[idx[:, :TOP_K]] + rank[:, :TOP_K]).reshape(-1).astype(jnp.int32)
        block_start = jnp.arange(N_BLOCKS, dtype=jnp.int32) * BLK
        block_e = jnp.minimum(jnp.sum(pad_end[None, :] <= block_start[:, None], axis=1), N_EXP - 1)
        block_e = block_e.astype(jnp.int32)
        comb = experts(l, block_e, invert_plan(dest), h2row, w_up, b_up_p, w_down, b_down_r)
        x_ctx, x_lat = combine(comb, x1, wgt, mod)

        new_k.append(ka[:N_CTX].reshape(BATCH, SEQ, HKV_A, HD_A))
        new_v.append(proj[:N_CTX, C_VA:C_VA + LANES].reshape(BATCH, SEQ, HKV_A, HD_A))
        new_ckv.append(ckvn[:N_CTX].reshape(BATCH, SEQ, KV_LORA))
        new_kr.append(proj[:N_CTX, C_KR + NOPE_C:C_KR + QK_C].reshape(BATCH, SEQ, ROPE_C))
        new_sre.append(f_re.reshape(2, BATCH, G_B, P_B).transpose(1, 0, 2, 3))
        new_sim.append(f_im.reshape(2, BATCH, G_B, P_B).transpose(1, 0, 2, 3))

    y_prompt = x_ctx.reshape(BATCH, SEQ, D_MODEL)
    y_sample = x_lat.reshape(DEC_BATCH, DEC_SEQ, D_MODEL)
    return (y_prompt, y_sample, jnp.stack(new_k, axis=1), jnp.stack(new_v, axis=1), jnp.stack(new_ckv, axis=1),
            jnp.stack(new_kr, axis=1), jnp.stack(new_sre, axis=1), jnp.stack(new_sim, axis=1))
```

```python
import functools
import math

import jax
import jax.numpy as jnp
from jax import lax
from jax.experimental import pallas as pl
from jax.experimental.pallas import tpu as pltpu

D_MODEL = 1024
BATCH = 32
SEQ = 256
DEPTH = 2
DEC_BATCH = 4
DEC_SEQ = 2048
PAST_LEN = 256
GRID_W = 64
ROPE_BASE = 10000.0
EPS = 1e-6
NEG_INF = -1e30
BAND = 128
BLK = 128
HQ_A, HKV_A, HD_A = 8, 2, 64
G_A = HQ_A // HKV_A
WINDOW = 128
D_B, GS_B, P_B = 512, 16, 64
G_B = D_B // GS_B
H_C, Q_LORA, KV_LORA, NOPE_C, ROPE_C, V_C = 8, 256, 128, 64, 32, 64
QK_C = NOPE_C + ROPE_C
N_EXP, TOP_K, D_FF = 32, 4, 1024
SWIGLU_ALPHA, SWIGLU_LIMIT = 1.702, 7.0

N_CTX = BATCH * SEQ
N_LAT = DEC_BATCH * DEC_SEQ
T_ALL = N_CTX + N_LAT
N_GROUPS = 8

LANES = 128
SUBLANES = 8

C_GL = 0
C_QA = 3 * D_MODEL
C_UB = C_QA + HQ_A * LANES
C_CQ = C_UB + D_B
C_KA = C_CQ + Q_LORA
C_VA = C_KA + LANES
C_CKV = C_VA + LANES
C_KR = C_CKV + LANES
N_PROJ = C_KR + LANES

N_ASSIGN = T_ALL * TOP_K
N_BLOCKS = N_ASSIGN // BLK + N_EXP
BUF_LEN = N_BLOCKS * BLK

F32 = jnp.float32
BF16 = jnp.bfloat16
ROW3 = (D_MODEL // LANES, LANES)
VMEM_LIMIT = 56 * 1024 * 1024


def _cp(sem, vmem=VMEM_LIMIT):
    return pltpu.CompilerParams(dimension_semantics=sem, vmem_limit_bytes=vmem)


def _group_of_tile(i, tm):
    n_ctx_tiles = N_CTX // tm
    per_batch = DEC_SEQ // tm
    return jnp.where(i < n_ctx_tiles, 0, 1 + (i - n_ctx_tiles) // per_batch)


def _sigmoid(x):
    return 1.0 / (1.0 + jnp.exp(-x))


def _x_specs(tm, tile_of):
    n_ctx_tiles = N_CTX // tm
    return [pl.BlockSpec((tm, D_MODEL), lambda *g: (jnp.minimum(tile_of(*g), n_ctx_tiles - 1), 0)),
            pl.BlockSpec((tm, D_MODEL), lambda *g: (jnp.maximum(tile_of(*g) - n_ctx_tiles, 0), 0))]


def _x_tile(xc_ref, xl_ref, tile, tm):
    return jnp.where(tile >= N_CTX // tm, xl_ref[...], xc_ref[...])


def _adaln_kernel(c_ref, w_ref, b_ref, o_ref):
    c = c_ref[...]
    s = c * _sigmoid(c)
    o_ref[...] = jnp.dot(s.astype(BF16), w_ref[...].astype(BF16), preferred_element_type=F32) + b_ref[...]


def adaln(cvecs, w_ada, b_ada):
    tn = 1536
    out = pl.pallas_call(
        _adaln_kernel,
        out_shape=jax.ShapeDtypeStruct((DEPTH, N_GROUPS, 6 * D_MODEL), F32),
        grid=(DEPTH, 6 * D_MODEL // tn),
        in_specs=[
            pl.BlockSpec((N_GROUPS, D_MODEL), lambda l, j: (0, 0)),
            pl.BlockSpec((None, D_MODEL, tn), lambda l, j: (l, 0, j)),
            pl.BlockSpec((None, 1, tn), lambda l, j: (l, 0, j)),
        ],
        out_specs=pl.BlockSpec((None, N_GROUPS, tn), lambda l, j: (l, 0, j)),
        compiler_params=_cp(("arbitrary", "arbitrary")),
        name="adaln",
    )(cvecs, w_ada, b_ada.reshape(DEPTH, 1, 6 * D_MODEL))
    return out.reshape(DEPTH, N_GROUPS, 6, D_MODEL)


def _rms(x, g):
    ms = jnp.mean(x * x, axis=-1, keepdims=True)
    return x * lax.rsqrt(ms + EPS) * g


INPROJ_TM = 512


def _inproj_kernel(xc_ref, xl_ref, g_ref, mod_ref, w_ref, o_ref):
    h = _rms(_x_tile(xc_ref, xl_ref, pl.program_id(1), INPROJ_TM), g_ref[...])
    h = h * (1.0 + mod_ref[1:2, :]) + mod_ref[0:1, :]
    o_ref[...] = jnp.dot(h.astype(BF16), w_ref[...], preferred_element_type=F32)


def inproj(layer, x_ctx, x_lat, g, mod, w):
    tm, tn = INPROJ_TM, 1792
    return pl.pallas_call(
        _inproj_kernel,
        out_shape=jax.ShapeDtypeStruct((T_ALL, N_PROJ), F32),
        grid=(N_PROJ // tn, T_ALL // tm),
        in_specs=_x_specs(tm, lambda j, i: i) + [
            pl.BlockSpec((1, D_MODEL), lambda j, i: (0, 0)),
            pl.BlockSpec((None, 6, D_MODEL), lambda j, i: (_group_of_tile(i, tm), 0, 0)),
            pl.BlockSpec((None, D_MODEL, tn), lambda j, i: (layer, 0, j)),
        ],
        out_specs=pl.BlockSpec((tm, tn), lambda j, i: (i, j)),
        compiler_params=_cp(("arbitrary", "arbitrary")),
        name="inproj",
    )(x_ctx, x_lat, g.reshape(1, D_MODEL), mod, w)


def _rope(slab, cos, sin, swap):
    hi = slab.astype(BF16)
    lo = (slab - hi.astype(F32)).astype(BF16)
    partner = jnp.dot(hi, swap, preferred_element_type=F32) + jnp.dot(lo, swap, preferred_element_type=F32)
    return slab * cos + partner * sin


def _slot_norm(slab, gain, n_real):
    ms = jnp.sum(slab * slab, axis=-1, keepdims=True) * (1.0 / n_real)
    return slab * lax.rsqrt(ms + EPS) * gain


def _mla_keys(ckvn, kr_blk, wukv_ref, gk, rope_c):
    kv = jnp.dot(ckvn.astype(BF16), wukv_ref[...], preferred_element_type=F32)
    ks = []
    for h in range(H_C):
        slab = kv[:, h * LANES:(h + 1) * LANES] + kr_blk
        slab = _slot_norm(slab, gk, QK_C)
        if rope_c is not None:
            slab = _rope(slab, *rope_c)
        ks.append(slab.astype(BF16))
    return jnp.concatenate(ks, axis=1), kv[:, H_C * LANES:].astype(BF16)


PREP_TM = 512


def _prep_kernel(qa_ref, cq_ref, ka_ref, ckv_ref, kr_ref, tab_ref,
                 gqa_ref, gka_ref, gcq_ref, gckv_ref, gqc_ref, gkc_ref, wuq_ref, wukv_ref, swap_ref,
                 qa_o, ka_o, qc_o, kc_o, vc_o, ckvn_o):
    def body(rotate):
        if rotate:
            tab = tab_ref[...]
            rope_a = (tab[:, 0:128], tab[:, 128:256], swap_ref[0])
            rope_c = (tab[:, 256:384], tab[:, 384:512], swap_ref[1])
            rot_a = lambda x: _rope(x, *rope_a)
            rot_c = lambda x: _rope(x, *rope_c)
        else:
            rope_c = None
            rot_a = rot_c = lambda x: x
        lane = lax.broadcasted_iota(jnp.int32, (1, LANES), 1)

        gqa = gqa_ref[...]
        qa = qa_ref[...]
        outs = []
        for h in range(HQ_A):
            slab = _slot_norm(qa[:, h * LANES:(h + 1) * LANES], gqa, HD_A)
            outs.append((rot_a(slab) * (HD_A ** -0.5)).astype(BF16))
        qa_o[...] = jnp.concatenate(outs, axis=1)

        ka = ka_ref[...]
        sq = ka * ka
        lo = lane < HD_A
        ms_lo = jnp.sum(jnp.where(lo, sq, 0.0), axis=-1, keepdims=True)
        ms_hi = jnp.sum(jnp.where(lo, 0.0, sq), axis=-1, keepdims=True)
        rs = jnp.where(lo, lax.rsqrt(ms_lo * (1.0 / HD_A) + EPS), lax.rsqrt(ms_hi * (1.0 / HD_A) + EPS))
        ka_o[...] = rot_a(ka * rs * gka_ref[...])

        cqn = _rms(cq_ref[...], gcq_ref[...])
        q = jnp.dot(cqn.astype(BF16), wuq_ref[...], preferred_element_type=F32)
        gqc = gqc_ref[...]
        outs = []
        for h in range(H_C):
            slab = _slot_norm(q[:, h * LANES:(h + 1) * LANES], gqc, QK_C)
            outs.append((rot_c(slab) * (QK_C ** -0.5)).astype(BF16))
        qc_o[...] = jnp.concatenate(outs, axis=1)

        ckvn = _rms(ckv_ref[...], gckv_ref[...])
        ckvn_o[...] = ckvn
        kc, vc = _mla_keys(ckvn, kr_ref[...], wukv_ref, gkc_ref[...], rope_c)
        kc_o[...] = kc
        vc_o[...] = vc

    is_latent = pl.program_id(0) >= N_CTX // PREP_TM
    pl.when(is_latent)(lambda: body(True))
    pl.when(jnp.logical_not(is_latent))(lambda: body(False))


def prep(proj, tab, gqa, gka, gcq, gckv, gqc, gkc, wuq, wukv):
    tm = PREP_TM
    n_ctx_tiles = N_CTX // tm
    per_batch = DEC_SEQ // tm

    def tab_map(i):
        return (jnp.where(i < n_ctx_tiles, 0, per_batch + (i - n_ctx_tiles) % per_batch), 0)

    def col(width, off):
        return pl.BlockSpec((tm, width), lambda i: (i, off // width))

    def full(shape):
        return pl.BlockSpec(shape, lambda i: (0,) * len(shape))

    def row_out(width):
        return pl.BlockSpec((tm, width), lambda i: (i, 0))

    return pl.pallas_call(
        _prep_kernel,
        out_shape=(
            jax.ShapeDtypeStruct((T_ALL, HQ_A * LANES), BF16),
            jax.ShapeDtypeStruct((T_ALL, LANES), F32),
            jax.ShapeDtypeStruct((T_ALL, H_C * LANES), BF16),
            jax.ShapeDtypeStruct((T_ALL, H_C * LANES), BF16),
            jax.ShapeDtypeStruct((T_ALL, H_C * V_C), BF16),
            jax.ShapeDtypeStruct((T_ALL, KV_LORA), F32),
        ),
        grid=(T_ALL // tm,),
        in_specs=[
            col(HQ_A * LANES, C_QA), col(Q_LORA, C_CQ), col(LANES, C_KA), col(LANES, C_CKV), col(LANES, C_KR),
            pl.BlockSpec((tm, 4 * LANES), tab_map),
            full((1, LANES)), full((1, LANES)), full((1, Q_LORA)), full((1, KV_LORA)),
            full((1, LANES)), full((1, LANES)),
            full((Q_LORA, H_C * LANES)), full((KV_LORA, H_C * LANES + H_C * V_C)), full((2, LANES, LANES)),
        ],
        out_specs=(row_out(HQ_A * LANES), row_out(LANES), row_out(H_C * LANES), row_out(H_C * LANES),
                   row_out(H_C * V_C), row_out(KV_LORA)),
        compiler_params=_cp(("arbitrary",)),
        name="prep",
    )(proj, proj, proj, proj, proj, tab, gqa, gka, gcq, gckv, gqc, gkc, wuq, wukv, _pair_swaps())


def _cachekeys_kernel(ckv_ref, kr_ref, gkc_ref, wukv_ref, kc_o, vc_o):
    kc, vc = _mla_keys(ckv_ref[...], kr_ref[...], wukv_ref, gkc_ref[...], None)
    kc_o[...] = kc
    vc_o[...] = vc


def cache_keys(ckv, kr_blk, gkc, wukv):
    r = ckv.shape[0]
    tm = 256
    return pl.pallas_call(
        _cachekeys_kernel,
        out_shape=(jax.ShapeDtypeStruct((r, H_C * LANES), BF16), jax.ShapeDtypeStruct((r, H_C * V_C), BF16)),
        grid=(r // tm,),
        in_specs=[
            pl.BlockSpec((tm, LANES), lambda i: (i, 0)),
            pl.BlockSpec((tm, LANES), lambda i: (i, 0)),
            pl.BlockSpec((1, LANES), lambda i: (0, 0)),
            pl.BlockSpec((KV_LORA, H_C * LANES + H_C * V_C), lambda i: (0, 0)),
        ],
        out_specs=(pl.BlockSpec((tm, H_C * LANES), lambda i: (i, 0)), pl.BlockSpec((tm, H_C * V_C), lambda i: (i, 0))),
        compiler_params=_cp(("arbitrary",)),
        name="cache_keys",
    )(ckv, kr_blk, gkc, wukv)


def _attn_body(q_ref, segs, sink_ref, o_ref, *, n_heads, k_slot, v_slab, v_half, tq, band_qi=None):
    outs = []
    for h in range(n_heads):
        qh = q_ref[:, h * LANES:(h + 1) * LANES]
        scores = []
        for k_ref, _, off in segs:
            kh = k_ref[:, k_slot(h) * LANES:(k_slot(h) + 1) * LANES].astype(BF16)
            s = lax.dot_general(qh, kh, (((1,), (1,)), ((), ())), preferred_element_type=F32)
            if off is not None:
                tk = s.shape[1]
                blk = band_qi + off
                q_pos = band_qi * tq + lax.broadcasted_iota(jnp.int32, (tq, tk), 0)
                k_pos = blk * tk + lax.broadcasted_iota(jnp.int32, (tq, tk), 1)
                ok = (jnp.abs(k_pos - q_pos) <= WINDOW) & (blk >= 0) & (blk < DEC_SEQ // tk)
                s = jnp.where(ok, s, NEG_INF)
            scores.append(s)
        m = scores[0].max(axis=-1, keepdims=True)
        for s in scores[1:]:
            m = jnp.maximum(m, s.max(axis=-1, keepdims=True))
        if sink_ref is not None:
            sink = sink_ref[h:h + 1, 0:1]
            m = jnp.maximum(m, sink)
            denom = jnp.exp(sink - m)
        else:
            denom = jnp.zeros_like(m)
        acc = None
        for s, (_, v_ref, _) in zip(scores, segs):
            p = jnp.exp(s - m)
            denom = denom + p.sum(axis=-1, keepdims=True)
            vs = v_ref[:, v_slab(h) * LANES:(v_slab(h) + 1) * LANES].astype(BF16)
            pv = jnp.dot(p.astype(BF16), vs, preferred_element_type=F32)
            acc = pv if acc is None else acc + pv
        half = v_half(h)
        outs.append(acc[:, half * 64:(half + 1) * 64] * (1.0 / denom))
    o_ref[...] = jnp.concatenate(outs, axis=1).astype(o_ref.dtype)


ATT_CTX_SEQS = 4
_C_CFG = dict(n_heads=H_C, k_slot=lambda h: h, v_slab=lambda h: h // 2, v_half=lambda h: h % 2)


def _attn_a_body(q_ref, segs, sink_ref, o_ref, *, tq, band_qi=None):
    rows = HQ_A * tq
    q = jnp.concatenate([q_ref[:, h * LANES:(h + 1) * LANES] for h in range(HQ_A)], axis=0)
    sink = jnp.concatenate([jnp.broadcast_to(sink_ref[h:h + 1, 0:1], (tq, 1)) for h in range(HQ_A)], axis=0)
    scores = []
    for k_ref, _, off in segs:
        s = lax.dot_general(q, k_ref[...].astype(BF16), (((1,), (1,)), ((), ())), preferred_element_type=F32)
        if off is not None:
            tk = s.shape[1]
            blk = band_qi + off
            q_pos = band_qi * tq + (lax.broadcasted_iota(jnp.int32, (rows, tk), 0) & (tq - 1))
            k_pos = blk * tk + lax.broadcasted_iota(jnp.int32, (rows, tk), 1)
            ok = (jnp.abs(k_pos - q_pos) <= WINDOW) & (blk >= 0) & (blk < DEC_SEQ // tk)
            s = jnp.where(ok, s, NEG_INF)
        scores.append(s)
    m = sink
    for s in scores:
        m = jnp.maximum(m, s.max(axis=-1, keepdims=True))
    denom = jnp.exp(sink - m)
    acc = None
    for s, (_, v_ref, _) in zip(scores, segs):
        p = jnp.exp(s - m)
        denom = denom + p.sum(axis=-1, keepdims=True)
        pv = jnp.dot(p.astype(BF16), v_ref[...].astype(BF16), preferred_element_type=F32)
        acc = pv if acc is None else acc + pv
    acc = acc * (1.0 / denom)
    outs = []
    for h in range(HQ_A):
        kv = h // G_A
        outs.append(acc[h * tq:(h + 1) * tq, kv * HD_A:(kv + 1) * HD_A])
    o_ref[...] = jnp.concatenate(outs, axis=1).astype(o_ref.dtype)


def _attn_a_ctx_kernel(q_ref, k_ref, v_ref, sink_ref, o_init, o_ref):
    del o_init
    for b in range(ATT_CTX_SEQS):
        rows = pl.ds(b * SEQ, SEQ)
        _attn_a_body(q_ref.at[rows], [(k_ref.at[rows], v_ref.at[rows], None)], sink_ref, o_ref.at[rows], tq=SEQ)


def _attn_a_lat_kernel(q_ref, k0, k1, k2, v0, v1, v2, kc_ref, vc_ref, sink_ref, o_ctx, o_ref):
    del o_ctx
    qi = pl.program_id(1)
    segs = [(k0, v0, -1), (k1, v1, 0), (k2, v2, 1), (kc_ref, vc_ref, None)]
    _attn_a_body(q_ref, segs, sink_ref, o_ref, tq=BAND, band_qi=qi)


def _attn_c_ctx_kernel(q_ref, k_ref, v_ref, o_init, o_ref):
    del o_init
    for b in range(ATT_CTX_SEQS):
        rows = pl.ds(b * SEQ, SEQ)
        _attn_body(q_ref.at[rows], [(k_ref.at[rows], v_ref.at[rows], None)], None, o_ref.at[rows], tq=SEQ, **_C_CFG)


def _attn_c_lat_kernel(q_ref, k_ref, v_ref, kc_ref, vc_ref, o_ctx, o_ref):
    del o_ctx
    _attn_body(q_ref, [(k_ref, v_ref, None), (kc_ref, vc_ref, None)], None, o_ref, tq=256, **_C_CFG)


def attn_a_ctx(qa, ka, proj, sink):
    return pl.pallas_call(
        _attn_a_ctx_kernel,
        out_shape=jax.ShapeDtypeStruct((T_ALL, HQ_A * HD_A), BF16),
        grid=(BATCH // ATT_CTX_SEQS,),
        in_specs=[
            pl.BlockSpec((ATT_CTX_SEQS * SEQ, HQ_A * LANES), lambda b: (b, 0)),
            pl.BlockSpec((ATT_CTX_SEQS * SEQ, LANES), lambda b: (b, 0)),
            pl.BlockSpec((ATT_CTX_SEQS * SEQ, LANES), lambda b: (b, C_VA // LANES)),
            pl.BlockSpec((SUBLANES, LANES), lambda b: (0, 0)),
            pl.BlockSpec(memory_space=pl.ANY),
        ],
        out_specs=pl.BlockSpec((ATT_CTX_SEQS * SEQ, HQ_A * HD_A), lambda b: (b, 0)),
        input_output_aliases={4: 0},
        compiler_params=_cp(("arbitrary",)),
        name="attn_a_ctx",
    )(qa, ka, proj, sink, jnp.zeros((T_ALL, HQ_A * HD_A), BF16))


def attn_a_lat(qa, ka, proj, k_cache, v_cache, sink, o_all):
    nb = DEC_SEQ // BAND
    base = N_CTX // BAND

    def band(off, colblk):
        return pl.BlockSpec((BAND, LANES), lambda b, i: (base + b * nb + jnp.clip(i + off, 0, nb - 1), colblk))

    return pl.pallas_call(
        _attn_a_lat_kernel,
        out_shape=jax.ShapeDtypeStruct((T_ALL, HQ_A * HD_A), BF16),
        grid=(DEC_BATCH, nb),
        in_specs=[
            pl.BlockSpec((BAND, HQ_A * LANES), lambda b, i: (base + b * nb + i, 0)),
            band(-1, 0), band(0, 0), band(1, 0),
            band(-1, C_VA // LANES), band(0, C_VA // LANES), band(1, C_VA // LANES),
            pl.BlockSpec((None, PAST_LEN, LANES), lambda b, i: (b, 0, 0)),
            pl.BlockSpec((None, PAST_LEN, LANES), lambda b, i: (b, 0, 0)),
            pl.BlockSpec((SUBLANES, LANES), lambda b, i: (0, 0)),
            pl.BlockSpec(memory_space=pl.ANY),
        ],
        out_specs=pl.BlockSpec((BAND, HQ_A * HD_A), lambda b, i: (base + b * nb + i, 0)),
        input_output_aliases={10: 0},
        compiler_params=_cp(("arbitrary", "arbitrary")),
        name="attn_a_lat",
    )(qa, ka, ka, ka, proj, proj, proj, k_cache, v_cache, sink, o_all)


def attn_c_ctx(qc, kc, vc):
    return pl.pallas_call(
        _attn_c_ctx_kernel,
        out_shape=jax.ShapeDtypeStruct((T_ALL, H_C * V_C), BF16),
        grid=(BATCH // ATT_CTX_SEQS,),
        in_specs=[
            pl.BlockSpec((ATT_CTX_SEQS * SEQ, H_C * LANES), lambda b: (b, 0)),
            pl.BlockSpec((ATT_CTX_SEQS * SEQ, H_C * LANES), lambda b: (b, 0)),
            pl.BlockSpec((ATT_CTX_SEQS * SEQ, H_C * V_C), lambda b: (b, 0)),
            pl.BlockSpec(memory_space=pl.ANY),
        ],
        out_specs=pl.BlockSpec((ATT_CTX_SEQS * SEQ, H_C * V_C), lambda b: (b, 0)),
        input_output_aliases={3: 0},
        compiler_params=_cp(("arbitrary",)),
        name="attn_c_ctx",
    )(qc, kc, vc, jnp.zeros((T_ALL, H_C * V_C), BF16))


def attn_c_lat(qc, kc, vc, kc_cache, vc_cache, o_all):
    tq = 256
    nq = DEC_SEQ // tq
    qbase = N_CTX // tq
    kbase = N_CTX // DEC_SEQ
    return pl.pallas_call(
        _attn_c_lat_kernel,
        out_shape=jax.ShapeDtypeStruct((T_ALL, H_C * V_C), BF16),
        grid=(DEC_BATCH, nq),
        in_specs=[
            pl.BlockSpec((tq, H_C * LANES), lambda b, i: (qbase + b * nq + i, 0)),
            pl.BlockSpec((DEC_SEQ, H_C * LANES), lambda b, i: (kbase + b, 0)),
            pl.BlockSpec((DEC_SEQ, H_C * V_C), lambda b, i: (kbase + b, 0)),
            pl.BlockSpec((PAST_LEN, H_C * LANES), lambda b, i: (b, 0)),
            pl.BlockSpec((PAST_LEN, H_C * V_C), lambda b, i: (b, 0)),
            pl.BlockSpec(memory_space=pl.ANY),
        ],
        out_specs=pl.BlockSpec((tq, H_C * V_C), lambda b, i: (qbase + b * nq + i, 0)),
        input_output_aliases={5: 0},
        compiler_params=_cp(("arbitrary", "arbitrary")),
        name="attn_c_lat",
    )(qc, kc, vc, kc_cache, vc_cache, o_all)


S5_TC = 256
S5_CH = 512
S5_UNROLL = 8
TOK_CHUNKS = T_ALL // S5_TC
CHUNK_GROUP = 8
assert SEQ == S5_TC and DEC_SEQ == CHUNK_GROUP * S5_TC and BATCH == 4 * CHUNK_GROUP


def _s5_bproj(u2, wb_ref, pick_first):
    def part(lo, hi):
        r = jnp.dot(u2, wb_ref[0, :, lo:hi], preferred_element_type=F32)
        if pick_first is not None:
            r = jnp.where(pick_first, r, jnp.dot(u2, wb_ref[1, :, lo:hi], preferred_element_type=F32))
        return r.reshape(S5_TC, SUBLANES, S5_CH)
    return part(0, S5_CH), part(S5_CH, 2 * S5_CH)


def _s5_cproj(xre, xim, wc_ref, pick_first):
    rows = S5_TC * SUBLANES
    xr = xre[...].reshape(rows, S5_CH).astype(BF16)
    xi = xim[...].reshape(rows, S5_CH).astype(BF16)

    def part(d):
        return (jnp.dot(xr, wc_ref[d, 0:S5_CH, :], preferred_element_type=F32)
                + jnp.dot(xi, wc_ref[d, S5_CH:, :], preferred_element_type=F32))

    y = part(0)
    if pick_first is not None:
        y = jnp.where(pick_first, y, part(1))
    return jnp.swapaxes(y.reshape(S5_TC, SUBLANES, LANES), 0, 1)


def _s5_ctx_kernel(u_ref, wb_ref, wc_ref, are_ref, aim_ref, y_init, y_ref, fre_ref, fim_ref, bre, bim):
    del y_init
    d = pl.program_id(2)
    rows = S5_TC * SUBLANES
    u2 = jnp.swapaxes(u_ref[...], 0, 1).reshape(rows, LANES).astype(BF16)
    b_re, b_im = _s5_bproj(u2, wb_ref, None)
    bre[...] = b_re
    bim[...] = b_im
    a_re = are_ref[...]
    a_im = aim_ref[...]

    def outer(i, carry):
        sr, si = carry
        for j in range(S5_UNROLL):
            t0 = i * S5_UNROLL + j
            t = jnp.where(d == 1, S5_TC - 1 - t0, t0)
            nr = a_re * sr - a_im * si + bre[t]
            ni = a_re * si + a_im * sr + bim[t]
            bre[t] = nr
            bim[t] = ni
            sr, si = nr, ni
        return sr, si

    zero = jnp.zeros((SUBLANES, S5_CH), F32)
    sr, si = lax.fori_loop(0, S5_TC // S5_UNROLL, outer, (zero, zero))
    fre_ref[...] = sr
    fim_ref[...] = si
    y = _s5_cproj(bre, bim, wc_ref, None)

    @pl.when(d == 0)
    def _():
        y_ref[...] = y

    @pl.when(d == 1)
    def _():
        y_ref[...] += y


def s5_ctx(proj4, wb, wc, a_re, a_im):
    nj = D_B // LANES
    ucol = C_UB // LANES
    return pl.pallas_call(
        _s5_ctx_kernel,
        out_shape=(
            jax.ShapeDtypeStruct((TOK_CHUNKS // CHUNK_GROUP, CHUNK_GROUP, S5_TC, D_B), F32),
            jax.ShapeDtypeStruct((2, BATCH, G_B * P_B), F32),
            jax.ShapeDtypeStruct((2, BATCH, G_B * P_B), F32),
        ),
        grid=(BATCH // CHUNK_GROUP, nj, 2),
        in_specs=[
            pl.BlockSpec((None, CHUNK_GROUP, S5_TC, LANES), lambda sb, j, d: (sb, 0, 0, ucol + j)),
            pl.BlockSpec((1, None, LANES, 2 * S5_CH), lambda sb, j, d: (d, j, 0, 0)),
            pl.BlockSpec((1, None, 2 * S5_CH, LANES), lambda sb, j, d: (d, j, 0, 0)),
            pl.BlockSpec((None, SUBLANES, S5_CH), lambda sb, j, d: (d, 0, j)),
            pl.BlockSpec((None, SUBLANES, S5_CH), lambda sb, j, d: (d, 0, j)),
            pl.BlockSpec(memory_space=pl.ANY),
        ],
        out_specs=(
            pl.BlockSpec((None, CHUNK_GROUP, S5_TC, LANES), lambda sb, j, d: (sb, 0, 0, j)),
            pl.BlockSpec((None, SUBLANES, S5_CH), lambda sb, j, d: (d, sb, j)),
            pl.BlockSpec((None, SUBLANES, S5_CH), lambda sb, j, d: (d, sb, j)),
        ),
        scratch_shapes=[pltpu.VMEM((S5_TC, SUBLANES, S5_CH), F32), pltpu.VMEM((S5_TC, SUBLANES, S5_CH), F32)],
        input_output_aliases={5: 0},
        compiler_params=_cp(("arbitrary", "arbitrary", "arbitrary")),
        name="s5_ctx",
    )(proj4, wb, wc, a_re, a_im, jnp.zeros((TOK_CHUNKS // CHUNK_GROUP, CHUNK_GROUP, S5_TC, D_B), F32))


def _s5_lat_kernel(uf_ref, ub_ref, wb_ref, wc_ref, are_ref, aim_ref, h0re_ref, h0im_ref, y_ctx,
                   yf_ref, yb_ref, bre, bim, xre, xim, sre, sim):
    del y_ctx
    k = pl.program_id(1)

    @pl.when(k == 0)
    def _():
        sre[...] = h0re_ref[...]
        sim[...] = h0im_ref[...]

    rows = S5_TC * SUBLANES
    half = SUBLANES // 2
    u8 = jnp.concatenate([uf_ref[...], ub_ref[...]], axis=0)
    u2 = jnp.swapaxes(u8, 0, 1).reshape(rows, LANES).astype(BF16)
    fwd_rows = (lax.broadcasted_iota(jnp.int32, (rows, 1), 0) % SUBLANES) < half
    b_re, b_im = _s5_bproj(u2, wb_ref, fwd_rows)
    bre[...] = b_re
    bim[...] = b_im
    a_re = are_ref[...]
    a_im = aim_ref[...]
    fwd8 = lax.broadcasted_iota(jnp.int32, (SUBLANES, 1), 0) < half

    def outer(i, carry):
        sr, si = carry
        for j in range(S5_UNROLL):
            t = i * S5_UNROLL + j
            tb = S5_TC - 1 - t
            nr = a_re * sr - a_im * si + jnp.where(fwd8, bre[t], bre[tb])
            ni = a_re * si + a_im * sr + jnp.where(fwd8, bim[t], bim[tb])
            xre[t, 0:half, :] = nr[0:half]
            xre[tb, half:, :] = nr[half:]
            xim[t, 0:half, :] = ni[0:half]
            xim[tb, half:, :] = ni[half:]
            sr, si = nr, ni
        return sr, si

    sr, si = lax.fori_loop(0, S5_TC // S5_UNROLL, outer, (sre[...], sim[...]))
    sre[...] = sr
    sim[...] = si
    y = _s5_cproj(xre, xim, wc_ref, fwd_rows)
    yf_ref[...] = y[0:half]
    yb_ref[...] = y[half:]


def s5_lat(proj4, y_all, wb, wc, a_re, a_im, h0_re, h0_im):
    nj = D_B // LANES
    nk = DEC_SEQ // S5_TC
    ucol = C_UB // LANES
    lat = lambda col0, rev: pl.BlockSpec(
        (DEC_BATCH, None, S5_TC, LANES), lambda j, k: (1, (nk - 1 - k) if rev else k, 0, col0 + j))
    vec = lambda: pl.BlockSpec((SUBLANES, S5_CH), lambda j, k: (0, j))
    buf = lambda: pltpu.VMEM((S5_TC, SUBLANES, S5_CH), F32)
    return pl.pallas_call(
        _s5_lat_kernel,
        out_shape=(
            jax.ShapeDtypeStruct(y_all.shape, F32),
            jax.ShapeDtypeStruct((DEC_BATCH, nk, S5_TC, D_B), F32),
        ),
        grid=(nj, nk),
        in_specs=[
            lat(ucol, False), lat(ucol, True),
            pl.BlockSpec((2, None, LANES, 2 * S5_CH), lambda j, k: (0, j, 0, 0)),
            pl.BlockSpec((2, None, 2 * S5_CH, LANES), lambda j, k: (0, j, 0, 0)),
            vec(), vec(), vec(), vec(),
            pl.BlockSpec(memory_space=pl.ANY),
        ],
        out_specs=(
            lat(0, False),
            pl.BlockSpec((DEC_BATCH, None, S5_TC, LANES), lambda j, k: (0, nk - 1 - k, 0, j)),
        ),
        scratch_shapes=[buf(), buf(), buf(), buf(),
                        pltpu.VMEM((SUBLANES, S5_CH), F32), pltpu.VMEM((SUBLANES, S5_CH), F32)],
        input_output_aliases={8: 0},
        compiler_params=_cp(("arbitrary", "arbitrary")),
        name="s5_lat",
    )(proj4, proj4, wb, wc, a_re, a_im, h0_re, h0_im, y_all)


S5FIN_TM = 512


def _s5fin_kernel(u_ref, y_ref, yb_ref, d_ref, w_ref, b_ref, o_ref):
    def body(latent):
        y = d_ref[...] * u_ref[...] + y_ref[...]
        if latent:
            y = y + yb_ref[...]
        y = 0.5 * y * (1.0 + jnp.tanh(math.sqrt(2.0 / math.pi) * (y + 0.044715 * (y * y * y))))
        z = jnp.dot(y.astype(BF16), w_ref[...], preferred_element_type=F32) + b_ref[...]
        o_ref[...] = (y * _sigmoid(z)).astype(o_ref.dtype)

    is_latent = pl.program_id(0) >= N_CTX // S5FIN_TM
    pl.when(is_latent)(lambda: body(True))
    pl.when(jnp.logical_not(is_latent))(lambda: body(False))


def s5_finish(proj, y, y_bwd_lat, d, w_glu, b_glu):
    tm = S5FIN_TM
    n_ctx_tiles = N_CTX // tm
    return pl.pallas_call(
        _s5fin_kernel,
        out_shape=jax.ShapeDtypeStruct((T_ALL, D_B), BF16),
        grid=(T_ALL // tm,),
        in_specs=[
            pl.BlockSpec((tm, D_B), lambda i: (i, C_UB // D_B)),
            pl.BlockSpec((tm, D_B), lambda i: (i, 0)),
            pl.BlockSpec((tm, D_B), lambda i: (jnp.maximum(i - n_ctx_tiles, 0), 0)),
            pl.BlockSpec((1, D_B), lambda i: (0, 0)),
            pl.BlockSpec((D_B, D_B), lambda i: (0, 0)),
            pl.BlockSpec((1, D_B), lambda i: (0, 0)),
        ],
        out_specs=pl.BlockSpec((tm, D_B), lambda i: (i, 0)),
        compiler_params=_cp(("arbitrary",)),
        name="s5_finish",
    )(proj, y, y_bwd_lat, d, w_glu, b_glu)


def _merge_kernel(xc_ref, xl_ref, gl_ref, oa_ref, ob_ref, oc_ref, mod_ref, g_ref, wa_ref, wb_ref, wc_ref, wo_ref,
                  wr_ref, br_ref, x1_ref, h2row_ref, idx_ref, rank_ref, wgt_ref, cnt_ref, cnt_acc):
    @pl.when(pl.program_id(0) == 0)
    def _():
        cnt_acc[...] = jnp.zeros_like(cnt_acc)

    m = None
    for br, (o_ref, w_ref) in enumerate(((oa_ref, wa_ref), (ob_ref, wb_ref), (oc_ref, wc_ref))):
        gate = _sigmoid(gl_ref[:, br * D_MODEL:(br + 1) * D_MODEL])
        t = gate * jnp.dot(o_ref[...], w_ref[...], preferred_element_type=F32)
        m = t if m is None else m + t
    x = _x_tile(xc_ref, xl_ref, pl.program_id(0), ROUTE_TM)
    x1 = x + mod_ref[2:3, :] * jnp.dot(m.astype(BF16), wo_ref[...], preferred_element_type=F32)
    x1_ref[...] = x1
    h2 = _rms(x1, g_ref[...])
    h2 = h2 * (1.0 + mod_ref[4:5, :]) + mod_ref[3:4, :]
    h2row_ref[...] = h2.reshape((h2.shape[0],) + ROW3)
    idx, rank, wgt, cnt = _route(h2, wr_ref, br_ref, cnt_acc[...])
    idx_ref[...] = idx
    rank_ref[...] = rank
    wgt_ref[...] = wgt
    cnt_acc[...] = cnt
    cnt_ref[...] = cnt


def merge(x_ctx, x_lat, proj, oa, ob, oc, mod, g, wa, wb, wc, wo, w_router, b_router):
    tm = ROUTE_TM
    full = lambda shape: pl.BlockSpec(shape, lambda i: (0,) * len(shape))
    tile = lambda: pl.BlockSpec((tm, LANES), lambda i: (i, 0))
    return pl.pallas_call(
        _merge_kernel,
        out_shape=(jax.ShapeDtypeStruct((T_ALL, D_MODEL), F32), jax.ShapeDtypeStruct((T_ALL,) + ROW3, F32),
                   jax.ShapeDtypeStruct((T_ALL, LANES), jnp.int32), jax.ShapeDtypeStruct((T_ALL, LANES), jnp.int32),
                   jax.ShapeDtypeStruct((T_ALL, LANES), F32), jax.ShapeDtypeStruct((1, LANES), F32)),
        grid=(T_ALL // tm,),
        in_specs=_x_specs(tm, lambda i: i) + [
            pl.BlockSpec((tm, 3 * D_MODEL), lambda i: (i, 0)),
            pl.BlockSpec((tm, 512), lambda i: (i, 0)),
            pl.BlockSpec((tm, 512), lambda i: (i, 0)),
            pl.BlockSpec((tm, 512), lambda i: (i, 0)),
            pl.BlockSpec((None, 6, D_MODEL), lambda i: (_group_of_tile(i, tm), 0, 0)),
            full((1, D_MODEL)),
            full((512, D_MODEL)), full((512, D_MODEL)), full((512, D_MODEL)), full((D_MODEL, D_MODEL)),
            full((D_MODEL, LANES)), full((1, LANES)),
        ],
        out_specs=(pl.BlockSpec((tm, D_MODEL), lambda i: (i, 0)), pl.BlockSpec((tm,) + ROW3, lambda i: (i, 0, 0)),
                   tile(), tile(), tile(), full((1, LANES))),
        scratch_shapes=[pltpu.VMEM((1, LANES), F32)],
        compiler_params=_cp(("arbitrary",)),
        name="merge",
    )(x_ctx, x_lat, proj, oa, ob, oc, mod, g, wa, wb, wc, wo, w_router, b_router)


ROUTE_TM = 512


def _route(h, w_ref, b_ref, cnt):
    w = w_ref[...]
    h_hi = h.astype(BF16)
    h_lo = (h - h_hi.astype(F32)).astype(BF16)
    w_hi = w.astype(BF16)
    w_lo = (w - w_hi.astype(F32)).astype(BF16)
    dot = functools.partial(jnp.dot, preferred_element_type=F32)
    logits = dot(h_hi, w_hi) + (dot(h_hi, w_lo) + dot(h_lo, w_hi)) + b_ref[...]
    lane_i = lax.broadcasted_iota(jnp.int32, (ROUTE_TM, LANES), 1)
    lane = lane_i.astype(F32)
    r_i = lax.broadcasted_iota(jnp.int32, (ROUTE_TM, ROUTE_TM), 0)
    c_i = lax.broadcasted_iota(jnp.int32, (ROUTE_TM, ROUTE_TM), 1)
    earlier = jnp.where(c_i < r_i, 1.0, 0.0).astype(BF16)

    idx_out = jnp.zeros((ROUTE_TM, LANES), F32)
    rank_out = jnp.zeros((ROUTE_TM, LANES), F32)
    val_out = jnp.zeros((ROUTE_TM, LANES), F32)
    v0 = None
    esum = None
    for k in range(TOP_K):
        m = logits.max(axis=-1, keepdims=True)
        sel = jnp.min(jnp.where(logits == m, lane, float(LANES)), axis=-1, keepdims=True)
        hit = lane == sel
        logits = jnp.where(hit, -jnp.inf, logits)
        onehot = jnp.where(hit, 1.0, 0.0)
        within = jnp.dot(earlier, onehot.astype(BF16), preferred_element_type=F32)
        rank = jnp.sum(onehot * (within + cnt), axis=-1, keepdims=True)
        cnt = cnt + jnp.sum(onehot, axis=0, keepdims=True)
        if k == 0:
            v0 = m
        e = jnp.exp(m - v0)
        esum = e if esum is None else esum + e
        idx_out = jnp.where(lane_i == k, sel, idx_out)
        rank_out = jnp.where(lane_i == k, rank, rank_out)
        val_out = jnp.where(lane_i == k, e, val_out)
    return idx_out.astype(jnp.int32), rank_out.astype(jnp.int32), val_out * (1.0 / esum), cnt


PLAN_UNROLL = 16
N_DUMP = 5 * BLK
PLAN_LEN = BUF_LEN + BLK
assert TOP_K == 4 and T_ALL & (T_ALL - 1) == 0


def _invert_kernel(dest_ref, fill_hbm, inv_ref):
    pltpu.sync_copy(fill_hbm, inv_ref)
    tok_step = PLAN_UNROLL // TOP_K

    def put(i, c):
        for j in range(PLAN_UNROLL):
            inv_ref[dest_ref[i * PLAN_UNROLL + j]] = i * tok_step + ((j % TOP_K) * T_ALL + j // TOP_K)
        return c

    lax.fori_loop(0, N_ASSIGN // PLAN_UNROLL, put, 0)


def invert_plan(dest):
    r = jnp.arange(PLAN_LEN, dtype=jnp.int32)
    fill = N_ASSIGN + jnp.where(r < BLK, 2 * BLK + r, (r - BLK) & (2 * BLK - 1))
    return pl.pallas_call(
        _invert_kernel,
        out_shape=jax.ShapeDtypeStruct((PLAN_LEN,), jnp.int32),
        in_specs=[pl.BlockSpec(memory_space=pltpu.SMEM), pl.BlockSpec(memory_space=pl.ANY)],
        out_specs=pl.BlockSpec(memory_space=pltpu.SMEM),
        name="invert_plan",
    )(dest, fill)


PAIR_TILE = 2 * LANES


def _pair_selection():
    r = lax.broadcasted_iota(jnp.int32, (PAIR_TILE, PAIR_TILE), 0)
    c = lax.broadcasted_iota(jnp.int32, (PAIR_TILE, PAIR_TILE), 1)
    return (r == jnp.where(c < LANES, 2 * c, 2 * (c - LANES) + 1)).astype(BF16)


EXP_NBUF = 3


def _expert_kernel(be_ref, first_ref, par_ref, nxt_ref, inv_ref, h_hbm, wu_hbm, bu_ref, wd_hbm, bd_ref, sel_ref,
                   comb_hbm, xbuf, ybuf, wu_ref, wd_ref, wu_st, wd_st, gsem, ssem, wsem, *, layer):
    i = pl.program_id(0)
    n = pl.num_programs(0)
    cur = i % EXP_NBUF
    nxt = (i + 2) % EXP_NBUF
    prv = nxt

    def gather(block, s):
        base = (block + 1) * BLK
        for r in range(BLK):
            tok = inv_ref[base + r] & (T_ALL - 1)
            pltpu.make_async_copy(h_hbm.at[tok], xbuf.at[s, r], gsem.at[s]).start()

    def scatter(block, s):
        base = (block + 1) * BLK
        for r in range(BLK):
            a = inv_ref[base + r]
            pltpu.make_async_copy(ybuf.at[s, r], comb_hbm.at[a], ssem.at[s]).start()

    def wait_block(sem):
        pltpu.make_async_copy(h_hbm.at[pl.ds(0, BLK)], xbuf.at[0], sem).wait()

    def weight_copies(e, p):
        return (pltpu.make_async_copy(wu_hbm.at[layer, e], wu_st.at[p], wsem.at[p]),
                pltpu.make_async_copy(wd_hbm.at[layer, e], wd_st.at[p], wsem.at[p]))

    @pl.when(i == 0)
    def _():
        for cp in weight_copies(be_ref[0], 0):
            cp.start()
        ybuf[...] = jnp.zeros_like(ybuf)
        for s in range(EXP_NBUF - 1):
            for r in range(BLK):
                dump = N_ASSIGN + (3 + s) * BLK + r
                pltpu.make_async_copy(ybuf.at[s, r], comb_hbm.at[dump], ssem.at[s]).start()
        gather(0, 0)
        gather(1, 1)

    wait_block(gsem.at[cur])
    wait_block(ssem.at[cur])

    @pl.when(first_ref[i] == 1)
    def _():
        p = par_ref[i]
        for cp in weight_copies(be_ref[i], p):
            cp.wait()

        @pl.when(nxt_ref[i] >= 0)
        def _():
            for cp in weight_copies(nxt_ref[i], 1 - p):
                cp.start()

        sel = sel_ref[...]
        for j in range(2 * D_FF // PAIR_TILE):
            cols = slice(j * PAIR_TILE, (j + 1) * PAIR_TILE)
            wu_ref[:, cols] = jnp.dot(wu_st[p, :, cols].astype(BF16), sel, preferred_element_type=F32).astype(BF16)
        wd_ref[...] = wd_st[p].astype(BF16)

    gather(jnp.minimum(i + 2, n - 1), nxt)
    scatter(i - 1, prv)

    x = xbuf[cur].reshape(BLK, D_MODEL).astype(BF16)
    h = jnp.dot(x, wu_ref[...], preferred_element_type=F32) + bu_ref[...]
    acts = []
    for j in range(2 * D_FF // PAIR_TILE):
        glu = jnp.minimum(h[:, j * PAIR_TILE:j * PAIR_TILE + LANES], SWIGLU_LIMIT)
        lin = jnp.clip(h[:, j * PAIR_TILE + LANES:(j + 1) * PAIR_TILE], -SWIGLU_LIMIT, SWIGLU_LIMIT)
        acts.append((glu * _sigmoid(SWIGLU_ALPHA * glu) * (lin + 1.0)).astype(BF16))
    act = jnp.concatenate(acts, axis=1)
    y = jnp.dot(act, wd_ref[...], preferred_element_type=F32) + bd_ref[...]
    ybuf[cur] = y.reshape((BLK,) + ROW3)

    @pl.when(i == n - 1)
    def _():
        for s in range(EXP_NBUF):
            @pl.when(s != cur)
            def _():
                wait_block(gsem.at[s])
                wait_block(ssem.at[s])
        scatter(i, cur)
        wait_block(ssem.at[cur])


def experts(layer, block_e, inv, h2row, wu, bu, wd, bd):
    first = jnp.concatenate([jnp.ones((1,), jnp.int32), (block_e[1:] != block_e[:-1]).astype(jnp.int32)])
    slot = (jnp.cumsum(first) - 1) & 1
    later = jnp.where(block_e[None, :] > block_e[:, None], block_e[None, :], N_EXP).min(axis=1)
    nxt = jnp.where(later == N_EXP, -1, later).astype(jnp.int32)
    per_expert = lambda *blk: pl.BlockSpec((None, None) + blk, lambda i, be, *_: (layer, be[i], 0, 0))
    return pl.pallas_call(
        functools.partial(_expert_kernel, layer=layer),
        out_shape=jax.ShapeDtypeStruct((N_ASSIGN + N_DUMP,) + ROW3, F32),
        grid_spec=pltpu.PrefetchScalarGridSpec(
            num_scalar_prefetch=5,
            grid=(N_BLOCKS,),
            in_specs=[
                pl.BlockSpec(memory_space=pl.ANY),
                pl.BlockSpec(memory_space=pl.ANY), per_expert(1, 2 * D_FF),
                pl.BlockSpec(memory_space=pl.ANY), per_expert(1, D_MODEL),
                pl.BlockSpec((PAIR_TILE, PAIR_TILE), lambda i, *_: (0, 0)),
            ],
            out_specs=pl.BlockSpec(memory_space=pl.ANY),
            scratch_shapes=[
                pltpu.VMEM((EXP_NBUF, BLK) + ROW3, F32), pltpu.VMEM((EXP_NBUF, BLK) + ROW3, F32),
                pltpu.VMEM((D_MODEL, 2 * D_FF), BF16), pltpu.VMEM((D_FF, D_MODEL), BF16),
                pltpu.VMEM((2, D_MODEL, 2 * D_FF), F32), pltpu.VMEM((2, D_FF, D_MODEL), F32),
                pltpu.SemaphoreType.DMA((EXP_NBUF,)), pltpu.SemaphoreType.DMA((EXP_NBUF,)),
                pltpu.SemaphoreType.DMA((2,)),
            ],
        ),
        compiler_params=_cp(("arbitrary",)),
        name="experts",
    )(block_e, first, slot.astype(jnp.int32), nxt, inv, h2row, wu, bu, wd, bd, _pair_selection())


COMB_TM = 256


def _combine_kernel(c0_ref, c1_ref, c2_ref, c3_ref, x_ref, w_ref, mod_ref, oc_ref, ol_ref):
    w = w_ref[...]
    acc = None
    for k, c_ref in enumerate((c0_ref, c1_ref, c2_ref, c3_ref)):
        t = w[:, k:k + 1] * c_ref[...].reshape(COMB_TM, D_MODEL)
        acc = t if acc is None else acc + t
    out = x_ref[...] + mod_ref[5:6, :] * acc
    is_latent = pl.program_id(0) >= N_CTX // COMB_TM

    @pl.when(is_latent)
    def _():
        ol_ref[...] = out

    @pl.when(jnp.logical_not(is_latent))
    def _():
        oc_ref[...] = out


def combine(comb, x1, wgt, mod):
    tiles = T_ALL // COMB_TM
    kth = lambda k: pl.BlockSpec((COMB_TM,) + ROW3, lambda i: (k * tiles + i, 0, 0))
    return pl.pallas_call(
        _combine_kernel,
        out_shape=(jax.ShapeDtypeStruct((N_CTX, D_MODEL), F32), jax.ShapeDtypeStruct((N_LAT, D_MODEL), F32)),
        grid=(tiles,),
        in_specs=[
            kth(0), kth(1), kth(2), kth(3),
            pl.BlockSpec((COMB_TM, D_MODEL), lambda i: (i, 0)),
            pl.BlockSpec((COMB_TM, LANES), lambda i: (i, 0)),
            pl.BlockSpec((None, 6, D_MODEL), lambda i: (_group_of_tile(i, COMB_TM), 0, 0)),
        ],
        out_specs=tuple(_x_specs(COMB_TM, lambda i: i)),
        compiler_params=_cp(("arbitrary",)),
        name="combine",
    )(comb, comb, comb, comb, x1, wgt, mod)


def _rope_table():
    pos = jnp.arange(DEC_SEQ)
    row = (pos // GRID_W).astype(F32)[:, None]
    col = (pos % GRID_W).astype(F32)[:, None]

    def parts(rot_dim):
        nf = rot_dim // 4
        inv = ROPE_BASE ** (-jnp.arange(nf, dtype=F32) / nf)
        cr, sr, cc, sc = jnp.cos(row * inv), jnp.sin(row * inv), jnp.cos(col * inv), jnp.sin(col * inv)
        return jnp.concatenate([cr, cr, cc, cc], axis=1), jnp.concatenate([-sr, sr, -sc, sc], axis=1)

    ca, sa = (jnp.tile(t, (1, 2)) for t in parts(HD_A))
    cc, sc = parts(ROPE_C)
    pad = lambda t, fill: jnp.pad(t, ((0, 0), (NOPE_C, LANES - QK_C)), constant_values=fill)
    lat = jnp.concatenate([ca, sa, pad(cc, 1.0), pad(sc, 0.0)], axis=1)
    ones, zeros = jnp.ones((DEC_SEQ, LANES), F32), jnp.zeros((DEC_SEQ, LANES), F32)
    ident = jnp.concatenate([ones, zeros, ones, zeros], axis=1)
    return jnp.concatenate([ident, lat], axis=0)


def _pair_swaps():
    j = lax.broadcasted_iota(jnp.int32, (LANES, LANES), 0)
    i = lax.broadcasted_iota(jnp.int32, (LANES, LANES), 1)

    def swap(half):
        first = (i % (2 * half)) < half
        return (j == jnp.where(first, i + half, i - half)).astype(BF16)

    return jnp.stack([swap(HD_A // 4), swap(ROPE_C // 4)])


def _arrange_w_in(w_in):
    o = 0
    parts = {}
    for name, n in (("qa", HQ_A * HD_A), ("ka", HKV_A * HD_A), ("va", HKV_A * HD_A), ("ub", D_B), ("cq", Q_LORA),
                    ("ckv", KV_LORA), ("kr", ROPE_C), ("gl", 3 * D_MODEL)):
        parts[name] = w_in[..., o:o + n]
        o += n
    qa = parts["qa"].reshape(DEPTH, D_MODEL, HKV_A, G_A, 1, HD_A)
    eye = jnp.eye(HKV_A, dtype=F32).reshape(1, 1, HKV_A, 1, HKV_A, 1)
    qa_slots = (qa * eye).reshape(DEPTH, D_MODEL, HQ_A * LANES)
    kr = jnp.pad(parts["kr"], ((0, 0), (0, 0), (NOPE_C, LANES - QK_C)))
    w = jnp.concatenate([parts["gl"], qa_slots, parts["ub"], parts["cq"], parts["ka"], parts["va"], parts["ckv"], kr],
                        axis=-1)
    return w.astype(BF16)


def _s5_params(lam_re, lam_im, log_dt, b_re, b_im, c_re, c_im):
    dt = jnp.exp(log_dt)[..., None]
    decay = jnp.exp(lam_re * dt)
    ab_re, ab_im = decay * jnp.cos(lam_im * dt), decay * jnp.sin(lam_im * dt)
    den = lam_re * lam_re + lam_im * lam_im
    f_re = ((ab_re - 1) * lam_re + ab_im * lam_im) / den
    f_im = (ab_im * lam_re - (ab_re - 1) * lam_im) / den
    bb_re = f_re[..., None] * b_re - f_im[..., None] * b_im
    bb_im = f_re[..., None] * b_im + f_im[..., None] * b_re
    nj, gpb = D_B // LANES, LANES // GS_B
    eye = jnp.eye(gpb, dtype=F32)

    def blockdiag_b(bb):
        t = bb.transpose(0, 1, 3, 2).reshape(2, nj, gpb, GS_B, P_B)
        return (t[:, :, :, :, None, :] * eye[None, None, :, None, :, None]).reshape(2, nj, LANES, gpb * P_B)

    def blockdiag_c(cc):
        t = cc.transpose(0, 1, 3, 2).reshape(2, nj, gpb, P_B, GS_B)
        return (t[:, :, :, :, None, :] * eye[None, None, :, None, :, None]).reshape(2, nj, gpb * P_B, LANES)

    wb = jnp.concatenate([blockdiag_b(bb_re), blockdiag_b(bb_im)], axis=-1).astype(BF16)
    wc = jnp.concatenate([blockdiag_c(c_re), -blockdiag_c(c_im)], axis=-2).astype(BF16)
    return wb, wc, ab_re.reshape(2, G_B * P_B), ab_im.reshape(2, G_B * P_B)


def kernel(x_prompt, x_sample, c, cache_attn_k, cache_attn_v, cache_mla_ckv, cache_mla_krope, state_ssm_re, state_ssm_im, c_ctx, w_ada, b_ada, norm_mix_g, norm_ffn_g, w_in, q_norm_a, k_norm_a, sink_a, q_a_norm_c, kv_a_norm_c, w_uq_c, w_ukv_c, q_norm_c, k_norm_c, ssm_lam_re, ssm_lam_im, ssm_log_dt, ssm_b_re, ssm_b_im, ssm_c_re, ssm_c_im, ssm_d, w_glu, b_glu, w_br_a, w_br_b, w_br_c, w_out, w_router, b_router, w_up, b_up, w_down, b_down):
    x_ctx, x_lat = x_prompt.reshape(N_CTX, D_MODEL), x_sample.reshape(N_LAT, D_MODEL)
    cvecs = jnp.concatenate([c_ctx[None], c, jnp.zeros((N_GROUPS - 1 - DEC_BATCH, D_MODEL), F32)], axis=0)
    mods = adaln(cvecs, w_ada, b_ada)

    tab = _rope_table()
    w_in_r = _arrange_w_in(w_in)
    pad_slot = lambda g: jnp.pad(g, ((0, 0), (0, LANES - QK_C))).reshape(DEPTH, 1, LANES)
    gqa = jnp.tile(q_norm_a, (1, 2)).reshape(DEPTH, 1, LANES)
    gka = jnp.tile(k_norm_a, (1, 2)).reshape(DEPTH, 1, LANES)
    gqc, gkc = pad_slot(q_norm_c), pad_slot(k_norm_c)
    wuq = jnp.pad(w_uq_c.reshape(DEPTH, Q_LORA, H_C, QK_C), ((0, 0), (0, 0), (0, 0), (0, LANES - QK_C)))
    wuq = wuq.reshape(DEPTH, Q_LORA, H_C * LANES).astype(BF16)
    wukv4 = w_ukv_c.reshape(DEPTH, KV_LORA, H_C, NOPE_C + V_C)
    wuk = jnp.pad(wukv4[..., :NOPE_C], ((0, 0), (0, 0), (0, 0), (0, LANES - NOPE_C))).reshape(DEPTH, KV_LORA, H_C * LANES)
    wuv = wukv4[..., NOPE_C:].reshape(DEPTH, KV_LORA, H_C * V_C)
    wukv = jnp.concatenate([wuk, wuv], axis=-1).astype(BF16)
    sink = jnp.broadcast_to(sink_a[:, :, None], (DEPTH, HQ_A, LANES))
    w_router_p = jnp.pad(w_router, ((0, 0), (0, 0), (0, LANES - N_EXP)))
    b_router_p = jnp.pad(b_router, ((0, 0), (0, LANES - N_EXP)), constant_values=-jnp.inf).reshape(DEPTH, 1, LANES)
    b_up_p = b_up.reshape(DEPTH, N_EXP, 2 * D_FF // PAIR_TILE, LANES, 2).transpose(0, 1, 2, 4, 3)
    b_up_p = b_up_p.reshape(DEPTH, N_EXP, 1, 2 * D_FF)
    b_down_r = b_down.reshape(DEPTH, N_EXP, 1, D_MODEL)
    kr_cache = jnp.pad(cache_mla_krope, ((0, 0), (0, 0), (0, 0), (NOPE_C, LANES - QK_C)))

    new_k, new_v, new_ckv, new_kr, new_sre, new_sim = [], [], [], [], [], []
    for l in range(DEPTH):
        mod = mods[l]
        proj = inproj(l, x_ctx, x_lat, norm_mix_g[l], mod, w_in_r)
        qa, ka, qc, kc, vc, ckvn = prep(proj, tab, gqa[l], gka[l], q_a_norm_c[l].reshape(1, Q_LORA),
                                         kv_a_norm_c[l].reshape(1, KV_LORA), gqc[l], gkc[l], wuq[l], wukv[l])
        kc_cache, vc_cache = cache_keys(cache_mla_ckv[:, l].reshape(DEC_BATCH * PAST_LEN, KV_LORA),
                                        kr_cache[:, l].reshape(DEC_BATCH * PAST_LEN, LANES), gkc[l], wukv[l])
        oa = attn_a_lat(qa, ka, proj, cache_attn_k[:, l].reshape(DEC_BATCH, PAST_LEN, LANES),
                        cache_attn_v[:, l].reshape(DEC_BATCH, PAST_LEN, LANES), sink[l],
                        attn_a_ctx(qa, ka, proj, sink[l]))
        oc = attn_c_lat(qc, kc, vc, kc_cache, vc_cache, attn_c_ctx(qc, kc, vc))
        wb, wc, a_re, a_im = _s5_params(ssm_lam_re[l], ssm_lam_im[l], ssm_log_dt[l], ssm_b_re[l], ssm_b_im[l],
                                        ssm_c_re[l], ssm_c_im[l])
        proj4 = proj.reshape(TOK_CHUNKS // CHUNK_GROUP, CHUNK_GROUP, S5_TC, N_PROJ)
        per_row = lambda a, rep: jnp.repeat(a, rep, axis=0)
        y_all, f_re, f_im = s5_ctx(proj4, wb, wc, per_row(a_re, SUBLANES).reshape(2, SUBLANES, -1),
                                   per_row(a_im, SUBLANES).reshape(2, SUBLANES, -1))
        h0 = lambda st: st[:, l].transpose(1, 0, 2, 3).reshape(2 * DEC_BATCH, G_B * P_B)
        y_all, y_bwd = s5_lat(proj4, y_all, wb, wc, per_row(a_re, DEC_BATCH), per_row(a_im, DEC_BATCH),
                              h0(state_ssm_re), h0(state_ssm_im))
        ob = s5_finish(proj, y_all.reshape(T_ALL, D_B), y_bwd.reshape(N_LAT, D_B), ssm_d[l].reshape(1, D_B),
                       w_glu[l].astype(BF16), b_glu[l].reshape(1, D_B))
        x1, h2row, idx, rank, wgt, cnt = merge(
            x_ctx, x_lat, proj, oa, ob, oc, mod, norm_ffn_g[l].reshape(1, D_MODEL), w_br_a[l].astype(BF16),
            w_br_b[l].astype(BF16), w_br_c[l].astype(BF16), w_out[l].astype(BF16), w_router_p[l], b_router_p[l])
        counts = cnt[0, :N_EXP].astype(jnp.int32)
        padded = (counts + BLK - 1) // BLK * BLK
        pad_end = jnp.cumsum(padded)
        pad_start = pad_end - padded
        dest = (BLK + pad_start[idx[:, :TOP_K]] + rank[:, :TOP_K]).reshape(-1).astype(jnp.int32)
        block_start = jnp.arange(N_BLOCKS, dtype=jnp.int32) * BLK
        block_e = jnp.minimum(jnp.sum(pad_end[None, :] <= block_start[:, None], axis=1), N_EXP - 1)
        block_e = block_e.astype(jnp.int32)
        comb = experts(l, block_e, invert_plan(dest), h2row, w_up, b_up_p, w_down, b_down_r)
        x_ctx, x_lat = combine(comb, x1, wgt, mod)

        new_k.append(ka[:N_CTX].reshape(BATCH, SEQ, HKV_A, HD_A))
        new_v.append(proj[:N_CTX, C_VA:C_VA + LANES].reshape(BATCH, SEQ, HKV_A, HD_A))
        new_ckv.append(ckvn[:N_CTX].reshape(BATCH, SEQ, KV_LORA))
        new_kr.append(proj[:N_CTX, C_KR + NOPE_C:C_KR + QK_C].reshape(BATCH, SEQ, ROPE_C))
        new_sre.append(f_re.reshape(2, BATCH, G_B, P_B).transpose(1, 0, 2, 3))
        new_sim.append(f_im.reshape(2, BATCH, G_B, P_B).transpose(1, 0, 2, 3))

    y_prompt = x_ctx.reshape(BATCH, SEQ, D_MODEL)
    y_sample = x_lat.reshape(DEC_BATCH, DEC_SEQ, D_MODEL)
    return (y_prompt, y_sample, jnp.stack(new_k, axis=1), jnp.stack(new_v, axis=1), jnp.stack(new_ckv, axis=1),
            jnp.stack(new_kr, axis=1), jnp.stack(new_sre, axis=1), jnp.stack(new_sim, axis=1))
```

```python
import functools
import math

import jax
import jax.numpy as jnp
from jax import lax
from jax.experimental import pallas as pl
from jax.experimental.pallas import tpu as pltpu

D_MODEL = 1024
BATCH = 32
SEQ = 256
DEPTH = 2
DEC_BATCH = 4
DEC_SEQ = 2048
PAST_LEN = 256
GRID_W = 64
ROPE_BASE = 10000.0
EPS = 1e-6
NEG_INF = -1e30
BAND = 128
BLK = 128
HQ_A, HKV_A, HD_A = 8, 2, 64
G_A = HQ_A // HKV_A
WINDOW = 128
D_B, GS_B, P_B = 512, 16, 64
G_B = D_B // GS_B
H_C, Q_LORA, KV_LORA, NOPE_C, ROPE_C, V_C = 8, 256, 128, 64, 32, 64
QK_C = NOPE_C + ROPE_C
N_EXP, TOP_K, D_FF = 32, 4, 1024
SWIGLU_ALPHA, SWIGLU_LIMIT = 1.702, 7.0

N_CTX = BATCH * SEQ
N_LAT = DEC_BATCH * DEC_SEQ
T_ALL = N_CTX + N_LAT
N_GROUPS = 8

LANES = 128
SUBLANES = 8

C_GL = 0
C_QA = 3 * D_MODEL
C_UB = C_QA + HQ_A * LANES
C_CQ = C_UB + D_B
C_KA = C_CQ + Q_LORA
C_VA = C_KA + LANES
C_CKV = C_VA + LANES
C_KR = C_CKV + LANES
N_PROJ = C_KR + LANES

N_ASSIGN = T_ALL * TOP_K
N_BLOCKS = N_ASSIGN // BLK + N_EXP
BUF_LEN = N_BLOCKS * BLK

F32 = jnp.float32
BF16 = jnp.bfloat16
ROW3 = (D_MODEL // LANES, LANES)
VMEM_LIMIT = 56 * 1024 * 1024


def _cp(sem, vmem=VMEM_LIMIT):
    return pltpu.CompilerParams(dimension_semantics=sem, vmem_limit_bytes=vmem)


def _group_of_tile(i, tm):
    n_ctx_tiles = N_CTX // tm
    per_batch = DEC_SEQ // tm
    return jnp.where(i < n_ctx_tiles, 0, 1 + (i - n_ctx_tiles) // per_batch)


def _sigmoid(x):
    return 1.0 / (1.0 + jnp.exp(-x))


def _x_specs(tm, tile_of):
    n_ctx_tiles = N_CTX // tm
    return [pl.BlockSpec((tm, D_MODEL), lambda *g: (jnp.minimum(tile_of(*g), n_ctx_tiles - 1), 0)),
            pl.BlockSpec((tm, D_MODEL), lambda *g: (jnp.maximum(tile_of(*g) - n_ctx_tiles, 0), 0))]


def _x_tile(xc_ref, xl_ref, tile, tm):
    return jnp.where(tile >= N_CTX // tm, xl_ref[...], xc_ref[...])


def _adaln_kernel(c_ref, w_ref, b_ref, o_ref):
    c = c_ref[...]
    s = c * _sigmoid(c)
    o_ref[...] = jnp.dot(s.astype(BF16), w_ref[...].astype(BF16), preferred_element_type=F32) + b_ref[...]


def adaln(cvecs, w_ada, b_ada):
    tn = 1536
    out = pl.pallas_call(
        _adaln_kernel,
        out_shape=jax.ShapeDtypeStruct((DEPTH, N_GROUPS, 6 * D_MODEL), F32),
        grid=(DEPTH, 6 * D_MODEL // tn),
        in_specs=[
            pl.BlockSpec((N_GROUPS, D_MODEL), lambda l, j: (0, 0)),
            pl.BlockSpec((None, D_MODEL, tn), lambda l, j: (l, 0, j)),
            pl.BlockSpec((None, 1, tn), lambda l, j: (l, 0, j)),
        ],
        out_specs=pl.BlockSpec((None, N_GROUPS, tn), lambda l, j: (l, 0, j)),
        compiler_params=_cp(("arbitrary", "arbitrary")),
        name="adaln",
    )(cvecs, w_ada, b_ada.reshape(DEPTH, 1, 6 * D_MODEL))
    return out.reshape(DEPTH, N_GROUPS, 6, D_MODEL)


def _rms(x, g):
    ms = jnp.mean(x * x, axis=-1, keepdims=True)
    return x * lax.rsqrt(ms + EPS) * g


INPROJ_TM = 512


def _inproj_kernel(xc_ref, xl_ref, g_ref, mod_ref, w_ref, o_ref):
    h = _rms(_x_tile(xc_ref, xl_ref, pl.program_id(1), INPROJ_TM), g_ref[...])
    h = h * (1.0 + mod_ref[1:2, :]) + mod_ref[0:1, :]
    o_ref[...] = jnp.dot(h.astype(BF16), w_ref[...], preferred_element_type=F32)


def inproj(layer, x_ctx, x_lat, g, mod, w):
    tm, tn = INPROJ_TM, 1792
    return pl.pallas_call(
        _inproj_kernel,
        out_shape=jax.ShapeDtypeStruct((T_ALL, N_PROJ), F32),
        grid=(N_PROJ // tn, T_ALL // tm),
        in_specs=_x_specs(tm, lambda j, i: i) + [
            pl.BlockSpec((1, D_MODEL), lambda j, i: (0, 0)),
            pl.BlockSpec((None, 6, D_MODEL), lambda j, i: (_group_of_tile(i, tm), 0, 0)),
            pl.BlockSpec((None, D_MODEL, tn), lambda j, i: (layer, 0, j)),
        ],
        out_specs=pl.BlockSpec((tm, tn), lambda j, i: (i, j)),
        compiler_params=_cp(("arbitrary", "arbitrary")),
        name="inproj",
    )(x_ctx, x_lat, g.reshape(1, D_MODEL), mod, w)


def _rope(slab, cos, sin, swap):
    hi = slab.astype(BF16)
    lo = (slab - hi.astype(F32)).astype(BF16)
    partner = jnp.dot(hi, swap, preferred_element_type=F32) + jnp.dot(lo, swap, preferred_element_type=F32)
    return slab * cos + partner * sin


def _slot_norm(slab, gain, n_real):
    ms = jnp.sum(slab * slab, axis=-1, keepdims=True) * (1.0 / n_real)
    return slab * lax.rsqrt(ms + EPS) * gain


def _mla_keys(ckvn, kr_blk, wukv_ref, gk, rope_c):
    kv = jnp.dot(ckvn.astype(BF16), wukv_ref[...], preferred_element_type=F32)
    ks = []
    for h in range(H_C):
        slab = kv[:, h * LANES:(h + 1) * LANES] + kr_blk
        slab = _slot_norm(slab, gk, QK_C)
        if rope_c is not None:
            slab = _rope(slab, *rope_c)
        ks.append(slab.astype(BF16))
    return jnp.concatenate(ks, axis=1), kv[:, H_C * LANES:].astype(BF16)


PREP_TM = 512


def _prep_kernel(qa_ref, cq_ref, ka_ref, ckv_ref, kr_ref, tab_ref,
                 gqa_ref, gka_ref, gcq_ref, gckv_ref, gqc_ref, gkc_ref, wuq_ref, wukv_ref, swap_ref,
                 qa_o, ka_o, qc_o, kc_o, vc_o, ckvn_o):
    def body(rotate):
        if rotate:
            tab = tab_ref[...]
            rope_a = (tab[:, 0:128], tab[:, 128:256], swap_ref[0])
            rope_c = (tab[:, 256:384], tab[:, 384:512], swap_ref[1])
            rot_a = lambda x: _rope(x, *rope_a)
            rot_c = lambda x: _rope(x, *rope_c)
        else:
            rope_c = None
            rot_a = rot_c = lambda x: x
        lane = lax.broadcasted_iota(jnp.int32, (1, LANES), 1)

        gqa = gqa_ref[...]
        qa = qa_ref[...]
        outs = []
        for h in range(HQ_A):
            slab = _slot_norm(qa[:, h * LANES:(h + 1) * LANES], gqa, HD_A)
            outs.append((rot_a(slab) * (HD_A ** -0.5)).astype(BF16))
        qa_o[...] = jnp.concatenate(outs, axis=1)

        ka = ka_ref[...]
        sq = ka * ka
        lo = lane < HD_A
        ms_lo = jnp.sum(jnp.where(lo, sq, 0.0), axis=-1, keepdims=True)
        ms_hi = jnp.sum(jnp.where(lo, 0.0, sq), axis=-1, keepdims=True)
        rs = jnp.where(lo, lax.rsqrt(ms_lo * (1.0 / HD_A) + EPS), lax.rsqrt(ms_hi * (1.0 / HD_A) + EPS))
        ka_o[...] = rot_a(ka * rs * gka_ref[...])

        cqn = _rms(cq_ref[...], gcq_ref[...])
        q = jnp.dot(cqn.astype(BF16), wuq_ref[...], preferred_element_type=F32)
        gqc = gqc_ref[...]
        outs = []
        for h in range(H_C):
            slab = _slot_norm(q[:, h * LANES:(h + 1) * LANES], gqc, QK_C)
            outs.append((rot_c(slab) * (QK_C ** -0.5)).astype(BF16))
        qc_o[...] = jnp.concatenate(outs, axis=1)

        ckvn = _rms(ckv_ref[...], gckv_ref[...])
        ckvn_o[...] = ckvn
        kc, vc = _mla_keys(ckvn, kr_ref[...], wukv_ref, gkc_ref[...], rope_c)
        kc_o[...] = kc
        vc_o[...] = vc

    is_latent = pl.program_id(0) >= N_CTX // PREP_TM
    pl.when(is_latent)(lambda: body(True))
    pl.when(jnp.logical_not(is_latent))(lambda: body(False))


def prep(proj, tab, gqa, gka, gcq, gckv, gqc, gkc, wuq, wukv):
    tm = PREP_TM
    n_ctx_tiles = N_CTX // tm
    per_batch = DEC_SEQ // tm

    def tab_map(i):
        return (jnp.where(i < n_ctx_tiles, 0, per_batch + (i - n_ctx_tiles) % per_batch), 0)

    def col(width, off):
        return pl.BlockSpec((tm, width), lambda i: (i, off // width))

    def full(shape):
        return pl.BlockSpec(shape, lambda i: (0,) * len(shape))

    def row_out(width):
        return pl.BlockSpec((tm, width), lambda i: (i, 0))

    return pl.pallas_call(
        _prep_kernel,
        out_shape=(
            jax.ShapeDtypeStruct((T_ALL, HQ_A * LANES), BF16),
            jax.ShapeDtypeStruct((T_ALL, LANES), F32),
            jax.ShapeDtypeStruct((T_ALL, H_C * LANES), BF16),
            jax.ShapeDtypeStruct((T_ALL, H_C * LANES), BF16),
            jax.ShapeDtypeStruct((T_ALL, H_C * V_C), BF16),
            jax.ShapeDtypeStruct((T_ALL, KV_LORA), F32),
        ),
        grid=(T_ALL // tm,),
        in_specs=[
            col(HQ_A * LANES, C_QA), col(Q_LORA, C_CQ), col(LANES, C_KA), col(LANES, C_CKV), col(LANES, C_KR),
            pl.BlockSpec((tm, 4 * LANES), tab_map),
            full((1, LANES)), full((1, LANES)), full((1, Q_LORA)), full((1, KV_LORA)),
            full((1, LANES)), full((1, LANES)),
            full((Q_LORA, H_C * LANES)), full((KV_LORA, H_C * LANES + H_C * V_C)), full((2, LANES, LANES)),
        ],
        out_specs=(row_out(HQ_A * LANES), row_out(LANES), row_out(H_C * LANES), row_out(H_C * LANES),
                   row_out(H_C * V_C), row_out(KV_LORA)),
        compiler_params=_cp(("arbitrary",)),
        name="prep",
    )(proj, proj, proj, proj, proj, tab, gqa, gka, gcq, gckv, gqc, gkc, wuq, wukv, _pair_swaps())


def _cachekeys_kernel(ckv_ref, kr_ref, gkc_ref, wukv_ref, kc_o, vc_o):
    kc, vc = _mla_keys(ckv_ref[...], kr_ref[...], wukv_ref, gkc_ref[...], None)
    kc_o[...] = kc
    vc_o[...] = vc


def cache_keys(ckv, kr_blk, gkc, wukv):
    r = ckv.shape[0]
    tm = 256
    return pl.pallas_call(
        _cachekeys_kernel,
        out_shape=(jax.ShapeDtypeStruct((r, H_C * LANES), BF16), jax.ShapeDtypeStruct((r, H_C * V_C), BF16)),
        grid=(r // tm,),
        in_specs=[
            pl.BlockSpec((tm, LANES), lambda i: (i, 0)),
            pl.BlockSpec((tm, LANES), lambda i: (i, 0)),
            pl.BlockSpec((1, LANES), lambda i: (0, 0)),
            pl.BlockSpec((KV_LORA, H_C * LANES + H_C * V_C), lambda i: (0, 0)),
        ],
        out_specs=(pl.BlockSpec((tm, H_C * LANES), lambda i: (i, 0)), pl.BlockSpec((tm, H_C * V_C), lambda i: (i, 0))),
        compiler_params=_cp(("arbitrary",)),
        name="cache_keys",
    )(ckv, kr_blk, gkc, wukv)


def _attn_body(q_ref, segs, sink_ref, o_ref, *, n_heads, k_slot, v_slab, v_half, tq, band_qi=None):
    outs = []
    for h in range(n_heads):
        qh = q_ref[:, h * LANES:(h + 1) * LANES]
        scores = []
        for k_ref, _, off in segs:
            kh = k_ref[:, k_slot(h) * LANES:(k_slot(h) + 1) * LANES].astype(BF16)
            s = lax.dot_general(qh, kh, (((1,), (1,)), ((), ())), preferred_element_type=F32)
            if off is not None:
                tk = s.shape[1]
                blk = band_qi + off
                q_pos = band_qi * tq + lax.broadcasted_iota(jnp.int32, (tq, tk), 0)
                k_pos = blk * tk + lax.broadcasted_iota(jnp.int32, (tq, tk), 1)
                ok = (jnp.abs(k_pos - q_pos) <= WINDOW) & (blk >= 0) & (blk < DEC_SEQ // tk)
                s = jnp.where(ok, s, NEG_INF)
            scores.append(s)
        m = scores[0].max(axis=-1, keepdims=True)
        for s in scores[1:]:
            m = jnp.maximum(m, s.max(axis=-1, keepdims=True))
        if sink_ref is not None:
            sink = sink_ref[h:h + 1, 0:1]
            m = jnp.maximum(m, sink)
            denom = jnp.exp(sink - m)
        else:
            denom = jnp.zeros_like(m)
        acc = None
        for s, (_, v_ref, _) in zip(scores, segs):
            p = jnp.exp(s - m)
            denom = denom + p.sum(axis=-1, keepdims=True)
            vs = v_ref[:, v_slab(h) * LANES:(v_slab(h) + 1) * LANES].astype(BF16)
            pv = jnp.dot(p.astype(BF16), vs, preferred_element_type=F32)
            acc = pv if acc is None else acc + pv
        half = v_half(h)
        outs.append(acc[:, half * 64:(half + 1) * 64] * (1.0 / denom))
    o_ref[...] = jnp.concatenate(outs, axis=1).astype(o_ref.dtype)


ATT_CTX_SEQS = 4
_C_CFG = dict(n_heads=H_C, k_slot=lambda h: h, v_slab=lambda h: h // 2, v_half=lambda h: h % 2)


def _attn_a_body(q_ref, segs, sink_ref, o_ref, *, tq, band_qi=None):
    rows = HQ_A * tq
    q = jnp.concatenate([q_ref[:, h * LANES:(h + 1) * LANES] for h in range(HQ_A)], axis=0)
    sink = jnp.concatenate([jnp.broadcast_to(sink_ref[h:h + 1, 0:1], (tq, 1)) for h in range(HQ_A)], axis=0)
    scores = []
    for k_ref, _, off in segs:
        s = lax.dot_general(q, k_ref[...].astype(BF16), (((1,), (1,)), ((), ())), preferred_element_type=F32)
        if off is not None:
            tk = s.shape[1]
            blk = band_qi + off
            q_pos = band_qi * tq + (lax.broadcasted_iota(jnp.int32, (rows, tk), 0) & (tq - 1))
            k_pos = blk * tk + lax.broadcasted_iota(jnp.int32, (rows, tk), 1)
            ok = (jnp.abs(k_pos - q_pos) <= WINDOW) & (blk >= 0) & (blk < DEC_SEQ // tk)
            s = jnp.where(ok, s, NEG_INF)
        scores.append(s)
    m = sink
    for s in scores:
        m = jnp.maximum(m, s.max(axis=-1, keepdims=True))
    denom = jnp.exp(sink - m)
    acc = None
    for s, (_, v_ref, _) in zip(scores, segs):
        p = jnp.exp(s - m)
        denom = denom + p.sum(axis=-1, keepdims=True)
        pv = jnp.dot(p.astype(BF16), v_ref[...].astype(BF16), preferred_element_type=F32)
        acc = pv if acc is None else acc + pv
    acc = acc * (1.0 / denom)
    outs = []
    for h in range(HQ_A):
        kv = h // G_A
        outs.append(acc[h * tq:(h + 1) * tq, kv * HD_A:(kv + 1) * HD_A])
    o_ref[...] = jnp.concatenate(outs, axis=1).astype(o_ref.dtype)


def _attn_a_ctx_kernel(q_ref, k_ref, v_ref, sink_ref, o_init, o_ref):
    del o_init
    for b in range(ATT_CTX_SEQS):
        rows = pl.ds(b * SEQ, SEQ)
        _attn_a_body(q_ref.at[rows], [(k_ref.at[rows], v_ref.at[rows], None)], sink_ref, o_ref.at[rows], tq=SEQ)


def _attn_a_lat_kernel(q_ref, k0, k1, k2, v0, v1, v2, kc_ref, vc_ref, sink_ref, o_ctx, o_ref):
    del o_ctx
    qi = pl.program_id(1)
    segs = [(k0, v0, -1), (k1, v1, 0), (k2, v2, 1), (kc_ref, vc_ref, None)]
    _attn_a_body(q_ref, segs, sink_ref, o_ref, tq=BAND, band_qi=qi)


def _attn_c_ctx_kernel(q_ref, k_ref, v_ref, o_init, o_ref):
    del o_init
    for b in range(ATT_CTX_SEQS):
        rows = pl.ds(b * SEQ, SEQ)
        _attn_body(q_ref.at[rows], [(k_ref.at[rows], v_ref.at[rows], None)], None, o_ref.at[rows], tq=SEQ, **_C_CFG)


def _attn_c_lat_kernel(q_ref, k_ref, v_ref, kc_ref, vc_ref, o_ctx, o_ref):
    del o_ctx
    _attn_body(q_ref, [(k_ref, v_ref, None), (kc_ref, vc_ref, None)], None, o_ref, tq=256, **_C_CFG)


def attn_a_ctx(qa, ka, proj, sink):
    return pl.pallas_call(
        _attn_a_ctx_kernel,
        out_shape=jax.ShapeDtypeStruct((T_ALL, HQ_A * HD_A), BF16),
        grid=(BATCH // ATT_CTX_SEQS,),
        in_specs=[
            pl.BlockSpec((ATT_CTX_SEQS * SEQ, HQ_A * LANES), lambda b: (b, 0)),
            pl.BlockSpec((ATT_CTX_SEQS * SEQ, LANES), lambda b: (b, 0)),
            pl.BlockSpec((ATT_CTX_SEQS * SEQ, LANES), lambda b: (b, C_VA // LANES)),
            pl.BlockSpec((SUBLANES, LANES), lambda b: (0, 0)),
            pl.BlockSpec(memory_space=pl.ANY),
        ],
        out_specs=pl.BlockSpec((ATT_CTX_SEQS * SEQ, HQ_A * HD_A), lambda b: (b, 0)),
        input_output_aliases={4: 0},
        compiler_params=_cp(("arbitrary",)),
        name="attn_a_ctx",
    )(qa, ka, proj, sink, jnp.zeros((T_ALL, HQ_A * HD_A), BF16))


def attn_a_lat(qa, ka, proj, k_cache, v_cache, sink, o_all):
    nb = DEC_SEQ // BAND
    base = N_CTX // BAND

    def band(off, colblk):
        return pl.BlockSpec((BAND, LANES), lambda b, i: (base + b * nb + jnp.clip(i + off, 0, nb - 1), colblk))

    return pl.pallas_call(
        _attn_a_lat_kernel,
        out_shape=jax.ShapeDtypeStruct((T_ALL, HQ_A * HD_A), BF16),
        grid=(DEC_BATCH, nb),
        in_specs=[
            pl.BlockSpec((BAND, HQ_A * LANES), lambda b, i: (base + b * nb + i, 0)),
            band(-1, 0), band(0, 0), band(1, 0),
            band(-1, C_VA // LANES), band(0, C_VA // LANES), band(1, C_VA // LANES),
            pl.BlockSpec((None, PAST_LEN, LANES), lambda b, i: (b, 0, 0)),
            pl.BlockSpec((None, PAST_LEN, LANES), lambda b, i: (b, 0, 0)),
            pl.BlockSpec((SUBLANES, LANES), lambda b, i: (0, 0)),
            pl.BlockSpec(memory_space=pl.ANY),
        ],
        out_specs=pl.BlockSpec((BAND, HQ_A * HD_A), lambda b, i: (base + b * nb + i, 0)),
        input_output_aliases={10: 0},
        compiler_params=_cp(("arbitrary", "arbitrary")),
        name="attn_a_lat",
    )(qa, ka, ka, ka, proj, proj, proj, k_cache, v_cache, sink, o_all)


def attn_c_ctx(qc, kc, vc):
    return pl.pallas_call(
        _attn_c_ctx_kernel,
        out_shape=jax.ShapeDtypeStruct((T_ALL, H_C * V_C), BF16),
        grid=(BATCH // ATT_CTX_SEQS,),
        in_specs=[
            pl.BlockSpec((ATT_CTX_SEQS * SEQ, H_C * LANES), lambda b: (b, 0)),
            pl.BlockSpec((ATT_CTX_SEQS * SEQ, H_C * LANES), lambda b: (b, 0)),
            pl.BlockSpec((ATT_CTX_SEQS * SEQ, H_C * V_C), lambda b: (b, 0)),
            pl.BlockSpec(memory_space=pl.ANY),
        ],
        out_specs=pl.BlockSpec((ATT_CTX_SEQS * SEQ, H_C * V_C), lambda b: (b, 0)),
        input_output_aliases={3: 0},
        compiler_params=_cp(("arbitrary",)),
        name="attn_c_ctx",
    )(qc, kc, vc, jnp.zeros((T_ALL, H_C * V_C), BF16))


def attn_c_lat(qc, kc, vc, kc_cache, vc_cache, o_all):
    tq = 256
    nq = DEC_SEQ // tq
    qbase = N_CTX // tq
    kbase = N_CTX // DEC_SEQ
    return pl.pallas_call(
        _attn_c_lat_kernel,
        out_shape=jax.ShapeDtypeStruct((T_ALL, H_C * V_C), BF16),
        grid=(DEC_BATCH, nq),
        in_specs=[
            pl.BlockSpec((tq, H_C * LANES), lambda b, i: (qbase + b * nq + i, 0)),
            pl.BlockSpec((DEC_SEQ, H_C * LANES), lambda b, i: (kbase + b, 0)),
            pl.BlockSpec((DEC_SEQ, H_C * V_C), lambda b, i: (kbase + b, 0)),
            pl.BlockSpec((PAST_LEN, H_C * LANES), lambda b, i: (b, 0)),
            pl.BlockSpec((PAST_LEN, H_C * V_C), lambda b, i: (b, 0)),
            pl.BlockSpec(memory_space=pl.ANY),
        ],
        out_specs=pl.BlockSpec((tq, H_C * V_C), lambda b, i: (qbase + b * nq + i, 0)),
        input_output_aliases={5: 0},
        compiler_params=_cp(("arbitrary", "arbitrary")),
        name="attn_c_lat",
    )(qc, kc, vc, kc_cache, vc_cache, o_all)


S5_TC = 256
S5_CH = 512
S5_UNROLL = 8
TOK_CHUNKS = T_ALL // S5_TC
CHUNK_GROUP = 8
assert SEQ == S5_TC and DEC_SEQ == CHUNK_GROUP * S5_TC and BATCH == 4 * CHUNK_GROUP


def _s5_bproj(u2, wb_ref, pick_first):
    def part(lo, hi):
        r = jnp.dot(u2, wb_ref[0, :, lo:hi], preferred_element_type=F32)
        if pick_first is not None:
            r = jnp.where(pick_first, r, jnp.dot(u2, wb_ref[1, :, lo:hi], preferred_element_type=F32))
        return r.reshape(S5_TC, SUBLANES, S5_CH)
    return part(0, S5_CH), part(S5_CH, 2 * S5_CH)


def _s5_cproj(xre, xim, wc_ref, pick_first):
    rows = S5_TC * SUBLANES
    xr = xre[...].reshape(rows, S5_CH).astype(BF16)
    xi = xim[...].reshape(rows, S5_CH).astype(BF16)

    def part(d):
        return (jnp.dot(xr, wc_ref[d, 0:S5_CH, :], preferred_element_type=F32)
                + jnp.dot(xi, wc_ref[d, S5_CH:, :], preferred_element_type=F32))

    y = part(0)
    if pick_first is not None:
        y = jnp.where(pick_first, y, part(1))
    return jnp.swapaxes(y.reshape(S5_TC, SUBLANES, LANES), 0, 1)


def _s5_ctx_kernel(u_ref, wb_ref, wc_ref, are_ref, aim_ref, y_init, y_ref, fre_ref, fim_ref, bre, bim):
    del y_init
    d = pl.program_id(2)
    rows = S5_TC * SUBLANES
    u2 = jnp.swapaxes(u_ref[...], 0, 1).reshape(rows, LANES).astype(BF16)
    b_re, b_im = _s5_bproj(u2, wb_ref, None)
    bre[...] = b_re
    bim[...] = b_im
    a_re = are_ref[...]
    a_im = aim_ref[...]

    def outer(i, carry):
        sr, si = carry
        for j in range(S5_UNROLL):
            t0 = i * S5_UNROLL + j
            t = jnp.where(d == 1, S5_TC - 1 - t0, t0)
            nr = a_re * sr - a_im * si + bre[t]
            ni = a_re * si + a_im * sr + bim[t]
            bre[t] = nr
            bim[t] = ni
            sr, si = nr, ni
        return sr, si

    zero = jnp.zeros((SUBLANES, S5_CH), F32)
    sr, si = lax.fori_loop(0, S5_TC // S5_UNROLL, outer, (zero, zero))
    fre_ref[...] = sr
    fim_ref[...] = si
    y = _s5_cproj(bre, bim, wc_ref, None)

    @pl.when(d == 0)
    def _():
        y_ref[...] = y

    @pl.when(d == 1)
    def _():
        y_ref[...] += y


def s5_ctx(proj4, wb, wc, a_re, a_im):
    nj = D_B // LANES
    ucol = C_UB // LANES
    return pl.pallas_call(
        _s5_ctx_kernel,
        out_shape=(
            jax.ShapeDtypeStruct((TOK_CHUNKS // CHUNK_GROUP, CHUNK_GROUP, S5_TC, D_B), F32),
            jax.ShapeDtypeStruct((2, BATCH, G_B * P_B), F32),
            jax.ShapeDtypeStruct((2, BATCH, G_B * P_B), F32),
        ),
        grid=(BATCH // CHUNK_GROUP, nj, 2),
        in_specs=[
            pl.BlockSpec((None, CHUNK_GROUP, S5_TC, LANES), lambda sb, j, d: (sb, 0, 0, ucol + j)),
            pl.BlockSpec((1, None, LANES, 2 * S5_CH), lambda sb, j, d: (d, j, 0, 0)),
            pl.BlockSpec((1, None, 2 * S5_CH, LANES), lambda sb, j, d: (d, j, 0, 0)),
            pl.BlockSpec((None, SUBLANES, S5_CH), lambda sb, j, d: (d, 0, j)),
            pl.BlockSpec((None, SUBLANES, S5_CH), lambda sb, j, d: (d, 0, j)),
            pl.BlockSpec(memory_space=pl.ANY),
        ],
        out_specs=(
            pl.BlockSpec((None, CHUNK_GROUP, S5_TC, LANES), lambda sb, j, d: (sb, 0, 0, j)),
            pl.BlockSpec((None, SUBLANES, S5_CH), lambda sb, j, d: (d, sb, j)),
            pl.BlockSpec((None, SUBLANES, S5_CH), lambda sb, j, d: (d, sb, j)),
        ),
        scratch_shapes=[pltpu.VMEM((S5_TC, SUBLANES, S5_CH), F32), pltpu.VMEM((S5_TC, SUBLANES, S5_CH), F32)],
        input_output_aliases={5: 0},
        compiler_params=_cp(("arbitrary", "arbitrary", "arbitrary")),
        name="s5_ctx",
    )(proj4, wb, wc, a_re, a_im, jnp.zeros((TOK_CHUNKS // CHUNK_GROUP, CHUNK_GROUP, S5_TC, D_B), F32))


def _s5_lat_kernel(uf_ref, ub_ref, wb_ref, wc_ref, are_ref, aim_ref, h0re_ref, h0im_ref, y_ctx,
                   yf_ref, yb_ref, bre, bim, xre, xim, sre, sim):
    del y_ctx
    k = pl.program_id(1)

    @pl.when(k == 0)
    def _():
        sre[...] = h0re_ref[...]
        sim[...] = h0im_ref[...]

    rows = S5_TC * SUBLANES
    half = SUBLANES // 2
    u8 = jnp.concatenate([uf_ref[...], ub_ref[...]], axis=0)
    u2 = jnp.swapaxes(u8, 0, 1).reshape(rows, LANES).astype(BF16)
    fwd_rows = (lax.broadcasted_iota(jnp.int32, (rows, 1), 0) % SUBLANES) < half
    b_re, b_im = _s5_bproj(u2, wb_ref, fwd_rows)
    bre[...] = b_re
    bim[...] = b_im
    a_re = are_ref[...]
    a_im = aim_ref[...]
    fwd8 = lax.broadcasted_iota(jnp.int32, (SUBLANES, 1), 0) < half

    def outer(i, carry):
        sr, si = carry
        for j in range(S5_UNROLL):
            t = i * S5_UNROLL + j
            tb = S5_TC - 1 - t
            nr = a_re * sr - a_im * si + jnp.where(fwd8, bre[t], bre[tb])
            ni = a_re * si + a_im * sr + jnp.where(fwd8, bim[t], bim[tb])
            xre[t, 0:half, :] = nr[0:half]
            xre[tb, half:, :] = nr[half:]
            xim[t, 0:half, :] = ni[0:half]
            xim[tb, half:, :] = ni[half:]
            sr, si = nr, ni
        return sr, si

    sr, si = lax.fori_loop(0, S5_TC // S5_UNROLL, outer, (sre[...], sim[...]))
    sre[...] = sr
    sim[...] = si
    y = _s5_cproj(xre, xim, wc_ref, fwd_rows)
    yf_ref[...] = y[0:half]
    yb_ref[...] = y[half:]


def s5_lat(proj4, y_all, wb, wc, a_re, a_im, h0_re, h0_im):
    nj = D_B // LANES
    nk = DEC_SEQ // S5_TC
    ucol = C_UB // LANES
    lat = lambda col0, rev: pl.BlockSpec(
        (DEC_BATCH, None, S5_TC, LANES), lambda j, k: (1, (nk - 1 - k) if rev else k, 0, col0 + j))
    vec = lambda: pl.BlockSpec((SUBLANES, S5_CH), lambda j, k: (0, j))
    buf = lambda: pltpu.VMEM((S5_TC, SUBLANES, S5_CH), F32)
    return pl.pallas_call(
        _s5_lat_kernel,
        out_shape=(
            jax.ShapeDtypeStruct(y_all.shape, F32),
            jax.ShapeDtypeStruct((DEC_BATCH, nk, S5_TC, D_B), F32),
        ),
        grid=(nj, nk),
        in_specs=[
            lat(ucol, False), lat(ucol, True),
            pl.BlockSpec((2, None, LANES, 2 * S5_CH), lambda j, k: (0, j, 0, 0)),
            pl.BlockSpec((2, None, 2 * S5_CH, LANES), lambda j, k: (0, j, 0, 0)),
            vec(), vec(), vec(), vec(),
            pl.BlockSpec(memory_space=pl.ANY),
        ],
        out_specs=(
            lat(0, False),
            pl.BlockSpec((DEC_BATCH, None, S5_TC, LANES), lambda j, k: (0, nk - 1 - k, 0, j)),
        ),
        scratch_shapes=[buf(), buf(), buf(), buf(),
                        pltpu.VMEM((SUBLANES, S5_CH), F32), pltpu.VMEM((SUBLANES, S5_CH), F32)],
        input_output_aliases={8: 0},
        compiler_params=_cp(("arbitrary", "arbitrary")),
        name="s5_lat",
    )(proj4, proj4, wb, wc, a_re, a_im, h0_re, h0_im, y_all)


S5FIN_TM = 512


def _s5fin_kernel(u_ref, y_ref, yb_ref, d_ref, w_ref, b_ref, o_ref):
    def body(latent):
        y = d_ref[...] * u_ref[...] + y_ref[...]
        if latent:
            y = y + yb_ref[...]
        y = 0.5 * y * (1.0 + jnp.tanh(math.sqrt(2.0 / math.pi) * (y + 0.044715 * (y * y * y))))
        z = jnp.dot(y.astype(BF16), w_ref[...], preferred_element_type=F32) + b_ref[...]
        o_ref[...] = (y * _sigmoid(z)).astype(o_ref.dtype)

    is_latent = pl.program_id(0) >= N_CTX // S5FIN_TM
    pl.when(is_latent)(lambda: body(True))
    pl.when(jnp.logical_not(is_latent))(lambda: body(False))


def s5_finish(proj, y, y_bwd_lat, d, w_glu, b_glu):
    tm = S5FIN_TM
    n_ctx_tiles = N_CTX // tm
    return pl.pallas_call(
        _s5fin_kernel,
        out_shape=jax.ShapeDtypeStruct((T_ALL, D_B), BF16),
        grid=(T_ALL // tm,),
        in_specs=[
            pl.BlockSpec((tm, D_B), lambda i: (i, C_UB // D_B)),
            pl.BlockSpec((tm, D_B), lambda i: (i, 0)),
            pl.BlockSpec((tm, D_B), lambda i: (jnp.maximum(i - n_ctx_tiles, 0), 0)),
            pl.BlockSpec((1, D_B), lambda i: (0, 0)),
            pl.BlockSpec((D_B, D_B), lambda i: (0, 0)),
            pl.BlockSpec((1, D_B), lambda i: (0, 0)),
        ],
        out_specs=pl.BlockSpec((tm, D_B), lambda i: (i, 0)),
        compiler_params=_cp(("arbitrary",)),
        name="s5_finish",
    )(proj, y, y_bwd_lat, d, w_glu, b_glu)


def _merge_kernel(xc_ref, xl_ref, gl_ref, oa_ref, ob_ref, oc_ref, mod_ref, g_ref, wa_ref, wb_ref, wc_ref, wo_ref,
                  wr_ref, br_ref, x1_ref, h2row_ref, idx_ref, rank_ref, wgt_ref, cnt_ref, cnt_acc):
    @pl.when(pl.program_id(0) == 0)
    def _():
        cnt_acc[...] = jnp.zeros_like(cnt_acc)

    m = None
    for br, (o_ref, w_ref) in enumerate(((oa_ref, wa_ref), (ob_ref, wb_ref), (oc_ref, wc_ref))):
        gate = _sigmoid(gl_ref[:, br * D_MODEL:(br + 1) * D_MODEL])
        t = gate * jnp.dot(o_ref[...], w_ref[...], preferred_element_type=F32)
        m = t if m is None else m + t
    x = _x_tile(xc_ref, xl_ref, pl.program_id(0), ROUTE_TM)
    x1 = x + mod_ref[2:3, :] * jnp.dot(m.astype(BF16), wo_ref[...], preferred_element_type=F32)
    x1_ref[...] = x1
    h2 = _rms(x1, g_ref[...])
    h2 = h2 * (1.0 + mod_ref[4:5, :]) + mod_ref[3:4, :]
    h2row_ref[...] = h2.reshape((h2.shape[0],) + ROW3)
    idx, rank, wgt, cnt = _route(h2, wr_ref, br_ref, cnt_acc[...])
    idx_ref[...] = idx
    rank_ref[...] = rank
    wgt_ref[...] = wgt
    cnt_acc[...] = cnt
    cnt_ref[...] = cnt


def merge(x_ctx, x_lat, proj, oa, ob, oc, mod, g, wa, wb, wc, wo, w_router, b_router):
    tm = ROUTE_TM
    full = lambda shape: pl.BlockSpec(shape, lambda i: (0,) * len(shape))
    tile = lambda: pl.BlockSpec((tm, LANES), lambda i: (i, 0))
    return pl.pallas_call(
        _merge_kernel,
        out_shape=(jax.ShapeDtypeStruct((T_ALL, D_MODEL), F32), jax.ShapeDtypeStruct((T_ALL,) + ROW3, F32),
                   jax.ShapeDtypeStruct((T_ALL, LANES), jnp.int32), jax.ShapeDtypeStruct((T_ALL, LANES), jnp.int32),
                   jax.ShapeDtypeStruct((T_ALL, LANES), F32), jax.ShapeDtypeStruct((1, LANES), F32)),
        grid=(T_ALL // tm,),
        in_specs=_x_specs(tm, lambda i: i) + [
            pl.BlockSpec((tm, 3 * D_MODEL), lambda i: (i, 0)),
            pl.BlockSpec((tm, 512), lambda i: (i, 0)),
            pl.BlockSpec((tm, 512), lambda i: (i, 0)),
            pl.BlockSpec((tm, 512), lambda i: (i, 0)),
            pl.BlockSpec((None, 6, D_MODEL), lambda i: (_group_of_tile(i, tm), 0, 0)),
            full((1, D_MODEL)),
            full((512, D_MODEL)), full((512, D_MODEL)), full((512, D_MODEL)), full((D_MODEL, D_MODEL)),
            full((D_MODEL, LANES)), full((1, LANES)),
        ],
        out_specs=(pl.BlockSpec((tm, D_MODEL), lambda i: (i, 0)), pl.BlockSpec((tm,) + ROW3, lambda i: (i, 0, 0)),
                   tile(), tile(), tile(), full((1, LANES))),
        scratch_shapes=[pltpu.VMEM((1, LANES), F32)],
        compiler_params=_cp(("arbitrary",)),
        name="merge",
    )(x_ctx, x_lat, proj, oa, ob, oc, mod, g, wa, wb, wc, wo, w_router, b_router)


ROUTE_TM = 512


def _route(h, w_ref, b_ref, cnt):
    w = w_ref[...]
    h_hi = h.astype(BF16)
    h_lo = (h - h_hi.astype(F32)).astype(BF16)
    w_hi = w.astype(BF16)
    w_lo = (w - w_hi.astype(F32)).astype(BF16)
    dot = functools.partial(jnp.dot, preferred_element_type=F32)
    logits = dot(h_hi, w_hi) + (dot(h_hi, w_lo) + dot(h_lo, w_hi)) + b_ref[...]
    lane_i = lax.broadcasted_iota(jnp.int32, (ROUTE_TM, LANES), 1)
    lane = lane_i.astype(F32)
    r_i = lax.broadcasted_iota(jnp.int32, (ROUTE_TM, ROUTE_TM), 0)
    c_i = lax.broadcasted_iota(jnp.int32, (ROUTE_TM, ROUTE_TM), 1)
    earlier = jnp.where(c_i < r_i, 1.0, 0.0).astype(BF16)

    idx_out = jnp.zeros((ROUTE_TM, LANES), F32)
    rank_out = jnp.zeros((ROUTE_TM, LANES), F32)
    val_out = jnp.zeros((ROUTE_TM, LANES), F32)
    v0 = None
    esum = None
    for k in range(TOP_K):
        m = logits.max(axis=-1, keepdims=True)
        sel = jnp.min(jnp.where(logits == m, lane, float(LANES)), axis=-1, keepdims=True)
        hit = lane == sel
        logits = jnp.where(hit, -jnp.inf, logits)
        onehot = jnp.where(hit, 1.0, 0.0)
        within = jnp.dot(earlier, onehot.astype(BF16), preferred_element_type=F32)
        rank = jnp.sum(onehot * (within + cnt), axis=-1, keepdims=True)
        cnt = cnt + jnp.sum(onehot, axis=0, keepdims=True)
        if k == 0:
            v0 = m
        e = jnp.exp(m - v0)
        esum = e if esum is None else esum + e
        idx_out = jnp.where(lane_i == k, sel, idx_out)
        rank_out = jnp.where(lane_i == k, rank, rank_out)
        val_out = jnp.where(lane_i == k, e, val_out)
    return idx_out.astype(jnp.int32), rank_out.astype(jnp.int32), val_out * (1.0 / esum), cnt


PLAN_UNROLL = 16
N_DUMP = 5 * BLK
PLAN_LEN = BUF_LEN + BLK
assert TOP_K == 4 and T_ALL & (T_ALL - 1) == 0


def _invert_kernel(dest_ref, fill_hbm, inv_ref):
    pltpu.sync_copy(fill_hbm, inv_ref)
    tok_step = PLAN_UNROLL // TOP_K

    def put(i, c):
        for j in range(PLAN_UNROLL):
            inv_ref[dest_ref[i * PLAN_UNROLL + j]] = i * tok_step + ((j % TOP_K) * T_ALL + j // TOP_K)
        return c

    lax.fori_loop(0, N_ASSIGN // PLAN_UNROLL, put, 0)


def invert_plan(dest):
    r = jnp.arange(PLAN_LEN, dtype=jnp.int32)
    fill = N_ASSIGN + jnp.where(r < BLK, 2 * BLK + r, (r - BLK) & (2 * BLK - 1))
    return pl.pallas_call(
        _invert_kernel,
        out_shape=jax.ShapeDtypeStruct((PLAN_LEN,), jnp.int32),
        in_specs=[pl.BlockSpec(memory_space=pltpu.SMEM), pl.BlockSpec(memory_space=pl.ANY)],
        out_specs=pl.BlockSpec(memory_space=pltpu.SMEM),
        name="invert_plan",
    )(dest, fill)


PAIR_TILE = 2 * LANES


def _pair_selection():
    r = lax.broadcasted_iota(jnp.int32, (PAIR_TILE, PAIR_TILE), 0)
    c = lax.broadcasted_iota(jnp.int32, (PAIR_TILE, PAIR_TILE), 1)
    return (r == jnp.where(c < LANES, 2 * c, 2 * (c - LANES) + 1)).astype(BF16)


EXP_NBUF = 3


def _expert_kernel(be_ref, first_ref, par_ref, nxt_ref, inv_ref, h_hbm, wu_hbm, bu_ref, wd_hbm, bd_ref, sel_ref,
                   comb_hbm, xbuf, ybuf, wu_ref, wd_ref, wu_st, wd_st, gsem, ssem, wsem, *, layer):
    i = pl.program_id(0)
    n = pl.num_programs(0)
    cur = i % EXP_NBUF
    nxt = (i + 2) % EXP_NBUF
    prv = nxt

    def gather(block, s):
        base = (block + 1) * BLK
        for r in range(BLK):
            tok = inv_ref[base + r] & (T_ALL - 1)
            pltpu.make_async_copy(h_hbm.at[tok], xbuf.at[s, r], gsem.at[s]).start()

    def scatter(block, s):
        base = (block + 1) * BLK
        for r in range(BLK):
            a = inv_ref[base + r]
            pltpu.make_async_copy(ybuf.at[s, r], comb_hbm.at[a], ssem.at[s]).start()

    def wait_block(sem):
        pltpu.make_async_copy(h_hbm.at[pl.ds(0, BLK)], xbuf.at[0], sem).wait()

    def weight_copies(e, p):
        return (pltpu.make_async_copy(wu_hbm.at[layer, e], wu_st.at[p], wsem.at[p]),
                pltpu.make_async_copy(wd_hbm.at[layer, e], wd_st.at[p], wsem.at[p]))

    @pl.when(i == 0)
    def _():
        for cp in weight_copies(be_ref[0], 0):
            cp.start()
        ybuf[...] = jnp.zeros_like(ybuf)
        for s in range(EXP_NBUF - 1):
            for r in range(BLK):
                dump = N_ASSIGN + (3 + s) * BLK + r
                pltpu.make_async_copy(ybuf.at[s, r], comb_hbm.at[dump], ssem.at[s]).start()
        gather(0, 0)
        gather(1, 1)

    wait_block(gsem.at[cur])
    wait_block(ssem.at[cur])

    @pl.when(first_ref[i] == 1)
    def _():
        p = par_ref[i]
        for cp in weight_copies(be_ref[i], p):
            cp.wait()

        @pl.when(nxt_ref[i] >= 0)
        def _():
            for cp in weight_copies(nxt_ref[i], 1 - p):
                cp.start()

        sel = sel_ref[...]
        for j in range(2 * D_FF // PAIR_TILE):
            cols = slice(j * PAIR_TILE, (j + 1) * PAIR_TILE)
            wu_ref[:, cols] = jnp.dot(wu_st[p, :, cols].astype(BF16), sel, preferred_element_type=F32).astype(BF16)
        wd_ref[...] = wd_st[p].astype(BF16)

    gather(jnp.minimum(i + 2, n - 1), nxt)
    scatter(i - 1, prv)

    x = xbuf[cur].reshape(BLK, D_MODEL).astype(BF16)
    h = jnp.dot(x, wu_ref[...], preferred_element_type=F32) + bu_ref[...]
    acts = []
    for j in range(2 * D_FF // PAIR_TILE):
        glu = jnp.minimum(h[:, j * PAIR_TILE:j * PAIR_TILE + LANES], SWIGLU_LIMIT)
        lin = jnp.clip(h[:, j * PAIR_TILE + LANES:(j + 1) * PAIR_TILE], -SWIGLU_LIMIT, SWIGLU_LIMIT)
        acts.append((glu * _sigmoid(SWIGLU_ALPHA * glu) * (lin + 1.0)).astype(BF16))
    act = jnp.concatenate(acts, axis=1)
    y = jnp.dot(act, wd_ref[...], preferred_element_type=F32) + bd_ref[...]
    ybuf[cur] = y.reshape((BLK,) + ROW3)

    @pl.when(i == n - 1)
    def _():
        for s in range(EXP_NBUF):
            @pl.when(s != cur)
            def _():
                wait_block(gsem.at[s])
                wait_block(ssem.at[s])
        scatter(i, cur)
        wait_block(ssem.at[cur])


def experts(layer, block_e, inv, h2row, wu, bu, wd, bd):
    first = jnp.concatenate([jnp.ones((1,), jnp.int32), (block_e[1:] != block_e[:-1]).astype(jnp.int32)])
    slot = (jnp.cumsum(first) - 1) & 1
    later = jnp.where(block_e[None, :] > block_e[:, None], block_e[None, :], N_EXP).min(axis=1)
    nxt = jnp.where(later == N_EXP, -1, later).astype(jnp.int32)
    per_expert = lambda *blk: pl.BlockSpec((None, None) + blk, lambda i, be, *_: (layer, be[i], 0, 0))
    return pl.pallas_call(
        functools.partial(_expert_kernel, layer=layer),
        out_shape=jax.ShapeDtypeStruct((N_ASSIGN + N_DUMP,) + ROW3, F32),
        grid_spec=pltpu.PrefetchScalarGridSpec(
            num_scalar_prefetch=5,
            grid=(N_BLOCKS,),
            in_specs=[
                pl.BlockSpec(memory_space=pl.ANY),
                pl.BlockSpec(memory_space=pl.ANY), per_expert(1, 2 * D_FF),
                pl.BlockSpec(memory_space=pl.ANY), per_expert(1, D_MODEL),
                pl.BlockSpec((PAIR_TILE, PAIR_TILE), lambda i, *_: (0, 0)),
            ],
            out_specs=pl.BlockSpec(memory_space=pl.ANY),
            scratch_shapes=[
                pltpu.VMEM((EXP_NBUF, BLK) + ROW3, F32), pltpu.VMEM((EXP_NBUF, BLK) + ROW3, F32),
                pltpu.VMEM((D_MODEL, 2 * D_FF), BF16), pltpu.VMEM((D_FF, D_MODEL), BF16),
                pltpu.VMEM((2, D_MODEL, 2 * D_FF), F32), pltpu.VMEM((2, D_FF, D_MODEL), F32),
                pltpu.SemaphoreType.DMA((EXP_NBUF,)), pltpu.SemaphoreType.DMA((EXP_NBUF,)),
                pltpu.SemaphoreType.DMA((2,)),
            ],
        ),
        compiler_params=_cp(("arbitrary",)),
        name="experts",
    )(block_e, first, slot.astype(jnp.int32), nxt, inv, h2row, wu, bu, wd, bd, _pair_selection())


COMB_TM = 256


def _combine_kernel(c0_ref, c1_ref, c2_ref, c3_ref, x_ref, w_ref, mod_ref, oc_ref, ol_ref):
    w = w_ref[...]
    acc = None
    for k, c_ref in enumerate((c0_ref, c1_ref, c2_ref, c3_ref)):
        t = w[:, k:k + 1] * c_ref[...].reshape(COMB_TM, D_MODEL)
        acc = t if acc is None else acc + t
    out = x_ref[...] + mod_ref[5:6, :] * acc
    is_latent = pl.program_id(0) >= N_CTX // COMB_TM

    @pl.when(is_latent)
    def _():
        ol_ref[...] = out

    @pl.when(jnp.logical_not(is_latent))
    def _():
        oc_ref[...] = out


def combine(comb, x1, wgt, mod):
    tiles = T_ALL // COMB_TM
    kth = lambda k: pl.BlockSpec((COMB_TM,) + ROW3, lambda i: (k * tiles + i, 0, 0))
    return pl.pallas_call(
        _combine_kernel,
        out_shape=(jax.ShapeDtypeStruct((N_CTX, D_MODEL), F32), jax.ShapeDtypeStruct((N_LAT, D_MODEL), F32)),
        grid=(tiles,),
        in_specs=[
            kth(0), kth(1), kth(2), kth(3),
            pl.BlockSpec((COMB_TM, D_MODEL), lambda i: (i, 0)),
            pl.BlockSpec((COMB_TM, LANES), lambda i: (i, 0)),
            pl.BlockSpec((None, 6, D_MODEL), lambda i: (_group_of_tile(i, COMB_TM), 0, 0)),
        ],
        out_specs=tuple(_x_specs(COMB_TM, lambda i: i)),
        compiler_params=_cp(("arbitrary",)),
        name="combine",
    )(comb, comb, comb, comb, x1, wgt, mod)


def _rope_table():
    pos = jnp.arange(DEC_SEQ)
    row = (pos // GRID_W).astype(F32)[:, None]
    col = (pos % GRID_W).astype(F32)[:, None]

    def parts(rot_dim):
        nf = rot_dim // 4
        inv = ROPE_BASE ** (-jnp.arange(nf, dtype=F32) / nf)
        cr, sr, cc, sc = jnp.cos(row * inv), jnp.sin(row * inv), jnp.cos(col * inv), jnp.sin(col * inv)
        return jnp.concatenate([cr, cr, cc, cc], axis=1), jnp.concatenate([-sr, sr, -sc, sc], axis=1)

    ca, sa = (jnp.tile(t, (1, 2)) for t in parts(HD_A))
    cc, sc = parts(ROPE_C)
    pad = lambda t, fill: jnp.pad(t, ((0, 0), (NOPE_C, LANES - QK_C)), constant_values=fill)
    lat = jnp.concatenate([ca, sa, pad(cc, 1.0), pad(sc, 0.0)], axis=1)
    ones, zeros = jnp.ones((DEC_SEQ, LANES), F32), jnp.zeros((DEC_SEQ, LANES), F32)
    ident = jnp.concatenate([ones, zeros, ones, zeros], axis=1)
    return jnp.concatenate([ident, lat], axis=0)


def _pair_swaps():
    j = lax.broadcasted_iota(jnp.int32, (LANES, LANES), 0)
    i = lax.broadcasted_iota(jnp.int32, (LANES, LANES), 1)

    def swap(half):
        first = (i % (2 * half)) < half
        return (j == jnp.where(first, i + half, i - half)).astype(BF16)

    return jnp.stack([swap(HD_A // 4), swap(ROPE_C // 4)])


def _arrange_w_in(w_in):
    o = 0
    parts = {}
    for name, n in (("qa", HQ_A * HD_A), ("ka", HKV_A * HD_A), ("va", HKV_A * HD_A), ("ub", D_B), ("cq", Q_LORA),
                    ("ckv", KV_LORA), ("kr", ROPE_C), ("gl", 3 * D_MODEL)):
        parts[name] = w_in[..., o:o + n]
        o += n
    qa = parts["qa"].reshape(DEPTH, D_MODEL, HKV_A, G_A, 1, HD_A)
    eye = jnp.eye(HKV_A, dtype=F32).reshape(1, 1, HKV_A, 1, HKV_A, 1)
    qa_slots = (qa * eye).reshape(DEPTH, D_MODEL, HQ_A * LANES)
    kr = jnp.pad(parts["kr"], ((0, 0), (0, 0), (NOPE_C, LANES - QK_C)))
    w = jnp.concatenate([parts["gl"], qa_slots, parts["ub"], parts["cq"], parts["ka"], parts["va"], parts["ckv"], kr],
                        axis=-1)
    return w.astype(BF16)


def _s5_params(lam_re, lam_im, log_dt, b_re, b_im, c_re, c_im):
    dt = jnp.exp(log_dt)[..., None]
    decay = jnp.exp(lam_re * dt)
    ab_re, ab_im = decay * jnp.cos(lam_im * dt), decay * jnp.sin(lam_im * dt)
    den = lam_re * lam_re + lam_im * lam_im
    f_re = ((ab_re - 1) * lam_re + ab_im * lam_im) / den
    f_im = (ab_im * lam_re - (ab_re - 1) * lam_im) / den
    bb_re = f_re[..., None] * b_re - f_im[..., None] * b_im
    bb_im = f_re[..., None] * b_im + f_im[..., None] * b_re
    nj, gpb = D_B // LANES, LANES // GS_B
    eye = jnp.eye(gpb, dtype=F32)

    def blockdiag_b(bb):
        t = bb.transpose(0, 1, 3, 2).reshape(2, nj, gpb, GS_B, P_B)
        return (t[:, :, :, :, None, :] * eye[None, None, :, None, :, None]).reshape(2, nj, LANES, gpb * P_B)

    def blockdiag_c(cc):
        t = cc.transpose(0, 1, 3, 2).reshape(2, nj, gpb, P_B, GS_B)
        return (t[:, :, :, :, None, :] * eye[None, None, :, None, :, None]).reshape(2, nj, gpb * P_B, LANES)

    wb = jnp.concatenate([blockdiag_b(bb_re), blockdiag_b(bb_im)], axis=-1).astype(BF16)
    wc = jnp.concatenate([blockdiag_c(c_re), -blockdiag_c(c_im)], axis=-2).astype(BF16)
    return wb, wc, ab_re.reshape(2, G_B * P_B), ab_im.reshape(2, G_B * P_B)


def kernel(x_prompt, x_sample, c, cache_attn_k, cache_attn_v, cache_mla_ckv, cache_mla_krope, state_ssm_re, state_ssm_im, c_ctx, w_ada, b_ada, norm_mix_g, norm_ffn_g, w_in, q_norm_a, k_norm_a, sink_a, q_a_norm_c, kv_a_norm_c, w_uq_c, w_ukv_c, q_norm_c, k_norm_c, ssm_lam_re, ssm_lam_im, ssm_log_dt, ssm_b_re, ssm_b_im, ssm_c_re, ssm_c_im, ssm_d, w_glu, b_glu, w_br_a, w_br_b, w_br_c, w_out, w_router, b_router, w_up, b_up, w_down, b_down):
    x_ctx, x_lat = x_prompt.reshape(N_CTX, D_MODEL), x_sample.reshape(N_LAT, D_MODEL)
    cvecs = jnp.concatenate([c_ctx[None], c, jnp.zeros((N_GROUPS - 1 - DEC_BATCH, D_MODEL), F32)], axis=0)
    mods = adaln(cvecs, w_ada, b_ada)

    tab = _rope_table()
    w_in_r = _arrange_w_in(w_in)
    pad_slot = lambda g: jnp.pad(g, ((0, 0), (0, LANES - QK_C))).reshape(DEPTH, 1, LANES)
    gqa = jnp.tile(q_norm_a, (1, 2)).reshape(DEPTH, 1, LANES)
    gka = jnp.tile(k_norm_a, (1, 2)).reshape(DEPTH, 1, LANES)
    gqc, gkc = pad_slot(q_norm_c), pad_slot(k_norm_c)
    wuq = jnp.pad(w_uq_c.reshape(DEPTH, Q_LORA, H_C, QK_C), ((0, 0), (0, 0), (0, 0), (0, LANES - QK_C)))
    wuq = wuq.reshape(DEPTH, Q_LORA, H_C * LANES).astype(BF16)
    wukv4 = w_ukv_c.reshape(DEPTH, KV_LORA, H_C, NOPE_C + V_C)
    wuk = jnp.pad(wukv4[..., :NOPE_C], ((0, 0), (0, 0), (0, 0), (0, LANES - NOPE_C))).reshape(DEPTH, KV_LORA, H_C * LANES)
    wuv = wukv4[..., NOPE_C:].reshape(DEPTH, KV_LORA, H_C * V_C)
    wukv = jnp.concatenate([wuk, wuv], axis=-1).astype(BF16)
    sink = jnp.broadcast_to(sink_a[:, :, None], (DEPTH, HQ_A, LANES))
    w_router_p = jnp.pad(w_router, ((0, 0), (0, 0), (0, LANES - N_EXP)))
    b_router_p = jnp.pad(b_router, ((0, 0), (0, LANES - N_EXP)), constant_values=-jnp.inf).reshape(DEPTH, 1, LANES)
    b_up_p = b_up.reshape(DEPTH, N_EXP, 2 * D_FF // PAIR_TILE, LANES, 2).transpose(0, 1, 2, 4, 3)
    b_up_p = b_up_p.reshape(DEPTH, N_EXP, 1, 2 * D_FF)
    b_down_r = b_down.reshape(DEPTH, N_EXP, 1, D_MODEL)
    kr_cache = jnp.pad(cache_mla_krope, ((0, 0), (0, 0), (0, 0), (NOPE_C, LANES - QK_C)))

    new_k, new_v, new_ckv, new_kr, new_sre, new_sim = [], [], [], [], [], []
    for l in range(DEPTH):
        mod = mods[l]
        proj = inproj(l, x_ctx, x_lat, norm_mix_g[l], mod, w_in_r)
        qa, ka, qc, kc, vc, ckvn = prep(proj, tab, gqa[l], gka[l], q_a_norm_c[l].reshape(1, Q_LORA),
                                         kv_a_norm_c[l].reshape(1, KV_LORA), gqc[l], gkc[l], wuq[l], wukv[l])
        kc_cache, vc_cache = cache_keys(cache_mla_ckv[:, l].reshape(DEC_BATCH * PAST_LEN, KV_LORA),
                                        kr_cache[:, l].reshape(DEC_BATCH * PAST_LEN, LANES), gkc[l], wukv[l])
        oa = attn_a_lat(qa, ka, proj, cache_attn_k[:, l].reshape(DEC_BATCH, PAST_LEN, LANES),
                        cache_attn_v[:, l].reshape(DEC_BATCH, PAST_LEN, LANES), sink[l],
                        attn_a_ctx(qa, ka, proj, sink[l]))
        oc = attn_c_lat(qc, kc, vc, kc_cache, vc_cache, attn_c_ctx(qc, kc, vc))
        wb, wc, a_re, a_im = _s5_params(ssm_lam_re[l], ssm_lam_im[l], ssm_log_dt[l], ssm_b_re[l], ssm_b_im[l],
                                        ssm_c_re[l], ssm_c_im[l])
        proj4 = proj.reshape(TOK_CHUNKS // CHUNK_GROUP, CHUNK_GROUP, S5_TC, N_PROJ)
        per_row = lambda a, rep: jnp.repeat(a, rep, axis=0)
        y_all, f_re, f_im = s5_ctx(proj4, wb, wc, per_row(a_re, SUBLANES).reshape(2, SUBLANES, -1),
                                   per_row(a_im, SUBLANES).reshape(2, SUBLANES, -1))
        h0 = lambda st: st[:, l].transpose(1, 0, 2, 3).reshape(2 * DEC_BATCH, G_B * P_B)
        y_all, y_bwd = s5_lat(proj4, y_all, wb, wc, per_row(a_re, DEC_BATCH), per_row(a_im, DEC_BATCH),
                              h0(state_ssm_re), h0(state_ssm_im))
        ob = s5_finish(proj, y_all.reshape(T_ALL, D_B), y_bwd.reshape(N_LAT, D_B), ssm_d[l].reshape(1, D_B),
                       w_glu[l].astype(BF16), b_glu[l].reshape(1, D_B))
        x1, h2row, idx, rank, wgt, cnt = merge(
            x_ctx, x_lat, proj, oa, ob, oc, mod, norm_ffn_g[l].reshape(1, D_MODEL), w_br_a[l].astype(BF16),
            w_br_b[l].astype(BF16), w_br_c[l].astype(BF16), w_out[l].astype(BF16), w_router_p[l], b_router_p[l])
        counts = cnt[0, :N_EXP].astype(jnp.int32)
        padded = (counts + BLK - 1) // BLK * BLK
        pad_end = jnp.cumsum(padded)
        pad_start = pad_end - padded
        flat = lambda a: a[:, :TOP_K].reshape(-1)
        dest = (BLK + pad_start[flat(idx)] + flat(rank)).astype(jnp.int32)
        block_start = jnp.arange(N_BLOCKS, dtype=jnp.int32) * BLK
        block_e = jnp.minimum(jnp.sum(pad_end[None, :] <= block_start[:, None], axis=1), N_EXP - 1)
        block_e = block_e.astype(jnp.int32)
        comb = experts(l, block_e, invert_plan(dest), h2row, w_up, b_up_p, w_down, b_down_r)
        x_ctx, x_lat = combine(comb, x1, wgt, mod)

        new_k.append(ka[:N_CTX].reshape(BATCH, SEQ, HKV_A, HD_A))
        new_v.append(proj[:N_CTX, C_VA:C_VA + LANES].reshape(BATCH, SEQ, HKV_A, HD_A))
        new_ckv.append(ckvn[:N_CTX].reshape(BATCH, SEQ, KV_LORA))
        new_kr.append(proj[:N_CTX, C_KR + NOPE_C:C_KR + QK_C].reshape(BATCH, SEQ, ROPE_C))
        new_sre.append(f_re.reshape(2, BATCH, G_B, P_B).transpose(1, 0, 2, 3))
        new_sim.append(f_im.reshape(2, BATCH, G_B, P_B).transpose(1, 0, 2, 3))

    y_prompt = x_ctx.reshape(BATCH, SEQ, D_MODEL)
    y_sample = x_lat.reshape(DEC_BATCH, DEC_SEQ, D_MODEL)
    return (y_prompt, y_sample, jnp.stack(new_k, axis=1), jnp.stack(new_v, axis=1), jnp.stack(new_ckv, axis=1),
            jnp.stack(new_kr, axis=1), jnp.stack(new_sre, axis=1), jnp.stack(new_sim, axis=1))
```

```python
import functools
import math

import jax
import jax.numpy as jnp
from jax import lax
from jax.experimental import pallas as pl
from jax.experimental.pallas import tpu as pltpu

D_MODEL = 1024
BATCH = 32
SEQ = 256
DEPTH = 2
DEC_BATCH = 4
DEC_SEQ = 2048
PAST_LEN = 256
GRID_W = 64
ROPE_BASE = 10000.0
EPS = 1e-6
NEG_INF = -1e30
BAND = 128
BLK = 128
HQ_A, HKV_A, HD_A = 8, 2, 64
G_A = HQ_A // HKV_A
WINDOW = 128
D_B, GS_B, P_B = 512, 16, 64
G_B = D_B // GS_B
H_C, Q_LORA, KV_LORA, NOPE_C, ROPE_C, V_C = 8, 256, 128, 64, 32, 64
QK_C = NOPE_C + ROPE_C
N_EXP, TOP_K, D_FF = 32, 4, 1024
SWIGLU_ALPHA, SWIGLU_LIMIT = 1.702, 7.0

N_CTX = BATCH * SEQ
N_LAT = DEC_BATCH * DEC_SEQ
T_ALL = N_CTX + N_LAT
N_GROUPS = 8

LANES = 128
SUBLANES = 8

C_GL = 0
C_QA = 3 * D_MODEL
C_UB = C_QA + HQ_A * LANES
C_CQ = C_UB + D_B
C_KA = C_CQ + Q_LORA
C_VA = C_KA + LANES
C_CKV = C_VA + LANES
C_KR = C_CKV + LANES
N_PROJ = C_KR + LANES

N_ASSIGN = T_ALL * TOP_K
N_BLOCKS = N_ASSIGN // BLK + N_EXP
BUF_LEN = N_BLOCKS * BLK

F32 = jnp.float32
BF16 = jnp.bfloat16
ROW3 = (D_MODEL // LANES, LANES)
VMEM_LIMIT = 56 * 1024 * 1024


def _cp(sem, vmem=VMEM_LIMIT):
    return pltpu.CompilerParams(dimension_semantics=sem, vmem_limit_bytes=vmem)


def _group_of_tile(i, tm):
    n_ctx_tiles = N_CTX // tm
    per_batch = DEC_SEQ // tm
    return jnp.where(i < n_ctx_tiles, 0, 1 + (i - n_ctx_tiles) // per_batch)


def _sigmoid(x):
    return 1.0 / (1.0 + jnp.exp(-x))


def _x_specs(tm, tile_of):
    n_ctx_tiles = N_CTX // tm
    return [pl.BlockSpec((tm, D_MODEL), lambda *g: (jnp.minimum(tile_of(*g), n_ctx_tiles - 1), 0)),
            pl.BlockSpec((tm, D_MODEL), lambda *g: (jnp.maximum(tile_of(*g) - n_ctx_tiles, 0), 0))]


def _x_tile(xc_ref, xl_ref, tile, tm):
    return jnp.where(tile >= N_CTX // tm, xl_ref[...], xc_ref[...])


def _adaln_kernel(c_ref, w_ref, b_ref, o_ref):
    c = c_ref[...]
    s = c * _sigmoid(c)
    o_ref[...] = jnp.dot(s.astype(BF16), w_ref[...].astype(BF16), preferred_element_type=F32) + b_ref[...]


def adaln(cvecs, w_ada, b_ada):
    tn = 1536
    out = pl.pallas_call(
        _adaln_kernel,
        out_shape=jax.ShapeDtypeStruct((DEPTH, N_GROUPS, 6 * D_MODEL), F32),
        grid=(DEPTH, 6 * D_MODEL // tn),
        in_specs=[
            pl.BlockSpec((N_GROUPS, D_MODEL), lambda l, j: (0, 0)),
            pl.BlockSpec((None, D_MODEL, tn), lambda l, j: (l, 0, j)),
            pl.BlockSpec((None, 1, tn), lambda l, j: (l, 0, j)),
        ],
        out_specs=pl.BlockSpec((None, N_GROUPS, tn), lambda l, j: (l, 0, j)),
        compiler_params=_cp(("arbitrary", "arbitrary")),
        name="adaln",
    )(cvecs, w_ada, b_ada.reshape(DEPTH, 1, 6 * D_MODEL))
    return out.reshape(DEPTH, N_GROUPS, 6, D_MODEL)


def _rms(x, g):
    ms = jnp.mean(x * x, axis=-1, keepdims=True)
    return x * lax.rsqrt(ms + EPS) * g


INPROJ_TM = 512


def _inproj_kernel(xc_ref, xl_ref, g_ref, mod_ref, w_ref, o_ref):
    h = _rms(_x_tile(xc_ref, xl_ref, pl.program_id(1), INPROJ_TM), g_ref[...])
    h = h * (1.0 + mod_ref[1:2, :]) + mod_ref[0:1, :]
    o_ref[...] = jnp.dot(h.astype(BF16), w_ref[...], preferred_element_type=F32)


def inproj(layer, x_ctx, x_lat, g, mod, w):
    tm, tn = INPROJ_TM, 1792
    return pl.pallas_call(
        _inproj_kernel,
        out_shape=jax.ShapeDtypeStruct((T_ALL, N_PROJ), F32),
        grid=(N_PROJ // tn, T_ALL // tm),
        in_specs=_x_specs(tm, lambda j, i: i) + [
            pl.BlockSpec((1, D_MODEL), lambda j, i: (0, 0)),
            pl.BlockSpec((None, 6, D_MODEL), lambda j, i: (_group_of_tile(i, tm), 0, 0)),
            pl.BlockSpec((None, D_MODEL, tn), lambda j, i: (layer, 0, j)),
        ],
        out_specs=pl.BlockSpec((tm, tn), lambda j, i: (i, j)),
        compiler_params=_cp(("arbitrary", "arbitrary")),
        name="inproj",
    )(x_ctx, x_lat, g.reshape(1, D_MODEL), mod, w)


def _rope(slab, cos, sin, swap):
    hi = slab.astype(BF16)
    lo = (slab - hi.astype(F32)).astype(BF16)
    partner = jnp.dot(hi, swap, preferred_element_type=F32) + jnp.dot(lo, swap, preferred_element_type=F32)
    return slab * cos + partner * sin


def _slot_norm(slab, gain, n_real):
    ms = jnp.sum(slab * slab, axis=-1, keepdims=True) * (1.0 / n_real)
    return slab * lax.rsqrt(ms + EPS) * gain


def _mla_keys(ckvn, kr_blk, wukv_ref, gk, rope_c):
    kv = jnp.dot(ckvn.astype(BF16), wukv_ref[...], preferred_element_type=F32)
    ks = []
    for h in range(H_C):
        slab = kv[:, h * LANES:(h + 1) * LANES] + kr_blk
        slab = _slot_norm(slab, gk, QK_C)
        if rope_c is not None:
            slab = _rope(slab, *rope_c)
        ks.append(slab.astype(BF16))
    return jnp.concatenate(ks, axis=1), kv[:, H_C * LANES:].astype(BF16)


PREP_TM = 512


def _prep_kernel(qa_ref, cq_ref, ka_ref, ckv_ref, kr_ref, tab_ref,
                 gqa_ref, gka_ref, gcq_ref, gckv_ref, gqc_ref, gkc_ref, wuq_ref, wukv_ref, swap_ref,
                 qa_o, ka_o, qc_o, kc_o, vc_o, ckvn_o):
    def body(rotate):
        if rotate:
            tab = tab_ref[...]
            rope_a = (tab[:, 0:128], tab[:, 128:256], swap_ref[0])
            rope_c = (tab[:, 256:384], tab[:, 384:512], swap_ref[1])
            rot_a = lambda x: _rope(x, *rope_a)
            rot_c = lambda x: _rope(x, *rope_c)
        else:
            rope_c = None
            rot_a = rot_c = lambda x: x
        lane = lax.broadcasted_iota(jnp.int32, (1, LANES), 1)

        gqa = gqa_ref[...]
        qa = qa_ref[...]
        outs = []
        for h in range(HQ_A):
            slab = _slot_norm(qa[:, h * LANES:(h + 1) * LANES], gqa, HD_A)
            outs.append((rot_a(slab) * (HD_A ** -0.5)).astype(BF16))
        qa_o[...] = jnp.concatenate(outs, axis=1)

        ka = ka_ref[...]
        sq = ka * ka
        lo = lane < HD_A
        ms_lo = jnp.sum(jnp.where(lo, sq, 0.0), axis=-1, keepdims=True)
        ms_hi = jnp.sum(jnp.where(lo, 0.0, sq), axis=-1, keepdims=True)
        rs = jnp.where(lo, lax.rsqrt(ms_lo * (1.0 / HD_A) + EPS), lax.rsqrt(ms_hi * (1.0 / HD_A) + EPS))
        ka_o[...] = rot_a(ka * rs * gka_ref[...])

        cqn = _rms(cq_ref[...], gcq_ref[...])
        q = jnp.dot(cqn.astype(BF16), wuq_ref[...], preferred_element_type=F32)
        gqc = gqc_ref[...]
        outs = []
        for h in range(H_C):
            slab = _slot_norm(q[:, h * LANES:(h + 1) * LANES], gqc, QK_C)
            outs.append((rot_c(slab) * (QK_C ** -0.5)).astype(BF16))
        qc_o[...] = jnp.concatenate(outs, axis=1)

        ckvn = _rms(ckv_ref[...], gckv_ref[...])
        ckvn_o[...] = ckvn
        kc, vc = _mla_keys(ckvn, kr_ref[...], wukv_ref, gkc_ref[...], rope_c)
        kc_o[...] = kc
        vc_o[...] = vc

    is_latent = pl.program_id(0) >= N_CTX // PREP_TM
    pl.when(is_latent)(lambda: body(True))
    pl.when(jnp.logical_not(is_latent))(lambda: body(False))


def prep(proj, tab, gqa, gka, gcq, gckv, gqc, gkc, wuq, wukv):
    tm = PREP_TM
    n_ctx_tiles = N_CTX // tm
    per_batch = DEC_SEQ // tm

    def tab_map(i):
        return (jnp.where(i < n_ctx_tiles, 0, per_batch + (i - n_ctx_tiles) % per_batch), 0)

    def col(width, off):
        return pl.BlockSpec((tm, width), lambda i: (i, off // width))

    def full(shape):
        return pl.BlockSpec(shape, lambda i: (0,) * len(shape))

    def row_out(width):
        return pl.BlockSpec((tm, width), lambda i: (i, 0))

    return pl.pallas_call(
        _prep_kernel,
        out_shape=(
            jax.ShapeDtypeStruct((T_ALL, HQ_A * LANES), BF16),
            jax.ShapeDtypeStruct((T_ALL, LANES), F32),
            jax.ShapeDtypeStruct((T_ALL, H_C * LANES), BF16),
            jax.ShapeDtypeStruct((T_ALL, H_C * LANES), BF16),
            jax.ShapeDtypeStruct((T_ALL, H_C * V_C), BF16),
            jax.ShapeDtypeStruct((T_ALL, KV_LORA), F32),
        ),
        grid=(T_ALL // tm,),
        in_specs=[
            col(HQ_A * LANES, C_QA), col(Q_LORA, C_CQ), col(LANES, C_KA), col(LANES, C_CKV), col(LANES, C_KR),
            pl.BlockSpec((tm, 4 * LANES), tab_map),
            full((1, LANES)), full((1, LANES)), full((1, Q_LORA)), full((1, KV_LORA)),
            full((1, LANES)), full((1, LANES)),
            full((Q_LORA, H_C * LANES)), full((KV_LORA, H_C * LANES + H_C * V_C)), full((2, LANES, LANES)),
        ],
        out_specs=(row_out(HQ_A * LANES), row_out(LANES), row_out(H_C * LANES), row_out(H_C * LANES),
                   row_out(H_C * V_C), row_out(KV_LORA)),
        compiler_params=_cp(("arbitrary",)),
        name="prep",
    )(proj, proj, proj, proj, proj, tab, gqa, gka, gcq, gckv, gqc, gkc, wuq, wukv, _pair_swaps())


def _cachekeys_kernel(ckv_ref, kr_ref, gkc_ref, wukv_ref, kc_o, vc_o):
    kc, vc = _mla_keys(ckv_ref[...], kr_ref[...], wukv_ref, gkc_ref[...], None)
    kc_o[...] = kc
    vc_o[...] = vc


def cache_keys(ckv, kr_blk, gkc, wukv):
    r = ckv.shape[0]
    tm = 256
    return pl.pallas_call(
        _cachekeys_kernel,
        out_shape=(jax.ShapeDtypeStruct((r, H_C * LANES), BF16), jax.ShapeDtypeStruct((r, H_C * V_C), BF16)),
        grid=(r // tm,),
        in_specs=[
            pl.BlockSpec((tm, LANES), lambda i: (i, 0)),
            pl.BlockSpec((tm, LANES), lambda i: (i, 0)),
            pl.BlockSpec((1, LANES), lambda i: (0, 0)),
            pl.BlockSpec((KV_LORA, H_C * LANES + H_C * V_C), lambda i: (0, 0)),
        ],
        out_specs=(pl.BlockSpec((tm, H_C * LANES), lambda i: (i, 0)), pl.BlockSpec((tm, H_C * V_C), lambda i: (i, 0))),
        compiler_params=_cp(("arbitrary",)),
        name="cache_keys",
    )(ckv, kr_blk, gkc, wukv)


def _attn_body(q_ref, segs, sink_ref, o_ref, *, n_heads, k_slot, v_slab, v_half, tq, band_qi=None, key_chunk=None):
    outs = []
    for h in range(n_heads):
        qh = q_ref[:, h * LANES:(h + 1) * LANES]
        scores = []
        for k_ref, _, off in segs:
            kh = k_ref[:, k_slot(h) * LANES:(k_slot(h) + 1) * LANES].astype(BF16)
            s = lax.dot_general(qh, kh, (((1,), (1,)), ((), ())), preferred_element_type=F32)
            if off is not None:
                tk = s.shape[1]
                blk = band_qi + off
                q_pos = band_qi * tq + lax.broadcasted_iota(jnp.int32, (tq, tk), 0)
                k_pos = blk * tk + lax.broadcasted_iota(jnp.int32, (tq, tk), 1)
                ok = (jnp.abs(k_pos - q_pos) <= WINDOW) & (blk >= 0) & (blk < DEC_SEQ // tk)
                s = jnp.where(ok, s, NEG_INF)
            scores.append(s)
        m = scores[0].max(axis=-1, keepdims=True)
        for s in scores[1:]:
            m = jnp.maximum(m, s.max(axis=-1, keepdims=True))
        if sink_ref is not None:
            sink = sink_ref[h:h + 1, 0:1]
            m = jnp.maximum(m, sink)
            denom = jnp.exp(sink - m)
        else:
            denom = jnp.zeros_like(m)
        acc = None
        if key_chunk is None:
            for s, (_, v_ref, _) in zip(scores, segs):
                p = jnp.exp(s - m)
                denom = denom + p.sum(axis=-1, keepdims=True)
                vs = v_ref[:, v_slab(h) * LANES:(v_slab(h) + 1) * LANES].astype(BF16)
                pv = jnp.dot(p.astype(BF16), vs, preferred_element_type=F32)
                acc = pv if acc is None else acc + pv
        else:
            part = None
            for s, (_, v_ref, _) in zip(scores, segs):
                for c in range(0, s.shape[1], key_chunk):
                    p = jnp.exp(s[:, c:c + key_chunk] - m)
                    part = p if part is None else part + p
                    vs = v_ref[c:c + key_chunk, v_slab(h) * LANES:(v_slab(h) + 1) * LANES].astype(BF16)
                    pv = jnp.dot(p.astype(BF16), vs, preferred_element_type=F32)
                    acc = pv if acc is None else acc + pv
            denom = denom + part.sum(axis=-1, keepdims=True)
        half = v_half(h)
        outs.append(acc[:, half * 64:(half + 1) * 64] * (1.0 / denom))
    o_ref[...] = jnp.concatenate(outs, axis=1).astype(o_ref.dtype)


ATT_CTX_SEQS = 4
_C_CFG = dict(n_heads=H_C, k_slot=lambda h: h, v_slab=lambda h: h // 2, v_half=lambda h: h % 2)


def _attn_a_body(q_ref, segs, sink_ref, o_ref, *, tq, band_qi=None):
    rows = HQ_A * tq
    q = jnp.concatenate([q_ref[:, h * LANES:(h + 1) * LANES] for h in range(HQ_A)], axis=0)
    sink = jnp.concatenate([jnp.broadcast_to(sink_ref[h:h + 1, 0:1], (tq, 1)) for h in range(HQ_A)], axis=0)
    scores = []
    for k_ref, _, off in segs:
        s = lax.dot_general(q, k_ref[...].astype(BF16), (((1,), (1,)), ((), ())), preferred_element_type=F32)
        if off is not None:
            tk = s.shape[1]
            blk = band_qi + off
            q_pos = band_qi * tq + (lax.broadcasted_iota(jnp.int32, (rows, tk), 0) & (tq - 1))
            k_pos = blk * tk + lax.broadcasted_iota(jnp.int32, (rows, tk), 1)
            ok = (jnp.abs(k_pos - q_pos) <= WINDOW) & (blk >= 0) & (blk < DEC_SEQ // tk)
            s = jnp.where(ok, s, NEG_INF)
        scores.append(s)
    m = sink
    for s in scores:
        m = jnp.maximum(m, s.max(axis=-1, keepdims=True))
    denom = jnp.exp(sink - m)
    acc = None
    for s, (_, v_ref, _) in zip(scores, segs):
        p = jnp.exp(s - m)
        denom = denom + p.sum(axis=-1, keepdims=True)
        pv = jnp.dot(p.astype(BF16), v_ref[...].astype(BF16), preferred_element_type=F32)
        acc = pv if acc is None else acc + pv
    acc = acc * (1.0 / denom)
    outs = []
    for h in range(HQ_A):
        kv = h // G_A
        outs.append(acc[h * tq:(h + 1) * tq, kv * HD_A:(kv + 1) * HD_A])
    o_ref[...] = jnp.concatenate(outs, axis=1).astype(o_ref.dtype)


def _attn_a_ctx_kernel(q_ref, k_ref, v_ref, sink_ref, o_init, o_ref):
    del o_init
    for b in range(ATT_CTX_SEQS):
        rows = pl.ds(b * SEQ, SEQ)
        _attn_a_body(q_ref.at[rows], [(k_ref.at[rows], v_ref.at[rows], None)], sink_ref, o_ref.at[rows], tq=SEQ)


def _attn_a_lat_kernel(q_ref, k0, k1, k2, v0, v1, v2, kc_ref, vc_ref, sink_ref, o_ctx, o_ref):
    del o_ctx
    qi = pl.program_id(1)
    segs = [(k0, v0, -1), (k1, v1, 0), (k2, v2, 1), (kc_ref, vc_ref, None)]
    _attn_a_body(q_ref, segs, sink_ref, o_ref, tq=BAND, band_qi=qi)


def _attn_c_ctx_kernel(q_ref, k_ref, v_ref, o_init, o_ref):
    del o_init
    for b in range(ATT_CTX_SEQS):
        rows = pl.ds(b * SEQ, SEQ)
        _attn_body(q_ref.at[rows], [(k_ref.at[rows], v_ref.at[rows], None)], None, o_ref.at[rows], tq=SEQ, **_C_CFG)


def _attn_c_lat_kernel(q_ref, k_ref, v_ref, kc_ref, vc_ref, o_ctx, o_ref):
    del o_ctx
    _attn_body(q_ref, [(k_ref, v_ref, None), (kc_ref, vc_ref, None)], None, o_ref, tq=256, key_chunk=LANES, **_C_CFG)


def attn_a_ctx(qa, ka, proj, sink):
    return pl.pallas_call(
        _attn_a_ctx_kernel,
        out_shape=jax.ShapeDtypeStruct((T_ALL, HQ_A * HD_A), BF16),
        grid=(BATCH // ATT_CTX_SEQS,),
        in_specs=[
            pl.BlockSpec((ATT_CTX_SEQS * SEQ, HQ_A * LANES), lambda b: (b, 0)),
            pl.BlockSpec((ATT_CTX_SEQS * SEQ, LANES), lambda b: (b, 0)),
            pl.BlockSpec((ATT_CTX_SEQS * SEQ, LANES), lambda b: (b, C_VA // LANES)),
            pl.BlockSpec((SUBLANES, LANES), lambda b: (0, 0)),
            pl.BlockSpec(memory_space=pl.ANY),
        ],
        out_specs=pl.BlockSpec((ATT_CTX_SEQS * SEQ, HQ_A * HD_A), lambda b: (b, 0)),
        input_output_aliases={4: 0},
        compiler_params=_cp(("arbitrary",)),
        name="attn_a_ctx",
    )(qa, ka, proj, sink, jnp.zeros((T_ALL, HQ_A * HD_A), BF16))


def attn_a_lat(qa, ka, proj, k_cache, v_cache, sink, o_all):
    nb = DEC_SEQ // BAND
    base = N_CTX // BAND

    def band(off, colblk):
        return pl.BlockSpec((BAND, LANES), lambda b, i: (base + b * nb + jnp.clip(i + off, 0, nb - 1), colblk))

    return pl.pallas_call(
        _attn_a_lat_kernel,
        out_shape=jax.ShapeDtypeStruct((T_ALL, HQ_A * HD_A), BF16),
        grid=(DEC_BATCH, nb),
        in_specs=[
            pl.BlockSpec((BAND, HQ_A * LANES), lambda b, i: (base + b * nb + i, 0)),
            band(-1, 0), band(0, 0), band(1, 0),
            band(-1, C_VA // LANES), band(0, C_VA // LANES), band(1, C_VA // LANES),
            pl.BlockSpec((None, PAST_LEN, LANES), lambda b, i: (b, 0, 0)),
            pl.BlockSpec((None, PAST_LEN, LANES), lambda b, i: (b, 0, 0)),
            pl.BlockSpec((SUBLANES, LANES), lambda b, i: (0, 0)),
            pl.BlockSpec(memory_space=pl.ANY),
        ],
        out_specs=pl.BlockSpec((BAND, HQ_A * HD_A), lambda b, i: (base + b * nb + i, 0)),
        input_output_aliases={10: 0},
        compiler_params=_cp(("arbitrary", "arbitrary")),
        name="attn_a_lat",
    )(qa, ka, ka, ka, proj, proj, proj, k_cache, v_cache, sink, o_all)


def attn_c_ctx(qc, kc, vc):
    return pl.pallas_call(
        _attn_c_ctx_kernel,
        out_shape=jax.ShapeDtypeStruct((T_ALL, H_C * V_C), BF16),
        grid=(BATCH // ATT_CTX_SEQS,),
        in_specs=[
            pl.BlockSpec((ATT_CTX_SEQS * SEQ, H_C * LANES), lambda b: (b, 0)),
            pl.BlockSpec((ATT_CTX_SEQS * SEQ, H_C * LANES), lambda b: (b, 0)),
            pl.BlockSpec((ATT_CTX_SEQS * SEQ, H_C * V_C), lambda b: (b, 0)),
            pl.BlockSpec(memory_space=pl.ANY),
        ],
        out_specs=pl.BlockSpec((ATT_CTX_SEQS * SEQ, H_C * V_C), lambda b: (b, 0)),
        input_output_aliases={3: 0},
        compiler_params=_cp(("arbitrary",)),
        name="attn_c_ctx",
    )(qc, kc, vc, jnp.zeros((T_ALL, H_C * V_C), BF16))


def attn_c_lat(qc, kc, vc, kc_cache, vc_cache, o_all):
    tq = 256
    nq = DEC_SEQ // tq
    qbase = N_CTX // tq
    kbase = N_CTX // DEC_SEQ
    return pl.pallas_call(
        _attn_c_lat_kernel,
        out_shape=jax.ShapeDtypeStruct((T_ALL, H_C * V_C), BF16),
        grid=(DEC_BATCH, nq),
        in_specs=[
            pl.BlockSpec((tq, H_C * LANES), lambda b, i: (qbase + b * nq + i, 0)),
            pl.BlockSpec((DEC_SEQ, H_C * LANES), lambda b, i: (kbase + b, 0)),
            pl.BlockSpec((DEC_SEQ, H_C * V_C), lambda b, i: (kbase + b, 0)),
            pl.BlockSpec((PAST_LEN, H_C * LANES), lambda b, i: (b, 0)),
            pl.BlockSpec((PAST_LEN, H_C * V_C), lambda b, i: (b, 0)),
            pl.BlockSpec(memory_space=pl.ANY),
        ],
        out_specs=pl.BlockSpec((tq, H_C * V_C), lambda b, i: (qbase + b * nq + i, 0)),
        input_output_aliases={5: 0},
        compiler_params=_cp(("arbitrary", "arbitrary")),
        name="attn_c_lat",
    )(qc, kc, vc, kc_cache, vc_cache, o_all)


S5_TC = 256
S5_CH = 512
S5_UNROLL = 8
TOK_CHUNKS = T_ALL // S5_TC
CHUNK_GROUP = 8
assert SEQ == S5_TC and DEC_SEQ == CHUNK_GROUP * S5_TC and BATCH == 4 * CHUNK_GROUP


def _s5_bproj(u2, wb_ref, pick_first):
    def part(lo, hi):
        r = jnp.dot(u2, wb_ref[0, :, lo:hi], preferred_element_type=F32)
        if pick_first is not None:
            r = jnp.where(pick_first, r, jnp.dot(u2, wb_ref[1, :, lo:hi], preferred_element_type=F32))
        return r.reshape(S5_TC, SUBLANES, S5_CH)
    return part(0, S5_CH), part(S5_CH, 2 * S5_CH)


def _s5_cproj(xre, xim, wc_ref, pick_first):
    rows = S5_TC * SUBLANES
    xr = xre[...].reshape(rows, S5_CH).astype(BF16)
    xi = xim[...].reshape(rows, S5_CH).astype(BF16)

    def part(d):
        return (jnp.dot(xr, wc_ref[d, 0:S5_CH, :], preferred_element_type=F32)
                + jnp.dot(xi, wc_ref[d, S5_CH:, :], preferred_element_type=F32))

    y = part(0)
    if pick_first is not None:
        y = jnp.where(pick_first, y, part(1))
    return jnp.swapaxes(y.reshape(S5_TC, SUBLANES, LANES), 0, 1)


def _s5_ctx_kernel(u_ref, wb_ref, wc_ref, are_ref, aim_ref, y_init, y_ref, fre_ref, fim_ref, bre, bim):
    del y_init
    d = pl.program_id(2)
    rows = S5_TC * SUBLANES
    u2 = jnp.swapaxes(u_ref[...], 0, 1).reshape(rows, LANES).astype(BF16)
    b_re, b_im = _s5_bproj(u2, wb_ref, None)
    bre[...] = b_re
    bim[...] = b_im
    a_re = are_ref[...]
    a_im = aim_ref[...]

    def outer(i, carry):
        sr, si = carry
        for j in range(S5_UNROLL):
            t0 = i * S5_UNROLL + j
            t = jnp.where(d == 1, S5_TC - 1 - t0, t0)
            nr = a_re * sr - a_im * si + bre[t]
            ni = a_re * si + a_im * sr + bim[t]
            bre[t] = nr
            bim[t] = ni
            sr, si = nr, ni
        return sr, si

    zero = jnp.zeros((SUBLANES, S5_CH), F32)
    sr, si = lax.fori_loop(0, S5_TC // S5_UNROLL, outer, (zero, zero))
    fre_ref[...] = sr
    fim_ref[...] = si
    y = _s5_cproj(bre, bim, wc_ref, None)

    @pl.when(d == 0)
    def _():
        y_ref[...] = y

    @pl.when(d == 1)
    def _():
        y_ref[...] += y


def s5_ctx(proj4, wb, wc, a_re, a_im):
    nj = D_B // LANES
    ucol = C_UB // LANES
    return pl.pallas_call(
        _s5_ctx_kernel,
        out_shape=(
            jax.ShapeDtypeStruct((TOK_CHUNKS // CHUNK_GROUP, CHUNK_GROUP, S5_TC, D_B), F32),
            jax.ShapeDtypeStruct((2, BATCH, G_B * P_B), F32),
            jax.ShapeDtypeStruct((2, BATCH, G_B * P_B), F32),
        ),
        grid=(BATCH // CHUNK_GROUP, nj, 2),
        in_specs=[
            pl.BlockSpec((None, CHUNK_GROUP, S5_TC, LANES), lambda sb, j, d: (sb, 0, 0, ucol + j)),
            pl.BlockSpec((1, None, LANES, 2 * S5_CH), lambda sb, j, d: (d, j, 0, 0)),
            pl.BlockSpec((1, None, 2 * S5_CH, LANES), lambda sb, j, d: (d, j, 0, 0)),
            pl.BlockSpec((None, SUBLANES, S5_CH), lambda sb, j, d: (d, 0, j)),
            pl.BlockSpec((None, SUBLANES, S5_CH), lambda sb, j, d: (d, 0, j)),
            pl.BlockSpec(memory_space=pl.ANY),
        ],
        out_specs=(
            pl.BlockSpec((None, CHUNK_GROUP, S5_TC, LANES), lambda sb, j, d: (sb, 0, 0, j)),
            pl.BlockSpec((None, SUBLANES, S5_CH), lambda sb, j, d: (d, sb, j)),
            pl.BlockSpec((None, SUBLANES, S5_CH), lambda sb, j, d: (d, sb, j)),
        ),
        scratch_shapes=[pltpu.VMEM((S5_TC, SUBLANES, S5_CH), F32), pltpu.VMEM((S5_TC, SUBLANES, S5_CH), F32)],
        input_output_aliases={5: 0},
        compiler_params=_cp(("arbitrary", "arbitrary", "arbitrary")),
        name="s5_ctx",
    )(proj4, wb, wc, a_re, a_im, jnp.zeros((TOK_CHUNKS // CHUNK_GROUP, CHUNK_GROUP, S5_TC, D_B), F32))


def _s5_lat_kernel(uf_ref, ub_ref, wb_ref, wc_ref, are_ref, aim_ref, h0re_ref, h0im_ref, y_ctx,
                   yf_ref, yb_ref, bre, bim, xre, xim, sre, sim):
    del y_ctx
    k = pl.program_id(1)

    @pl.when(k == 0)
    def _():
        sre[...] = h0re_ref[...]
        sim[...] = h0im_ref[...]

    rows = S5_TC * SUBLANES
    half = SUBLANES // 2
    u8 = jnp.concatenate([uf_ref[...], ub_ref[...]], axis=0)
    u2 = jnp.swapaxes(u8, 0, 1).reshape(rows, LANES).astype(BF16)
    fwd_rows = (lax.broadcasted_iota(jnp.int32, (rows, 1), 0) % SUBLANES) < half
    b_re, b_im = _s5_bproj(u2, wb_ref, fwd_rows)
    bre[...] = b_re
    bim[...] = b_im
    a_re = are_ref[...]
    a_im = aim_ref[...]
    fwd8 = lax.broadcasted_iota(jnp.int32, (SUBLANES, 1), 0) < half

    def outer(i, carry):
        sr, si = carry
        for j in range(S5_UNROLL):
            t = i * S5_UNROLL + j
            tb = S5_TC - 1 - t
            nr = a_re * sr - a_im * si + jnp.where(fwd8, bre[t], bre[tb])
            ni = a_re * si + a_im * sr + jnp.where(fwd8, bim[t], bim[tb])
            xre[t, 0:half, :] = nr[0:half]
            xre[tb, half:, :] = nr[half:]
            xim[t, 0:half, :] = ni[0:half]
            xim[tb, half:, :] = ni[half:]
            sr, si = nr, ni
        return sr, si

    sr, si = lax.fori_loop(0, S5_TC // S5_UNROLL, outer, (sre[...], sim[...]))
    sre[...] = sr
    sim[...] = si
    y = _s5_cproj(xre, xim, wc_ref, fwd_rows)
    yf_ref[...] = y[0:half]
    yb_ref[...] = y[half:]


def s5_lat(proj4, y_all, wb, wc, a_re, a_im, h0_re, h0_im):
    nj = D_B // LANES
    nk = DEC_SEQ // S5_TC
    ucol = C_UB // LANES
    lat = lambda col0, rev: pl.BlockSpec(
        (DEC_BATCH, None, S5_TC, LANES), lambda j, k: (1, (nk - 1 - k) if rev else k, 0, col0 + j))
    vec = lambda: pl.BlockSpec((SUBLANES, S5_CH), lambda j, k: (0, j))
    buf = lambda: pltpu.VMEM((S5_TC, SUBLANES, S5_CH), F32)
    return pl.pallas_call(
        _s5_lat_kernel,
        out_shape=(
            jax.ShapeDtypeStruct(y_all.shape, F32),
            jax.ShapeDtypeStruct((DEC_BATCH, nk, S5_TC, D_B), F32),
        ),
        grid=(nj, nk),
        in_specs=[
            lat(ucol, False), lat(ucol, True),
            pl.BlockSpec((2, None, LANES, 2 * S5_CH), lambda j, k: (0, j, 0, 0)),
            pl.BlockSpec((2, None, 2 * S5_CH, LANES), lambda j, k: (0, j, 0, 0)),
            vec(), vec(), vec(), vec(),
            pl.BlockSpec(memory_space=pl.ANY),
        ],
        out_specs=(
            lat(0, False),
            pl.BlockSpec((DEC_BATCH, None, S5_TC, LANES), lambda j, k: (0, nk - 1 - k, 0, j)),
        ),
        scratch_shapes=[buf(), buf(), buf(), buf(),
                        pltpu.VMEM((SUBLANES, S5_CH), F32), pltpu.VMEM((SUBLANES, S5_CH), F32)],
        input_output_aliases={8: 0},
        compiler_params=_cp(("arbitrary", "arbitrary")),
        name="s5_lat",
    )(proj4, proj4, wb, wc, a_re, a_im, h0_re, h0_im, y_all)


S5FIN_TM = 512


def _s5fin_kernel(u_ref, y_ref, yb_ref, d_ref, w_ref, b_ref, o_ref):
    def body(latent):
        y = d_ref[...] * u_ref[...] + y_ref[...]
        if latent:
            y = y + yb_ref[...]
        y = 0.5 * y * (1.0 + jnp.tanh(math.sqrt(2.0 / math.pi) * (y + 0.044715 * (y * y * y))))
        z = jnp.dot(y.astype(BF16), w_ref[...], preferred_element_type=F32) + b_ref[...]
        o_ref[...] = (y * _sigmoid(z)).astype(o_ref.dtype)

    is_latent = pl.program_id(0) >= N_CTX // S5FIN_TM
    pl.when(is_latent)(lambda: body(True))
    pl.when(jnp.logical_not(is_latent))(lambda: body(False))


def s5_finish(proj, y, y_bwd_lat, d, w_glu, b_glu):
    tm = S5FIN_TM
    n_ctx_tiles = N_CTX // tm
    return pl.pallas_call(
        _s5fin_kernel,
        out_shape=jax.ShapeDtypeStruct((T_ALL, D_B), BF16),
        grid=(T_ALL // tm,),
        in_specs=[
            pl.BlockSpec((tm, D_B), lambda i: (i, C_UB // D_B)),
            pl.BlockSpec((tm, D_B), lambda i: (i, 0)),
            pl.BlockSpec((tm, D_B), lambda i: (jnp.maximum(i - n_ctx_tiles, 0), 0)),
            pl.BlockSpec((1, D_B), lambda i: (0, 0)),
            pl.BlockSpec((D_B, D_B), lambda i: (0, 0)),
            pl.BlockSpec((1, D_B), lambda i: (0, 0)),
        ],
        out_specs=pl.BlockSpec((tm, D_B), lambda i: (i, 0)),
        compiler_params=_cp(("arbitrary",)),
        name="s5_finish",
    )(proj, y, y_bwd_lat, d, w_glu, b_glu)


def _merge_kernel(xc_ref, xl_ref, gl_ref, oa_ref, ob_ref, oc_ref, mod_ref, g_ref, wa_ref, wb_ref, wc_ref, wo_ref,
                  wr_ref, br_ref, x1_ref, h2row_ref, idx_ref, rank_ref, wgt_ref, cnt_ref, cnt_acc):
    @pl.when(pl.program_id(0) == 0)
    def _():
        cnt_acc[...] = jnp.zeros_like(cnt_acc)

    m = None
    for br, (o_ref, w_ref) in enumerate(((oa_ref, wa_ref), (ob_ref, wb_ref), (oc_ref, wc_ref))):
        gate = _sigmoid(gl_ref[:, br * D_MODEL:(br + 1) * D_MODEL])
        t = gate * jnp.dot(o_ref[...], w_ref[...], preferred_element_type=F32)
        m = t if m is None else m + t
    x = _x_tile(xc_ref, xl_ref, pl.program_id(0), ROUTE_TM)
    x1 = x + mod_ref[2:3, :] * jnp.dot(m.astype(BF16), wo_ref[...], preferred_element_type=F32)
    x1_ref[...] = x1
    h2 = _rms(x1, g_ref[...])
    h2 = h2 * (1.0 + mod_ref[4:5, :]) + mod_ref[3:4, :]
    h2row_ref[...] = h2.reshape((h2.shape[0],) + ROW3)
    idx, rank, wgt, cnt = _route(h2, wr_ref, br_ref, cnt_acc[...])
    idx_ref[...] = idx
    rank_ref[...] = rank
    wgt_ref[...] = wgt
    cnt_acc[...] = cnt
    cnt_ref[...] = cnt


def merge(x_ctx, x_lat, proj, oa, ob, oc, mod, g, wa, wb, wc, wo, w_router, b_router):
    tm = ROUTE_TM
    full = lambda shape: pl.BlockSpec(shape, lambda i: (0,) * len(shape))
    tile = lambda: pl.BlockSpec((tm, LANES), lambda i: (i, 0))
    return pl.pallas_call(
        _merge_kernel,
        out_shape=(jax.ShapeDtypeStruct((T_ALL, D_MODEL), F32), jax.ShapeDtypeStruct((T_ALL,) + ROW3, F32),
                   jax.ShapeDtypeStruct((T_ALL, LANES), jnp.int32), jax.ShapeDtypeStruct((T_ALL, LANES), jnp.int32),
                   jax.ShapeDtypeStruct((T_ALL, LANES), F32), jax.ShapeDtypeStruct((1, LANES), F32)),
        grid=(T_ALL // tm,),
        in_specs=_x_specs(tm, lambda i: i) + [
            pl.BlockSpec((tm, 3 * D_MODEL), lambda i: (i, 0)),
            pl.BlockSpec((tm, 512), lambda i: (i, 0)),
            pl.BlockSpec((tm, 512), lambda i: (i, 0)),
            pl.BlockSpec((tm, 512), lambda i: (i, 0)),
            pl.BlockSpec((None, 6, D_MODEL), lambda i: (_group_of_tile(i, tm), 0, 0)),
            full((1, D_MODEL)),
            full((512, D_MODEL)), full((512, D_MODEL)), full((512, D_MODEL)), full((D_MODEL, D_MODEL)),
            full((D_MODEL, LANES)), full((1, LANES)),
        ],
        out_specs=(pl.BlockSpec((tm, D_MODEL), lambda i: (i, 0)), pl.BlockSpec((tm,) + ROW3, lambda i: (i, 0, 0)),
                   tile(), tile(), tile(), full((1, LANES))),
        scratch_shapes=[pltpu.VMEM((1, LANES), F32)],
        compiler_params=_cp(("arbitrary",)),
        name="merge",
    )(x_ctx, x_lat, proj, oa, ob, oc, mod, g, wa, wb, wc, wo, w_router, b_router)


ROUTE_TM = 512


def _route(h, w_ref, b_ref, cnt):
    w = w_ref[...]
    h_hi = h.astype(BF16)
    h_lo = (h - h_hi.astype(F32)).astype(BF16)
    w_hi = w.astype(BF16)
    w_lo = (w - w_hi.astype(F32)).astype(BF16)
    dot = functools.partial(jnp.dot, preferred_element_type=F32)
    logits = dot(h_hi, w_hi) + (dot(h_hi, w_lo) + dot(h_lo, w_hi)) + b_ref[...]
    lane_i = lax.broadcasted_iota(jnp.int32, (ROUTE_TM, LANES), 1)
    lane = lane_i.astype(F32)
    r_i = lax.broadcasted_iota(jnp.int32, (ROUTE_TM, ROUTE_TM), 0)
    c_i = lax.broadcasted_iota(jnp.int32, (ROUTE_TM, ROUTE_TM), 1)
    earlier = jnp.where(c_i < r_i, 1.0, 0.0).astype(BF16)

    idx_out = jnp.zeros((ROUTE_TM, LANES), F32)
    rank_out = jnp.zeros((ROUTE_TM, LANES), F32)
    val_out = jnp.zeros((ROUTE_TM, LANES), F32)
    v0 = None
    esum = None
    for k in range(TOP_K):
        m = logits.max(axis=-1, keepdims=True)
        sel = jnp.min(jnp.where(logits == m, lane, float(LANES)), axis=-1, keepdims=True)
        hit = lane == sel
        logits = jnp.where(hit, -jnp.inf, logits)
        onehot = jnp.where(hit, 1.0, 0.0)
        within = jnp.dot(earlier, onehot.astype(BF16), preferred_element_type=F32)
        rank = jnp.sum(onehot * (within + cnt), axis=-1, keepdims=True)
        cnt = cnt + jnp.sum(onehot, axis=0, keepdims=True)
        if k == 0:
            v0 = m
        e = jnp.exp(m - v0)
        esum = e if esum is None else esum + e
        idx_out = jnp.where(lane_i == k, sel, idx_out)
        rank_out = jnp.where(lane_i == k, rank, rank_out)
        val_out = jnp.where(lane_i == k, e, val_out)
    return idx_out.astype(jnp.int32), rank_out.astype(jnp.int32), val_out * (1.0 / esum), cnt


PLAN_UNROLL = 16
N_DUMP = 5 * BLK
PLAN_LEN = BUF_LEN + BLK
assert TOP_K == 4 and T_ALL & (T_ALL - 1) == 0


def _invert_kernel(dest_ref, fill_hbm, inv_ref):
    pltpu.sync_copy(fill_hbm, inv_ref)
    tok_step = PLAN_UNROLL // TOP_K

    def put(i, c):
        for j in range(PLAN_UNROLL):
            inv_ref[dest_ref[i * PLAN_UNROLL + j]] = i * tok_step + ((j % TOP_K) * T_ALL + j // TOP_K)
        return c

    lax.fori_loop(0, N_ASSIGN // PLAN_UNROLL, put, 0)


def invert_plan(dest):
    r = jnp.arange(PLAN_LEN, dtype=jnp.int32)
    fill = N_ASSIGN + jnp.where(r < BLK, 2 * BLK + r, (r - BLK) & (2 * BLK - 1))
    return pl.pallas_call(
        _invert_kernel,
        out_shape=jax.ShapeDtypeStruct((PLAN_LEN,), jnp.int32),
        in_specs=[pl.BlockSpec(memory_space=pltpu.SMEM), pl.BlockSpec(memory_space=pl.ANY)],
        out_specs=pl.BlockSpec(memory_space=pltpu.SMEM),
        name="invert_plan",
    )(dest, fill)


PAIR_TILE = 2 * LANES


def _pair_selection():
    r = lax.broadcasted_iota(jnp.int32, (PAIR_TILE, PAIR_TILE), 0)
    c = lax.broadcasted_iota(jnp.int32, (PAIR_TILE, PAIR_TILE), 1)
    return (r == jnp.where(c < LANES, 2 * c, 2 * (c - LANES) + 1)).astype(BF16)


EXP_NBUF = 3


def _expert_kernel(be_ref, first_ref, par_ref, nxt_ref, inv_ref, h_hbm, wu_hbm, bu_ref, wd_hbm, bd_ref, sel_ref,
                   comb_hbm, xbuf, ybuf, wu_ref, wd_ref, wu_st, wd_st, gsem, ssem, wsem, *, layer):
    i = pl.program_id(0)
    n = pl.num_programs(0)
    cur = i % EXP_NBUF
    nxt = (i + 2) % EXP_NBUF
    prv = nxt

    def gather(block, s):
        base = (block + 1) * BLK
        for r in range(BLK):
            tok = inv_ref[base + r] & (T_ALL - 1)
            pltpu.make_async_copy(h_hbm.at[tok], xbuf.at[s, r], gsem.at[s]).start()

    def scatter(block, s):
        base = (block + 1) * BLK
        for r in range(BLK):
            a = inv_ref[base + r]
            pltpu.make_async_copy(ybuf.at[s, r], comb_hbm.at[a], ssem.at[s]).start()

    def wait_block(sem):
        pltpu.make_async_copy(h_hbm.at[pl.ds(0, BLK)], xbuf.at[0], sem).wait()

    def weight_copies(e, p):
        return (pltpu.make_async_copy(wu_hbm.at[layer, e], wu_st.at[p], wsem.at[p]),
                pltpu.make_async_copy(wd_hbm.at[layer, e], wd_st.at[p], wsem.at[p]))

    @pl.when(i == 0)
    def _():
        for cp in weight_copies(be_ref[0], 0):
            cp.start()
        ybuf[...] = jnp.zeros_like(ybuf)
        for s in range(EXP_NBUF - 1):
            for r in range(BLK):
                dump = N_ASSIGN + (3 + s) * BLK + r
                pltpu.make_async_copy(ybuf.at[s, r], comb_hbm.at[dump], ssem.at[s]).start()
        gather(0, 0)
        gather(1, 1)

    wait_block(gsem.at[cur])
    wait_block(ssem.at[cur])

    @pl.when(first_ref[i] == 1)
    def _():
        p = par_ref[i]
        for cp in weight_copies(be_ref[i], p):
            cp.wait()

        @pl.when(nxt_ref[i] >= 0)
        def _():
            for cp in weight_copies(nxt_ref[i], 1 - p):
                cp.start()

        sel = sel_ref[...]
        for j in range(2 * D_FF // PAIR_TILE):
            cols = slice(j * PAIR_TILE, (j + 1) * PAIR_TILE)
            wu_ref[:, cols] = jnp.dot(wu_st[p, :, cols].astype(BF16), sel, preferred_element_type=F32).astype(BF16)
        wd_ref[...] = wd_st[p].astype(BF16)

    gather(jnp.minimum(i + 2, n - 1), nxt)
    scatter(i - 1, prv)

    x = xbuf[cur].reshape(BLK, D_MODEL).astype(BF16)
    h = jnp.dot(x, wu_ref[...], preferred_element_type=F32) + bu_ref[...]
    acts = []
    for j in range(2 * D_FF // PAIR_TILE):
        glu = jnp.minimum(h[:, j * PAIR_TILE:j * PAIR_TILE + LANES], SWIGLU_LIMIT)
        lin = jnp.clip(h[:, j * PAIR_TILE + LANES:(j + 1) * PAIR_TILE], -SWIGLU_LIMIT, SWIGLU_LIMIT)
        acts.append((glu * _sigmoid(SWIGLU_ALPHA * glu) * (lin + 1.0)).astype(BF16))
    act = jnp.concatenate(acts, axis=1)
    y = jnp.dot(act, wd_ref[...], preferred_element_type=F32) + bd_ref[...]
    ybuf[cur] = y.reshape((BLK,) + ROW3)

    @pl.when(i == n - 1)
    def _():
        for s in range(EXP_NBUF):
            @pl.when(s != cur)
            def _():
                wait_block(gsem.at[s])
                wait_block(ssem.at[s])
        scatter(i, cur)
        wait_block(ssem.at[cur])


def experts(layer, block_e, inv, h2row, wu, bu, wd, bd):
    first = jnp.concatenate([jnp.ones((1,), jnp.int32), (block_e[1:] != block_e[:-1]).astype(jnp.int32)])
    slot = (jnp.cumsum(first) - 1) & 1
    later = jnp.where(block_e[None, :] > block_e[:, None], block_e[None, :], N_EXP).min(axis=1)
    nxt = jnp.where(later == N_EXP, -1, later).astype(jnp.int32)
    per_expert = lambda *blk: pl.BlockSpec((None, None) + blk, lambda i, be, *_: (layer, be[i], 0, 0))
    return pl.pallas_call(
        functools.partial(_expert_kernel, layer=layer),
        out_shape=jax.ShapeDtypeStruct((N_ASSIGN + N_DUMP,) + ROW3, F32),
        grid_spec=pltpu.PrefetchScalarGridSpec(
            num_scalar_prefetch=5,
            grid=(N_BLOCKS,),
            in_specs=[
                pl.BlockSpec(memory_space=pl.ANY),
                pl.BlockSpec(memory_space=pl.ANY), per_expert(1, 2 * D_FF),
                pl.BlockSpec(memory_space=pl.ANY), per_expert(1, D_MODEL),
                pl.BlockSpec((PAIR_TILE, PAIR_TILE), lambda i, *_: (0, 0)),
            ],
            out_specs=pl.BlockSpec(memory_space=pl.ANY),
            scratch_shapes=[
                pltpu.VMEM((EXP_NBUF, BLK) + ROW3, F32), pltpu.VMEM((EXP_NBUF, BLK) + ROW3, F32),
                pltpu.VMEM((D_MODEL, 2 * D_FF), BF16), pltpu.VMEM((D_FF, D_MODEL), BF16),
                pltpu.VMEM((2, D_MODEL, 2 * D_FF), F32), pltpu.VMEM((2, D_FF, D_MODEL), F32),
                pltpu.SemaphoreType.DMA((EXP_NBUF,)), pltpu.SemaphoreType.DMA((EXP_NBUF,)),
                pltpu.SemaphoreType.DMA((2,)),
            ],
        ),
        compiler_params=_cp(("arbitrary",)),
        name="experts",
    )(block_e, first, slot.astype(jnp.int32), nxt, inv, h2row, wu, bu, wd, bd, _pair_selection())


COMB_TM = 256


def _combine_kernel(c0_ref, c1_ref, c2_ref, c3_ref, x_ref, w_ref, mod_ref, oc_ref, ol_ref):
    w = w_ref[...]
    acc = None
    for k, c_ref in enumerate((c0_ref, c1_ref, c2_ref, c3_ref)):
        t = w[:, k:k + 1] * c_ref[...].reshape(COMB_TM, D_MODEL)
        acc = t if acc is None else acc + t
    out = x_ref[...] + mod_ref[5:6, :] * acc
    is_latent = pl.program_id(0) >= N_CTX // COMB_TM

    @pl.when(is_latent)
    def _():
        ol_ref[...] = out

    @pl.when(jnp.logical_not(is_latent))
    def _():
        oc_ref[...] = out


def combine(comb, x1, wgt, mod):
    tiles = T_ALL // COMB_TM
    kth = lambda k: pl.BlockSpec((COMB_TM,) + ROW3, lambda i: (k * tiles + i, 0, 0))
    return pl.pallas_call(
        _combine_kernel,
        out_shape=(jax.ShapeDtypeStruct((N_CTX, D_MODEL), F32), jax.ShapeDtypeStruct((N_LAT, D_MODEL), F32)),
        grid=(tiles,),
        in_specs=[
            kth(0), kth(1), kth(2), kth(3),
            pl.BlockSpec((COMB_TM, D_MODEL), lambda i: (i, 0)),
            pl.BlockSpec((COMB_TM, LANES), lambda i: (i, 0)),
            pl.BlockSpec((None, 6, D_MODEL), lambda i: (_group_of_tile(i, COMB_TM), 0, 0)),
        ],
        out_specs=tuple(_x_specs(COMB_TM, lambda i: i)),
        compiler_params=_cp(("arbitrary",)),
        name="combine",
    )(comb, comb, comb, comb, x1, wgt, mod)


def _rope_table():
    pos = jnp.arange(DEC_SEQ)
    row = (pos // GRID_W).astype(F32)[:, None]
    col = (pos % GRID_W).astype(F32)[:, None]

    def parts(rot_dim):
        nf = rot_dim // 4
        inv = ROPE_BASE ** (-jnp.arange(nf, dtype=F32) / nf)
        cr, sr, cc, sc = jnp.cos(row * inv), jnp.sin(row * inv), jnp.cos(col * inv), jnp.sin(col * inv)
        return jnp.concatenate([cr, cr, cc, cc], axis=1), jnp.concatenate([-sr, sr, -sc, sc], axis=1)

    ca, sa = (jnp.tile(t, (1, 2)) for t in parts(HD_A))
    cc, sc = parts(ROPE_C)
    pad = lambda t, fill: jnp.pad(t, ((0, 0), (NOPE_C, LANES - QK_C)), constant_values=fill)
    lat = jnp.concatenate([ca, sa, pad(cc, 1.0), pad(sc, 0.0)], axis=1)
    ones, zeros = jnp.ones((DEC_SEQ, LANES), F32), jnp.zeros((DEC_SEQ, LANES), F32)
    ident = jnp.concatenate([ones, zeros, ones, zeros], axis=1)
    return jnp.concatenate([ident, lat], axis=0)


def _pair_swaps():
    j = lax.broadcasted_iota(jnp.int32, (LANES, LANES), 0)
    i = lax.broadcasted_iota(jnp.int32, (LANES, LANES), 1)

    def swap(half):
        first = (i % (2 * half)) < half
        return (j == jnp.where(first, i + half, i - half)).astype(BF16)

    return jnp.stack([swap(HD_A // 4), swap(ROPE_C // 4)])


def _arrange_w_in(w_in):
    o = 0
    parts = {}
    for name, n in (("qa", HQ_A * HD_A), ("ka", HKV_A * HD_A), ("va", HKV_A * HD_A), ("ub", D_B), ("cq", Q_LORA),
                    ("ckv", KV_LORA), ("kr", ROPE_C), ("gl", 3 * D_MODEL)):
        parts[name] = w_in[..., o:o + n]
        o += n
    qa = parts["qa"].reshape(DEPTH, D_MODEL, HKV_A, G_A, 1, HD_A)
    eye = jnp.eye(HKV_A, dtype=F32).reshape(1, 1, HKV_A, 1, HKV_A, 1)
    qa_slots = (qa * eye).reshape(DEPTH, D_MODEL, HQ_A * LANES)
    kr = jnp.pad(parts["kr"], ((0, 0), (0, 0), (NOPE_C, LANES - QK_C)))
    w = jnp.concatenate([parts["gl"], qa_slots, parts["ub"], parts["cq"], parts["ka"], parts["va"], parts["ckv"], kr],
                        axis=-1)
    return w.astype(BF16)


def _s5_params(lam_re, lam_im, log_dt, b_re, b_im, c_re, c_im):
    dt = jnp.exp(log_dt)[..., None]
    decay = jnp.exp(lam_re * dt)
    ab_re, ab_im = decay * jnp.cos(lam_im * dt), decay * jnp.sin(lam_im * dt)
    den = lam_re * lam_re + lam_im * lam_im
    f_re = ((ab_re - 1) * lam_re + ab_im * lam_im) / den
    f_im = (ab_im * lam_re - (ab_re - 1) * lam_im) / den
    bb_re = f_re[..., None] * b_re - f_im[..., None] * b_im
    bb_im = f_re[..., None] * b_im + f_im[..., None] * b_re
    nj, gpb = D_B // LANES, LANES // GS_B
    eye = jnp.eye(gpb, dtype=F32)

    def blockdiag_b(bb):
        t = bb.transpose(0, 1, 3, 2).reshape(2, nj, gpb, GS_B, P_B)
        return (t[:, :, :, :, None, :] * eye[None, None, :, None, :, None]).reshape(2, nj, LANES, gpb * P_B)

    def blockdiag_c(cc):
        t = cc.transpose(0, 1, 3, 2).reshape(2, nj, gpb, P_B, GS_B)
        return (t[:, :, :, :, None, :] * eye[None, None, :, None, :, None]).reshape(2, nj, gpb * P_B, LANES)

    wb = jnp.concatenate([blockdiag_b(bb_re), blockdiag_b(bb_im)], axis=-1).astype(BF16)
    wc = jnp.concatenate([blockdiag_c(c_re), -blockdiag_c(c_im)], axis=-2).astype(BF16)
    return wb, wc, ab_re.reshape(2, G_B * P_B), ab_im.reshape(2, G_B * P_B)


def kernel(x_prompt, x_sample, c, cache_attn_k, cache_attn_v, cache_mla_ckv, cache_mla_krope, state_ssm_re, state_ssm_im, c_ctx, w_ada, b_ada, norm_mix_g, norm_ffn_g, w_in, q_norm_a, k_norm_a, sink_a, q_a_norm_c, kv_a_norm_c, w_uq_c, w_ukv_c, q_norm_c, k_norm_c, ssm_lam_re, ssm_lam_im, ssm_log_dt, ssm_b_re, ssm_b_im, ssm_c_re, ssm_c_im, ssm_d, w_glu, b_glu, w_br_a, w_br_b, w_br_c, w_out, w_router, b_router, w_up, b_up, w_down, b_down):
    x_ctx, x_lat = x_prompt.reshape(N_CTX, D_MODEL), x_sample.reshape(N_LAT, D_MODEL)
    cvecs = jnp.concatenate([c_ctx[None], c, jnp.zeros((N_GROUPS - 1 - DEC_BATCH, D_MODEL), F32)], axis=0)
    mods = adaln(cvecs, w_ada, b_ada)

    tab = _rope_table()
    w_in_r = _arrange_w_in(w_in)
    pad_slot = lambda g: jnp.pad(g, ((0, 0), (0, LANES - QK_C))).reshape(DEPTH, 1, LANES)
    gqa = jnp.tile(q_norm_a, (1, 2)).reshape(DEPTH, 1, LANES)
    gka = jnp.tile(k_norm_a, (1, 2)).reshape(DEPTH, 1, LANES)
    gqc, gkc = pad_slot(q_norm_c), pad_slot(k_norm_c)
    wuq = jnp.pad(w_uq_c.reshape(DEPTH, Q_LORA, H_C, QK_C), ((0, 0), (0, 0), (0, 0), (0, LANES - QK_C)))
    wuq = wuq.reshape(DEPTH, Q_LORA, H_C * LANES).astype(BF16)
    wukv4 = w_ukv_c.reshape(DEPTH, KV_LORA, H_C, NOPE_C + V_C)
    wuk = jnp.pad(wukv4[..., :NOPE_C], ((0, 0), (0, 0), (0, 0), (0, LANES - NOPE_C))).reshape(DEPTH, KV_LORA, H_C * LANES)
    wuv = wukv4[..., NOPE_C:].reshape(DEPTH, KV_LORA, H_C * V_C)
    wukv = jnp.concatenate([wuk, wuv], axis=-1).astype(BF16)
    sink = jnp.broadcast_to(sink_a[:, :, None], (DEPTH, HQ_A, LANES))
    w_router_p = jnp.pad(w_router, ((0, 0), (0, 0), (0, LANES - N_EXP)))
    b_router_p = jnp.pad(b_router, ((0, 0), (0, LANES - N_EXP)), constant_values=-jnp.inf).reshape(DEPTH, 1, LANES)
    b_up_p = b_up.reshape(DEPTH, N_EXP, 2 * D_FF // PAIR_TILE, LANES, 2).transpose(0, 1, 2, 4, 3)
    b_up_p = b_up_p.reshape(DEPTH, N_EXP, 1, 2 * D_FF)
    b_down_r = b_down.reshape(DEPTH, N_EXP, 1, D_MODEL)
    kr_cache = jnp.pad(cache_mla_krope, ((0, 0), (0, 0), (0, 0), (NOPE_C, LANES - QK_C)))

    new_k, new_v, new_ckv, new_kr, new_sre, new_sim = [], [], [], [], [], []
    for l in range(DEPTH):
        mod = mods[l]
        proj = inproj(l, x_ctx, x_lat, norm_mix_g[l], mod, w_in_r)
        qa, ka, qc, kc, vc, ckvn = prep(proj, tab, gqa[l], gka[l], q_a_norm_c[l].reshape(1, Q_LORA),
                                         kv_a_norm_c[l].reshape(1, KV_LORA), gqc[l], gkc[l], wuq[l], wukv[l])
        kc_cache, vc_cache = cache_keys(cache_mla_ckv[:, l].reshape(DEC_BATCH * PAST_LEN, KV_LORA),
                                        kr_cache[:, l].reshape(DEC_BATCH * PAST_LEN, LANES), gkc[l], wukv[l])
        oa = attn_a_lat(qa, ka, proj, cache_attn_k[:, l].reshape(DEC_BATCH, PAST_LEN, LANES),
                        cache_attn_v[:, l].reshape(DEC_BATCH, PAST_LEN, LANES), sink[l],
                        attn_a_ctx(qa, ka, proj, sink[l]))
        oc = attn_c_lat(qc, kc, vc, kc_cache, vc_cache, attn_c_ctx(qc, kc, vc))
        wb, wc, a_re, a_im = _s5_params(ssm_lam_re[l], ssm_lam_im[l], ssm_log_dt[l], ssm_b_re[l], ssm_b_im[l],
                                        ssm_c_re[l], ssm_c_im[l])
        proj4 = proj.reshape(TOK_CHUNKS // CHUNK_GROUP, CHUNK_GROUP, S5_TC, N_PROJ)
        per_row = lambda a, rep: jnp.repeat(a, rep, axis=0)
        y_all, f_re, f_im = s5_ctx(proj4, wb, wc, per_row(a_re, SUBLANES).reshape(2, SUBLANES, -1),
                                   per_row(a_im, SUBLANES).reshape(2, SUBLANES, -1))
        h0 = lambda st: st[:, l].transpose(1, 0, 2, 3).reshape(2 * DEC_BATCH, G_B * P_B)
        y_all, y_bwd = s5_lat(proj4, y_all, wb, wc, per_row(a_re, DEC_BATCH), per_row(a_im, DEC_BATCH),
                              h0(state_ssm_re), h0(state_ssm_im))
        ob = s5_finish(proj, y_all.reshape(T_ALL, D_B), y_bwd.reshape(N_LAT, D_B), ssm_d[l].reshape(1, D_B),
                       w_glu[l].astype(BF16), b_glu[l].reshape(1, D_B))
        x1, h2row, idx, rank, wgt, cnt = merge(
            x_ctx, x_lat, proj, oa, ob, oc, mod, norm_ffn_g[l].reshape(1, D_MODEL), w_br_a[l].astype(BF16),
            w_br_b[l].astype(BF16), w_br_c[l].astype(BF16), w_out[l].astype(BF16), w_router_p[l], b_router_p[l])
        counts = cnt[0, :N_EXP].astype(jnp.int32)
        padded = (counts + BLK - 1) // BLK * BLK
        pad_end = jnp.cumsum(padded)
        pad_start = pad_end - padded
        dest = (BLK + pad_start[idx[:, :TOP_K]] + rank[:, :TOP_K]).reshape(-1).astype(jnp.int32)
        block_start = jnp.arange(N_BLOCKS, dtype=jnp.int32) * BLK
        block_e = jnp.minimum(jnp.sum(pad_end[None, :] <= block_start[:, None], axis=1), N_EXP - 1)
        block_e = block_e.astype(jnp.int32)
        comb = experts(l, block_e, invert_plan(dest), h2row, w_up, b_up_p, w_down, b_down_r)
        x_ctx, x_lat = combine(comb, x1, wgt, mod)

        new_k.append(ka[:N_CTX].reshape(BATCH, SEQ, HKV_A, HD_A))
        new_v.append(proj[:N_CTX, C_VA:C_VA + LANES].reshape(BATCH, SEQ, HKV_A, HD_A))
        new_ckv.append(ckvn[:N_CTX].reshape(BATCH, SEQ, KV_LORA))
        new_kr.append(proj[:N_CTX, C_KR + NOPE_C:C_KR + QK_C].reshape(BATCH, SEQ, ROPE_C))
        new_sre.append(f_re.reshape(2, BATCH, G_B, P_B).transpose(1, 0, 2, 3))
        new_sim.append(f_im.reshape(2, BATCH, G_B, P_B).transpose(1, 0, 2, 3))

    y_prompt = x_ctx.reshape(BATCH, SEQ, D_MODEL)
    y_sample = x_lat.reshape(DEC_BATCH, DEC_SEQ, D_MODEL)
    return (y_prompt, y_sample, jnp.stack(new_k, axis=1), jnp.stack(new_v, axis=1), jnp.stack(new_ckv, axis=1),
            jnp.stack(new_kr, axis=1), jnp.stack(new_sre, axis=1), jnp.stack(new_sim, axis=1))
```
